```python
import math
import jax
import jax.numpy as jnp
from jax import lax
import numpy as np

D_MODEL = 2048
BATCH = 4
SEQ = 2048
DEPTH = 1
DEC_BATCH = 128
DEC_SEQ = 4
PAST_LEN = 16384
PAGE_SIZE = 128

MIX_A = D_MODEL // 2
MIX_B = D_MODEL - MIX_A
GDN_HEADS = 8
GDN_DK = MIX_A // GDN_HEADS
GDN_DV = MIX_A // GDN_HEADS
QK_W = GDN_HEADS * GDN_DK
QKV_W = 2 * QK_W + GDN_HEADS * GDN_DV
CONV_W = 4
CHUNK = 64
POOL_WINDOWS = (2, 4, 8, 16)
POOL_GROUPS = len(POOL_WINDOWS)
POOL_CH = MIX_B // POOL_GROUPS
POOL_BUF = max(POOL_WINDOWS) - 1
N_MEM = 256
X_HEADS = 4
X_HEAD_DIM = D_MODEL // X_HEADS
EPS = 1e-6
COL_ZA = QKV_W
COL_A = COL_ZA + GDN_HEADS * GDN_DV
COL_B = COL_A + GDN_HEADS
COL_U = COL_B + GDN_HEADS
COL_ZB = COL_U + MIX_B
IN_COLS = COL_ZB + MIX_B

kernel_name = "hymba_gdn_pool_memxattn_step"


def rmsnorm(x, g):
    xf = x.astype(jnp.float32)
    y = xf * lax.rsqrt(jnp.mean(xf * xf, axis=-1, keepdims=True) + EPS) * g.astype(jnp.float32)
    return y.astype(x.dtype)


def l2norm(x):
    xf = x.astype(jnp.float32)
    return xf * lax.rsqrt(jnp.sum(xf * xf, axis=-1, keepdims=True) + EPS)


def short_conv(x, buf, w):
    t_len = x.shape[1]
    ext = jnp.concatenate([buf.astype(x.dtype), x], axis=1)
    y = sum(ext[:, j:j + t_len] * w[j] for j in range(CONV_W))
    return jax.nn.silu(y), ext[:, ext.shape[1] - (CONV_W - 1):]


def gated_delta_rule(q, k, v, g, beta, s0, chunk):
    bsz, t_len, n_h, _ = q.shape
    dv = v.shape[-1]
    n_c = t_len // chunk

    def blocks(a):
        a = a.astype(jnp.float32).reshape((bsz, n_c, chunk, n_h) + a.shape[3:])
        return jnp.moveaxis(a, (1, 3), (0, 2))

    qc, kc, vc, bc = blocks(q), blocks(k), blocks(v), blocks(beta)
    gc = jnp.cumsum(blocks(g), axis=-1)
    kb = kc * bc[..., None]
    vb = vc * bc[..., None]
    idx = jnp.arange(chunk)
    incl = idx[:, None] >= idx[None, :]
    strict = idx[:, None] > idx[None, :]
    diff = gc[..., :, None] - gc[..., None, :]
    decay_incl = jnp.exp(jnp.where(incl, diff, -jnp.inf))
    decay_strict = jnp.where(strict, decay_incl, 0.0)
    low = jnp.einsum('nbhik,nbhjk->nbhij', kb, kc) * decay_strict
    eye = jnp.eye(chunk, dtype=jnp.float32)
    t_inv = lax.linalg.triangular_solve(low + eye, jnp.broadcast_to(eye, low.shape),
                                        left_side=True, lower=True, unit_diagonal=True)
    u = jnp.einsum('nbhij,nbhjv->nbhiv', t_inv, vb)
    w = jnp.einsum('nbhij,nbhjk->nbhik', t_inv, kb * jnp.exp(gc)[..., None])
    a_intra = jnp.einsum('nbhik,nbhjk->nbhij', qc, kc) * decay_incl

    def step(s, xs):
        q_, k_, u_, w_, g_, a_ = xs
        v_new = u_ - jnp.einsum('bhck,bhkv->bhcv', w_, s)
        o = (jnp.einsum('bhck,bhkv->bhcv', q_ * jnp.exp(g_)[..., None], s)
             + jnp.einsum('bhij,bhjv->bhiv', a_, v_new))
        g_last = g_[..., -1]
        k_dec = k_ * jnp.exp(g_last[..., None] - g_)[..., None]
        s = s * jnp.exp(g_last)[..., None, None] + jnp.einsum('bhck,bhcv->bhkv', k_dec, v_new)
        return s, o

    s_fin, o = lax.scan(step, s0.astype(jnp.float32), (qc, kc, u, w, gc, a_intra))
    o = jnp.moveaxis(o, (0, 2), (1, 3)).reshape(bsz, t_len, n_h, dv)
    return o, s_fin


def multiscale_pool(u, buf, pos0, pool_w, pool_scale):
    bsz, t_len, c = u.shape
    ext = jnp.concatenate([buf.astype(u.dtype), u], axis=1)
    cs = jnp.cumsum(ext.astype(jnp.float32), axis=1)
    cs = jnp.concatenate([jnp.zeros((bsz, 1, c), jnp.float32), cs], axis=1)
    pos = pos0 + jnp.arange(t_len)
    outs = []
    for gi, win in enumerate(POOL_WINDOWS):
        sl = slice(gi * POOL_CH, (gi + 1) * POOL_CH)
        hi = cs[:, POOL_BUF + 1:POOL_BUF + 1 + t_len, sl]
        lo = cs[:, POOL_BUF + 1 - win:POOL_BUF + 1 - win + t_len, sl]
        cnt = jnp.minimum(win, pos + 1).astype(jnp.float32)[None, :, None]
        d = (hi - lo) / cnt - u[:, :, sl].astype(jnp.float32)
        outs.append(jnp.einsum('btc,cd->btd', d.astype(u.dtype), pool_w[gi]))
    y = jnp.concatenate(outs, axis=-1) * pool_scale
    return y, ext[:, ext.shape[1] - POOL_BUF:]


def memory_kv(mem, norm_mem, w_ck, w_cv):
    bsz, n_mem, _ = mem.shape
    hm = rmsnorm(mem, norm_mem)
    mk = (hm @ w_ck).reshape(bsz, n_mem, X_HEADS, X_HEAD_DIM)
    mv = (hm @ w_cv).reshape(bsz, n_mem, X_HEADS, X_HEAD_DIM)
    return mk, mv


def hybrid_layer(x, mem_k, mem_v, s0, conv_buf, pool_buf, pos0, chunk,
                 norm_mix, w_in, conv_w, a_log, dt_bias, gdn_norm, pool_w, pool_scale, w_out,
                 norm_cross, w_cq, w_co):
    bsz, t_len, _ = x.shape
    h = rmsnorm(x, norm_mix)
    proj = h @ w_in
    qkv, conv_new = short_conv(proj[..., :QKV_W], conv_buf, conv_w)
    q = l2norm(qkv[..., :QK_W].reshape(bsz, t_len, GDN_HEADS, GDN_DK)) * (GDN_DK ** -0.5)
    k = l2norm(qkv[..., QK_W:2 * QK_W].reshape(bsz, t_len, GDN_HEADS, GDN_DK))
    v = qkv[..., 2 * QK_W:].reshape(bsz, t_len, GDN_HEADS, GDN_DV)
    z_a = proj[..., COL_ZA:COL_A].reshape(bsz, t_len, GDN_HEADS, GDN_DV)
    beta = jax.nn.sigmoid(proj[..., COL_B:COL_U].astype(jnp.float32))
    g = -jnp.exp(a_log.astype(jnp.float32)) * jax.nn.softplus(
        proj[..., COL_A:COL_B].astype(jnp.float32) + dt_bias.astype(jnp.float32))
    o, s_new = gated_delta_rule(q, k, v, g, beta, s0, chunk)
    o_a = (rmsnorm(o.astype(x.dtype), gdn_norm) * jax.nn.silu(z_a)).reshape(bsz, t_len, MIX_A)
    pooled, pool_new = multiscale_pool(proj[..., COL_U:COL_ZB], pool_buf, pos0, pool_w, pool_scale)
    o_b = pooled * jax.nn.silu(proj[..., COL_ZB:])
    x = x + jnp.concatenate([o_a, o_b], axis=-1) @ w_out
    h2 = rmsnorm(x, norm_cross)
    qx = (h2 @ w_cq).reshape(bsz, t_len, X_HEADS, X_HEAD_DIM)
    scores = jnp.einsum('bthd,bmhd->bhtm', qx, mem_k).astype(jnp.float32) * (X_HEAD_DIM ** -0.5)
    probs = jax.nn.softmax(scores, axis=-1).astype(x.dtype)
    ctx = jnp.einsum('bhtm,bmhd->bthd', probs, mem_v).reshape(bsz, t_len, D_MODEL)
    x = x + ctx @ w_co
    return x, s_new, conv_new, pool_new


def setup_inputs(seed: int = 0) -> dict:
    key = jax.random.key(seed)
    ks = jax.random.split(key, 26)
    f32 = jnp.float32
    n_l = DEPTH

    def nrm(k, shape, scale):
        return jax.random.normal(k, shape, f32) * scale

    def gain(k, shape):
        return 1.0 + 0.02 * jax.random.normal(k, shape, f32)

    dt = jnp.exp(jax.random.uniform(ks[11], (n_l, GDN_HEADS), f32, math.log(1e-3), math.log(1e-1)))
    dt_bias = dt + jnp.log(-jnp.expm1(-dt))
    a_log = jnp.log(jax.random.uniform(ks[12], (n_l, GDN_HEADS), f32, 1.0, 16.0))
    return {
        "x_prompt": nrm(ks[0], (BATCH, SEQ, D_MODEL), 1.0),
        "x_sample": nrm(ks[1], (DEC_BATCH, DEC_SEQ, D_MODEL), 1.0),
        "mem_prompt": nrm(ks[2], (BATCH, N_MEM, D_MODEL), 1.0),
        "cache_mem_k": nrm(ks[3], (n_l, DEC_BATCH, N_MEM, X_HEADS, X_HEAD_DIM), 1.0),
        "cache_mem_v": nrm(ks[4], (n_l, DEC_BATCH, N_MEM, X_HEADS, X_HEAD_DIM), 1.0),
        "state_delta": nrm(ks[5], (n_l, DEC_BATCH, GDN_HEADS, GDN_DK, GDN_DV), 0.1),
        "state_conv": nrm(ks[6], (n_l, DEC_BATCH, CONV_W - 1, QKV_W), 1.0),
        "state_pool": nrm(ks[7], (n_l, DEC_BATCH, POOL_BUF, MIX_B), 1.0),
        "norm_mix": gain(ks[8], (n_l, D_MODEL)),
        "w_in": nrm(ks[9], (n_l, D_MODEL, IN_COLS), D_MODEL ** -0.5),
        "conv_w": nrm(ks[10], (n_l, CONV_W, QKV_W), CONV_W ** -0.5),
        "a_log": a_log,
        "dt_bias": dt_bias,
        "gdn_norm": gain(ks[13], (n_l, GDN_DV)),
        "pool_w": nrm(ks[14], (n_l, POOL_GROUPS, POOL_CH, POOL_CH), POOL_CH ** -0.5),
        "pool_scale": gain(ks[15], (n_l, MIX_B)),
        "w_out": nrm(ks[16], (n_l, D_MODEL, D_MODEL), D_MODEL ** -0.5),
        "norm_mem": gain(ks[17], (n_l, D_MODEL)),
        "norm_cross": gain(ks[18], (n_l, D_MODEL)),
        "w_cq": nrm(ks[19], (n_l, D_MODEL, D_MODEL), D_MODEL ** -0.5),
        "w_ck": nrm(ks[20], (n_l, D_MODEL, D_MODEL), D_MODEL ** -0.5),
        "w_cv": nrm(ks[21], (n_l, D_MODEL, D_MODEL), D_MODEL ** -0.5),
        "w_co": nrm(ks[22], (n_l, D_MODEL, D_MODEL), D_MODEL ** -0.5),
        "norm_final": gain(ks[23], (D_MODEL,)),
    }


def reference(x_prompt, x_sample, mem_prompt, cache_mem_k, cache_mem_v, state_delta, state_conv,
              state_pool, norm_mix, w_in, conv_w, a_log, dt_bias, gdn_norm, pool_w, pool_scale,
              w_out, norm_mem, norm_cross, w_cq, w_ck, w_cv, w_co, norm_final):
    xp, xs = x_prompt, x_sample
    bp, tp, _ = xp.shape
    ts = xs.shape[1]
    chunk_p = min(CHUNK, tp)
    mk_l, mv_l, dp_l, cp_l, pp_l, ds_l, cs_l, ps_l = [], [], [], [], [], [], [], []
    for l in range(DEPTH):
        lw = dict(norm_mix=norm_mix[l], w_in=w_in[l], conv_w=conv_w[l], a_log=a_log[l],
                  dt_bias=dt_bias[l], gdn_norm=gdn_norm[l], pool_w=pool_w[l],
                  pool_scale=pool_scale[l], w_out=w_out[l], norm_cross=norm_cross[l],
                  w_cq=w_cq[l], w_co=w_co[l])
        mk_p, mv_p = memory_kv(mem_prompt, norm_mem[l], w_ck[l], w_cv[l])
        xp, d_p, c_p, p_p = hybrid_layer(
            xp, mk_p, mv_p,
            jnp.zeros((bp, GDN_HEADS, GDN_DK, GDN_DV), jnp.float32),
            jnp.zeros((bp, CONV_W - 1, QKV_W), xp.dtype),
            jnp.zeros((bp, POOL_BUF, MIX_B), xp.dtype),
            0, chunk_p, **lw)
        xs, d_s, c_s, p_s = hybrid_layer(
            xs, cache_mem_k[l], cache_mem_v[l], state_delta[l], state_conv[l], state_pool[l],
            PAST_LEN, ts, **lw)
        mk_l.append(mk_p); mv_l.append(mv_p); dp_l.append(d_p); cp_l.append(c_p); pp_l.append(p_p)
        ds_l.append(d_s); cs_l.append(c_s); ps_l.append(p_s)
    y_prompt = rmsnorm(xp, norm_final)
    y_sample = rmsnorm(xs, norm_final)
    return (y_prompt, y_sample, jnp.stack(mk_l), jnp.stack(mv_l), jnp.stack(dp_l), jnp.stack(cp_l),
            jnp.stack(pp_l), jnp.stack(ds_l), jnp.stack(cs_l), jnp.stack(ps_l))
```

```python
import functools
import math

import jax
import jax.numpy as jnp
from jax import lax
from jax.experimental import pallas as pl
from jax.experimental.pallas import tpu as pltpu

F32 = jnp.float32
BF16 = jnp.bfloat16

D_MODEL = 2048
MIX_A = D_MODEL // 2
MIX_B = D_MODEL - MIX_A
GDN_HEADS = 8
GDN_DK = MIX_A // GDN_HEADS
GDN_DV = MIX_A // GDN_HEADS
QK_W = GDN_HEADS * GDN_DK
QKV_W = 2 * QK_W + GDN_HEADS * GDN_DV
CONV_W = 4
POOL_WINDOWS = (2, 4, 8, 16)
POOL_GROUPS = len(POOL_WINDOWS)
POOL_CH = MIX_B // POOL_GROUPS
POOL_BUF = max(POOL_WINDOWS) - 1
X_HEADS = 4
X_HEAD_DIM = D_MODEL // X_HEADS
PAST_LEN = 16384
EPS = 1e-6
COL_ZA = QKV_W
COL_A = COL_ZA + GDN_HEADS * GDN_DV
COL_B = COL_A + GDN_HEADS
COL_U = COL_B + GDN_HEADS
COL_ZB = COL_U + MIX_B
IN_COLS = COL_ZB + MIX_B

W_MAIN = IN_COLS - 2 * GDN_HEADS
LANES = 128
SUBLANES = 8
GDN_CHUNK = 128
POOL_HALO = 16
VMEM_LIMIT = 56 * 1024 * 1024


def _cparams(sem):
    return pltpu.CompilerParams(dimension_semantics=sem, vmem_limit_bytes=VMEM_LIMIT)


def _dot(a, b):
    return jnp.dot(a, b, preferred_element_type=F32)


def _dot_nt(a, b):
    return lax.dot_general(a, b, (((1,), (1,)), ((), ())), preferred_element_type=F32)


def _dot_tn(a, b):
    return lax.dot_general(a, b, (((0,), (0,)), ((), ())), preferred_element_type=F32)


def _rms(x, g):
    return x * lax.rsqrt(jnp.mean(x * x, axis=-1, keepdims=True) + EPS) * g


def _silu(x):
    return x * jax.nn.sigmoid(x)


def _softplus(x):
    return jnp.maximum(x, 0.0) + jnp.log1p(jnp.exp(-jnp.abs(x)))


def _norm_proj_kernel(x_ref, g_ref, w_ref, *rest, with_side):
    if with_side:
        ws_ref, out_ref, side_ref, h_scr = rest
    else:
        out_ref, h_scr = rest

    @pl.when(pl.program_id(1) == 0)
    def _():
        h = _rms(x_ref[...], g_ref[...]).astype(BF16)
        h_scr[...] = h
        if with_side:
            side_ref[...] = _dot(h, ws_ref[...])

    out_ref[...] = _dot(h_scr[...], w_ref[...]).astype(out_ref.dtype)


def _norm_proj(x, g, w, w_side=None, *, tm, tn, out_dtype):
    n, d = x.shape
    ncol = w.shape[1]
    with_side = w_side is not None
    in_specs = [pl.BlockSpec((tm, d), lambda i, j: (i, 0)),
                pl.BlockSpec((1, d), lambda i, j: (0, 0)),
                pl.BlockSpec((d, tn), lambda i, j: (0, j))]
    out_specs = [pl.BlockSpec((tm, tn), lambda i, j: (i, j))]
    out_shape = [jax.ShapeDtypeStruct((n, ncol), out_dtype)]
    args = [x, g.reshape(1, d), w]
    if with_side:
        in_specs.append(pl.BlockSpec((d, LANES), lambda i, j: (0, 0)))
        out_specs.append(pl.BlockSpec((tm, LANES), lambda i, j: (i, 0)))
        out_shape.append(jax.ShapeDtypeStruct((n, LANES), F32))
        args.append(w_side)
    res = pl.pallas_call(
        functools.partial(_norm_proj_kernel, with_side=with_side),
        grid=(n // tm, ncol // tn),
        in_specs=in_specs, out_specs=out_specs, out_shape=out_shape,
        scratch_shapes=[pltpu.VMEM((tm, d), BF16)],
        compiler_params=_cparams(("parallel", "arbitrary")),
        name="norm_proj_side" if with_side else "norm_proj",
    )(*args)
    return res if with_side else res[0]


def _unit_lower_inverse(low, row, col):
    c = low.shape[0]
    lg = 4
    eye = jnp.where(row == col, 1.0, 0.0)
    ld = jnp.where((row >> lg) == (col >> lg), low, 0.0)
    x = eye - ld
    p = ld
    for _ in range(3):
        pb = p.astype(BF16)
        p = _dot(pb, pb)
        x = x + _dot(x.astype(BF16), p.astype(BF16))
    while (1 << lg) < c:
        pair = (row >> (lg + 1)) == (col >> (lg + 1))
        m = jnp.where(pair, (row >> lg) - (col >> lg), 0) > 0
        lm = jnp.where(m, low, 0.0).astype(BF16)
        xb = x.astype(BF16)
        y = _dot(lm, xb)
        x = x - _dot(xb, y.astype(BF16))
        lg += 1
    return x


def _gdn_prompt_kernel(q_ref, k_ref, v_ref, z_ref, ab_ref, cwq_ref, cwk_ref, cwv_ref, gp_ref,
                       gn_ref, o_ref, s_ref, qs, ks, vs, gcs, bts, s_scr):
    h = pl.program_id(1)
    t_len = q_ref.shape[1]
    c = GDN_CHUNK
    n_c = t_len // c

    def conv_silu(x_ref, cw_ref):
        x = x_ref[0].astype(F32)
        cw = cw_ref[...]
        row = lax.broadcasted_iota(jnp.int32, x.shape, 0)
        y = x * cw[CONV_W - 1:CONV_W]
        for s in range(1, CONV_W):
            xs = jnp.where(row >= s, pltpu.roll(x, s, axis=0), 0.0)
            y = y + xs * cw[CONV_W - 1 - s:CONV_W - s]
        return _silu(y)

    def l2n(x):
        return x * lax.rsqrt(jnp.sum(x * x, axis=-1, keepdims=True) + EPS)

    qs[...] = l2n(conv_silu(q_ref, cwq_ref)) * (GDN_DK ** -0.5)
    ks[...] = l2n(conv_silu(k_ref, cwk_ref))
    vs[...] = conv_silu(v_ref, cwv_ref)

    ab = ab_ref[0]
    lane = lax.broadcasted_iota(jnp.int32, ab.shape, 1)
    alpha = jnp.sum(jnp.where(lane == h, ab, 0.0), axis=-1, keepdims=True)
    braw = jnp.sum(jnp.where(lane == h + GDN_HEADS, ab, 0.0), axis=-1, keepdims=True)
    gp = gp_ref[...]
    lane8 = lax.broadcasted_iota(jnp.int32, (1, LANES), 1)
    a_log = jnp.sum(jnp.where(lane8 == h, gp[0:1], 0.0), axis=-1, keepdims=True)
    dt_b = jnp.sum(jnp.where(lane8 == h, gp[1:2], 0.0), axis=-1, keepdims=True)
    g = -jnp.exp(a_log) * _softplus(alpha + dt_b)
    rowc = lax.broadcasted_iota(jnp.int32, g.shape, 0) & (c - 1)
    gc = g
    s = 1
    while s < c:
        gc = gc + jnp.where(rowc >= s, pltpu.roll(gc, s, axis=0), 0.0)
        s *= 2
    gcs[...] = gc
    bts[...] = jax.nn.sigmoid(braw)

    row = lax.broadcasted_iota(jnp.int32, (c, c), 0)
    col = lax.broadcasted_iota(jnp.int32, (c, c), 1)
    sel0 = (col == 0).astype(BF16)
    gn = gn_ref[...]
    s_scr[...] = jnp.zeros_like(s_scr)

    def chunk(ci, carry):
        r0 = pl.multiple_of(ci * c, c)
        q = qs[pl.ds(r0, c), :]
        k = ks[pl.ds(r0, c), :]
        v = vs[pl.ds(r0, c), :]
        gcc = gcs[pl.ds(r0, c), :]
        beta = bts[pl.ds(r0, c), :]
        xg = jnp.broadcast_to(gcc, (c, c))
        hi = xg.astype(BF16)
        r1 = xg - hi.astype(F32)
        mid = r1.astype(BF16)
        lo = (r1 - mid.astype(F32)).astype(BF16)
        grow = _dot_nt(sel0, hi) + _dot_nt(sel0, mid) + _dot_nt(sel0, lo)
        diff = gcc - grow
        d_incl = jnp.where(row >= col, jnp.exp(jnp.where(row >= col, diff, 0.0)), 0.0)
        d_strict = jnp.where(row > col, d_incl, 0.0)
        egc = jnp.exp(gcc)
        g_last = gcc[c - 1:c, :]
        kb = k * beta
        kbf = k.astype(BF16)
        low = _dot_nt(kb.astype(BF16), kbf) * d_strict
        a_in = _dot_nt(q.astype(BF16), kbf) * d_incl
        t_inv = _unit_lower_inverse(low, row, col).astype(BF16)
        u = _dot(t_inv, (v * beta).astype(BF16))
        w = _dot(t_inv, (kb * egc).astype(BF16))
        s_old = s_scr[...]
        sb = s_old.astype(BF16)
        v_new = u - _dot(w.astype(BF16), sb)
        vnb = v_new.astype(BF16)
        o = _dot((q * egc).astype(BF16), sb) + _dot(a_in.astype(BF16), vnb)
        k_dec = k * jnp.exp(g_last - gcc)
        s_scr[...] = s_old * jnp.exp(g_last) + _dot_tn(k_dec.astype(BF16), vnb)
        zz = z_ref[0, pl.ds(r0, c), :].astype(F32)
        o_ref[0, pl.ds(r0, c), :] = (_rms(o, gn) * _silu(zz)).astype(o_ref.dtype)
        return carry

    lax.fori_loop(0, n_c, chunk, 0)
    s_ref[0, 0] = s_scr[...]


def _gdn_prompt(proj3, ab3, conv_w, gate_par, gdn_norm):
    bsz, t_len, _ = proj3.shape
    nh = GDN_HEADS
    blk = lambda off: pl.BlockSpec((1, t_len, LANES), lambda b, h: (b, 0, off + h))
    cw = lambda off: pl.BlockSpec((CONV_W, LANES), lambda b, h: (0, off + h))
    return pl.pallas_call(
        _gdn_prompt_kernel,
        grid=(bsz, nh),
        in_specs=[blk(0), blk(nh), blk(2 * nh), blk(3 * nh),
                  pl.BlockSpec((1, t_len, LANES), lambda b, h: (b, 0, 0)),
                  cw(0), cw(nh), cw(2 * nh),
                  pl.BlockSpec((SUBLANES, LANES), lambda b, h: (0, 0)),
                  pl.BlockSpec((1, LANES), lambda b, h: (0, 0))],
        out_specs=[pl.BlockSpec((1, t_len, LANES), lambda b, h: (b, 0, h)),
                   pl.BlockSpec((1, 1, GDN_DK, GDN_DV), lambda b, h: (b, h, 0, 0))],
        out_shape=[jax.ShapeDtypeStruct((bsz, t_len, MIX_A), BF16),
                   jax.ShapeDtypeStruct((bsz, nh, GDN_DK, GDN_DV), F32)],
        scratch_shapes=[pltpu.VMEM((t_len, LANES), F32), pltpu.VMEM((t_len, LANES), F32),
                        pltpu.VMEM((t_len, LANES), F32), pltpu.VMEM((t_len, 1), F32),
                        pltpu.VMEM((t_len, 1), F32), pltpu.VMEM((GDN_DK, GDN_DV), F32)],
        compiler_params=_cparams(("parallel", "arbitrary")),
        name="gdn_prompt",
    )(proj3, proj3, proj3, proj3, ab3, conv_w, conv_w, conv_w, gate_par, gdn_norm.reshape(1, LANES))


def _gdn_sample_kernel(x_ref, ab_ref, hist_ref, cw_ref, gp_ref, gn_ref, s0_ref, o_ref, s_ref):
    t_len = x_ref.shape[1]
    nh = GDN_HEADS

    def ext_row(e, sec):
        sl = slice(sec * QK_W, (sec + 1) * QK_W)
        if e < CONV_W - 1:
            return hist_ref[0, e:e + 1, sl].astype(F32)
        return x_ref[0, e - (CONV_W - 1):e - (CONV_W - 2), sl].astype(F32)

    def conv_rows(sec):
        sl = slice(sec * QK_W, (sec + 1) * QK_W)
        rows = []
        for t in range(t_len):
            y = ext_row(t, sec) * cw_ref[0:1, sl]
            for j in range(1, CONV_W):
                y = y + ext_row(t + j, sec) * cw_ref[j:j + 1, sl]
            rows.append(_silu(y))
        return rows

    qr, kr, vr = conv_rows(0), conv_rows(1), conv_rows(2)
    gp = gp_ref[...]
    gn = gn_ref[...]
    rid = lax.broadcasted_iota(jnp.int32, (GDN_DK, GDN_DV), 0)
    cid = lax.broadcasted_iota(jnp.int32, (GDN_DK, GDN_DV), 1)
    diag = rid == cid

    def to_col(r):
        return jnp.sum(jnp.where(diag, jnp.broadcast_to(r, (GDN_DK, GDN_DV)), 0.0),
                       axis=-1, keepdims=True)

    def l2n(x):
        return x * lax.rsqrt(jnp.sum(x * x, axis=-1, keepdims=True) + EPS)

    for h in range(nh):
        hs = slice(h * GDN_DK, (h + 1) * GDN_DK)
        a_neg = -jnp.exp(gp[0:1, h:h + 1])
        dt_b = gp[1:2, h:h + 1]
        s_mat = s0_ref[0, h]
        for t in range(t_len):
            q = l2n(qr[t][:, hs]) * (GDN_DK ** -0.5)
            k = l2n(kr[t][:, hs])
            v = vr[t][:, hs]
            ab_t = ab_ref[0, t:t + 1, :]
            g = a_neg * _softplus(ab_t[:, h:h + 1] + dt_b)
            beta = jax.nn.sigmoid(ab_t[:, nh + h:nh + h + 1])
            k_col = to_col(k)
            ks_row = jnp.sum(k_col * s_mat, axis=0, keepdims=True)
            s_mat = jnp.exp(g) * (s_mat - (beta * k_col) * ks_row) + (beta * k_col) * v
            o = jnp.sum(to_col(q) * s_mat, axis=0, keepdims=True)
            z = x_ref[0, t:t + 1, 3 * QK_W + h * GDN_DV:3 * QK_W + (h + 1) * GDN_DV].astype(F32)
            o_ref[0, t:t + 1, hs] = (_rms(o, gn) * _silu(z)).astype(o_ref.dtype)
        s_ref[0, h] = s_mat


def _gdn_sample(proj3, ab3, hist, conv_w, gate_par, gdn_norm, s0):
    bsz, t_len, _ = proj3.shape
    nh = GDN_HEADS
    return pl.pallas_call(
        _gdn_sample_kernel,
        grid=(bsz,),
        in_specs=[pl.BlockSpec((1, t_len, 4 * QK_W), lambda b: (b, 0, 0)),
                  pl.BlockSpec((1, t_len, LANES), lambda b: (b, 0, 0)),
                  pl.BlockSpec((1, CONV_W - 1, QKV_W), lambda b: (b, 0, 0)),
                  pl.BlockSpec((CONV_W, QKV_W), lambda b: (0, 0)),
                  pl.BlockSpec((SUBLANES, LANES), lambda b: (0, 0)),
                  pl.BlockSpec((1, LANES), lambda b: (0, 0)),
                  pl.BlockSpec((1, nh, GDN_DK, GDN_DV), lambda b: (b, 0, 0, 0))],
        out_specs=[pl.BlockSpec((1, t_len, MIX_A), lambda b: (b, 0, 0)),
                   pl.BlockSpec((1, nh, GDN_DK, GDN_DV), lambda b: (b, 0, 0, 0))],
        out_shape=[jax.ShapeDtypeStruct((bsz, t_len, MIX_A), BF16),
                   jax.ShapeDtypeStruct((bsz, nh, GDN_DK, GDN_DV), F32)],
        compiler_params=_cparams(("parallel",)),
        name="gdn_sample",
    )(proj3, ab3, hist, conv_w, gate_par, gdn_norm.reshape(1, LANES), s0)


def _pool_finish(d_groups, pw_ref, ps_ref, zb):
    outs = [_dot(d.astype(BF16), pw_ref[gi]) for gi, d in enumerate(d_groups)]
    return jnp.concatenate(outs, axis=-1) * ps_ref[...] * _silu(zb)


def _pool_prompt_kernel(u_ref, halo_ref, zb_ref, pw_ref, ps_ref, o_ref, *, pos0):
    i = pl.program_id(1)
    tt = u_ref.shape[1]
    u = u_ref[0].astype(F32)
    halo = jnp.where(i > 0, halo_ref[0].astype(F32), 0.0)
    ext = jnp.concatenate([halo, u], axis=0)
    pos = pos0 + i * tt + lax.broadcasted_iota(jnp.int32, (tt, 1), 0)
    d_groups = []
    for gi, win in enumerate(POOL_WINDOWS):
        sl = slice(gi * POOL_CH, (gi + 1) * POOL_CH)
        lvl = ext[:, sl]
        s = 1
        while s < win:
            lvl = lvl + pltpu.roll(lvl, s, axis=0)
            s *= 2
        cnt = jnp.minimum(win, pos + 1).astype(F32)
        d_groups.append(lvl[POOL_HALO:, :] / cnt - u[:, sl])
    o_ref[0] = _pool_finish(d_groups, pw_ref, ps_ref, zb_ref[0].astype(F32)).astype(o_ref.dtype)


def _pool_prompt(proj3, pool_w, pool_scale, *, tt, pos0):
    bsz, t_len, _ = proj3.shape
    ub, zb = COL_ZA // MIX_B + 1, COL_ZA // MIX_B + 2
    hb = tt // POOL_HALO
    return pl.pallas_call(
        functools.partial(_pool_prompt_kernel, pos0=pos0),
        grid=(bsz, t_len // tt),
        in_specs=[pl.BlockSpec((1, tt, MIX_B), lambda b, i: (b, i, ub)),
                  pl.BlockSpec((1, POOL_HALO, MIX_B), lambda b, i: (b, jnp.maximum(i * hb - 1, 0), ub)),
                  pl.BlockSpec((1, tt, MIX_B), lambda b, i: (b, i, zb)),
                  pl.BlockSpec((POOL_GROUPS, POOL_CH, POOL_CH), lambda b, i: (0, 0, 0)),
                  pl.BlockSpec((1, MIX_B), lambda b, i: (0, 0))],
        out_specs=pl.BlockSpec((1, tt, MIX_B), lambda b, i: (b, i, 0)),
        out_shape=jax.ShapeDtypeStruct((bsz, t_len, MIX_B), BF16),
        compiler_params=_cparams(("parallel", "arbitrary")),
        name="pool_prompt",
    )(proj3, proj3, proj3, pool_w, pool_scale.reshape(1, MIX_B))


def _pool_sample_kernel(ext_ref, zb_ref, pw_ref, ps_ref, o_ref, *, pos0):
    t_len = zb_ref.shape[0]
    for t in range(t_len):
        e = POOL_BUF + t
        d_groups = []
        for gi, win in enumerate(POOL_WINDOWS):
            sl = slice(gi * POOL_CH, (gi + 1) * POOL_CH)
            tot = ext_ref[e, :, sl]
            for j in range(1, win):
                tot = tot + ext_ref[e - j, :, sl]
            cnt = float(min(win, pos0 + t + 1))
            d_groups.append(tot / cnt - ext_ref[e, :, sl])
        o_ref[t] = _pool_finish(d_groups, pw_ref, ps_ref, zb_ref[t].astype(F32)).astype(o_ref.dtype)


def _pool_sample(ext_tm, zb_tm, pool_w, pool_scale, *, bb, pos0):
    t_len, bsz, _ = zb_tm.shape
    return pl.pallas_call(
        functools.partial(_pool_sample_kernel, pos0=pos0),
        grid=(bsz // bb,),
        in_specs=[pl.BlockSpec((POOL_BUF + t_len, bb, MIX_B), lambda i: (0, i, 0)),
                  pl.BlockSpec((t_len, bb, MIX_B), lambda i: (0, i, 0)),
                  pl.BlockSpec((POOL_GROUPS, POOL_CH, POOL_CH), lambda i: (0, 0, 0)),
                  pl.BlockSpec((1, MIX_B), lambda i: (0, 0))],
        out_specs=pl.BlockSpec((t_len, bb, MIX_B), lambda i: (0, i, 0)),
        out_shape=jax.ShapeDtypeStruct((t_len, bsz, MIX_B), BF16),
        compiler_params=_cparams(("parallel",)),
        name="pool_sample",
    )(ext_tm, zb_tm, pool_w, pool_scale.reshape(1, MIX_B))


def _mix_out_kernel(x_ref, oa_ref, ob_ref, wo_ref, nc_ref, wq_ref, x1_ref, qx_ref):
    acc = _dot(oa_ref[...], wo_ref[0:MIX_A, :]) + _dot(ob_ref[...], wo_ref[MIX_A:D_MODEL, :])
    x1 = x_ref[...] + acc
    x1_ref[...] = x1
    qx_ref[...] = _dot(_rms(x1, nc_ref[...]).astype(BF16), wq_ref[...]).astype(qx_ref.dtype)


def _mix_out(x, oa, ob, w_out, norm_cross, w_cq, *, tm):
    n, d = x.shape
    const = lambda i: (0, 0)
    return pl.pallas_call(
        _mix_out_kernel,
        grid=(n // tm,),
        in_specs=[pl.BlockSpec((tm, d), lambda i: (i, 0)),
                  pl.BlockSpec((tm, MIX_A), lambda i: (i, 0)),
                  pl.BlockSpec((tm, MIX_B), lambda i: (i, 0)),
                  pl.BlockSpec((d, d), const), pl.BlockSpec((1, d), const), pl.BlockSpec((d, d), const)],
        out_specs=[pl.BlockSpec((tm, d), lambda i: (i, 0)), pl.BlockSpec((tm, d), lambda i: (i, 0))],
        out_shape=[jax.ShapeDtypeStruct((n, d), F32), jax.ShapeDtypeStruct((n, d), BF16)],
        compiler_params=_cparams(("parallel",)),
        name="mix_out",
    )(x, oa, ob, w_out, norm_cross.reshape(1, d), w_cq)


def _xattn_kernel(q_ref, k_ref, v_ref, ctx_ref):
    q = q_ref[0]
    scale = X_HEAD_DIM ** -0.5
    for h in range(X_HEADS):
        sl = slice(h * X_HEAD_DIM, (h + 1) * X_HEAD_DIM)
        s = _dot_nt(q[:, sl], k_ref[0, :, sl].astype(BF16)) * scale
        p = jnp.exp(s - jnp.max(s, axis=-1, keepdims=True))
        p = p / jnp.sum(p, axis=-1, keepdims=True)
        ctx_ref[0, :, sl] = _dot(p.astype(BF16), v_ref[0, :, sl].astype(BF16)).astype(ctx_ref.dtype)


def _xattn(qx3, mk3, mv3, *, tq):
    bsz, t_len, d = qx3.shape
    n_mem = mk3.shape[1]
    return pl.pallas_call(
        _xattn_kernel,
        grid=(bsz, t_len // tq),
        in_specs=[pl.BlockSpec((1, tq, d), lambda b, i: (b, i, 0)),
                  pl.BlockSpec((1, n_mem, d), lambda b, i: (b, 0, 0)),
                  pl.BlockSpec((1, n_mem, d), lambda b, i: (b, 0, 0))],
        out_specs=pl.BlockSpec((1, tq, d), lambda b, i: (b, i, 0)),
        out_shape=jax.ShapeDtypeStruct((bsz, t_len, d), BF16),
        compiler_params=_cparams(("parallel", "arbitrary")),
        name="xattn",
    )(qx3, mk3, mv3)


def _attn_out_kernel(x1_ref, ctx_ref, wco_ref, nf_ref, y_ref):
    x2 = x1_ref[...] + _dot(ctx_ref[...], wco_ref[...])
    y_ref[...] = _rms(x2, nf_ref[...])


def _attn_out(x1, ctx, w_co, norm_final, *, tm):
    n, d = x1.shape
    const = lambda i: (0, 0)
    return pl.pallas_call(
        _attn_out_kernel,
        grid=(n // tm,),
        in_specs=[pl.BlockSpec((tm, d), lambda i: (i, 0)), pl.BlockSpec((tm, d), lambda i: (i, 0)),
                  pl.BlockSpec((d, d), const), pl.BlockSpec((1, d), const)],
        out_specs=pl.BlockSpec((tm, d), lambda i: (i, 0)),
        out_shape=jax.ShapeDtypeStruct((n, d), F32),
        compiler_params=_cparams(("parallel",)),
        name="attn_out",
    )(x1, ctx, w_co, norm_final.reshape(1, d))


def _layer_tail(x2d, oa, ob, mk3, mv3, bsz, t_len, w_out, norm_cross, w_cq, w_co, norm_final,
                *, tm, tq):
    x1, qx = _mix_out(x2d, oa.reshape(-1, MIX_A), ob.reshape(-1, MIX_B), w_out, norm_cross, w_cq, tm=tm)
    ctx = _xattn(qx.reshape(bsz, t_len, D_MODEL), mk3, mv3, tq=tq)
    y = _attn_out(x1, ctx.reshape(-1, D_MODEL), w_co, norm_final, tm=tm)
    return y.reshape(bsz, t_len, D_MODEL)


def kernel(x_prompt, x_sample, mem_prompt, cache_mem_k, cache_mem_v, state_delta, state_conv,
           state_pool, norm_mix, w_in, conv_w, a_log, dt_bias, gdn_norm, pool_w, pool_scale,
           w_out, norm_mem, norm_cross, w_cq, w_ck, w_cv, w_co, norm_final):
    bp, tp, d = x_prompt.shape
    bs, ts, _ = x_sample.shape
    n_mem = mem_prompt.shape[1]

    wi = w_in[0]
    w_main = jnp.concatenate([wi[:, :COL_A], wi[:, COL_U:]], axis=1).astype(BF16)
    w_gate = jnp.pad(wi[:, COL_A:COL_U], ((0, 0), (0, LANES - 2 * GDN_HEADS))).astype(BF16)
    wo, wcq, wck, wcv, wco = (w[0].astype(BF16) for w in (w_out, w_cq, w_ck, w_cv, w_co))
    pw = pool_w[0].astype(BF16)
    gate_par = jnp.zeros((SUBLANES, LANES), F32)
    gate_par = gate_par.at[0, :GDN_HEADS].set(a_log[0]).at[1, :GDN_HEADS].set(dt_bias[0])
    cw = conv_w[0]

    mem2d = mem_prompt.reshape(bp * n_mem, d)
    mk = _norm_proj(mem2d, norm_mem[0], wck, tm=512, tn=1024, out_dtype=F32)
    mv = _norm_proj(mem2d, norm_mem[0], wcv, tm=512, tn=1024, out_dtype=F32)

    proj_p, ab_p = _norm_proj(x_prompt.reshape(bp * tp, d), norm_mix[0], w_main, w_gate,
                              tm=512, tn=1024, out_dtype=BF16)
    proj_p3 = proj_p.reshape(bp, tp, W_MAIN)
    oa_p, delta_p = _gdn_prompt(proj_p3, ab_p.reshape(bp, tp, LANES), cw, gate_par, gdn_norm[0])
    ob_p = _pool_prompt(proj_p3, pw, pool_scale[0], tt=512, pos0=0)
    y_p = _layer_tail(x_prompt.reshape(bp * tp, d), oa_p, ob_p, mk.reshape(bp, n_mem, d),
                      mv.reshape(bp, n_mem, d), bp, tp, wo, norm_cross[0], wcq, wco, norm_final,
                      tm=256, tq=512)
    conv_p = proj_p3[:, tp - (CONV_W - 1):, :QKV_W].astype(F32)
    pool_p = proj_p3[:, tp - POOL_BUF:, COL_ZA + MIX_A:COL_ZA + MIX_A + MIX_B].astype(F32)

    proj_s, ab_s = _norm_proj(x_sample.reshape(bs * ts, d), norm_mix[0], w_main, w_gate,
                              tm=256, tn=1024, out_dtype=BF16)
    proj_s3 = proj_s.reshape(bs, ts, W_MAIN)
    oa_s, delta_s = _gdn_sample(proj_s3, ab_s.reshape(bs, ts, LANES), state_conv[0], cw, gate_par,
                                gdn_norm[0], state_delta[0])
    u_s = proj_s3[:, :, COL_ZA + MIX_A:COL_ZA + MIX_A + MIX_B].astype(F32)
    ext_s = jnp.concatenate([state_pool[0], u_s], axis=1)
    zb_tm = proj_s3[:, :, COL_ZA + MIX_A + MIX_B:].transpose(1, 0, 2)
    ob_s = _pool_sample(ext_s.transpose(1, 0, 2), zb_tm, pw, pool_scale[0], bb=32,
                        pos0=PAST_LEN).transpose(1, 0, 2)
    y_s = _layer_tail(x_sample.reshape(bs * ts, d), oa_s, ob_s, cache_mem_k[0].reshape(bs, n_mem, d),
                      cache_mem_v[0].reshape(bs, n_mem, d), bs, ts, wo, norm_cross[0], wcq, wco,
                      norm_final, tm=256, tq=ts)
    conv_s = jnp.concatenate([state_conv[0], proj_s3[:, :, :QKV_W].astype(F32)], axis=1)[:, ts:]
    pool_s = ext_s[:, ts:]

    hd = (X_HEADS, X_HEAD_DIM)
    return (y_p, y_s, mk.reshape(1, bp, n_mem, *hd), mv.reshape(1, bp, n_mem, *hd),
            delta_p[None], conv_p[None], pool_p[None], delta_s[None], conv_s[None], pool_s[None])
```

```python
import functools
import math

import jax
import jax.numpy as jnp
from jax import lax
from jax.experimental import pallas as pl
from jax.experimental.pallas import tpu as pltpu

F32 = jnp.float32
BF16 = jnp.bfloat16

D_MODEL = 2048
MIX_A = D_MODEL // 2
MIX_B = D_MODEL - MIX_A
GDN_HEADS = 8
GDN_DK = MIX_A // GDN_HEADS
GDN_DV = MIX_A // GDN_HEADS
QK_W = GDN_HEADS * GDN_DK
QKV_W = 2 * QK_W + GDN_HEADS * GDN_DV
CONV_W = 4
POOL_WINDOWS = (2, 4, 8, 16)
POOL_GROUPS = len(POOL_WINDOWS)
POOL_CH = MIX_B // POOL_GROUPS
POOL_BUF = max(POOL_WINDOWS) - 1
X_HEADS = 4
X_HEAD_DIM = D_MODEL // X_HEADS
PAST_LEN = 16384
EPS = 1e-6
COL_ZA = QKV_W
COL_A = COL_ZA + GDN_HEADS * GDN_DV
COL_B = COL_A + GDN_HEADS
COL_U = COL_B + GDN_HEADS
COL_ZB = COL_U + MIX_B
IN_COLS = COL_ZB + MIX_B

W_MAIN = IN_COLS - 2 * GDN_HEADS
LANES = 128
SUBLANES = 8
GDN_CHUNK = 128
POOL_HALO = 16
VMEM_LIMIT = 56 * 1024 * 1024


def _cparams(sem):
    return pltpu.CompilerParams(dimension_semantics=sem, vmem_limit_bytes=VMEM_LIMIT)


def _dot(a, b):
    return jnp.dot(a, b, preferred_element_type=F32)


def _dot_nt(a, b):
    return lax.dot_general(a, b, (((1,), (1,)), ((), ())), preferred_element_type=F32)


def _dot_tn(a, b):
    return lax.dot_general(a, b, (((0,), (0,)), ((), ())), preferred_element_type=F32)


def _rms(x, g):
    return x * lax.rsqrt(jnp.mean(x * x, axis=-1, keepdims=True) + EPS) * g


def _silu(x):
    return x * jax.nn.sigmoid(x)


def _softplus(x):
    return jnp.maximum(x, 0.0) + jnp.log1p(jnp.exp(-jnp.abs(x)))


def _norm_proj_kernel(x_ref, g_ref, w_ref, *rest, with_side):
    if with_side:
        ws_ref, out_ref, side_ref, h_scr = rest
    else:
        out_ref, h_scr = rest

    @pl.when(pl.program_id(1) == 0)
    def _():
        h = _rms(x_ref[...], g_ref[...]).astype(BF16)
        h_scr[...] = h
        if with_side:
            side_ref[...] = _dot(h, ws_ref[...])

    out_ref[...] = _dot(h_scr[...], w_ref[...]).astype(out_ref.dtype)


def _norm_proj(x, g, w, w_side=None, *, tm, tn, out_dtype):
    n, d = x.shape
    ncol = w.shape[1]
    with_side = w_side is not None
    in_specs = [pl.BlockSpec((tm, d), lambda i, j: (i, 0)),
                pl.BlockSpec((1, d), lambda i, j: (0, 0)),
                pl.BlockSpec((d, tn), lambda i, j: (0, j))]
    out_specs = [pl.BlockSpec((tm, tn), lambda i, j: (i, j))]
    out_shape = [jax.ShapeDtypeStruct((n, ncol), out_dtype)]
    args = [x, g.reshape(1, d), w]
    if with_side:
        in_specs.append(pl.BlockSpec((d, LANES), lambda i, j: (0, 0)))
        out_specs.append(pl.BlockSpec((tm, LANES), lambda i, j: (i, 0)))
        out_shape.append(jax.ShapeDtypeStruct((n, LANES), F32))
        args.append(w_side)
    res = pl.pallas_call(
        functools.partial(_norm_proj_kernel, with_side=with_side),
        grid=(n // tm, ncol // tn),
        in_specs=in_specs, out_specs=out_specs, out_shape=out_shape,
        scratch_shapes=[pltpu.VMEM((tm, d), BF16)],
        compiler_params=_cparams(("parallel", "arbitrary")),
        name="norm_proj_side" if with_side else "norm_proj",
    )(*args)
    return res if with_side else res[0]


def _unit_lower_inverse(low, row, col):
    c = low.shape[0]
    lg = 4
    eye = jnp.where(row == col, 1.0, 0.0)
    ld = jnp.where((row >> lg) == (col >> lg), low, 0.0)
    x = eye - ld
    p = ld
    for _ in range(3):
        pb = p.astype(BF16)
        p = _dot(pb, pb)
        x = x + _dot(x.astype(BF16), p.astype(BF16))
    while (1 << lg) < c:
        pair = (row >> (lg + 1)) == (col >> (lg + 1))
        m = jnp.where(pair, (row >> lg) - (col >> lg), 0) > 0
        lm = jnp.where(m, low, 0.0).astype(BF16)
        xb = x.astype(BF16)
        y = _dot(lm, xb)
        x = x - _dot(xb, y.astype(BF16))
        lg += 1
    return x


def _gdn_prompt_kernel(q_ref, k_ref, v_ref, z_ref, ab_ref, cwq_ref, cwk_ref, cwv_ref, gp_ref,
                       gn_ref, o_ref, s_ref, qs, ks, vs, gcs, bts, s_scr):
    h = pl.program_id(1)
    t_len = q_ref.shape[1]
    c = GDN_CHUNK
    n_c = t_len // c

    def conv_silu(x_ref, cw_ref):
        x = x_ref[0].astype(F32)
        cw = cw_ref[...]
        row = lax.broadcasted_iota(jnp.int32, x.shape, 0)
        y = x * cw[CONV_W - 1:CONV_W]
        for s in range(1, CONV_W):
            xs = jnp.where(row >= s, pltpu.roll(x, s, axis=0), 0.0)
            y = y + xs * cw[CONV_W - 1 - s:CONV_W - s]
        return _silu(y)

    def l2n(x):
        return x * lax.rsqrt(jnp.sum(x * x, axis=-1, keepdims=True) + EPS)

    qs[...] = l2n(conv_silu(q_ref, cwq_ref)) * (GDN_DK ** -0.5)
    ks[...] = l2n(conv_silu(k_ref, cwk_ref))
    vs[...] = conv_silu(v_ref, cwv_ref)

    ab = ab_ref[0]
    lane = lax.broadcasted_iota(jnp.int32, ab.shape, 1)
    alpha = jnp.sum(jnp.where(lane == h, ab, 0.0), axis=-1, keepdims=True)
    braw = jnp.sum(jnp.where(lane == h + GDN_HEADS, ab, 0.0), axis=-1, keepdims=True)
    gp = gp_ref[...]
    lane8 = lax.broadcasted_iota(jnp.int32, (1, LANES), 1)
    a_log = jnp.sum(jnp.where(lane8 == h, gp[0:1], 0.0), axis=-1, keepdims=True)
    dt_b = jnp.sum(jnp.where(lane8 == h, gp[1:2], 0.0), axis=-1, keepdims=True)
    g = -jnp.exp(a_log) * _softplus(alpha + dt_b)
    rowc = lax.broadcasted_iota(jnp.int32, g.shape, 0) & (c - 1)
    gc = g
    s = 1
    while s < c:
        gc = gc + jnp.where(rowc >= s, pltpu.roll(gc, s, axis=0), 0.0)
        s *= 2
    gcs[...] = gc
    bts[...] = jax.nn.sigmoid(braw)

    row = lax.broadcasted_iota(jnp.int32, (c, c), 0)
    col = lax.broadcasted_iota(jnp.int32, (c, c), 1)
    sel0 = (col == 0).astype(BF16)
    gn = gn_ref[...]
    s_scr[...] = jnp.zeros_like(s_scr)

    def chunk(ci, carry):
        r0 = pl.multiple_of(ci * c, c)
        q = qs[pl.ds(r0, c), :]
        k = ks[pl.ds(r0, c), :]
        v = vs[pl.ds(r0, c), :]
        gcc = gcs[pl.ds(r0, c), :]
        beta = bts[pl.ds(r0, c), :]
        xg = jnp.broadcast_to(gcc, (c, c))
        hi = xg.astype(BF16)
        r1 = xg - hi.astype(F32)
        mid = r1.astype(BF16)
        lo = (r1 - mid.astype(F32)).astype(BF16)
        grow = _dot_nt(sel0, hi) + _dot_nt(sel0, mid) + _dot_nt(sel0, lo)
        diff = gcc - grow
        d_incl = jnp.where(row >= col, jnp.exp(jnp.where(row >= col, diff, 0.0)), 0.0)
        d_strict = jnp.where(row > col, d_incl, 0.0)
        egc = jnp.exp(gcc)
        g_last = gcc[c - 1:c, :]
        kb = k * beta
        kbf = k.astype(BF16)
        low = _dot_nt(kb.astype(BF16), kbf) * d_strict
        a_in = _dot_nt(q.astype(BF16), kbf) * d_incl
        t_inv = _unit_lower_inverse(low, row, col).astype(BF16)
        u = _dot(t_inv, (v * beta).astype(BF16))
        w = _dot(t_inv, (kb * egc).astype(BF16))
        s_old = s_scr[...]
        sb = s_old.astype(BF16)
        v_new = u - _dot(w.astype(BF16), sb)
        vnb = v_new.astype(BF16)
        o = _dot((q * egc).astype(BF16), sb) + _dot(a_in.astype(BF16), vnb)
        k_dec = k * jnp.exp(g_last - gcc)
        s_scr[...] = s_old * jnp.exp(g_last) + _dot_tn(k_dec.astype(BF16), vnb)
        zz = z_ref[0, pl.ds(r0, c), :].astype(F32)
        o_ref[0, pl.ds(r0, c), :] = (_rms(o, gn) * _silu(zz)).astype(o_ref.dtype)
        return carry

    lax.fori_loop(0, n_c, chunk, 0)
    s_ref[0, 0] = s_scr[...]


def _gdn_prompt(proj3, ab3, conv_w, gate_par, gdn_norm):
    bsz, t_len, _ = proj3.shape
    nh = GDN_HEADS
    blk = lambda off: pl.BlockSpec((1, t_len, LANES), lambda b, h: (b, 0, off + h))
    cw = lambda off: pl.BlockSpec((CONV_W, LANES), lambda b, h: (0, off + h))
    return pl.pallas_call(
        _gdn_prompt_kernel,
        grid=(bsz, nh),
        in_specs=[blk(0), blk(nh), blk(2 * nh), blk(3 * nh),
                  pl.BlockSpec((1, t_len, LANES), lambda b, h: (b, 0, 0)),
                  cw(0), cw(nh), cw(2 * nh),
                  pl.BlockSpec((SUBLANES, LANES), lambda b, h: (0, 0)),
                  pl.BlockSpec((1, LANES), lambda b, h: (0, 0))],
        out_specs=[pl.BlockSpec((1, t_len, LANES), lambda b, h: (b, 0, h)),
                   pl.BlockSpec((1, 1, GDN_DK, GDN_DV), lambda b, h: (b, h, 0, 0))],
        out_shape=[jax.ShapeDtypeStruct((bsz, t_len, MIX_A), BF16),
                   jax.ShapeDtypeStruct((bsz, nh, GDN_DK, GDN_DV), F32)],
        scratch_shapes=[pltpu.VMEM((t_len, LANES), F32), pltpu.VMEM((t_len, LANES), F32),
                        pltpu.VMEM((t_len, LANES), F32), pltpu.VMEM((t_len, 1), F32),
                        pltpu.VMEM((t_len, 1), F32), pltpu.VMEM((GDN_DK, GDN_DV), F32)],
        compiler_params=_cparams(("parallel", "arbitrary")),
        name="gdn_prompt",
    )(proj3, proj3, proj3, proj3, ab3, conv_w, conv_w, conv_w, gate_par, gdn_norm.reshape(1, LANES))


def _gdn_sample_kernel(x_ref, ab_ref, hist_ref, cw_ref, gp_ref, gn_ref, s0_ref, o_ref, s_ref,
                       *, t_len):
    nh = GDN_HEADS
    g8 = SUBLANES
    n = nh * g8
    assert t_len + CONV_W - 1 <= g8
    cw = cw_ref[...]
    gp = gp_ref[...]
    gn = gn_ref[...]
    row8 = lax.broadcasted_iota(jnp.int32, (g8, QKV_W), 0)
    tpos = lax.broadcasted_iota(jnp.int32, (n, 1), 0) & (g8 - 1)
    rhead = lax.broadcasted_iota(jnp.int32, (n, GDN_DV), 0) >> 3
    ri = lax.broadcasted_iota(jnp.int32, (n, n), 0)
    ci = lax.broadcasted_iota(jnp.int32, (n, n), 1)
    tri = jnp.where((ri >> 3) == (ci >> 3), ri - ci, -1)
    eye = jnp.where(ri == ci, 1.0, 0.0)
    sel0 = jnp.where(lax.broadcasted_iota(jnp.int32, (n, LANES), 1) == 0, 1.0, 0.0).astype(BF16)
    valid = tpos < t_len
    a_neg = jnp.concatenate([jnp.broadcast_to(-jnp.exp(gp[0:1, h:h + 1]), (g8, 1)) for h in range(nh)], 0)
    dt_b = jnp.concatenate([jnp.broadcast_to(gp[1:2, h:h + 1], (g8, 1)) for h in range(nh)], 0)

    def l2n(x):
        return x * lax.rsqrt(jnp.sum(x * x, axis=-1, keepdims=True) + EPS)

    for b in range(x_ref.shape[0]):
        xb = x_ref[b]
        x = xb[:, :QKV_W].astype(F32)
        hst = hist_ref[b]
        y = x * cw[CONV_W - 1:CONV_W]
        for s in range(1, CONV_W):
            y = y + pltpu.roll(x, s, axis=0) * cw[CONV_W - 1 - s:CONV_W - s]
        for j in range(CONV_W - 1):
            hj = hst if j == 0 else pltpu.roll(hst, g8 - j, axis=0)
            y = y + jnp.where(row8 + j < CONV_W - 1, hj, 0.0) * cw[j:j + 1]
        qkv = jnp.where(row8 < t_len, _silu(y), 0.0)

        def stack(off):
            return jnp.concatenate([qkv[:, off + h * GDN_DK:off + (h + 1) * GDN_DK] for h in range(nh)], 0)

        q = l2n(stack(0)) * (GDN_DK ** -0.5)
        k = l2n(stack(QK_W))
        v = stack(2 * QK_W)
        z = jnp.concatenate([xb[:, 3 * QK_W + h * GDN_DV:3 * QK_W + (h + 1) * GDN_DV].astype(F32)
                             for h in range(nh)], 0)
        ab = ab_ref[b]
        alpha = jnp.concatenate([ab[:, h:h + 1] for h in range(nh)], 0)
        braw = jnp.concatenate([ab[:, nh + h:nh + h + 1] for h in range(nh)], 0)
        g = jnp.where(valid, a_neg * _softplus(alpha + dt_b), 0.0)
        beta = jnp.where(valid, jax.nn.sigmoid(braw), 0.0)
        gc = g
        s = 1
        while s < g8:
            gc = gc + jnp.where(tpos >= s, pltpu.roll(gc, s, axis=0), 0.0)
            s *= 2
        g_last = jnp.concatenate([jnp.broadcast_to(gc[h * g8 + g8 - 1:(h + 1) * g8, :], (g8, 1))
                                  for h in range(nh)], 0)
        xg = jnp.broadcast_to(gc, (n, LANES))
        hi = xg.astype(BF16)
        r1 = xg - hi.astype(F32)
        mid = r1.astype(BF16)
        lo = (r1 - mid.astype(F32)).astype(BF16)
        grow = _dot_nt(sel0, hi) + _dot_nt(sel0, mid) + _dot_nt(sel0, lo)
        d_incl = jnp.where(tri >= 0, jnp.exp(jnp.where(tri >= 0, gc - grow, 0.0)), 0.0)
        d_strict = jnp.where(tri > 0, d_incl, 0.0)
        egc = jnp.exp(gc)
        kb = k * beta
        kbf = k.astype(BF16)
        low = _dot_nt(kb.astype(BF16), kbf) * d_strict
        a_in = _dot_nt(q.astype(BF16), kbf) * d_incl
        t_inv = eye - low
        p = low
        covered = 2
        while covered < t_len:
            pb = p.astype(BF16)
            p = _dot(pb, pb)
            t_inv = t_inv + _dot(t_inv.astype(BF16), p.astype(BF16))
            covered *= 2
        t_inv = t_inv.astype(BF16)
        u = _dot(t_inv, (v * beta).astype(BF16))
        w = _dot(t_inv, (kb * egc).astype(BF16))
        qg = q * egc
        k_dec = (k * jnp.exp(g_last - gc)).astype(BF16)
        ws, qs = [], []
        for h in range(nh):
            hs = slice(h * g8, (h + 1) * g8)
            lhs = jnp.concatenate([w[hs], qg[hs]], 0).astype(BF16)
            r = _dot(lhs, s0_ref[b, h].astype(BF16))
            ws.append(r[:g8])
            qs.append(r[g8:])
        v_new = u - jnp.concatenate(ws, 0)
        o = jnp.concatenate(qs, 0) + _dot(a_in.astype(BF16), v_new.astype(BF16))
        for h in range(nh):
            vm = jnp.where(rhead == h, v_new, 0.0).astype(BF16)
            dec = jnp.exp(g_last[h * g8:h * g8 + 1, :])
            s_ref[b, h] = s0_ref[b, h] * dec + _dot_tn(k_dec, vm)
        o_ref[b] = (_rms(o, gn) * _silu(z)).astype(o_ref.dtype)


def _gdn_sample(proj3, ab3, hist, conv_w, gate_par, gdn_norm, s0, *, bb):
    bsz, t_len, _ = proj3.shape
    nh, g8 = GDN_HEADS, SUBLANES
    pad_t = lambda a, rows: jnp.pad(a, ((0, 0), (0, g8 - rows), (0, 0)))
    x8 = pad_t(proj3[:, :, :4 * QK_W], t_len)
    ab8 = pad_t(ab3, t_len)
    hist8 = pad_t(hist, CONV_W - 1)
    o, s_new = pl.pallas_call(
        functools.partial(_gdn_sample_kernel, t_len=t_len),
        grid=(bsz // bb,),
        in_specs=[pl.BlockSpec((bb, g8, 4 * QK_W), lambda i: (i, 0, 0)),
                  pl.BlockSpec((bb, g8, LANES), lambda i: (i, 0, 0)),
                  pl.BlockSpec((bb, g8, QKV_W), lambda i: (i, 0, 0)),
                  pl.BlockSpec((CONV_W, QKV_W), lambda i: (0, 0)),
                  pl.BlockSpec((SUBLANES, LANES), lambda i: (0, 0)),
                  pl.BlockSpec((1, LANES), lambda i: (0, 0)),
                  pl.BlockSpec((bb, nh, GDN_DK, GDN_DV), lambda i: (i, 0, 0, 0))],
        out_specs=[pl.BlockSpec((bb, nh * g8, GDN_DV), lambda i: (i, 0, 0)),
                   pl.BlockSpec((bb, nh, GDN_DK, GDN_DV), lambda i: (i, 0, 0, 0))],
        out_shape=[jax.ShapeDtypeStruct((bsz, nh * g8, GDN_DV), BF16),
                   jax.ShapeDtypeStruct((bsz, nh, GDN_DK, GDN_DV), F32)],
        compiler_params=_cparams(("parallel",)),
        name="gdn_sample",
    )(x8, ab8, hist8, conv_w, gate_par, gdn_norm.reshape(1, LANES), s0)
    o = o.reshape(bsz, nh, g8, GDN_DV)[:, :, :t_len].transpose(0, 2, 1, 3)
    return o.reshape(bsz, t_len, MIX_A), s_new


def _pool_finish(d_groups, pw_ref, ps_ref, zb):
    outs = [_dot(d.astype(BF16), pw_ref[gi]) for gi, d in enumerate(d_groups)]
    return jnp.concatenate(outs, axis=-1) * ps_ref[...] * _silu(zb)


def _pool_prompt_kernel(u_ref, halo_ref, zb_ref, pw_ref, ps_ref, o_ref, *, pos0):
    i = pl.program_id(1)
    tt = u_ref.shape[1]
    u = u_ref[0].astype(F32)
    halo = jnp.where(i > 0, halo_ref[0].astype(F32), 0.0)
    ext = jnp.concatenate([halo, u], axis=0)
    pos = pos0 + i * tt + lax.broadcasted_iota(jnp.int32, (tt, 1), 0)
    d_groups = []
    for gi, win in enumerate(POOL_WINDOWS):
        sl = slice(gi * POOL_CH, (gi + 1) * POOL_CH)
        lvl = ext[:, sl]
        s = 1
        while s < win:
            lvl = lvl + pltpu.roll(lvl, s, axis=0)
            s *= 2
        cnt = jnp.minimum(win, pos + 1).astype(F32)
        d_groups.append(lvl[POOL_HALO:, :] / cnt - u[:, sl])
    o_ref[0] = _pool_finish(d_groups, pw_ref, ps_ref, zb_ref[0].astype(F32)).astype(o_ref.dtype)


def _pool_prompt(proj3, pool_w, pool_scale, *, tt, pos0):
    bsz, t_len, _ = proj3.shape
    ub, zb = COL_ZA // MIX_B + 1, COL_ZA // MIX_B + 2
    hb = tt // POOL_HALO
    return pl.pallas_call(
        functools.partial(_pool_prompt_kernel, pos0=pos0),
        grid=(bsz, t_len // tt),
        in_specs=[pl.BlockSpec((1, tt, MIX_B), lambda b, i: (b, i, ub)),
                  pl.BlockSpec((1, POOL_HALO, MIX_B), lambda b, i: (b, jnp.maximum(i * hb - 1, 0), ub)),
                  pl.BlockSpec((1, tt, MIX_B), lambda b, i: (b, i, zb)),
                  pl.BlockSpec((POOL_GROUPS, POOL_CH, POOL_CH), lambda b, i: (0, 0, 0)),
                  pl.BlockSpec((1, MIX_B), lambda b, i: (0, 0))],
        out_specs=pl.BlockSpec((1, tt, MIX_B), lambda b, i: (b, i, 0)),
        out_shape=jax.ShapeDtypeStruct((bsz, t_len, MIX_B), BF16),
        compiler_params=_cparams(("parallel", "arbitrary")),
        name="pool_prompt",
    )(proj3, proj3, proj3, pool_w, pool_scale.reshape(1, MIX_B))


def _pool_sample_kernel(ext_ref, zb_ref, pw_ref, ps_ref, o_ref, *, pos0):
    t_len = zb_ref.shape[0]
    for t in range(t_len):
        e = POOL_BUF + t
        d_groups = []
        for gi, win in enumerate(POOL_WINDOWS):
            sl = slice(gi * POOL_CH, (gi + 1) * POOL_CH)
            tot = ext_ref[e, :, sl]
            for j in range(1, win):
                tot = tot + ext_ref[e - j, :, sl]
            cnt = float(min(win, pos0 + t + 1))
            d_groups.append(tot / cnt - ext_ref[e, :, sl])
        o_ref[t] = _pool_finish(d_groups, pw_ref, ps_ref, zb_ref[t].astype(F32)).astype(o_ref.dtype)


def _pool_sample(ext_tm, zb_tm, pool_w, pool_scale, *, bb, pos0):
    t_len, bsz, _ = zb_tm.shape
    return pl.pallas_call(
        functools.partial(_pool_sample_kernel, pos0=pos0),
        grid=(bsz // bb,),
        in_specs=[pl.BlockSpec((POOL_BUF + t_len, bb, MIX_B), lambda i: (0, i, 0)),
                  pl.BlockSpec((t_len, bb, MIX_B), lambda i: (0, i, 0)),
                  pl.BlockSpec((POOL_GROUPS, POOL_CH, POOL_CH), lambda i: (0, 0, 0)),
                  pl.BlockSpec((1, MIX_B), lambda i: (0, 0))],
        out_specs=pl.BlockSpec((t_len, bb, MIX_B), lambda i: (0, i, 0)),
        out_shape=jax.ShapeDtypeStruct((t_len, bsz, MIX_B), BF16),
        compiler_params=_cparams(("parallel",)),
        name="pool_sample",
    )(ext_tm, zb_tm, pool_w, pool_scale.reshape(1, MIX_B))


def _mix_out_kernel(x_ref, oa_ref, ob_ref, wo_ref, nc_ref, wq_ref, x1_ref, qx_ref):
    acc = _dot(oa_ref[...], wo_ref[0:MIX_A, :]) + _dot(ob_ref[...], wo_ref[MIX_A:D_MODEL, :])
    x1 = x_ref[...] + acc
    x1_ref[...] = x1
    qx_ref[...] = _dot(_rms(x1, nc_ref[...]).astype(BF16), wq_ref[...]).astype(qx_ref.dtype)


def _mix_out(x, oa, ob, w_out, norm_cross, w_cq, *, tm):
    n, d = x.shape
    const = lambda i: (0, 0)
    return pl.pallas_call(
        _mix_out_kernel,
        grid=(n // tm,),
        in_specs=[pl.BlockSpec((tm, d), lambda i: (i, 0)),
                  pl.BlockSpec((tm, MIX_A), lambda i: (i, 0)),
                  pl.BlockSpec((tm, MIX_B), lambda i: (i, 0)),
                  pl.BlockSpec((d, d), const), pl.BlockSpec((1, d), const), pl.BlockSpec((d, d), const)],
        out_specs=[pl.BlockSpec((tm, d), lambda i: (i, 0)), pl.BlockSpec((tm, d), lambda i: (i, 0))],
        out_shape=[jax.ShapeDtypeStruct((n, d), F32), jax.ShapeDtypeStruct((n, d), BF16)],
        compiler_params=_cparams(("parallel",)),
        name="mix_out",
    )(x, oa, ob, w_out, norm_cross.reshape(1, d), w_cq)


def _xattn_kernel(q_ref, k_ref, v_ref, ctx_ref):
    q = q_ref[0]
    scale = X_HEAD_DIM ** -0.5
    for h in range(X_HEADS):
        sl = slice(h * X_HEAD_DIM, (h + 1) * X_HEAD_DIM)
        s = _dot_nt(q[:, sl], k_ref[0, :, sl].astype(BF16)) * scale
        p = jnp.exp(s - jnp.max(s, axis=-1, keepdims=True))
        p = p / jnp.sum(p, axis=-1, keepdims=True)
        ctx_ref[0, :, sl] = _dot(p.astype(BF16), v_ref[0, :, sl].astype(BF16)).astype(ctx_ref.dtype)


def _xattn(qx3, mk3, mv3, *, tq):
    bsz, t_len, d = qx3.shape
    n_mem = mk3.shape[1]
    return pl.pallas_call(
        _xattn_kernel,
        grid=(bsz, t_len // tq),
        in_specs=[pl.BlockSpec((1, tq, d), lambda b, i: (b, i, 0)),
                  pl.BlockSpec((1, n_mem, d), lambda b, i: (b, 0, 0)),
                  pl.BlockSpec((1, n_mem, d), lambda b, i: (b, 0, 0))],
        out_specs=pl.BlockSpec((1, tq, d), lambda b, i: (b, i, 0)),
        out_shape=jax.ShapeDtypeStruct((bsz, t_len, d), BF16),
        compiler_params=_cparams(("parallel", "arbitrary")),
        name="xattn",
    )(qx3, mk3, mv3)


def _xattn_native_kernel(q_ref, k_ref, v_ref, o_ref, *, t_len):
    nj = X_HEAD_DIM // LANES
    grp = nj * X_HEADS
    th = t_len * X_HEADS
    scale = X_HEAD_DIM ** -0.5
    for b in range(q_ref.shape[0]):
        z = _dot_nt(q_ref[b], k_ref[b].astype(BF16))
        ncol = z.shape[1]
        r = lax.broadcasted_iota(jnp.int32, (th, ncol), 0) & (X_HEADS - 1)
        c = lax.broadcasted_iota(jnp.int32, (th, ncol), 1) & (grp - 1)
        s = None
        for j in range(nj):
            zj = jnp.where(c == r + j * X_HEADS, z[j * th:(j + 1) * th, :], 0.0)
            if j:
                zj = pltpu.roll(zj, ncol - j * X_HEADS, axis=1)
            s = zj if s is None else s + zj
        sm = jnp.where(c == r, s * scale, -jnp.inf)
        p = jnp.exp(sm - jnp.max(sm, axis=1, keepdims=True))
        p = p / jnp.sum(p, axis=1, keepdims=True)
        pp = jnp.concatenate([p if j == 0 else pltpu.roll(p, j * X_HEADS, axis=1)
                              for j in range(nj)], axis=0).astype(BF16)
        o_ref[b] = _dot(pp, v_ref[b].astype(BF16)).astype(o_ref.dtype)


def _xattn_native(q_rows, k_rows, v_rows, *, t_len, bb):
    bsz, nq, _ = q_rows.shape
    nk = k_rows.shape[1]
    return pl.pallas_call(
        functools.partial(_xattn_native_kernel, t_len=t_len),
        grid=(bsz // bb,),
        in_specs=[pl.BlockSpec((bb, nq, LANES), lambda i: (i, 0, 0)),
                  pl.BlockSpec((bb, nk, LANES), lambda i: (i, 0, 0)),
                  pl.BlockSpec((bb, nk, LANES), lambda i: (i, 0, 0))],
        out_specs=pl.BlockSpec((bb, nq, LANES), lambda i: (i, 0, 0)),
        out_shape=jax.ShapeDtypeStruct((bsz, nq, LANES), BF16),
        compiler_params=_cparams(("parallel",)),
        name="xattn_native",
    )(q_rows, k_rows, v_rows)


def _kv_rows(cache):
    bsz, n_mem, nh, dh = cache.shape
    nj = dh // LANES
    return cache.reshape(bsz, n_mem, nh, nj, LANES).transpose(0, 1, 3, 2, 4).reshape(
        bsz, n_mem * nj * nh, LANES)


def _attn_out_kernel(x1_ref, ctx_ref, wco_ref, nf_ref, y_ref):
    x2 = x1_ref[...] + _dot(ctx_ref[...], wco_ref[...])
    y_ref[...] = _rms(x2, nf_ref[...])


def _attn_out(x1, ctx, w_co, norm_final, *, tm):
    n, d = x1.shape
    const = lambda i: (0, 0)
    return pl.pallas_call(
        _attn_out_kernel,
        grid=(n // tm,),
        in_specs=[pl.BlockSpec((tm, d), lambda i: (i, 0)), pl.BlockSpec((tm, d), lambda i: (i, 0)),
                  pl.BlockSpec((d, d), const), pl.BlockSpec((1, d), const)],
        out_specs=pl.BlockSpec((tm, d), lambda i: (i, 0)),
        out_shape=jax.ShapeDtypeStruct((n, d), F32),
        compiler_params=_cparams(("parallel",)),
        name="attn_out",
    )(x1, ctx, w_co, norm_final.reshape(1, d))


def _layer_tail(x2d, oa, ob, mk3, mv3, bsz, t_len, w_out, norm_cross, w_cq, w_co, norm_final,
                *, tm, tq):
    x1, qx = _mix_out(x2d, oa.reshape(-1, MIX_A), ob.reshape(-1, MIX_B), w_out, norm_cross, w_cq, tm=tm)
    ctx = _xattn(qx.reshape(bsz, t_len, D_MODEL), mk3, mv3, tq=tq)
    y = _attn_out(x1, ctx.reshape(-1, D_MODEL), w_co, norm_final, tm=tm)
    return y.reshape(bsz, t_len, D_MODEL)


def kernel(x_prompt, x_sample, mem_prompt, cache_mem_k, cache_mem_v, state_delta, state_conv,
           state_pool, norm_mix, w_in, conv_w, a_log, dt_bias, gdn_norm, pool_w, pool_scale,
           w_out, norm_mem, norm_cross, w_cq, w_ck, w_cv, w_co, norm_final):
    bp, tp, d = x_prompt.shape
    bs, ts, _ = x_sample.shape
    n_mem = mem_prompt.shape[1]

    wi = w_in[0]
    w_main = jnp.concatenate([wi[:, :COL_A], wi[:, COL_U:]], axis=1).astype(BF16)
    w_gate = jnp.pad(wi[:, COL_A:COL_U], ((0, 0), (0, LANES - 2 * GDN_HEADS))).astype(BF16)
    wo, wcq, wck, wcv, wco = (w[0].astype(BF16) for w in (w_out, w_cq, w_ck, w_cv, w_co))
    pw = pool_w[0].astype(BF16)
    gate_par = jnp.zeros((SUBLANES, LANES), F32)
    gate_par = gate_par.at[0, :GDN_HEADS].set(a_log[0]).at[1, :GDN_HEADS].set(dt_bias[0])
    cw = conv_w[0]

    mem2d = mem_prompt.reshape(bp * n_mem, d)
    mk = _norm_proj(mem2d, norm_mem[0], wck, tm=512, tn=1024, out_dtype=F32)
    mv = _norm_proj(mem2d, norm_mem[0], wcv, tm=512, tn=1024, out_dtype=F32)

    proj_p, ab_p = _norm_proj(x_prompt.reshape(bp * tp, d), norm_mix[0], w_main, w_gate,
                              tm=512, tn=1024, out_dtype=BF16)
    proj_p3 = proj_p.reshape(bp, tp, W_MAIN)
    oa_p, delta_p = _gdn_prompt(proj_p3, ab_p.reshape(bp, tp, LANES), cw, gate_par, gdn_norm[0])
    ob_p = _pool_prompt(proj_p3, pw, pool_scale[0], tt=512, pos0=0)
    y_p = _layer_tail(x_prompt.reshape(bp * tp, d), oa_p, ob_p, mk.reshape(bp, n_mem, d),
                      mv.reshape(bp, n_mem, d), bp, tp, wo, norm_cross[0], wcq, wco, norm_final,
                      tm=256, tq=512)
    conv_p = proj_p3[:, tp - (CONV_W - 1):, :QKV_W].astype(F32)
    pool_p = proj_p3[:, tp - POOL_BUF:, COL_ZA + MIX_A:COL_ZA + MIX_A + MIX_B].astype(F32)

    proj_s, ab_s = _norm_proj(x_sample.reshape(bs * ts, d), norm_mix[0], w_main, w_gate,
                              tm=256, tn=1024, out_dtype=BF16)
    proj_s3 = proj_s.reshape(bs, ts, W_MAIN)
    oa_s, delta_s = _gdn_sample(proj_s3, ab_s.reshape(bs, ts, LANES), state_conv[0], cw, gate_par,
                                gdn_norm[0], state_delta[0], bb=4)
    u_s = proj_s3[:, :, COL_ZA + MIX_A:COL_ZA + MIX_A + MIX_B].astype(F32)
    ext_s = jnp.concatenate([state_pool[0], u_s], axis=1)
    zb_tm = proj_s3[:, :, COL_ZA + MIX_A + MIX_B:].transpose(1, 0, 2)
    ob_s = _pool_sample(ext_s.transpose(1, 0, 2), zb_tm, pw, pool_scale[0], bb=32,
                        pos0=PAST_LEN).transpose(1, 0, 2)
    x1_s, qx_s = _mix_out(x_sample.reshape(bs * ts, d), oa_s.reshape(-1, MIX_A),
                          ob_s.reshape(-1, MIX_B), wo, norm_cross[0], wcq, tm=256)
    nj = X_HEAD_DIM // LANES
    q_rows = qx_s.reshape(bs, ts, X_HEADS, nj, LANES).transpose(0, 3, 1, 2, 4).reshape(
        bs, nj * ts * X_HEADS, LANES)
    ctx_rows = _xattn_native(q_rows, _kv_rows(cache_mem_k[0]), _kv_rows(cache_mem_v[0]),
                             t_len=ts, bb=2)
    ctx_s = ctx_rows.reshape(bs, nj, ts, X_HEADS, LANES).transpose(0, 2, 3, 1, 4).reshape(bs * ts, d)
    y_s = _attn_out(x1_s, ctx_s, wco, norm_final, tm=256).reshape(bs, ts, d)
    conv_s = jnp.concatenate([state_conv[0], proj_s3[:, :, :QKV_W].astype(F32)], axis=1)[:, ts:]
    pool_s = ext_s[:, ts:]

    hd = (X_HEADS, X_HEAD_DIM)
    return (y_p, y_s, mk.reshape(1, bp, n_mem, *hd), mv.reshape(1, bp, n_mem, *hd),
            delta_p[None], conv_p[None], pool_p[None], delta_s[None], conv_s[None], pool_s[None])
```

```python
import functools
import math

import jax
import jax.numpy as jnp
from jax import lax
from jax.experimental import pallas as pl
from jax.experimental.pallas import tpu as pltpu

F32 = jnp.float32
BF16 = jnp.bfloat16

D_MODEL = 2048
MIX_A = D_MODEL // 2
MIX_B = D_MODEL - MIX_A
GDN_HEADS = 8
GDN_DK = MIX_A // GDN_HEADS
GDN_DV = MIX_A // GDN_HEADS
QK_W = GDN_HEADS * GDN_DK
QKV_W = 2 * QK_W + GDN_HEADS * GDN_DV
CONV_W = 4
POOL_WINDOWS = (2, 4, 8, 16)
POOL_GROUPS = len(POOL_WINDOWS)
POOL_CH = MIX_B // POOL_GROUPS
POOL_BUF = max(POOL_WINDOWS) - 1
X_HEADS = 4
X_HEAD_DIM = D_MODEL // X_HEADS
PAST_LEN = 16384
EPS = 1e-6
COL_ZA = QKV_W
COL_A = COL_ZA + GDN_HEADS * GDN_DV
COL_B = COL_A + GDN_HEADS
COL_U = COL_B + GDN_HEADS
COL_ZB = COL_U + MIX_B
IN_COLS = COL_ZB + MIX_B

W_MAIN = IN_COLS - 2 * GDN_HEADS
LANES = 128
SUBLANES = 8
GDN_CHUNK = 128
POOL_HALO = 16
VMEM_LIMIT = 56 * 1024 * 1024


def _cparams(sem):
    return pltpu.CompilerParams(dimension_semantics=sem, vmem_limit_bytes=VMEM_LIMIT)


def _dot(a, b):
    return jnp.dot(a, b, preferred_element_type=F32)


def _dot_nt(a, b):
    return lax.dot_general(a, b, (((1,), (1,)), ((), ())), preferred_element_type=F32)


def _dot_tn(a, b):
    return lax.dot_general(a, b, (((0,), (0,)), ((), ())), preferred_element_type=F32)


def _rms(x, g):
    return x * lax.rsqrt(jnp.mean(x * x, axis=-1, keepdims=True) + EPS) * g


def _silu(x):
    return x * jax.nn.sigmoid(x)


def _softplus(x):
    return jnp.maximum(x, 0.0) + jnp.log1p(jnp.exp(-jnp.abs(x)))


def _norm_proj_kernel(x_ref, g_ref, w_ref, *rest, with_side):
    if with_side:
        ws_ref, out_ref, side_ref, h_scr = rest
    else:
        out_ref, h_scr = rest

    @pl.when(pl.program_id(1) == 0)
    def _():
        h = _rms(x_ref[...], g_ref[...]).astype(BF16)
        h_scr[...] = h
        if with_side:
            side_ref[...] = _dot(h, ws_ref[...])

    out_ref[...] = _dot(h_scr[...], w_ref[...]).astype(out_ref.dtype)


def _norm_proj(x, g, w, w_side=None, *, tm, tn, out_dtype):
    n, d = x.shape
    ncol = w.shape[1]
    with_side = w_side is not None
    in_specs = [pl.BlockSpec((tm, d), lambda i, j: (i, 0)),
                pl.BlockSpec((1, d), lambda i, j: (0, 0)),
                pl.BlockSpec((d, tn), lambda i, j: (0, j))]
    out_specs = [pl.BlockSpec((tm, tn), lambda i, j: (i, j))]
    out_shape = [jax.ShapeDtypeStruct((n, ncol), out_dtype)]
    args = [x, g.reshape(1, d), w]
    if with_side:
        in_specs.append(pl.BlockSpec((d, LANES), lambda i, j: (0, 0)))
        out_specs.append(pl.BlockSpec((tm, LANES), lambda i, j: (i, 0)))
        out_shape.append(jax.ShapeDtypeStruct((n, LANES), F32))
        args.append(w_side)
    res = pl.pallas_call(
        functools.partial(_norm_proj_kernel, with_side=with_side),
        grid=(n // tm, ncol // tn),
        in_specs=in_specs, out_specs=out_specs, out_shape=out_shape,
        scratch_shapes=[pltpu.VMEM((tm, d), BF16)],
        compiler_params=_cparams(("parallel", "arbitrary")),
        name="norm_proj_side" if with_side else "norm_proj",
    )(*args)
    return res if with_side else res[0]


INV_BASE_LOG2 = 4
_GDN_WORK = (("k", 1, BF16), ("kb", 1, BF16), ("q", 1, BF16), ("kd", 1, BF16), ("vk", 2, BF16),
             ("dinc", 1, F32), ("low", 1, F32), ("x", 1, F32), ("p", 1, BF16), ("wu", 2, BF16))


def _split3(x):
    hi = x.astype(BF16)
    r1 = x - hi.astype(F32)
    mid = r1.astype(BF16)
    lo = (r1 - mid.astype(F32)).astype(BF16)
    return hi, mid, lo


def _gdn_prompt_kernel(q_ref, k_ref, v_ref, z_ref, ab_ref, cwq_ref, cwk_ref, cwv_ref, gp_ref,
                       gn_ref, sh_ref, o_ref, s_ref,
                       gc_s, gct_s, egc_s, ekd_s, beta_s, u_s, n_s, pw_s, qa_s, sv_s, *work_refs, group):
    nw = len(_GDN_WORK)
    work = [{name: ref for (name, _, _), ref in zip(_GDN_WORK, work_refs[j * nw:(j + 1) * nw])}
            for j in range(group)]
    h = pl.program_id(1)
    t_len = q_ref.shape[1]
    c = GDN_CHUNK
    n_c = t_len // c
    row = lax.broadcasted_iota(jnp.int32, (c, c), 0)
    col = lax.broadcasted_iota(jnp.int32, (c, c), 1)

    @pl.when(h == 0)
    def _():
        gp = gp_ref[...]
        a_neg = -jnp.exp(gp[0:1])
        dt_b = gp[1:2]
        tri = jnp.where(row >= col, 1.0, 0.0).astype(BF16)

        def gate_chunk(ci, carry):
            rows = pl.ds(pl.multiple_of(ci * c, c), c)
            ab = ab_ref[0, rows, :]
            hi, mid, lo = _split3(a_neg * _softplus(ab + dt_b))
            gc = _dot(tri, hi) + _dot(tri, mid) + _dot(tri, lo)
            gc_s[rows, :] = gc
            gct_s[rows, :] = gc.T
            egc_s[rows, :] = jnp.exp(gc)
            ekd_s[rows, :] = jnp.exp(gc[c - 1:c, :] - gc)
            beta_s[rows, :] = jax.nn.sigmoid(ab)
            return carry

        lax.fori_loop(0, n_c, gate_chunk, 0)

    def l2n(x):
        return x * lax.rsqrt(jnp.sum(x * x, axis=-1, keepdims=True) + EPS)

    def operands(w, ci):
        r0 = pl.multiple_of(ci * c, c)
        rp = pl.multiple_of(jnp.maximum(ci - 1, 0) * c, c)
        rows = pl.ds(r0, c)

        def conv_silu(x_ref, cw_ref):
            cur = x_ref[0, rows, :]
            prev = x_ref[0, pl.ds(rp, c), :]
            prev = jnp.where(ci > 0, prev, jnp.zeros_like(prev))
            sh = _dot(sh_ref[...], jnp.concatenate([prev, cur], axis=0))
            cw = cw_ref[...]
            y = cur.astype(F32) * cw[CONV_W - 1:CONV_W]
            for s in range(1, CONV_W):
                y = y + sh[(s - 1) * c:s * c] * cw[CONV_W - 1 - s:CONV_W - s]
            return _silu(y)

        def column(scr, idx):
            return jnp.sum(jnp.where(col == idx, scr[rows, :], 0.0), axis=-1, keepdims=True)

        q = l2n(conv_silu(q_ref, cwq_ref)) * (GDN_DK ** -0.5)
        k = l2n(conv_silu(k_ref, cwk_ref))
        v = conv_silu(v_ref, cwv_ref)
        gcc = column(gc_s, h)
        egc = column(egc_s, h)
        ekd = column(ekd_s, h)
        beta = column(beta_s, h + GDN_HEADS)
        grow = gct_s[pl.ds(r0 + h, 1), :]
        tri_i = row >= col
        w["dinc"][...] = jnp.where(tri_i, jnp.exp(jnp.where(tri_i, gcc - grow, 0.0)), 0.0)
        kb = k * beta
        w["k"][...] = k.astype(BF16)
        w["kb"][...] = kb.astype(BF16)
        w["q"][...] = q.astype(BF16)
        w["kd"][...] = (k * ekd).astype(BF16)
        w["vk"][:, :GDN_DV] = (v * beta).astype(BF16)
        w["vk"][:, GDN_DV:] = (kb * egc).astype(BF16)
        qa_s[ci, :, 0:GDN_DK] = (q * egc).astype(BF16)

    def gram(w, ci):
        r = _dot_nt(jnp.concatenate([w["kb"][...], w["q"][...]], axis=0), w["k"][...])
        d_incl = w["dinc"][...]
        low = r[:c] * jnp.where(row > col, d_incl, 0.0)
        qa_s[ci, :, GDN_DK:] = (r[c:] * d_incl).astype(BF16)
        w["low"][...] = low
        ld = jnp.where((row >> INV_BASE_LOG2) == (col >> INV_BASE_LOG2), low, 0.0)
        w["x"][...] = jnp.where(row == col, 1.0, 0.0) - ld
        w["p"][...] = ld.astype(BF16)

    def neumann(w, first, last):
        pb = w["p"][...]
        if first:
            w["p"][...] = _dot(pb, pb).astype(BF16)
        elif last:
            x = w["x"][...]
            w["x"][...] = x + _dot(x.astype(BF16), pb)
        else:
            x = w["x"][...]
            r = _dot(jnp.concatenate([x.astype(BF16), pb], axis=0), pb)
            w["x"][...] = x + r[:c]
            w["p"][...] = r[c:].astype(BF16)

    def merge_a(w, lg):
        pair = (row >> (lg + 1)) == (col >> (lg + 1))
        m = jnp.where(pair, (row >> lg) - (col >> lg), 0) > 0
        lm = jnp.where(m, w["low"][...], 0.0).astype(BF16)
        w["p"][...] = _dot(lm, w["x"][...].astype(BF16)).astype(BF16)

    def merge_b(w):
        x = w["x"][...]
        w["x"][...] = x - _dot(x.astype(BF16), w["p"][...])

    def solve(w, ci):
        uw = _dot(w["x"][...].astype(BF16), w["vk"][...])
        u_s[ci] = uw[:, :GDN_DV]
        uwb = uw.astype(BF16)
        w["wu"][...] = uwb
        pw_s[ci, c:2 * c, :] = uwb[:, GDN_DV:]

    def outer(w, ci):
        np_ = _dot_tn(w["kd"][...], w["wu"][...])
        n_s[ci] = np_[:, :GDN_DV]
        pw_s[ci, 0:c, :] = np_[:, GDN_DV:].astype(BF16)

    n_sq = INV_BASE_LOG2 - 1
    lgc = c.bit_length() - 1

    def prepare_group(gi, carry):
        cis = [gi * group + j for j in range(group)]
        for j in range(group):
            operands(work[j], cis[j])
        for j in range(group):
            gram(work[j], cis[j])
        for step in range(n_sq + 1):
            for j in range(group):
                neumann(work[j], step == 0, step == n_sq)
        for lg in range(INV_BASE_LOG2, lgc):
            for j in range(group):
                merge_a(work[j], lg)
            for j in range(group):
                merge_b(work[j])
        for j in range(group):
            solve(work[j], cis[j])
        for j in range(group):
            outer(work[j], cis[j])
        return carry

    lax.fori_loop(0, n_c // group, prepare_group, 0)

    gn = gn_ref[...]
    lane1 = lax.broadcasted_iota(jnp.int32, (1, LANES), 1)

    def advance(ci, s_mat):
        r0 = pl.multiple_of(ci * c, c)
        sb = s_mat.astype(BF16)
        r = _dot(pw_s[ci], sb)
        g_last = jnp.sum(jnp.where(lane1 == h, gc_s[pl.ds(r0 + c - 1, 1), :], 0.0),
                         axis=-1, keepdims=True)
        sv_s[ci, 0:c, :] = sb
        sv_s[ci, c:2 * c, :] = (u_s[ci] - r[c:2 * c]).astype(BF16)
        return s_mat * jnp.exp(g_last) - r[0:c] + n_s[ci]

    s_ref[0, 0] = lax.fori_loop(0, n_c, advance, jnp.zeros((GDN_DK, GDN_DV), F32))

    def emit_group(gi, carry):
        cis = [gi * group + j for j in range(group)]
        os_ = [_dot(qa_s[ci], sv_s[ci]) for ci in cis]
        for ci, o in zip(cis, os_):
            rows = pl.ds(pl.multiple_of(ci * c, c), c)
            zz = z_ref[0, rows, :].astype(F32)
            o_ref[0, rows, :] = (_rms(o, gn) * _silu(zz)).astype(o_ref.dtype)
        return carry

    lax.fori_loop(0, n_c // group, emit_group, 0)


def _gdn_prompt(proj3, ab3, conv_w, gate_par, gdn_norm):
    bsz, t_len, _ = proj3.shape
    nh = GDN_HEADS
    blk = lambda off: pl.BlockSpec((1, t_len, LANES), lambda b, h: (b, 0, off + h))
    cw = lambda off: pl.BlockSpec((CONV_W, LANES), lambda b, h: (0, off + h))
    c = GDN_CHUNK
    n_c = t_len // c
    rr = jnp.arange((CONV_W - 1) * c)[:, None]
    shift = (jnp.arange(2 * c)[None, :] == c + (rr % c) - (rr // c + 1)).astype(BF16)
    gate_scr = pltpu.VMEM((t_len, LANES), F32)
    group = 8
    work = [pltpu.VMEM((c, wide * LANES), dt) for _ in range(group) for _, wide, dt in _GDN_WORK]
    return pl.pallas_call(
        functools.partial(_gdn_prompt_kernel, group=group),
        grid=(bsz, nh),
        in_specs=[blk(0), blk(nh), blk(2 * nh), blk(3 * nh),
                  pl.BlockSpec((1, t_len, LANES), lambda b, h: (b, 0, 0)),
                  cw(0), cw(nh), cw(2 * nh),
                  pl.BlockSpec((SUBLANES, LANES), lambda b, h: (0, 0)),
                  pl.BlockSpec((1, LANES), lambda b, h: (0, 0)),
                  pl.BlockSpec(((CONV_W - 1) * c, 2 * c), lambda b, h: (0, 0))],
        out_specs=[pl.BlockSpec((1, t_len, LANES), lambda b, h: (b, 0, h)),
                   pl.BlockSpec((1, 1, GDN_DK, GDN_DV), lambda b, h: (b, h, 0, 0))],
        out_shape=[jax.ShapeDtypeStruct((bsz, t_len, MIX_A), BF16),
                   jax.ShapeDtypeStruct((bsz, nh, GDN_DK, GDN_DV), F32)],
        scratch_shapes=[gate_scr, gate_scr, gate_scr, gate_scr, gate_scr,
                        pltpu.VMEM((n_c, c, GDN_DV), F32), pltpu.VMEM((n_c, GDN_DK, GDN_DV), F32),
                        pltpu.VMEM((n_c, 2 * c, GDN_DK), BF16), pltpu.VMEM((n_c, c, GDN_DK + c), BF16),
                        pltpu.VMEM((n_c, GDN_DK + c, GDN_DV), BF16)] + work,
        compiler_params=_cparams(("parallel", "arbitrary")),
        name="gdn_prompt",
    )(proj3, proj3, proj3, proj3, ab3, conv_w, conv_w, conv_w, gate_par, gdn_norm.reshape(1, LANES),
      shift)


def _gdn_sample_kernel(x_ref, ab_ref, hist_ref, cw_ref, gp_ref, gn_ref, s0_ref, o_ref, s_ref,
                       *, t_len):
    nh = GDN_HEADS
    g8 = SUBLANES
    n = nh * g8
    assert t_len + CONV_W - 1 <= g8
    cw = cw_ref[...]
    gp = gp_ref[...]
    gn = gn_ref[...]
    row8 = lax.broadcasted_iota(jnp.int32, (g8, QKV_W), 0)
    tpos = lax.broadcasted_iota(jnp.int32, (n, 1), 0) & (g8 - 1)
    rhead = lax.broadcasted_iota(jnp.int32, (n, GDN_DV), 0) >> 3
    ri = lax.broadcasted_iota(jnp.int32, (n, n), 0)
    ci = lax.broadcasted_iota(jnp.int32, (n, n), 1)
    tri = jnp.where((ri >> 3) == (ci >> 3), ri - ci, -1)
    eye = jnp.where(ri == ci, 1.0, 0.0)
    sel0 = jnp.where(lax.broadcasted_iota(jnp.int32, (n, LANES), 1) == 0, 1.0, 0.0).astype(BF16)
    valid = tpos < t_len
    a_neg = jnp.concatenate([jnp.broadcast_to(-jnp.exp(gp[0:1, h:h + 1]), (g8, 1)) for h in range(nh)], 0)
    dt_b = jnp.concatenate([jnp.broadcast_to(gp[1:2, h:h + 1], (g8, 1)) for h in range(nh)], 0)

    def l2n(x):
        return x * lax.rsqrt(jnp.sum(x * x, axis=-1, keepdims=True) + EPS)

    for b in range(x_ref.shape[0]):
        xb = x_ref[b]
        x = xb[:, :QKV_W].astype(F32)
        hst = hist_ref[b]
        y = x * cw[CONV_W - 1:CONV_W]
        for s in range(1, CONV_W):
            y = y + pltpu.roll(x, s, axis=0) * cw[CONV_W - 1 - s:CONV_W - s]
        for j in range(CONV_W - 1):
            hj = hst if j == 0 else pltpu.roll(hst, g8 - j, axis=0)
            y = y + jnp.where(row8 + j < CONV_W - 1, hj, 0.0) * cw[j:j + 1]
        qkv = jnp.where(row8 < t_len, _silu(y), 0.0)

        def stack(off):
            return jnp.concatenate([qkv[:, off + h * GDN_DK:off + (h + 1) * GDN_DK] for h in range(nh)], 0)

        q = l2n(stack(0)) * (GDN_DK ** -0.5)
        k = l2n(stack(QK_W))
        v = stack(2 * QK_W)
        z = jnp.concatenate([xb[:, 3 * QK_W + h * GDN_DV:3 * QK_W + (h + 1) * GDN_DV].astype(F32)
                             for h in range(nh)], 0)
        ab = ab_ref[b]
        alpha = jnp.concatenate([ab[:, h:h + 1] for h in range(nh)], 0)
        braw = jnp.concatenate([ab[:, nh + h:nh + h + 1] for h in range(nh)], 0)
        g = jnp.where(valid, a_neg * _softplus(alpha + dt_b), 0.0)
        beta = jnp.where(valid, jax.nn.sigmoid(braw), 0.0)
        gc = g
        s = 1
        while s < g8:
            gc = gc + jnp.where(tpos >= s, pltpu.roll(gc, s, axis=0), 0.0)
            s *= 2
        g_last = jnp.concatenate([jnp.broadcast_to(gc[h * g8 + g8 - 1:(h + 1) * g8, :], (g8, 1))
                                  for h in range(nh)], 0)
        xg = jnp.broadcast_to(gc, (n, LANES))
        hi = xg.astype(BF16)
        r1 = xg - hi.astype(F32)
        mid = r1.astype(BF16)
        lo = (r1 - mid.astype(F32)).astype(BF16)
        grow = _dot_nt(sel0, hi) + _dot_nt(sel0, mid) + _dot_nt(sel0, lo)
        d_incl = jnp.where(tri >= 0, jnp.exp(jnp.where(tri >= 0, gc - grow, 0.0)), 0.0)
        d_strict = jnp.where(tri > 0, d_incl, 0.0)
        egc = jnp.exp(gc)
        kb = k * beta
        kbf = k.astype(BF16)
        low = _dot_nt(kb.astype(BF16), kbf) * d_strict
        a_in = _dot_nt(q.astype(BF16), kbf) * d_incl
        t_inv = eye - low
        p = low
        covered = 2
        while covered < t_len:
            pb = p.astype(BF16)
            p = _dot(pb, pb)
            t_inv = t_inv + _dot(t_inv.astype(BF16), p.astype(BF16))
            covered *= 2
        t_inv = t_inv.astype(BF16)
        u = _dot(t_inv, (v * beta).astype(BF16))
        w = _dot(t_inv, (kb * egc).astype(BF16))
        qg = q * egc
        k_dec = (k * jnp.exp(g_last - gc)).astype(BF16)
        ws, qs = [], []
        for h in range(nh):
            hs = slice(h * g8, (h + 1) * g8)
            lhs = jnp.concatenate([w[hs], qg[hs]], 0).astype(BF16)
            r = _dot(lhs, s0_ref[b, h].astype(BF16))
            ws.append(r[:g8])
            qs.append(r[g8:])
        v_new = u - jnp.concatenate(ws, 0)
        o = jnp.concatenate(qs, 0) + _dot(a_in.astype(BF16), v_new.astype(BF16))
        for h in range(nh):
            vm = jnp.where(rhead == h, v_new, 0.0).astype(BF16)
            dec = jnp.exp(g_last[h * g8:h * g8 + 1, :])
            s_ref[b, h] = s0_ref[b, h] * dec + _dot_tn(k_dec, vm)
        o_ref[b] = (_rms(o, gn) * _silu(z)).astype(o_ref.dtype)


def _gdn_sample(proj3, ab3, hist, conv_w, gate_par, gdn_norm, s0, *, bb):
    bsz, t_len, _ = proj3.shape
    nh, g8 = GDN_HEADS, SUBLANES
    pad_t = lambda a, rows: jnp.pad(a, ((0, 0), (0, g8 - rows), (0, 0)))
    x8 = pad_t(proj3[:, :, :4 * QK_W], t_len)
    ab8 = pad_t(ab3, t_len)
    hist8 = pad_t(hist, CONV_W - 1)
    o, s_new = pl.pallas_call(
        functools.partial(_gdn_sample_kernel, t_len=t_len),
        grid=(bsz // bb,),
        in_specs=[pl.BlockSpec((bb, g8, 4 * QK_W), lambda i: (i, 0, 0)),
                  pl.BlockSpec((bb, g8, LANES), lambda i: (i, 0, 0)),
                  pl.BlockSpec((bb, g8, QKV_W), lambda i: (i, 0, 0)),
                  pl.BlockSpec((CONV_W, QKV_W), lambda i: (0, 0)),
                  pl.BlockSpec((SUBLANES, LANES), lambda i: (0, 0)),
                  pl.BlockSpec((1, LANES), lambda i: (0, 0)),
                  pl.BlockSpec((bb, nh, GDN_DK, GDN_DV), lambda i: (i, 0, 0, 0))],
        out_specs=[pl.BlockSpec((bb, nh * g8, GDN_DV), lambda i: (i, 0, 0)),
                   pl.BlockSpec((bb, nh, GDN_DK, GDN_DV), lambda i: (i, 0, 0, 0))],
        out_shape=[jax.ShapeDtypeStruct((bsz, nh * g8, GDN_DV), BF16),
                   jax.ShapeDtypeStruct((bsz, nh, GDN_DK, GDN_DV), F32)],
        compiler_params=_cparams(("parallel",)),
        name="gdn_sample",
    )(x8, ab8, hist8, conv_w, gate_par, gdn_norm.reshape(1, LANES), s0)
    o = o.reshape(bsz, nh, g8, GDN_DV)[:, :, :t_len].transpose(0, 2, 1, 3)
    return o.reshape(bsz, t_len, MIX_A), s_new


def _pool_finish(d_groups, pw_ref, ps_ref, zb):
    outs = [_dot(d.astype(BF16), pw_ref[gi]) for gi, d in enumerate(d_groups)]
    return jnp.concatenate(outs, axis=-1) * ps_ref[...] * _silu(zb)


def _pool_prompt_kernel(u_ref, halo_ref, zb_ref, pw_ref, ps_ref, o_ref, *, pos0):
    i = pl.program_id(1)
    tt = u_ref.shape[1]
    u = u_ref[0].astype(F32)
    halo = jnp.where(i > 0, halo_ref[0].astype(F32), 0.0)
    ext = jnp.concatenate([halo, u], axis=0)
    pos = pos0 + i * tt + lax.broadcasted_iota(jnp.int32, (tt, 1), 0)
    d_groups = []
    for gi, win in enumerate(POOL_WINDOWS):
        sl = slice(gi * POOL_CH, (gi + 1) * POOL_CH)
        lvl = ext[:, sl]
        s = 1
        while s < win:
            lvl = lvl + pltpu.roll(lvl, s, axis=0)
            s *= 2
        cnt = jnp.minimum(win, pos + 1).astype(F32)
        d_groups.append(lvl[POOL_HALO:, :] / cnt - u[:, sl])
    o_ref[0] = _pool_finish(d_groups, pw_ref, ps_ref, zb_ref[0].astype(F32)).astype(o_ref.dtype)


def _pool_prompt(proj3, pool_w, pool_scale, *, tt, pos0):
    bsz, t_len, _ = proj3.shape
    ub, zb = COL_ZA // MIX_B + 1, COL_ZA // MIX_B + 2
    hb = tt // POOL_HALO
    return pl.pallas_call(
        functools.partial(_pool_prompt_kernel, pos0=pos0),
        grid=(bsz, t_len // tt),
        in_specs=[pl.BlockSpec((1, tt, MIX_B), lambda b, i: (b, i, ub)),
                  pl.BlockSpec((1, POOL_HALO, MIX_B), lambda b, i: (b, jnp.maximum(i * hb - 1, 0), ub)),
                  pl.BlockSpec((1, tt, MIX_B), lambda b, i: (b, i, zb)),
                  pl.BlockSpec((POOL_GROUPS, POOL_CH, POOL_CH), lambda b, i: (0, 0, 0)),
                  pl.BlockSpec((1, MIX_B), lambda b, i: (0, 0))],
        out_specs=pl.BlockSpec((1, tt, MIX_B), lambda b, i: (b, i, 0)),
        out_shape=jax.ShapeDtypeStruct((bsz, t_len, MIX_B), BF16),
        compiler_params=_cparams(("parallel", "arbitrary")),
        name="pool_prompt",
    )(proj3, proj3, proj3, pool_w, pool_scale.reshape(1, MIX_B))


def _pool_sample_kernel(ext_ref, zb_ref, pw_ref, ps_ref, o_ref, *, pos0):
    t_len = zb_ref.shape[0]
    for t in range(t_len):
        e = POOL_BUF + t
        d_groups = []
        for gi, win in enumerate(POOL_WINDOWS):
            sl = slice(gi * POOL_CH, (gi + 1) * POOL_CH)
            tot = ext_ref[e, :, sl]
            for j in range(1, win):
                tot = tot + ext_ref[e - j, :, sl]
            cnt = float(min(win, pos0 + t + 1))
            d_groups.append(tot / cnt - ext_ref[e, :, sl])
        o_ref[t] = _pool_finish(d_groups, pw_ref, ps_ref, zb_ref[t].astype(F32)).astype(o_ref.dtype)


def _pool_sample(ext_tm, zb_tm, pool_w, pool_scale, *, bb, pos0):
    t_len, bsz, _ = zb_tm.shape
    return pl.pallas_call(
        functools.partial(_pool_sample_kernel, pos0=pos0),
        grid=(bsz // bb,),
        in_specs=[pl.BlockSpec((POOL_BUF + t_len, bb, MIX_B), lambda i: (0, i, 0)),
                  pl.BlockSpec((t_len, bb, MIX_B), lambda i: (0, i, 0)),
                  pl.BlockSpec((POOL_GROUPS, POOL_CH, POOL_CH), lambda i: (0, 0, 0)),
                  pl.BlockSpec((1, MIX_B), lambda i: (0, 0))],
        out_specs=pl.BlockSpec((t_len, bb, MIX_B), lambda i: (0, i, 0)),
        out_shape=jax.ShapeDtypeStruct((t_len, bsz, MIX_B), BF16),
        compiler_params=_cparams(("parallel",)),
        name="pool_sample",
    )(ext_tm, zb_tm, pool_w, pool_scale.reshape(1, MIX_B))


def _mix_out_kernel(x_ref, oa_ref, ob_ref, wo_ref, nc_ref, wq_ref, x1_ref, qx_ref):
    acc = _dot(oa_ref[...], wo_ref[0:MIX_A, :]) + _dot(ob_ref[...], wo_ref[MIX_A:D_MODEL, :])
    x1 = x_ref[...] + acc
    x1_ref[...] = x1
    qx_ref[...] = _dot(_rms(x1, nc_ref[...]).astype(BF16), wq_ref[...]).astype(qx_ref.dtype)


def _mix_out(x, oa, ob, w_out, norm_cross, w_cq, *, tm):
    n, d = x.shape
    const = lambda i: (0, 0)
    return pl.pallas_call(
        _mix_out_kernel,
        grid=(n // tm,),
        in_specs=[pl.BlockSpec((tm, d), lambda i: (i, 0)),
                  pl.BlockSpec((tm, MIX_A), lambda i: (i, 0)),
                  pl.BlockSpec((tm, MIX_B), lambda i: (i, 0)),
                  pl.BlockSpec((d, d), const), pl.BlockSpec((1, d), const), pl.BlockSpec((d, d), const)],
        out_specs=[pl.BlockSpec((tm, d), lambda i: (i, 0)), pl.BlockSpec((tm, d), lambda i: (i, 0))],
        out_shape=[jax.ShapeDtypeStruct((n, d), F32), jax.ShapeDtypeStruct((n, d), BF16)],
        compiler_params=_cparams(("parallel",)),
        name="mix_out",
    )(x, oa, ob, w_out, norm_cross.reshape(1, d), w_cq)


def _xattn_kernel(q_ref, k_ref, v_ref, ctx_ref):
    q = q_ref[0]
    scale = X_HEAD_DIM ** -0.5
    for h in range(X_HEADS):
        sl = slice(h * X_HEAD_DIM, (h + 1) * X_HEAD_DIM)
        s = _dot_nt(q[:, sl], k_ref[0, :, sl].astype(BF16)) * scale
        p = jnp.exp(s - jnp.max(s, axis=-1, keepdims=True))
        p = p / jnp.sum(p, axis=-1, keepdims=True)
        ctx_ref[0, :, sl] = _dot(p.astype(BF16), v_ref[0, :, sl].astype(BF16)).astype(ctx_ref.dtype)


def _xattn(qx3, mk3, mv3, *, tq):
    bsz, t_len, d = qx3.shape
    n_mem = mk3.shape[1]
    return pl.pallas_call(
        _xattn_kernel,
        grid=(bsz, t_len // tq),
        in_specs=[pl.BlockSpec((1, tq, d), lambda b, i: (b, i, 0)),
                  pl.BlockSpec((1, n_mem, d), lambda b, i: (b, 0, 0)),
                  pl.BlockSpec((1, n_mem, d), lambda b, i: (b, 0, 0))],
        out_specs=pl.BlockSpec((1, tq, d), lambda b, i: (b, i, 0)),
        out_shape=jax.ShapeDtypeStruct((bsz, t_len, d), BF16),
        compiler_params=_cparams(("parallel", "arbitrary")),
        name="xattn",
    )(qx3, mk3, mv3)


def _xattn_native_kernel(q_ref, k_ref, v_ref, o_ref, *, t_len):
    nj = X_HEAD_DIM // LANES
    grp = nj * X_HEADS
    th = t_len * X_HEADS
    scale = X_HEAD_DIM ** -0.5
    for b in range(q_ref.shape[0]):
        z = _dot_nt(q_ref[b], k_ref[b].astype(BF16))
        ncol = z.shape[1]
        r = lax.broadcasted_iota(jnp.int32, (th, ncol), 0) & (X_HEADS - 1)
        c = lax.broadcasted_iota(jnp.int32, (th, ncol), 1) & (grp - 1)
        s = None
        for j in range(nj):
            zj = jnp.where(c == r + j * X_HEADS, z[j * th:(j + 1) * th, :], 0.0)
            if j:
                zj = pltpu.roll(zj, ncol - j * X_HEADS, axis=1)
            s = zj if s is None else s + zj
        sm = jnp.where(c == r, s * scale, -jnp.inf)
        p = jnp.exp(sm - jnp.max(sm, axis=1, keepdims=True))
        p = p / jnp.sum(p, axis=1, keepdims=True)
        pp = jnp.concatenate([p if j == 0 else pltpu.roll(p, j * X_HEADS, axis=1)
                              for j in range(nj)], axis=0).astype(BF16)
        o_ref[b] = _dot(pp, v_ref[b].astype(BF16)).astype(o_ref.dtype)


def _xattn_native(q_rows, k_rows, v_rows, *, t_len, bb):
    bsz, nq, _ = q_rows.shape
    nk = k_rows.shape[1]
    return pl.pallas_call(
        functools.partial(_xattn_native_kernel, t_len=t_len),
        grid=(bsz // bb,),
        in_specs=[pl.BlockSpec((bb, nq, LANES), lambda i: (i, 0, 0)),
                  pl.BlockSpec((bb, nk, LANES), lambda i: (i, 0, 0)),
                  pl.BlockSpec((bb, nk, LANES), lambda i: (i, 0, 0))],
        out_specs=pl.BlockSpec((bb, nq, LANES), lambda i: (i, 0, 0)),
        out_shape=jax.ShapeDtypeStruct((bsz, nq, LANES), BF16),
        compiler_params=_cparams(("parallel",)),
        name="xattn_native",
    )(q_rows, k_rows, v_rows)


def _kv_rows(cache):
    bsz, n_mem, nh, dh = cache.shape
    nj = dh // LANES
    return cache.reshape(bsz, n_mem, nh, nj, LANES).transpose(0, 1, 3, 2, 4).reshape(
        bsz, n_mem * nj * nh, LANES)


def _attn_out_kernel(x1_ref, ctx_ref, wco_ref, nf_ref, y_ref):
    x2 = x1_ref[...] + _dot(ctx_ref[...], wco_ref[...])
    y_ref[...] = _rms(x2, nf_ref[...])


def _attn_out(x1, ctx, w_co, norm_final, *, tm):
    n, d = x1.shape
    const = lambda i: (0, 0)
    return pl.pallas_call(
        _attn_out_kernel,
        grid=(n // tm,),
        in_specs=[pl.BlockSpec((tm, d), lambda i: (i, 0)), pl.BlockSpec((tm, d), lambda i: (i, 0)),
                  pl.BlockSpec((d, d), const), pl.BlockSpec((1, d), const)],
        out_specs=pl.BlockSpec((tm, d), lambda i: (i, 0)),
        out_shape=jax.ShapeDtypeStruct((n, d), F32),
        compiler_params=_cparams(("parallel",)),
        name="attn_out",
    )(x1, ctx, w_co, norm_final.reshape(1, d))


def _layer_tail(x2d, oa, ob, mk3, mv3, bsz, t_len, w_out, norm_cross, w_cq, w_co, norm_final,
                *, tm, tq):
    x1, qx = _mix_out(x2d, oa.reshape(-1, MIX_A), ob.reshape(-1, MIX_B), w_out, norm_cross, w_cq, tm=tm)
    ctx = _xattn(qx.reshape(bsz, t_len, D_MODEL), mk3, mv3, tq=tq)
    y = _attn_out(x1, ctx.reshape(-1, D_MODEL), w_co, norm_final, tm=tm)
    return y.reshape(bsz, t_len, D_MODEL)


def kernel(x_prompt, x_sample, mem_prompt, cache_mem_k, cache_mem_v, state_delta, state_conv,
           state_pool, norm_mix, w_in, conv_w, a_log, dt_bias, gdn_norm, pool_w, pool_scale,
           w_out, norm_mem, norm_cross, w_cq, w_ck, w_cv, w_co, norm_final):
    bp, tp, d = x_prompt.shape
    bs, ts, _ = x_sample.shape
    n_mem = mem_prompt.shape[1]

    wi = w_in[0]
    w_main = jnp.concatenate([wi[:, :COL_A], wi[:, COL_U:]], axis=1).astype(BF16)
    w_gate = jnp.pad(wi[:, COL_A:COL_U], ((0, 0), (0, LANES - 2 * GDN_HEADS))).astype(BF16)
    wo, wcq, wck, wcv, wco = (w[0].astype(BF16) for w in (w_out, w_cq, w_ck, w_cv, w_co))
    pw = pool_w[0].astype(BF16)
    gate_par = jnp.zeros((SUBLANES, LANES), F32)
    gate_par = gate_par.at[0, :GDN_HEADS].set(a_log[0]).at[1, :GDN_HEADS].set(dt_bias[0])
    cw = conv_w[0]

    mem2d = mem_prompt.reshape(bp * n_mem, d)
    mk = _norm_proj(mem2d, norm_mem[0], wck, tm=512, tn=1024, out_dtype=F32)
    mv = _norm_proj(mem2d, norm_mem[0], wcv, tm=512, tn=1024, out_dtype=F32)

    proj_p, ab_p = _norm_proj(x_prompt.reshape(bp * tp, d), norm_mix[0], w_main, w_gate,
                              tm=512, tn=1024, out_dtype=BF16)
    proj_p3 = proj_p.reshape(bp, tp, W_MAIN)
    oa_p, delta_p = _gdn_prompt(proj_p3, ab_p.reshape(bp, tp, LANES), cw, gate_par, gdn_norm[0])
    ob_p = _pool_prompt(proj_p3, pw, pool_scale[0], tt=512, pos0=0)
    y_p = _layer_tail(x_prompt.reshape(bp * tp, d), oa_p, ob_p, mk.reshape(bp, n_mem, d),
                      mv.reshape(bp, n_mem, d), bp, tp, wo, norm_cross[0], wcq, wco, norm_final,
                      tm=256, tq=512)
    conv_p = proj_p3[:, tp - (CONV_W - 1):, :QKV_W].astype(F32)
    pool_p = proj_p3[:, tp - POOL_BUF:, COL_ZA + MIX_A:COL_ZA + MIX_A + MIX_B].astype(F32)

    proj_s, ab_s = _norm_proj(x_sample.reshape(bs * ts, d), norm_mix[0], w_main, w_gate,
                              tm=256, tn=1024, out_dtype=BF16)
    proj_s3 = proj_s.reshape(bs, ts, W_MAIN)
    oa_s, delta_s = _gdn_sample(proj_s3, ab_s.reshape(bs, ts, LANES), state_conv[0], cw, gate_par,
                                gdn_norm[0], state_delta[0], bb=4)
    u_s = proj_s3[:, :, COL_ZA + MIX_A:COL_ZA + MIX_A + MIX_B].astype(F32)
    ext_s = jnp.concatenate([state_pool[0], u_s], axis=1)
    zb_tm = proj_s3[:, :, COL_ZA + MIX_A + MIX_B:].transpose(1, 0, 2)
    ob_s = _pool_sample(ext_s.transpose(1, 0, 2), zb_tm, pw, pool_scale[0], bb=32,
                        pos0=PAST_LEN).transpose(1, 0, 2)
    x1_s, qx_s = _mix_out(x_sample.reshape(bs * ts, d), oa_s.reshape(-1, MIX_A),
                          ob_s.reshape(-1, MIX_B), wo, norm_cross[0], wcq, tm=256)
    nj = X_HEAD_DIM // LANES
    q_rows = qx_s.reshape(bs, ts, X_HEADS, nj, LANES).transpose(0, 3, 1, 2, 4).reshape(
        bs, nj * ts * X_HEADS, LANES)
    ctx_rows = _xattn_native(q_rows, _kv_rows(cache_mem_k[0]), _kv_rows(cache_mem_v[0]),
                             t_len=ts, bb=2)
    ctx_s = ctx_rows.reshape(bs, nj, ts, X_HEADS, LANES).transpose(0, 2, 3, 1, 4).reshape(bs * ts, d)
    y_s = _attn_out(x1_s, ctx_s, wco, norm_final, tm=256).reshape(bs, ts, d)
    conv_s = jnp.concatenate([state_conv[0], proj_s3[:, :, :QKV_W].astype(F32)], axis=1)[:, ts:]
    pool_s = ext_s[:, ts:]

    hd = (X_HEADS, X_HEAD_DIM)
    return (y_p, y_s, mk.reshape(1, bp, n_mem, *hd), mv.reshape(1, bp, n_mem, *hd),
            delta_p[None], conv_p[None], pool_p[None], delta_s[None], conv_s[None], pool_s[None])
```

```python
import functools
import math

import jax
import jax.numpy as jnp
from jax import lax
from jax.experimental import pallas as pl
from jax.experimental.pallas import tpu as pltpu

F32 = jnp.float32
BF16 = jnp.bfloat16

D_MODEL = 2048
MIX_A = D_MODEL // 2
MIX_B = D_MODEL - MIX_A
GDN_HEADS = 8
GDN_DK = MIX_A // GDN_HEADS
GDN_DV = MIX_A // GDN_HEADS
QK_W = GDN_HEADS * GDN_DK
QKV_W = 2 * QK_W + GDN_HEADS * GDN_DV
CONV_W = 4
POOL_WINDOWS = (2, 4, 8, 16)
POOL_GROUPS = len(POOL_WINDOWS)
POOL_CH = MIX_B // POOL_GROUPS
POOL_BUF = max(POOL_WINDOWS) - 1
X_HEADS = 4
X_HEAD_DIM = D_MODEL // X_HEADS
PAST_LEN = 16384
EPS = 1e-6
COL_ZA = QKV_W
COL_A = COL_ZA + GDN_HEADS * GDN_DV
COL_B = COL_A + GDN_HEADS
COL_U = COL_B + GDN_HEADS
COL_ZB = COL_U + MIX_B
IN_COLS = COL_ZB + MIX_B

W_MAIN = IN_COLS - 2 * GDN_HEADS
LANES = 128
SUBLANES = 8
BF16_ROWS = 16
GDN_CHUNK = 128
POOL_HALO = 16
VMEM_LIMIT = 56 * 1024 * 1024


def _cparams(sem):
    return pltpu.CompilerParams(dimension_semantics=sem, vmem_limit_bytes=VMEM_LIMIT)


def _dot(a, b):
    return jnp.dot(a, b, preferred_element_type=F32)


def _dot_nt(a, b):
    return lax.dot_general(a, b, (((1,), (1,)), ((), ())), preferred_element_type=F32)


def _dot_tn(a, b):
    return lax.dot_general(a, b, (((0,), (0,)), ((), ())), preferred_element_type=F32)


def _rms(x, g):
    return x * lax.rsqrt(jnp.mean(x * x, axis=-1, keepdims=True) + EPS) * g


def _silu(x):
    return x * jax.nn.sigmoid(x)


def _softplus(x):
    return jnp.maximum(x, 0.0) + jnp.log1p(jnp.exp(-jnp.abs(x)))


def _repack_kernel(w_ref, nxt_ref, main_ref, gate_ref, *, first_shifted):
    j = pl.program_id(1)
    n_gate = COL_U - COL_A

    @pl.when(j < first_shifted)
    def _():
        main_ref[...] = w_ref[...].astype(main_ref.dtype)

    @pl.when(j >= first_shifted)
    def _():
        both = jnp.concatenate([w_ref[...], nxt_ref[...]], axis=1)
        main_ref[...] = both[:, n_gate:n_gate + main_ref.shape[1]].astype(main_ref.dtype)

    @pl.when(j == first_shifted)
    def _():
        head = w_ref[:, 0:LANES]
        lane = lax.broadcasted_iota(jnp.int32, head.shape, 1)
        gate_ref[...] = jnp.where(lane < n_gate, head, 0.0).astype(gate_ref.dtype)


def _repack_w_in(w, *, tk, tn):
    k_dim = w.shape[0]
    first_shifted = COL_A // tn
    per = tn // LANES
    return pl.pallas_call(
        functools.partial(_repack_kernel, first_shifted=first_shifted),
        grid=(k_dim // tk, W_MAIN // tn),
        in_specs=[pl.BlockSpec((tk, tn), lambda i, j: (i, j)),
                  pl.BlockSpec((tk, LANES), lambda i, j: (i, (j + 1) * per))],
        out_specs=[pl.BlockSpec((tk, tn), lambda i, j: (i, j)),
                   pl.BlockSpec((tk, LANES), lambda i, j: (i, 0))],
        out_shape=[jax.ShapeDtypeStruct((k_dim, W_MAIN), BF16),
                   jax.ShapeDtypeStruct((k_dim, LANES), BF16)],
        compiler_params=_cparams(("parallel", "arbitrary")),
        name="repack_w_in",
    )(w, w)


def _norm_proj_kernel(x_ref, g_ref, w_ref, *rest, with_side):
    if with_side:
        ws_ref, out_ref, side_ref, h_scr = rest
    else:
        out_ref, h_scr = rest

    @pl.when(pl.program_id(1) == 0)
    def _():
        h = _rms(x_ref[...], g_ref[...]).astype(BF16)
        h_scr[...] = h
        if with_side:
            side_ref[...] = _dot(h, ws_ref[...])

    out_ref[...] = _dot(h_scr[...], w_ref[...]).astype(out_ref.dtype)


def _norm_proj(x, g, w, w_side=None, *, tm, tn, out_dtype):
    n, d = x.shape
    ncol = w.shape[1]
    with_side = w_side is not None
    in_specs = [pl.BlockSpec((tm, d), lambda i, j: (i, 0)),
                pl.BlockSpec((1, d), lambda i, j: (0, 0)),
                pl.BlockSpec((d, tn), lambda i, j: (0, j))]
    out_specs = [pl.BlockSpec((tm, tn), lambda i, j: (i, j))]
    out_shape = [jax.ShapeDtypeStruct((n, ncol), out_dtype)]
    args = [x, g.reshape(1, d), w]
    if with_side:
        in_specs.append(pl.BlockSpec((d, LANES), lambda i, j: (0, 0)))
        out_specs.append(pl.BlockSpec((tm, LANES), lambda i, j: (i, 0)))
        out_shape.append(jax.ShapeDtypeStruct((n, LANES), F32))
        args.append(w_side)
    res = pl.pallas_call(
        functools.partial(_norm_proj_kernel, with_side=with_side),
        grid=(n // tm, ncol // tn),
        in_specs=in_specs, out_specs=out_specs, out_shape=out_shape,
        scratch_shapes=[pltpu.VMEM((tm, d), BF16)],
        compiler_params=_cparams(("parallel", "arbitrary")),
        name="norm_proj_side" if with_side else "norm_proj",
    )(*args)
    return res if with_side else res[0]


INV_BASE_LOG2 = 4
_GDN_WORK = (("k", 1, BF16), ("kb", 1, BF16), ("q", 1, BF16), ("kd", 1, BF16), ("vk", 2, BF16),
             ("dinc", 1, F32), ("low", 1, F32), ("x", 1, F32), ("p", 1, BF16), ("wu", 2, BF16))


def _split3(x):
    hi = x.astype(BF16)
    r1 = x - hi.astype(F32)
    mid = r1.astype(BF16)
    lo = (r1 - mid.astype(F32)).astype(BF16)
    return hi, mid, lo


def _gdn_prompt_kernel(q_ref, k_ref, v_ref, z_ref, ab_ref, cwq_ref, cwk_ref, cwv_ref, gp_ref,
                       gn_ref, o_ref, s_ref,
                       gc_s, gct_s, egc_s, ekd_s, beta_s, u_s, n_s, pw_s, qa_s, sv_s, *work_refs, group):
    nw = len(_GDN_WORK)
    work = [{name: ref for (name, _, _), ref in zip(_GDN_WORK, work_refs[j * nw:(j + 1) * nw])}
            for j in range(group)]
    h = pl.program_id(1)
    t_len = q_ref.shape[1]
    c = GDN_CHUNK
    n_c = t_len // c
    row = lax.broadcasted_iota(jnp.int32, (c, c), 0)
    col = lax.broadcasted_iota(jnp.int32, (c, c), 1)

    @pl.when(h == 0)
    def _():
        gp = gp_ref[...]
        a_neg = -jnp.exp(gp[0:1])
        dt_b = gp[1:2]
        tri = jnp.where(row >= col, 1.0, 0.0).astype(BF16)

        def gate_chunk(ci, carry):
            rows = pl.ds(pl.multiple_of(ci * c, c), c)
            ab = ab_ref[0, rows, :]
            hi, mid, lo = _split3(a_neg * _softplus(ab + dt_b))
            gc = _dot(tri, hi) + _dot(tri, mid) + _dot(tri, lo)
            gc_s[rows, :] = gc
            gct_s[rows, :] = gc.T
            egc_s[rows, :] = jnp.exp(gc)
            ekd_s[rows, :] = jnp.exp(gc[c - 1:c, :] - gc)
            beta_s[rows, :] = jax.nn.sigmoid(ab)
            return carry

        lax.fori_loop(0, n_c, gate_chunk, 0)

    def l2n(x):
        return x * lax.rsqrt(jnp.sum(x * x, axis=-1, keepdims=True) + EPS)

    def operands(w, ci):
        r0 = pl.multiple_of(ci * c, c)
        rp = pl.multiple_of(jnp.maximum(ci - 1, 0) * c, c)
        rows = pl.ds(r0, c)

        def conv_silu(x_ref, cw_ref):
            cur = x_ref[0, rows, :].astype(F32)
            tail = x_ref[0, pl.ds(pl.multiple_of(rp + c - BF16_ROWS, BF16_ROWS), BF16_ROWS), :]
            tail = tail.astype(F32)[BF16_ROWS - SUBLANES:]
            tail = jnp.where(ci > 0, tail, 0.0)
            ext = jnp.concatenate([tail, cur], axis=0)
            cw = cw_ref[...]
            y = cur * cw[CONV_W - 1:CONV_W]
            for s in range(1, CONV_W):
                y = y + ext[SUBLANES - s:SUBLANES - s + c] * cw[CONV_W - 1 - s:CONV_W - s]
            return _silu(y)

        def column(scr, idx):
            return jnp.sum(jnp.where(col == idx, scr[rows, :], 0.0), axis=-1, keepdims=True)

        q = l2n(conv_silu(q_ref, cwq_ref)) * (GDN_DK ** -0.5)
        k = l2n(conv_silu(k_ref, cwk_ref))
        v = conv_silu(v_ref, cwv_ref)
        gcc = column(gc_s, h)
        egc = column(egc_s, h)
        ekd = column(ekd_s, h)
        beta = column(beta_s, h + GDN_HEADS)
        grow = gct_s[pl.ds(r0 + h, 1), :]
        tri_i = row >= col
        w["dinc"][...] = jnp.where(tri_i, jnp.exp(jnp.where(tri_i, gcc - grow, 0.0)), 0.0)
        kb = k * beta
        w["k"][...] = k.astype(BF16)
        w["kb"][...] = kb.astype(BF16)
        w["q"][...] = q.astype(BF16)
        w["kd"][...] = (k * ekd).astype(BF16)
        w["vk"][:, :GDN_DV] = (v * beta).astype(BF16)
        w["vk"][:, GDN_DV:] = (kb * egc).astype(BF16)
        qa_s[ci, :, 0:GDN_DK] = (q * egc).astype(BF16)

    def gram(w, ci):
        r = _dot_nt(jnp.concatenate([w["kb"][...], w["q"][...]], axis=0), w["k"][...])
        d_incl = w["dinc"][...]
        low = r[:c] * jnp.where(row > col, d_incl, 0.0)
        qa_s[ci, :, GDN_DK:] = (r[c:] * d_incl).astype(BF16)
        w["low"][...] = low
        ld = jnp.where((row >> INV_BASE_LOG2) == (col >> INV_BASE_LOG2), low, 0.0)
        w["x"][...] = jnp.where(row == col, 1.0, 0.0) - ld
        w["p"][...] = ld.astype(BF16)

    def neumann(w, first, last):
        pb = w["p"][...]
        if first:
            w["p"][...] = _dot(pb, pb).astype(BF16)
        elif last:
            x = w["x"][...]
            w["x"][...] = x + _dot(x.astype(BF16), pb)
        else:
            x = w["x"][...]
            r = _dot(jnp.concatenate([x.astype(BF16), pb], axis=0), pb)
            w["x"][...] = x + r[:c]
            w["p"][...] = r[c:].astype(BF16)

    def merge_a(w, lg):
        pair = (row >> (lg + 1)) == (col >> (lg + 1))
        m = jnp.where(pair, (row >> lg) - (col >> lg), 0) > 0
        lm = jnp.where(m, w["low"][...], 0.0).astype(BF16)
        w["p"][...] = _dot(lm, w["x"][...].astype(BF16)).astype(BF16)

    def merge_b(w):
        x = w["x"][...]
        w["x"][...] = x - _dot(x.astype(BF16), w["p"][...])

    def solve(w, ci):
        uw = _dot(w["x"][...].astype(BF16), w["vk"][...])
        u_s[ci] = uw[:, :GDN_DV]
        uwb = uw.astype(BF16)
        w["wu"][...] = uwb
        pw_s[ci, c:2 * c, :] = uwb[:, GDN_DV:]

    def outer(w, ci):
        np_ = _dot_tn(w["kd"][...], w["wu"][...])
        n_s[ci] = np_[:, :GDN_DV]
        pw_s[ci, 0:c, :] = np_[:, GDN_DV:].astype(BF16)

    n_sq = INV_BASE_LOG2 - 1
    lgc = c.bit_length() - 1

    def prepare_group(gi, carry):
        cis = [gi * group + j for j in range(group)]
        for j in range(group):
            operands(work[j], cis[j])
        for j in range(group):
            gram(work[j], cis[j])
        for step in range(n_sq + 1):
            for j in range(group):
                neumann(work[j], step == 0, step == n_sq)
        for lg in range(INV_BASE_LOG2, lgc):
            for j in range(group):
                merge_a(work[j], lg)
            for j in range(group):
                merge_b(work[j])
        for j in range(group):
            solve(work[j], cis[j])
        for j in range(group):
            outer(work[j], cis[j])
        return carry

    lax.fori_loop(0, n_c // group, prepare_group, 0)

    gn = gn_ref[...]
    lane1 = lax.broadcasted_iota(jnp.int32, (1, LANES), 1)

    def advance(ci, s_mat):
        r0 = pl.multiple_of(ci * c, c)
        sb = s_mat.astype(BF16)
        r = _dot(pw_s[ci], sb)
        g_last = jnp.sum(jnp.where(lane1 == h, gc_s[pl.ds(r0 + c - 1, 1), :], 0.0),
                         axis=-1, keepdims=True)
        sv_s[ci, 0:c, :] = sb
        sv_s[ci, c:2 * c, :] = (u_s[ci] - r[c:2 * c]).astype(BF16)
        return s_mat * jnp.exp(g_last) - r[0:c] + n_s[ci]

    s_ref[0, 0] = lax.fori_loop(0, n_c, advance, jnp.zeros((GDN_DK, GDN_DV), F32))

    def emit_group(gi, carry):
        cis = [gi * group + j for j in range(group)]
        os_ = [_dot(qa_s[ci], sv_s[ci]) for ci in cis]
        for ci, o in zip(cis, os_):
            rows = pl.ds(pl.multiple_of(ci * c, c), c)
            zz = z_ref[0, rows, :].astype(F32)
            o_ref[0, rows, :] = (_rms(o, gn) * _silu(zz)).astype(o_ref.dtype)
        return carry

    lax.fori_loop(0, n_c // group, emit_group, 0)


def _gdn_prompt(proj3, ab3, conv_w, gate_par, gdn_norm):
    bsz, t_len, _ = proj3.shape
    nh = GDN_HEADS
    blk = lambda off: pl.BlockSpec((1, t_len, LANES), lambda b, h: (b, 0, off + h))
    cw = lambda off: pl.BlockSpec((CONV_W, LANES), lambda b, h: (0, off + h))
    c = GDN_CHUNK
    n_c = t_len // c
    gate_scr = pltpu.VMEM((t_len, LANES), F32)
    group = 8
    work = [pltpu.VMEM((c, wide * LANES), dt) for _ in range(group) for _, wide, dt in _GDN_WORK]
    return pl.pallas_call(
        functools.partial(_gdn_prompt_kernel, group=group),
        grid=(bsz, nh),
        in_specs=[blk(0), blk(nh), blk(2 * nh), blk(3 * nh),
                  pl.BlockSpec((1, t_len, LANES), lambda b, h: (b, 0, 0)),
                  cw(0), cw(nh), cw(2 * nh),
                  pl.BlockSpec((SUBLANES, LANES), lambda b, h: (0, 0)),
                  pl.BlockSpec((1, LANES), lambda b, h: (0, 0))],
        out_specs=[pl.BlockSpec((1, t_len, LANES), lambda b, h: (b, 0, h)),
                   pl.BlockSpec((1, 1, GDN_DK, GDN_DV), lambda b, h: (b, h, 0, 0))],
        out_shape=[jax.ShapeDtypeStruct((bsz, t_len, MIX_A), BF16),
                   jax.ShapeDtypeStruct((bsz, nh, GDN_DK, GDN_DV), F32)],
        scratch_shapes=[gate_scr, gate_scr, gate_scr, gate_scr, gate_scr,
                        pltpu.VMEM((n_c, c, GDN_DV), F32), pltpu.VMEM((n_c, GDN_DK, GDN_DV), F32),
                        pltpu.VMEM((n_c, 2 * c, GDN_DK), BF16), pltpu.VMEM((n_c, c, GDN_DK + c), BF16),
                        pltpu.VMEM((n_c, GDN_DK + c, GDN_DV), BF16)] + work,
        compiler_params=_cparams(("parallel", "arbitrary")),
        name="gdn_prompt",
    )(proj3, proj3, proj3, proj3, ab3, conv_w, conv_w, conv_w, gate_par, gdn_norm.reshape(1, LANES))


def _gdn_sample_kernel(x_ref, ab_ref, hist_ref, cw_ref, gp_ref, gn_ref, s0_ref, o_ref, s_ref,
                       *, t_len):
    nh = GDN_HEADS
    g8 = SUBLANES
    n = nh * g8
    assert t_len + CONV_W - 1 <= g8
    cw = cw_ref[...]
    gp = gp_ref[...]
    gn = gn_ref[...]
    row8 = lax.broadcasted_iota(jnp.int32, (g8, QKV_W), 0)
    tpos = lax.broadcasted_iota(jnp.int32, (n, 1), 0) & (g8 - 1)
    rhead = lax.broadcasted_iota(jnp.int32, (n, GDN_DV), 0) >> 3
    ri = lax.broadcasted_iota(jnp.int32, (n, n), 0)
    ci = lax.broadcasted_iota(jnp.int32, (n, n), 1)
    tri = jnp.where((ri >> 3) == (ci >> 3), ri - ci, -1)
    eye = jnp.where(ri == ci, 1.0, 0.0)
    sel0 = jnp.where(lax.broadcasted_iota(jnp.int32, (n, LANES), 1) == 0, 1.0, 0.0).astype(BF16)
    valid = tpos < t_len
    a_neg = jnp.concatenate([jnp.broadcast_to(-jnp.exp(gp[0:1, h:h + 1]), (g8, 1)) for h in range(nh)], 0)
    dt_b = jnp.concatenate([jnp.broadcast_to(gp[1:2, h:h + 1], (g8, 1)) for h in range(nh)], 0)

    def l2n(x):
        return x * lax.rsqrt(jnp.sum(x * x, axis=-1, keepdims=True) + EPS)

    for b in range(x_ref.shape[0]):
        xb = x_ref[b]
        x = xb[:, :QKV_W].astype(F32)
        hst = hist_ref[b]
        y = x * cw[CONV_W - 1:CONV_W]
        for s in range(1, CONV_W):
            y = y + pltpu.roll(x, s, axis=0) * cw[CONV_W - 1 - s:CONV_W - s]
        for j in range(CONV_W - 1):
            hj = hst if j == 0 else pltpu.roll(hst, g8 - j, axis=0)
            y = y + jnp.where(row8 + j < CONV_W - 1, hj, 0.0) * cw[j:j + 1]
        qkv = jnp.where(row8 < t_len, _silu(y), 0.0)

        def stack(off):
            return jnp.concatenate([qkv[:, off + h * GDN_DK:off + (h + 1) * GDN_DK] for h in range(nh)], 0)

        q = l2n(stack(0)) * (GDN_DK ** -0.5)
        k = l2n(stack(QK_W))
        v = stack(2 * QK_W)
        z = jnp.concatenate([xb[:, 3 * QK_W + h * GDN_DV:3 * QK_W + (h + 1) * GDN_DV].astype(F32)
                             for h in range(nh)], 0)
        ab = ab_ref[b]
        alpha = jnp.concatenate([ab[:, h:h + 1] for h in range(nh)], 0)
        braw = jnp.concatenate([ab[:, nh + h:nh + h + 1] for h in range(nh)], 0)
        g = jnp.where(valid, a_neg * _softplus(alpha + dt_b), 0.0)
        beta = jnp.where(valid, jax.nn.sigmoid(braw), 0.0)
        gc = g
        s = 1
        while s < g8:
            gc = gc + jnp.where(tpos >= s, pltpu.roll(gc, s, axis=0), 0.0)
            s *= 2
        g_last = jnp.concatenate([jnp.broadcast_to(gc[h * g8 + g8 - 1:(h + 1) * g8, :], (g8, 1))
                                  for h in range(nh)], 0)
        xg = jnp.broadcast_to(gc, (n, LANES))
        hi = xg.astype(BF16)
        r1 = xg - hi.astype(F32)
        mid = r1.astype(BF16)
        lo = (r1 - mid.astype(F32)).astype(BF16)
        grow = _dot_nt(sel0, hi) + _dot_nt(sel0, mid) + _dot_nt(sel0, lo)
        d_incl = jnp.where(tri >= 0, jnp.exp(jnp.where(tri >= 0, gc - grow, 0.0)), 0.0)
        d_strict = jnp.where(tri > 0, d_incl, 0.0)
        egc = jnp.exp(gc)
        kb = k * beta
        kbf = k.astype(BF16)
        low = _dot_nt(kb.astype(BF16), kbf) * d_strict
        a_in = _dot_nt(q.astype(BF16), kbf) * d_incl
        t_inv = eye - low
        p = low
        covered = 2
        while covered < t_len:
            pb = p.astype(BF16)
            p = _dot(pb, pb)
            t_inv = t_inv + _dot(t_inv.astype(BF16), p.astype(BF16))
            covered *= 2
        t_inv = t_inv.astype(BF16)
        u = _dot(t_inv, (v * beta).astype(BF16))
        w = _dot(t_inv, (kb * egc).astype(BF16))
        qg = q * egc
        k_dec = (k * jnp.exp(g_last - gc)).astype(BF16)
        ws, qs = [], []
        for h in range(nh):
            hs = slice(h * g8, (h + 1) * g8)
            lhs = jnp.concatenate([w[hs], qg[hs]], 0).astype(BF16)
            r = _dot(lhs, s0_ref[b, h].astype(BF16))
            ws.append(r[:g8])
            qs.append(r[g8:])
        v_new = u - jnp.concatenate(ws, 0)
        o = jnp.concatenate(qs, 0) + _dot(a_in.astype(BF16), v_new.astype(BF16))
        for h in range(nh):
            vm = jnp.where(rhead == h, v_new, 0.0).astype(BF16)
            dec = jnp.exp(g_last[h * g8:h * g8 + 1, :])
            s_ref[b, h] = s0_ref[b, h] * dec + _dot_tn(k_dec, vm)
        o_ref[b] = (_rms(o, gn) * _silu(z)).astype(o_ref.dtype)


def _gdn_sample(proj3, ab3, hist, conv_w, gate_par, gdn_norm, s0, *, bb):
    bsz, t_len, _ = proj3.shape
    nh, g8 = GDN_HEADS, SUBLANES
    pad_t = lambda a, rows: jnp.pad(a, ((0, 0), (0, g8 - rows), (0, 0)))
    x8 = pad_t(proj3[:, :, :4 * QK_W], t_len)
    ab8 = pad_t(ab3, t_len)
    hist8 = pad_t(hist, CONV_W - 1)
    o, s_new = pl.pallas_call(
        functools.partial(_gdn_sample_kernel, t_len=t_len),
        grid=(bsz // bb,),
        in_specs=[pl.BlockSpec((bb, g8, 4 * QK_W), lambda i: (i, 0, 0)),
                  pl.BlockSpec((bb, g8, LANES), lambda i: (i, 0, 0)),
                  pl.BlockSpec((bb, g8, QKV_W), lambda i: (i, 0, 0)),
                  pl.BlockSpec((CONV_W, QKV_W), lambda i: (0, 0)),
                  pl.BlockSpec((SUBLANES, LANES), lambda i: (0, 0)),
                  pl.BlockSpec((1, LANES), lambda i: (0, 0)),
                  pl.BlockSpec((bb, nh, GDN_DK, GDN_DV), lambda i: (i, 0, 0, 0))],
        out_specs=[pl.BlockSpec((bb, nh * g8, GDN_DV), lambda i: (i, 0, 0)),
                   pl.BlockSpec((bb, nh, GDN_DK, GDN_DV), lambda i: (i, 0, 0, 0))],
        out_shape=[jax.ShapeDtypeStruct((bsz, nh * g8, GDN_DV), BF16),
                   jax.ShapeDtypeStruct((bsz, nh, GDN_DK, GDN_DV), F32)],
        compiler_params=_cparams(("parallel",)),
        name="gdn_sample",
    )(x8, ab8, hist8, conv_w, gate_par, gdn_norm.reshape(1, LANES), s0)
    o = o.reshape(bsz, nh, g8, GDN_DV)[:, :, :t_len].transpose(0, 2, 1, 3)
    return o.reshape(bsz, t_len, MIX_A), s_new


def _pool_finish(d_groups, pw_ref, ps_ref, zb):
    outs = [_dot(d.astype(BF16), pw_ref[gi]) for gi, d in enumerate(d_groups)]
    return jnp.concatenate(outs, axis=-1) * ps_ref[...] * _silu(zb)


def _pool_prompt_kernel(u_ref, halo_ref, zb_ref, pw_ref, ps_ref, o_ref, *, pos0):
    i = pl.program_id(1)
    tt = u_ref.shape[1]
    u = u_ref[0].astype(F32)
    halo = jnp.where(i > 0, halo_ref[0].astype(F32), 0.0)
    ext = jnp.concatenate([halo, u], axis=0)
    pos = pos0 + i * tt + lax.broadcasted_iota(jnp.int32, (tt, 1), 0)
    d_groups = []
    for gi, win in enumerate(POOL_WINDOWS):
        sl = slice(gi * POOL_CH, (gi + 1) * POOL_CH)
        lvl = ext[:, sl]
        s = 1
        while s < win:
            lvl = lvl + pltpu.roll(lvl, s, axis=0)
            s *= 2
        cnt = jnp.minimum(win, pos + 1).astype(F32)
        d_groups.append(lvl[POOL_HALO:, :] / cnt - u[:, sl])
    o_ref[0] = _pool_finish(d_groups, pw_ref, ps_ref, zb_ref[0].astype(F32)).astype(o_ref.dtype)


def _pool_prompt(proj3, pool_w, pool_scale, *, tt, pos0):
    bsz, t_len, _ = proj3.shape
    ub, zb = COL_ZA // MIX_B + 1, COL_ZA // MIX_B + 2
    hb = tt // POOL_HALO
    return pl.pallas_call(
        functools.partial(_pool_prompt_kernel, pos0=pos0),
        grid=(bsz, t_len // tt),
        in_specs=[pl.BlockSpec((1, tt, MIX_B), lambda b, i: (b, i, ub)),
                  pl.BlockSpec((1, POOL_HALO, MIX_B), lambda b, i: (b, jnp.maximum(i * hb - 1, 0), ub)),
                  pl.BlockSpec((1, tt, MIX_B), lambda b, i: (b, i, zb)),
                  pl.BlockSpec((POOL_GROUPS, POOL_CH, POOL_CH), lambda b, i: (0, 0, 0)),
                  pl.BlockSpec((1, MIX_B), lambda b, i: (0, 0))],
        out_specs=pl.BlockSpec((1, tt, MIX_B), lambda b, i: (b, i, 0)),
        out_shape=jax.ShapeDtypeStruct((bsz, t_len, MIX_B), BF16),
        compiler_params=_cparams(("parallel", "arbitrary")),
        name="pool_prompt",
    )(proj3, proj3, proj3, pool_w, pool_scale.reshape(1, MIX_B))


def _pool_sample_kernel(ext_ref, zb_ref, pw_ref, ps_ref, o_ref, *, pos0):
    t_len = zb_ref.shape[0]
    for t in range(t_len):
        e = POOL_BUF + t
        d_groups = []
        for gi, win in enumerate(POOL_WINDOWS):
            sl = slice(gi * POOL_CH, (gi + 1) * POOL_CH)
            tot = ext_ref[e, :, sl]
            for j in range(1, win):
                tot = tot + ext_ref[e - j, :, sl]
            cnt = float(min(win, pos0 + t + 1))
            d_groups.append(tot / cnt - ext_ref[e, :, sl])
        o_ref[t] = _pool_finish(d_groups, pw_ref, ps_ref, zb_ref[t].astype(F32)).astype(o_ref.dtype)


def _pool_sample(ext_tm, zb_tm, pool_w, pool_scale, *, bb, pos0):
    t_len, bsz, _ = zb_tm.shape
    return pl.pallas_call(
        functools.partial(_pool_sample_kernel, pos0=pos0),
        grid=(bsz // bb,),
        in_specs=[pl.BlockSpec((POOL_BUF + t_len, bb, MIX_B), lambda i: (0, i, 0)),
                  pl.BlockSpec((t_len, bb, MIX_B), lambda i: (0, i, 0)),
                  pl.BlockSpec((POOL_GROUPS, POOL_CH, POOL_CH), lambda i: (0, 0, 0)),
                  pl.BlockSpec((1, MIX_B), lambda i: (0, 0))],
        out_specs=pl.BlockSpec((t_len, bb, MIX_B), lambda i: (0, i, 0)),
        out_shape=jax.ShapeDtypeStruct((t_len, bsz, MIX_B), BF16),
        compiler_params=_cparams(("parallel",)),
        name="pool_sample",
    )(ext_tm, zb_tm, pool_w, pool_scale.reshape(1, MIX_B))


def _mix_out_kernel(x_ref, oa_ref, ob_ref, wo_ref, nc_ref, wq_ref, x1_ref, qx_ref):
    acc = _dot(oa_ref[...], wo_ref[0:MIX_A, :]) + _dot(ob_ref[...], wo_ref[MIX_A:D_MODEL, :])
    x1 = x_ref[...] + acc
    x1_ref[...] = x1
    qx_ref[...] = _dot(_rms(x1, nc_ref[...]).astype(BF16), wq_ref[...]).astype(qx_ref.dtype)


def _mix_out(x, oa, ob, w_out, norm_cross, w_cq, *, tm):
    n, d = x.shape
    const = lambda i: (0, 0)
    return pl.pallas_call(
        _mix_out_kernel,
        grid=(n // tm,),
        in_specs=[pl.BlockSpec((tm, d), lambda i: (i, 0)),
                  pl.BlockSpec((tm, MIX_A), lambda i: (i, 0)),
                  pl.BlockSpec((tm, MIX_B), lambda i: (i, 0)),
                  pl.BlockSpec((d, d), const), pl.BlockSpec((1, d), const), pl.BlockSpec((d, d), const)],
        out_specs=[pl.BlockSpec((tm, d), lambda i: (i, 0)), pl.BlockSpec((tm, d), lambda i: (i, 0))],
        out_shape=[jax.ShapeDtypeStruct((n, d), F32), jax.ShapeDtypeStruct((n, d), BF16)],
        compiler_params=_cparams(("parallel",)),
        name="mix_out",
    )(x, oa, ob, w_out, norm_cross.reshape(1, d), w_cq)


def _xattn_kernel(q_ref, k_ref, v_ref, ctx_ref):
    q = q_ref[0]
    scale = X_HEAD_DIM ** -0.5
    for h in range(X_HEADS):
        sl = slice(h * X_HEAD_DIM, (h + 1) * X_HEAD_DIM)
        s = _dot_nt(q[:, sl], k_ref[0, :, sl].astype(BF16)) * scale
        p = jnp.exp(s - jnp.max(s, axis=-1, keepdims=True))
        p = p / jnp.sum(p, axis=-1, keepdims=True)
        ctx_ref[0, :, sl] = _dot(p.astype(BF16), v_ref[0, :, sl].astype(BF16)).astype(ctx_ref.dtype)


def _xattn(qx3, mk3, mv3, *, tq):
    bsz, t_len, d = qx3.shape
    n_mem = mk3.shape[1]
    return pl.pallas_call(
        _xattn_kernel,
        grid=(bsz, t_len // tq),
        in_specs=[pl.BlockSpec((1, tq, d), lambda b, i: (b, i, 0)),
                  pl.BlockSpec((1, n_mem, d), lambda b, i: (b, 0, 0)),
                  pl.BlockSpec((1, n_mem, d), lambda b, i: (b, 0, 0))],
        out_specs=pl.BlockSpec((1, tq, d), lambda b, i: (b, i, 0)),
        out_shape=jax.ShapeDtypeStruct((bsz, t_len, d), BF16),
        compiler_params=_cparams(("parallel", "arbitrary")),
        name="xattn",
    )(qx3, mk3, mv3)


def _xattn_native_kernel(q_ref, k_ref, v_ref, o_ref, *, t_len):
    nj = X_HEAD_DIM // LANES
    grp = nj * X_HEADS
    th = t_len * X_HEADS
    scale = X_HEAD_DIM ** -0.5
    for b in range(q_ref.shape[0]):
        z = _dot_nt(q_ref[b], k_ref[b].astype(BF16))
        ncol = z.shape[1]
        r = lax.broadcasted_iota(jnp.int32, (th, ncol), 0) & (X_HEADS - 1)
        c = lax.broadcasted_iota(jnp.int32, (th, ncol), 1) & (grp - 1)
        s = None
        for j in range(nj):
            zj = jnp.where(c == r + j * X_HEADS, z[j * th:(j + 1) * th, :], 0.0)
            if j:
                zj = pltpu.roll(zj, ncol - j * X_HEADS, axis=1)
            s = zj if s is None else s + zj
        sm = jnp.where(c == r, s * scale, -jnp.inf)
        p = jnp.exp(sm - jnp.max(sm, axis=1, keepdims=True))
        p = p / jnp.sum(p, axis=1, keepdims=True)
        pp = jnp.concatenate([p if j == 0 else pltpu.roll(p, j * X_HEADS, axis=1)
                              for j in range(nj)], axis=0).astype(BF16)
        o_ref[b] = _dot(pp, v_ref[b].astype(BF16)).astype(o_ref.dtype)


def _xattn_native(q_rows, k_rows, v_rows, *, t_len, bb):
    bsz, nq, _ = q_rows.shape
    nk = k_rows.shape[1]
    return pl.pallas_call(
        functools.partial(_xattn_native_kernel, t_len=t_len),
        grid=(bsz // bb,),
        in_specs=[pl.BlockSpec((bb, nq, LANES), lambda i: (i, 0, 0)),
                  pl.BlockSpec((bb, nk, LANES), lambda i: (i, 0, 0)),
                  pl.BlockSpec((bb, nk, LANES), lambda i: (i, 0, 0))],
        out_specs=pl.BlockSpec((bb, nq, LANES), lambda i: (i, 0, 0)),
        out_shape=jax.ShapeDtypeStruct((bsz, nq, LANES), BF16),
        compiler_params=_cparams(("parallel",)),
        name="xattn_native",
    )(q_rows, k_rows, v_rows)


def _kv_rows(cache):
    bsz, n_mem, nh, dh = cache.shape
    nj = dh // LANES
    return cache.reshape(bsz, n_mem, nh, nj, LANES).transpose(0, 1, 3, 2, 4).reshape(
        bsz, n_mem * nj * nh, LANES)


def _attn_out_kernel(x1_ref, ctx_ref, wco_ref, nf_ref, y_ref):
    x2 = x1_ref[...] + _dot(ctx_ref[...], wco_ref[...])
    y_ref[...] = _rms(x2, nf_ref[...])


def _attn_out(x1, ctx, w_co, norm_final, *, tm):
    n, d = x1.shape
    const = lambda i: (0, 0)
    return pl.pallas_call(
        _attn_out_kernel,
        grid=(n // tm,),
        in_specs=[pl.BlockSpec((tm, d), lambda i: (i, 0)), pl.BlockSpec((tm, d), lambda i: (i, 0)),
                  pl.BlockSpec((d, d), const), pl.BlockSpec((1, d), const)],
        out_specs=pl.BlockSpec((tm, d), lambda i: (i, 0)),
        out_shape=jax.ShapeDtypeStruct((n, d), F32),
        compiler_params=_cparams(("parallel",)),
        name="attn_out",
    )(x1, ctx, w_co, norm_final.reshape(1, d))


def _layer_tail(x2d, oa, ob, mk3, mv3, bsz, t_len, w_out, norm_cross, w_cq, w_co, norm_final,
                *, tm, tq):
    x1, qx = _mix_out(x2d, oa.reshape(-1, MIX_A), ob.reshape(-1, MIX_B), w_out, norm_cross, w_cq, tm=tm)
    ctx = _xattn(qx.reshape(bsz, t_len, D_MODEL), mk3, mv3, tq=tq)
    y = _attn_out(x1, ctx.reshape(-1, D_MODEL), w_co, norm_final, tm=tm)
    return y.reshape(bsz, t_len, D_MODEL)


def kernel(x_prompt, x_sample, mem_prompt, cache_mem_k, cache_mem_v, state_delta, state_conv,
           state_pool, norm_mix, w_in, conv_w, a_log, dt_bias, gdn_norm, pool_w, pool_scale,
           w_out, norm_mem, norm_cross, w_cq, w_ck, w_cv, w_co, norm_final):
    bp, tp, d = x_prompt.shape
    bs, ts, _ = x_sample.shape
    n_mem = mem_prompt.shape[1]

    w_main, w_gate = _repack_w_in(w_in[0], tk=512, tn=1024)
    wo, wcq, wck, wcv, wco = (w[0].astype(BF16) for w in (w_out, w_cq, w_ck, w_cv, w_co))
    pw = pool_w[0].astype(BF16)
    gate_par = jnp.zeros((SUBLANES, LANES), F32)
    gate_par = gate_par.at[0, :GDN_HEADS].set(a_log[0]).at[1, :GDN_HEADS].set(dt_bias[0])
    cw = conv_w[0]

    mem2d = mem_prompt.reshape(bp * n_mem, d)
    mk = _norm_proj(mem2d, norm_mem[0], wck, tm=512, tn=1024, out_dtype=F32)
    mv = _norm_proj(mem2d, norm_mem[0], wcv, tm=512, tn=1024, out_dtype=F32)

    proj_p, ab_p = _norm_proj(x_prompt.reshape(bp * tp, d), norm_mix[0], w_main, w_gate,
                              tm=1024, tn=1024, out_dtype=BF16)
    proj_p3 = proj_p.reshape(bp, tp, W_MAIN)
    oa_p, delta_p = _gdn_prompt(proj_p3, ab_p.reshape(bp, tp, LANES), cw, gate_par, gdn_norm[0])
    ob_p = _pool_prompt(proj_p3, pw, pool_scale[0], tt=512, pos0=0)
    y_p = _layer_tail(x_prompt.reshape(bp * tp, d), oa_p, ob_p, mk.reshape(bp, n_mem, d),
                      mv.reshape(bp, n_mem, d), bp, tp, wo, norm_cross[0], wcq, wco, norm_final,
                      tm=256, tq=512)
    conv_p = proj_p3[:, tp - (CONV_W - 1):, :QKV_W].astype(F32)
    pool_p = proj_p3[:, tp - POOL_BUF:, COL_ZA + MIX_A:COL_ZA + MIX_A + MIX_B].astype(F32)

    proj_s, ab_s = _norm_proj(x_sample.reshape(bs * ts, d), norm_mix[0], w_main, w_gate,
                              tm=256, tn=1024, out_dtype=BF16)
    proj_s3 = proj_s.reshape(bs, ts, W_MAIN)
    oa_s, delta_s = _gdn_sample(proj_s3, ab_s.reshape(bs, ts, LANES), state_conv[0], cw, gate_par,
                                gdn_norm[0], state_delta[0], bb=4)
    u_s = proj_s3[:, :, COL_ZA + MIX_A:COL_ZA + MIX_A + MIX_B].astype(F32)
    ext_s = jnp.concatenate([state_pool[0], u_s], axis=1)
    zb_tm = proj_s3[:, :, COL_ZA + MIX_A + MIX_B:].transpose(1, 0, 2)
    ob_s = _pool_sample(ext_s.transpose(1, 0, 2), zb_tm, pw, pool_scale[0], bb=32,
                        pos0=PAST_LEN).transpose(1, 0, 2)
    x1_s, qx_s = _mix_out(x_sample.reshape(bs * ts, d), oa_s.reshape(-1, MIX_A),
                          ob_s.reshape(-1, MIX_B), wo, norm_cross[0], wcq, tm=256)
    nj = X_HEAD_DIM // LANES
    q_rows = qx_s.reshape(bs, ts, X_HEADS, nj, LANES).transpose(0, 3, 1, 2, 4).reshape(
        bs, nj * ts * X_HEADS, LANES)
    ctx_rows = _xattn_native(q_rows, _kv_rows(cache_mem_k[0]), _kv_rows(cache_mem_v[0]),
                             t_len=ts, bb=2)
    ctx_s = ctx_rows.reshape(bs, nj, ts, X_HEADS, LANES).transpose(0, 2, 3, 1, 4).reshape(bs * ts, d)
    y_s = _attn_out(x1_s, ctx_s, wco, norm_final, tm=256).reshape(bs, ts, d)
    conv_s = jnp.concatenate([state_conv[0], proj_s3[:, :, :QKV_W].astype(F32)], axis=1)[:, ts:]
    pool_s = ext_s[:, ts:]

    hd = (X_HEADS, X_HEAD_DIM)
    return (y_p, y_s, mk.reshape(1, bp, n_mem, *hd), mv.reshape(1, bp, n_mem, *hd),
            delta_p[None], conv_p[None], pool_p[None], delta_s[None], conv_s[None], pool_s[None])
```

```python
import functools
import math

import jax
import jax.numpy as jnp
from jax import lax
from jax.experimental import pallas as pl
from jax.experimental.pallas import tpu as pltpu

F32 = jnp.float32
BF16 = jnp.bfloat16

D_MODEL = 2048
MIX_A = D_MODEL // 2
MIX_B = D_MODEL - MIX_A
GDN_HEADS = 8
GDN_DK = MIX_A // GDN_HEADS
GDN_DV = MIX_A // GDN_HEADS
QK_W = GDN_HEADS * GDN_DK
QKV_W = 2 * QK_W + GDN_HEADS * GDN_DV
CONV_W = 4
POOL_WINDOWS = (2, 4, 8, 16)
POOL_GROUPS = len(POOL_WINDOWS)
POOL_CH = MIX_B // POOL_GROUPS
POOL_BUF = max(POOL_WINDOWS) - 1
X_HEADS = 4
X_HEAD_DIM = D_MODEL // X_HEADS
PAST_LEN = 16384
EPS = 1e-6
COL_ZA = QKV_W
COL_A = COL_ZA + GDN_HEADS * GDN_DV
COL_B = COL_A + GDN_HEADS
COL_U = COL_B + GDN_HEADS
COL_ZB = COL_U + MIX_B
IN_COLS = COL_ZB + MIX_B

W_MAIN = IN_COLS - 2 * GDN_HEADS
LANES = 128
SUBLANES = 8
BF16_ROWS = 16
GDN_CHUNK = 128
POOL_HALO = 16
VMEM_LIMIT = 56 * 1024 * 1024


def _cparams(sem):
    return pltpu.CompilerParams(dimension_semantics=sem, vmem_limit_bytes=VMEM_LIMIT)


def _dot(a, b):
    return jnp.dot(a, b, preferred_element_type=F32)


def _dot_nt(a, b):
    return lax.dot_general(a, b, (((1,), (1,)), ((), ())), preferred_element_type=F32)


def _dot_tn(a, b):
    return lax.dot_general(a, b, (((0,), (0,)), ((), ())), preferred_element_type=F32)


def _rms(x, g):
    return x * lax.rsqrt(jnp.mean(x * x, axis=-1, keepdims=True) + EPS) * g


def _silu(x):
    return x * jax.nn.sigmoid(x)


def _softplus(x):
    return jnp.maximum(x, 0.0) + jnp.log1p(jnp.exp(-jnp.abs(x)))


def _repack_kernel(w_ref, nxt_ref, main_ref, gate_ref, *, first_shifted):
    j = pl.program_id(1)
    n_gate = COL_U - COL_A

    @pl.when(j < first_shifted)
    def _():
        main_ref[...] = w_ref[...].astype(main_ref.dtype)

    @pl.when(j >= first_shifted)
    def _():
        both = jnp.concatenate([w_ref[...], nxt_ref[...]], axis=1)
        main_ref[...] = both[:, n_gate:n_gate + main_ref.shape[1]].astype(main_ref.dtype)

    @pl.when(j == first_shifted)
    def _():
        head = w_ref[:, 0:LANES]
        lane = lax.broadcasted_iota(jnp.int32, head.shape, 1)
        gate_ref[...] = jnp.where(lane < n_gate, head, 0.0).astype(gate_ref.dtype)


def _repack_w_in(w, *, tk, tn):
    k_dim = w.shape[0]
    first_shifted = COL_A // tn
    per = tn // LANES
    return pl.pallas_call(
        functools.partial(_repack_kernel, first_shifted=first_shifted),
        grid=(k_dim // tk, W_MAIN // tn),
        in_specs=[pl.BlockSpec((tk, tn), lambda i, j: (i, j)),
                  pl.BlockSpec((tk, LANES), lambda i, j: (i, (j + 1) * per))],
        out_specs=[pl.BlockSpec((tk, tn), lambda i, j: (i, j)),
                   pl.BlockSpec((tk, LANES), lambda i, j: (i, 0))],
        out_shape=[jax.ShapeDtypeStruct((k_dim, W_MAIN), BF16),
                   jax.ShapeDtypeStruct((k_dim, LANES), BF16)],
        compiler_params=_cparams(("parallel", "arbitrary")),
        name="repack_w_in",
    )(w, w)


def _norm_proj_kernel(x_ref, g_ref, w_ref, *rest, with_side):
    if with_side:
        ws_ref, out_ref, side_ref, h_scr = rest
    else:
        out_ref, h_scr = rest

    @pl.when(pl.program_id(1) == 0)
    def _():
        h = _rms(x_ref[...], g_ref[...]).astype(BF16)
        h_scr[...] = h
        if with_side:
            side_ref[...] = _dot(h, ws_ref[...])

    out_ref[...] = _dot(h_scr[...], w_ref[...]).astype(out_ref.dtype)


def _norm_proj(x, g, w, w_side=None, *, tm, tn, out_dtype):
    n, d = x.shape
    ncol = w.shape[1]
    with_side = w_side is not None
    in_specs = [pl.BlockSpec((tm, d), lambda i, j: (i, 0)),
                pl.BlockSpec((1, d), lambda i, j: (0, 0)),
                pl.BlockSpec((d, tn), lambda i, j: (0, j))]
    out_specs = [pl.BlockSpec((tm, tn), lambda i, j: (i, j))]
    out_shape = [jax.ShapeDtypeStruct((n, ncol), out_dtype)]
    args = [x, g.reshape(1, d), w]
    if with_side:
        in_specs.append(pl.BlockSpec((d, LANES), lambda i, j: (0, 0)))
        out_specs.append(pl.BlockSpec((tm, LANES), lambda i, j: (i, 0)))
        out_shape.append(jax.ShapeDtypeStruct((n, LANES), F32))
        args.append(w_side)
    res = pl.pallas_call(
        functools.partial(_norm_proj_kernel, with_side=with_side),
        grid=(n // tm, ncol // tn),
        in_specs=in_specs, out_specs=out_specs, out_shape=out_shape,
        scratch_shapes=[pltpu.VMEM((tm, d), BF16)],
        compiler_params=_cparams(("parallel", "arbitrary")),
        name="norm_proj_side" if with_side else "norm_proj",
    )(*args)
    return res if with_side else res[0]


INV_BASE_LOG2 = 4
_GDN_WORK = (("k", 1, BF16), ("kb", 1, BF16), ("q", 1, BF16), ("kd", 1, BF16), ("vk", 2, BF16),
             ("dinc", 1, F32), ("low", 1, F32), ("x", 1, F32), ("p", 1, BF16), ("wu", 2, BF16))


def _split3(x):
    hi = x.astype(BF16)
    r1 = x - hi.astype(F32)
    mid = r1.astype(BF16)
    lo = (r1 - mid.astype(F32)).astype(BF16)
    return hi, mid, lo


def _gdn_prompt_kernel(q_ref, k_ref, v_ref, z_ref, ab_ref, cwq_ref, cwk_ref, cwv_ref, gp_ref,
                       gn_ref, o_ref, s_ref,
                       gc_s, gct_s, egc_s, ekd_s, beta_s, u_s, n_s, pw_s, qa_s, sv_s, *work_refs, group):
    nw = len(_GDN_WORK)
    work = [{name: ref for (name, _, _), ref in zip(_GDN_WORK, work_refs[j * nw:(j + 1) * nw])}
            for j in range(len(work_refs) // nw)]
    h = pl.program_id(1)
    t_len = q_ref.shape[1]
    c = GDN_CHUNK
    n_c = t_len // c
    row = lax.broadcasted_iota(jnp.int32, (c, c), 0)
    col = lax.broadcasted_iota(jnp.int32, (c, c), 1)

    @pl.when(h == 0)
    def _():
        gp = gp_ref[...]
        a_neg = -jnp.exp(gp[0:1])
        dt_b = gp[1:2]
        tri = jnp.where(row >= col, 1.0, 0.0).astype(BF16)

        def gate_chunk(ci, carry):
            rows = pl.ds(pl.multiple_of(ci * c, c), c)
            ab = ab_ref[0, rows, :]
            hi, mid, lo = _split3(a_neg * _softplus(ab + dt_b))
            gc = _dot(tri, hi) + _dot(tri, mid) + _dot(tri, lo)
            gc_s[rows, :] = gc
            gct_s[rows, :] = gc.T
            egc_s[rows, :] = jnp.exp(gc)
            ekd_s[rows, :] = jnp.exp(gc[c - 1:c, :] - gc)
            beta_s[rows, :] = jax.nn.sigmoid(ab)
            return carry

        lax.fori_loop(0, n_c, gate_chunk, 0)

    def l2n(x):
        return x * lax.rsqrt(jnp.sum(x * x, axis=-1, keepdims=True) + EPS)

    def operands(w, ci):
        r0 = ci * c
        rows = pl.ds(r0, c)

        def conv_silu(x_ref, cw_ref):
            cur = x_ref[0, rows, :].astype(F32)
            if ci > 0:
                tail = x_ref[0, pl.ds(r0 - BF16_ROWS, BF16_ROWS), :].astype(F32)[BF16_ROWS - SUBLANES:]
            else:
                tail = jnp.zeros((SUBLANES, cur.shape[1]), F32)
            ext = jnp.concatenate([tail, cur], axis=0)
            cw = cw_ref[...]
            y = cur * cw[CONV_W - 1:CONV_W]
            for s in range(1, CONV_W):
                y = y + ext[SUBLANES - s:SUBLANES - s + c] * cw[CONV_W - 1 - s:CONV_W - s]
            return _silu(y)

        def column(scr, idx):
            return jnp.sum(jnp.where(col == idx, scr[rows, :], 0.0), axis=-1, keepdims=True)

        q = l2n(conv_silu(q_ref, cwq_ref)) * (GDN_DK ** -0.5)
        k = l2n(conv_silu(k_ref, cwk_ref))
        v = conv_silu(v_ref, cwv_ref)
        gcc = column(gc_s, h)
        egc = column(egc_s, h)
        ekd = column(ekd_s, h)
        beta = column(beta_s, h + GDN_HEADS)
        grow = gct_s[pl.ds(r0 + h, 1), :]
        tri_i = row >= col
        w["dinc"][...] = jnp.where(tri_i, jnp.exp(jnp.where(tri_i, gcc - grow, 0.0)), 0.0)
        kb = k * beta
        w["k"][...] = k.astype(BF16)
        w["kb"][...] = kb.astype(BF16)
        w["q"][...] = q.astype(BF16)
        w["kd"][...] = (k * ekd).astype(BF16)
        w["vk"][:, :GDN_DV] = (v * beta).astype(BF16)
        w["vk"][:, GDN_DV:] = (kb * egc).astype(BF16)
        qa_s[ci, :, 0:GDN_DK] = (q * egc).astype(BF16)

    def gram(w, ci):
        r = _dot_nt(jnp.concatenate([w["kb"][...], w["q"][...]], axis=0), w["k"][...])
        d_incl = w["dinc"][...]
        low = r[:c] * jnp.where(row > col, d_incl, 0.0)
        qa_s[ci, :, GDN_DK:] = (r[c:] * d_incl).astype(BF16)
        w["low"][...] = low
        ld = jnp.where((row >> INV_BASE_LOG2) == (col >> INV_BASE_LOG2), low, 0.0)
        w["x"][...] = jnp.where(row == col, 1.0, 0.0) - ld
        w["p"][...] = ld.astype(BF16)

    def neumann(w, first, last):
        pb = w["p"][...]
        if first:
            w["p"][...] = _dot(pb, pb).astype(BF16)
        elif last:
            x = w["x"][...]
            w["x"][...] = x + _dot(x.astype(BF16), pb)
        else:
            x = w["x"][...]
            r = _dot(jnp.concatenate([x.astype(BF16), pb], axis=0), pb)
            w["x"][...] = x + r[:c]
            w["p"][...] = r[c:].astype(BF16)

    def merge_a(w, lg):
        pair = (row >> (lg + 1)) == (col >> (lg + 1))
        m = jnp.where(pair, (row >> lg) - (col >> lg), 0) > 0
        lm = jnp.where(m, w["low"][...], 0.0).astype(BF16)
        w["p"][...] = _dot(lm, w["x"][...].astype(BF16)).astype(BF16)

    def merge_b(w):
        x = w["x"][...]
        w["x"][...] = x - _dot(x.astype(BF16), w["p"][...])

    def solve(w, ci):
        uw = _dot(w["x"][...].astype(BF16), w["vk"][...])
        u_s[ci] = uw[:, :GDN_DV]
        uwb = uw.astype(BF16)
        w["wu"][...] = uwb
        pw_s[ci, c:2 * c, :] = uwb[:, GDN_DV:]

    def outer(w, ci):
        np_ = _dot_tn(w["kd"][...], w["wu"][...])
        n_s[ci] = np_[:, :GDN_DV]
        pw_s[ci, 0:c, :] = np_[:, GDN_DV:].astype(BF16)

    n_sq = INV_BASE_LOG2 - 1
    lgc = c.bit_length() - 1

    def prepare_stages(cis):
        over = lambda fn, *a: (lambda: [fn(work[ci], *[ci if x is cis else x for x in a])
                                        for ci in cis])
        stages = [over(operands, cis), over(gram, cis)]
        stages += [over(neumann, step == 0, step == n_sq) for step in range(n_sq + 1)]
        for lg in range(INV_BASE_LOG2, lgc):
            stages += [over(merge_a, lg), over(merge_b)]
        return stages + [over(solve, cis), over(outer, cis)]

    gn = gn_ref[...]
    lane1 = lax.broadcasted_iota(jnp.int32, (1, LANES), 1)

    def advance(ci, s_mat):
        sb = s_mat.astype(BF16)
        r = _dot(pw_s[ci], sb)
        g_last = jnp.sum(jnp.where(lane1 == h, gc_s[pl.ds(ci * c + c - 1, 1), :], 0.0),
                         axis=-1, keepdims=True)
        sv_s[ci, 0:c, :] = sb
        sv_s[ci, c:2 * c, :] = (u_s[ci] - r[c:2 * c]).astype(BF16)
        return s_mat * jnp.exp(g_last) - r[0:c] + n_s[ci]

    def emit(ci):
        rows = pl.ds(ci * c, c)
        o = _dot(qa_s[ci], sv_s[ci])
        zz = z_ref[0, rows, :].astype(F32)
        o_ref[0, rows, :] = (_rms(o, gn) * _silu(zz)).astype(o_ref.dtype)

    s_mat = jnp.zeros((GDN_DK, GDN_DV), F32)
    chain_todo, emit_todo = [], []
    for g in range(n_c // group):
        cis = list(range(g * group, (g + 1) * group))
        for stage in prepare_stages(cis):
            stage()
            if chain_todo:
                emit_todo.append(chain_todo[0])
                s_mat = advance(chain_todo.pop(0), s_mat)
        chain_todo += cis
    while chain_todo:
        if emit_todo:
            emit(emit_todo.pop(0))
        emit_todo.append(chain_todo[0])
        s_mat = advance(chain_todo.pop(0), s_mat)
    for ci in emit_todo:
        emit(ci)
    s_ref[0, 0] = s_mat


def _gdn_prompt(proj3, ab3, conv_w, gate_par, gdn_norm):
    bsz, t_len, _ = proj3.shape
    nh = GDN_HEADS
    blk = lambda off: pl.BlockSpec((1, t_len, LANES), lambda b, h: (b, 0, off + h))
    cw = lambda off: pl.BlockSpec((CONV_W, LANES), lambda b, h: (0, off + h))
    c = GDN_CHUNK
    n_c = t_len // c
    gate_scr = pltpu.VMEM((t_len, LANES), F32)
    group = 8
    work = [pltpu.VMEM((c, wide * LANES), dt) for _ in range(n_c) for _, wide, dt in _GDN_WORK]
    return pl.pallas_call(
        functools.partial(_gdn_prompt_kernel, group=group),
        grid=(bsz, nh),
        in_specs=[blk(0), blk(nh), blk(2 * nh), blk(3 * nh),
                  pl.BlockSpec((1, t_len, LANES), lambda b, h: (b, 0, 0)),
                  cw(0), cw(nh), cw(2 * nh),
                  pl.BlockSpec((SUBLANES, LANES), lambda b, h: (0, 0)),
                  pl.BlockSpec((1, LANES), lambda b, h: (0, 0))],
        out_specs=[pl.BlockSpec((1, t_len, LANES), lambda b, h: (b, 0, h)),
                   pl.BlockSpec((1, 1, GDN_DK, GDN_DV), lambda b, h: (b, h, 0, 0))],
        out_shape=[jax.ShapeDtypeStruct((bsz, t_len, MIX_A), BF16),
                   jax.ShapeDtypeStruct((bsz, nh, GDN_DK, GDN_DV), F32)],
        scratch_shapes=[gate_scr, gate_scr, gate_scr, gate_scr, gate_scr,
                        pltpu.VMEM((n_c, c, GDN_DV), F32), pltpu.VMEM((n_c, GDN_DK, GDN_DV), F32),
                        pltpu.VMEM((n_c, 2 * c, GDN_DK), BF16), pltpu.VMEM((n_c, c, GDN_DK + c), BF16),
                        pltpu.VMEM((n_c, GDN_DK + c, GDN_DV), BF16)] + work,
        compiler_params=_cparams(("parallel", "arbitrary")),
        name="gdn_prompt",
    )(proj3, proj3, proj3, proj3, ab3, conv_w, conv_w, conv_w, gate_par, gdn_norm.reshape(1, LANES))


def _gdn_sample_kernel(x_ref, ab_ref, hist_ref, cw_ref, gp_ref, gn_ref, s0_ref, o_ref, s_ref,
                       *, t_len):
    nh = GDN_HEADS
    g8 = SUBLANES
    n = nh * g8
    assert t_len + CONV_W - 1 <= g8
    cw = cw_ref[...]
    gp = gp_ref[...]
    gn = gn_ref[...]
    row8 = lax.broadcasted_iota(jnp.int32, (g8, QKV_W), 0)
    tpos = lax.broadcasted_iota(jnp.int32, (n, 1), 0) & (g8 - 1)
    rhead = lax.broadcasted_iota(jnp.int32, (n, GDN_DV), 0) >> 3
    ri = lax.broadcasted_iota(jnp.int32, (n, n), 0)
    ci = lax.broadcasted_iota(jnp.int32, (n, n), 1)
    tri = jnp.where((ri >> 3) == (ci >> 3), ri - ci, -1)
    eye = jnp.where(ri == ci, 1.0, 0.0)
    sel0 = jnp.where(lax.broadcasted_iota(jnp.int32, (n, LANES), 1) == 0, 1.0, 0.0).astype(BF16)
    valid = tpos < t_len
    a_neg = jnp.concatenate([jnp.broadcast_to(-jnp.exp(gp[0:1, h:h + 1]), (g8, 1)) for h in range(nh)], 0)
    dt_b = jnp.concatenate([jnp.broadcast_to(gp[1:2, h:h + 1], (g8, 1)) for h in range(nh)], 0)

    def l2n(x):
        return x * lax.rsqrt(jnp.sum(x * x, axis=-1, keepdims=True) + EPS)

    for b in range(x_ref.shape[0]):
        xb = x_ref[b]
        x = xb[:, :QKV_W].astype(F32)
        hst = hist_ref[b]
        y = x * cw[CONV_W - 1:CONV_W]
        for s in range(1, CONV_W):
            y = y + pltpu.roll(x, s, axis=0) * cw[CONV_W - 1 - s:CONV_W - s]
        for j in range(CONV_W - 1):
            hj = hst if j == 0 else pltpu.roll(hst, g8 - j, axis=0)
            y = y + jnp.where(row8 + j < CONV_W - 1, hj, 0.0) * cw[j:j + 1]
        qkv = jnp.where(row8 < t_len, _silu(y), 0.0)

        def stack(off):
            return jnp.concatenate([qkv[:, off + h * GDN_DK:off + (h + 1) * GDN_DK] for h in range(nh)], 0)

        q = l2n(stack(0)) * (GDN_DK ** -0.5)
        k = l2n(stack(QK_W))
        v = stack(2 * QK_W)
        z = jnp.concatenate([xb[:, 3 * QK_W + h * GDN_DV:3 * QK_W + (h + 1) * GDN_DV].astype(F32)
                             for h in range(nh)], 0)
        ab = ab_ref[b]
        alpha = jnp.concatenate([ab[:, h:h + 1] for h in range(nh)], 0)
        braw = jnp.concatenate([ab[:, nh + h:nh + h + 1] for h in range(nh)], 0)
        g = jnp.where(valid, a_neg * _softplus(alpha + dt_b), 0.0)
        beta = jnp.where(valid, jax.nn.sigmoid(braw), 0.0)
        gc = g
        s = 1
        while s < g8:
            gc = gc + jnp.where(tpos >= s, pltpu.roll(gc, s, axis=0), 0.0)
            s *= 2
        g_last = jnp.concatenate([jnp.broadcast_to(gc[h * g8 + g8 - 1:(h + 1) * g8, :], (g8, 1))
                                  for h in range(nh)], 0)
        xg = jnp.broadcast_to(gc, (n, LANES))
        hi = xg.astype(BF16)
        r1 = xg - hi.astype(F32)
        mid = r1.astype(BF16)
        lo = (r1 - mid.astype(F32)).astype(BF16)
        grow = _dot_nt(sel0, hi) + _dot_nt(sel0, mid) + _dot_nt(sel0, lo)
        d_incl = jnp.where(tri >= 0, jnp.exp(jnp.where(tri >= 0, gc - grow, 0.0)), 0.0)
        d_strict = jnp.where(tri > 0, d_incl, 0.0)
        egc = jnp.exp(gc)
        kb = k * beta
        kbf = k.astype(BF16)
        low = _dot_nt(kb.astype(BF16), kbf) * d_strict
        a_in = _dot_nt(q.astype(BF16), kbf) * d_incl
        t_inv = eye - low
        p = low
        covered = 2
        while covered < t_len:
            pb = p.astype(BF16)
            p = _dot(pb, pb)
            t_inv = t_inv + _dot(t_inv.astype(BF16), p.astype(BF16))
            covered *= 2
        t_inv = t_inv.astype(BF16)
        u = _dot(t_inv, (v * beta).astype(BF16))
        w = _dot(t_inv, (kb * egc).astype(BF16))
        qg = q * egc
        k_dec = (k * jnp.exp(g_last - gc)).astype(BF16)
        ws, qs = [], []
        for h in range(nh):
            hs = slice(h * g8, (h + 1) * g8)
            lhs = jnp.concatenate([w[hs], qg[hs]], 0).astype(BF16)
            r = _dot(lhs, s0_ref[b, h].astype(BF16))
            ws.append(r[:g8])
            qs.append(r[g8:])
        v_new = u - jnp.concatenate(ws, 0)
        o = jnp.concatenate(qs, 0) + _dot(a_in.astype(BF16), v_new.astype(BF16))
        for h in range(nh):
            vm = jnp.where(rhead == h, v_new, 0.0).astype(BF16)
            dec = jnp.exp(g_last[h * g8:h * g8 + 1, :])
            s_ref[b, h] = s0_ref[b, h] * dec + _dot_tn(k_dec, vm)
        o_ref[b] = (_rms(o, gn) * _silu(z)).astype(o_ref.dtype)


def _gdn_sample(proj3, ab3, hist, conv_w, gate_par, gdn_norm, s0, *, bb):
    bsz, t_len, _ = proj3.shape
    nh, g8 = GDN_HEADS, SUBLANES
    pad_t = lambda a, rows: jnp.pad(a, ((0, 0), (0, g8 - rows), (0, 0)))
    x8 = pad_t(proj3[:, :, :4 * QK_W], t_len)
    ab8 = pad_t(ab3, t_len)
    hist8 = pad_t(hist, CONV_W - 1)
    o, s_new = pl.pallas_call(
        functools.partial(_gdn_sample_kernel, t_len=t_len),
        grid=(bsz // bb,),
        in_specs=[pl.BlockSpec((bb, g8, 4 * QK_W), lambda i: (i, 0, 0)),
                  pl.BlockSpec((bb, g8, LANES), lambda i: (i, 0, 0)),
                  pl.BlockSpec((bb, g8, QKV_W), lambda i: (i, 0, 0)),
                  pl.BlockSpec((CONV_W, QKV_W), lambda i: (0, 0)),
                  pl.BlockSpec((SUBLANES, LANES), lambda i: (0, 0)),
                  pl.BlockSpec((1, LANES), lambda i: (0, 0)),
                  pl.BlockSpec((bb, nh, GDN_DK, GDN_DV), lambda i: (i, 0, 0, 0))],
        out_specs=[pl.BlockSpec((bb, nh * g8, GDN_DV), lambda i: (i, 0, 0)),
                   pl.BlockSpec((bb, nh, GDN_DK, GDN_DV), lambda i: (i, 0, 0, 0))],
        out_shape=[jax.ShapeDtypeStruct((bsz, nh * g8, GDN_DV), BF16),
                   jax.ShapeDtypeStruct((bsz, nh, GDN_DK, GDN_DV), F32)],
        compiler_params=_cparams(("parallel",)),
        name="gdn_sample",
    )(x8, ab8, hist8, conv_w, gate_par, gdn_norm.reshape(1, LANES), s0)
    o = o.reshape(bsz, nh, g8, GDN_DV)[:, :, :t_len].transpose(0, 2, 1, 3)
    return o.reshape(bsz, t_len, MIX_A), s_new


def _pool_finish(d_groups, pw_ref, ps_ref, zb):
    outs = [_dot(d.astype(BF16), pw_ref[gi]) for gi, d in enumerate(d_groups)]
    return jnp.concatenate(outs, axis=-1) * ps_ref[...] * _silu(zb)


def _pool_prompt_kernel(u_ref, halo_ref, zb_ref, pw_ref, ps_ref, o_ref, *, pos0):
    i = pl.program_id(1)
    tt = u_ref.shape[1]
    u = u_ref[0].astype(F32)
    halo = jnp.where(i > 0, halo_ref[0].astype(F32), 0.0)
    ext = jnp.concatenate([halo, u], axis=0)
    pos = pos0 + i * tt + lax.broadcasted_iota(jnp.int32, (tt, 1), 0)
    d_groups = []
    for gi, win in enumerate(POOL_WINDOWS):
        sl = slice(gi * POOL_CH, (gi + 1) * POOL_CH)
        lvl = ext[:, sl]
        s = 1
        while s < win:
            lvl = lvl + pltpu.roll(lvl, s, axis=0)
            s *= 2
        cnt = jnp.minimum(win, pos + 1).astype(F32)
        d_groups.append(lvl[POOL_HALO:, :] / cnt - u[:, sl])
    o_ref[0] = _pool_finish(d_groups, pw_ref, ps_ref, zb_ref[0].astype(F32)).astype(o_ref.dtype)


def _pool_prompt(proj3, pool_w, pool_scale, *, tt, pos0):
    bsz, t_len, _ = proj3.shape
    ub, zb = COL_ZA // MIX_B + 1, COL_ZA // MIX_B + 2
    hb = tt // POOL_HALO
    return pl.pallas_call(
        functools.partial(_pool_prompt_kernel, pos0=pos0),
        grid=(bsz, t_len // tt),
        in_specs=[pl.BlockSpec((1, tt, MIX_B), lambda b, i: (b, i, ub)),
                  pl.BlockSpec((1, POOL_HALO, MIX_B), lambda b, i: (b, jnp.maximum(i * hb - 1, 0), ub)),
                  pl.BlockSpec((1, tt, MIX_B), lambda b, i: (b, i, zb)),
                  pl.BlockSpec((POOL_GROUPS, POOL_CH, POOL_CH), lambda b, i: (0, 0, 0)),
                  pl.BlockSpec((1, MIX_B), lambda b, i: (0, 0))],
        out_specs=pl.BlockSpec((1, tt, MIX_B), lambda b, i: (b, i, 0)),
        out_shape=jax.ShapeDtypeStruct((bsz, t_len, MIX_B), BF16),
        compiler_params=_cparams(("parallel", "arbitrary")),
        name="pool_prompt",
    )(proj3, proj3, proj3, pool_w, pool_scale.reshape(1, MIX_B))


def _pool_sample_kernel(ext_ref, zb_ref, pw_ref, ps_ref, o_ref, *, pos0):
    t_len = zb_ref.shape[0]
    for t in range(t_len):
        e = POOL_BUF + t
        d_groups = []
        for gi, win in enumerate(POOL_WINDOWS):
            sl = slice(gi * POOL_CH, (gi + 1) * POOL_CH)
            tot = ext_ref[e, :, sl]
            for j in range(1, win):
                tot = tot + ext_ref[e - j, :, sl]
            cnt = float(min(win, pos0 + t + 1))
            d_groups.append(tot / cnt - ext_ref[e, :, sl])
        o_ref[t] = _pool_finish(d_groups, pw_ref, ps_ref, zb_ref[t].astype(F32)).astype(o_ref.dtype)


def _pool_sample(ext_tm, zb_tm, pool_w, pool_scale, *, bb, pos0):
    t_len, bsz, _ = zb_tm.shape
    return pl.pallas_call(
        functools.partial(_pool_sample_kernel, pos0=pos0),
        grid=(bsz // bb,),
        in_specs=[pl.BlockSpec((POOL_BUF + t_len, bb, MIX_B), lambda i: (0, i, 0)),
                  pl.BlockSpec((t_len, bb, MIX_B), lambda i: (0, i, 0)),
                  pl.BlockSpec((POOL_GROUPS, POOL_CH, POOL_CH), lambda i: (0, 0, 0)),
                  pl.BlockSpec((1, MIX_B), lambda i: (0, 0))],
        out_specs=pl.BlockSpec((t_len, bb, MIX_B), lambda i: (0, i, 0)),
        out_shape=jax.ShapeDtypeStruct((t_len, bsz, MIX_B), BF16),
        compiler_params=_cparams(("parallel",)),
        name="pool_sample",
    )(ext_tm, zb_tm, pool_w, pool_scale.reshape(1, MIX_B))


def _mix_out_kernel(x_ref, oa_ref, ob_ref, wo_ref, nc_ref, wq_ref, x1_ref, qx_ref):
    acc = _dot(oa_ref[...], wo_ref[0:MIX_A, :]) + _dot(ob_ref[...], wo_ref[MIX_A:D_MODEL, :])
    x1 = x_ref[...] + acc
    x1_ref[...] = x1
    qx_ref[...] = _dot(_rms(x1, nc_ref[...]).astype(BF16), wq_ref[...]).astype(qx_ref.dtype)


def _mix_out(x, oa, ob, w_out, norm_cross, w_cq, *, tm):
    n, d = x.shape
    const = lambda i: (0, 0)
    return pl.pallas_call(
        _mix_out_kernel,
        grid=(n // tm,),
        in_specs=[pl.BlockSpec((tm, d), lambda i: (i, 0)),
                  pl.BlockSpec((tm, MIX_A), lambda i: (i, 0)),
                  pl.BlockSpec((tm, MIX_B), lambda i: (i, 0)),
                  pl.BlockSpec((d, d), const), pl.BlockSpec((1, d), const), pl.BlockSpec((d, d), const)],
        out_specs=[pl.BlockSpec((tm, d), lambda i: (i, 0)), pl.BlockSpec((tm, d), lambda i: (i, 0))],
        out_shape=[jax.ShapeDtypeStruct((n, d), F32), jax.ShapeDtypeStruct((n, d), BF16)],
        compiler_params=_cparams(("parallel",)),
        name="mix_out",
    )(x, oa, ob, w_out, norm_cross.reshape(1, d), w_cq)


def _xattn_kernel(q_ref, k_ref, v_ref, ctx_ref):
    q = q_ref[0]
    scale = X_HEAD_DIM ** -0.5
    for h in range(X_HEADS):
        sl = slice(h * X_HEAD_DIM, (h + 1) * X_HEAD_DIM)
        s = _dot_nt(q[:, sl], k_ref[0, :, sl].astype(BF16)) * scale
        p = jnp.exp(s - jnp.max(s, axis=-1, keepdims=True))
        p = p / jnp.sum(p, axis=-1, keepdims=True)
        ctx_ref[0, :, sl] = _dot(p.astype(BF16), v_ref[0, :, sl].astype(BF16)).astype(ctx_ref.dtype)


def _xattn(qx3, mk3, mv3, *, tq):
    bsz, t_len, d = qx3.shape
    n_mem = mk3.shape[1]
    return pl.pallas_call(
        _xattn_kernel,
        grid=(bsz, t_len // tq),
        in_specs=[pl.BlockSpec((1, tq, d), lambda b, i: (b, i, 0)),
                  pl.BlockSpec((1, n_mem, d), lambda b, i: (b, 0, 0)),
                  pl.BlockSpec((1, n_mem, d), lambda b, i: (b, 0, 0))],
        out_specs=pl.BlockSpec((1, tq, d), lambda b, i: (b, i, 0)),
        out_shape=jax.ShapeDtypeStruct((bsz, t_len, d), BF16),
        compiler_params=_cparams(("parallel", "arbitrary")),
        name="xattn",
    )(qx3, mk3, mv3)


def _xattn_native_kernel(q_ref, k_ref, v_ref, o_ref, *, t_len):
    nj = X_HEAD_DIM // LANES
    grp = nj * X_HEADS
    th = t_len * X_HEADS
    scale = X_HEAD_DIM ** -0.5
    for b in range(q_ref.shape[0]):
        z = _dot_nt(q_ref[b], k_ref[b].astype(BF16))
        ncol = z.shape[1]
        r = lax.broadcasted_iota(jnp.int32, (th, ncol), 0) & (X_HEADS - 1)
        c = lax.broadcasted_iota(jnp.int32, (th, ncol), 1) & (grp - 1)
        s = None
        for j in range(nj):
            zj = jnp.where(c == r + j * X_HEADS, z[j * th:(j + 1) * th, :], 0.0)
            if j:
                zj = pltpu.roll(zj, ncol - j * X_HEADS, axis=1)
            s = zj if s is None else s + zj
        sm = jnp.where(c == r, s * scale, -jnp.inf)
        p = jnp.exp(sm - jnp.max(sm, axis=1, keepdims=True))
        p = p / jnp.sum(p, axis=1, keepdims=True)
        pp = jnp.concatenate([p if j == 0 else pltpu.roll(p, j * X_HEADS, axis=1)
                              for j in range(nj)], axis=0).astype(BF16)
        o_ref[b] = _dot(pp, v_ref[b].astype(BF16)).astype(o_ref.dtype)


def _xattn_native(q_rows, k_rows, v_rows, *, t_len, bb):
    bsz, nq, _ = q_rows.shape
    nk = k_rows.shape[1]
    return pl.pallas_call(
        functools.partial(_xattn_native_kernel, t_len=t_len),
        grid=(bsz // bb,),
        in_specs=[pl.BlockSpec((bb, nq, LANES), lambda i: (i, 0, 0)),
                  pl.BlockSpec((bb, nk, LANES), lambda i: (i, 0, 0)),
                  pl.BlockSpec((bb, nk, LANES), lambda i: (i, 0, 0))],
        out_specs=pl.BlockSpec((bb, nq, LANES), lambda i: (i, 0, 0)),
        out_shape=jax.ShapeDtypeStruct((bsz, nq, LANES), BF16),
        compiler_params=_cparams(("parallel",)),
        name="xattn_native",
    )(q_rows, k_rows, v_rows)


def _kv_rows(cache):
    bsz, n_mem, nh, dh = cache.shape
    nj = dh // LANES
    return cache.reshape(bsz, n_mem, nh, nj, LANES).transpose(0, 1, 3, 2, 4).reshape(
        bsz, n_mem * nj * nh, LANES)


def _attn_out_kernel(x1_ref, ctx_ref, wco_ref, nf_ref, y_ref):
    x2 = x1_ref[...] + _dot(ctx_ref[...], wco_ref[...])
    y_ref[...] = _rms(x2, nf_ref[...])


def _attn_out(x1, ctx, w_co, norm_final, *, tm):
    n, d = x1.shape
    const = lambda i: (0, 0)
    return pl.pallas_call(
        _attn_out_kernel,
        grid=(n // tm,),
        in_specs=[pl.BlockSpec((tm, d), lambda i: (i, 0)), pl.BlockSpec((tm, d), lambda i: (i, 0)),
                  pl.BlockSpec((d, d), const), pl.BlockSpec((1, d), const)],
        out_specs=pl.BlockSpec((tm, d), lambda i: (i, 0)),
        out_shape=jax.ShapeDtypeStruct((n, d), F32),
        compiler_params=_cparams(("parallel",)),
        name="attn_out",
    )(x1, ctx, w_co, norm_final.reshape(1, d))


def _layer_tail(x2d, oa, ob, mk3, mv3, bsz, t_len, w_out, norm_cross, w_cq, w_co, norm_final,
                *, tm, tq):
    x1, qx = _mix_out(x2d, oa.reshape(-1, MIX_A), ob.reshape(-1, MIX_B), w_out, norm_cross, w_cq, tm=tm)
    ctx = _xattn(qx.reshape(bsz, t_len, D_MODEL), mk3, mv3, tq=tq)
    y = _attn_out(x1, ctx.reshape(-1, D_MODEL), w_co, norm_final, tm=tm)
    return y.reshape(bsz, t_len, D_MODEL)


def kernel(x_prompt, x_sample, mem_prompt, cache_mem_k, cache_mem_v, state_delta, state_conv,
           state_pool, norm_mix, w_in, conv_w, a_log, dt_bias, gdn_norm, pool_w, pool_scale,
           w_out, norm_mem, norm_cross, w_cq, w_ck, w_cv, w_co, norm_final):
    bp, tp, d = x_prompt.shape
    bs, ts, _ = x_sample.shape
    n_mem = mem_prompt.shape[1]

    w_main, w_gate = _repack_w_in(w_in[0], tk=512, tn=1024)
    wo, wcq, wck, wcv, wco = (w[0].astype(BF16) for w in (w_out, w_cq, w_ck, w_cv, w_co))
    pw = pool_w[0].astype(BF16)
    gate_par = jnp.zeros((SUBLANES, LANES), F32)
    gate_par = gate_par.at[0, :GDN_HEADS].set(a_log[0]).at[1, :GDN_HEADS].set(dt_bias[0])
    cw = conv_w[0]

    mem2d = mem_prompt.reshape(bp * n_mem, d)
    mk = _norm_proj(mem2d, norm_mem[0], wck, tm=512, tn=1024, out_dtype=F32)
    mv = _norm_proj(mem2d, norm_mem[0], wcv, tm=512, tn=1024, out_dtype=F32)

    proj_p, ab_p = _norm_proj(x_prompt.reshape(bp * tp, d), norm_mix[0], w_main, w_gate,
                              tm=1024, tn=1024, out_dtype=BF16)
    proj_p3 = proj_p.reshape(bp, tp, W_MAIN)
    oa_p, delta_p = _gdn_prompt(proj_p3, ab_p.reshape(bp, tp, LANES), cw, gate_par, gdn_norm[0])
    ob_p = _pool_prompt(proj_p3, pw, pool_scale[0], tt=512, pos0=0)
    y_p = _layer_tail(x_prompt.reshape(bp * tp, d), oa_p, ob_p, mk.reshape(bp, n_mem, d),
                      mv.reshape(bp, n_mem, d), bp, tp, wo, norm_cross[0], wcq, wco, norm_final,
                      tm=256, tq=512)
    conv_p = proj_p3[:, tp - (CONV_W - 1):, :QKV_W].astype(F32)
    pool_p = proj_p3[:, tp - POOL_BUF:, COL_ZA + MIX_A:COL_ZA + MIX_A + MIX_B].astype(F32)

    proj_s, ab_s = _norm_proj(x_sample.reshape(bs * ts, d), norm_mix[0], w_main, w_gate,
                              tm=256, tn=1024, out_dtype=BF16)
    proj_s3 = proj_s.reshape(bs, ts, W_MAIN)
    oa_s, delta_s = _gdn_sample(proj_s3, ab_s.reshape(bs, ts, LANES), state_conv[0], cw, gate_par,
                                gdn_norm[0], state_delta[0], bb=4)
    u_s = proj_s3[:, :, COL_ZA + MIX_A:COL_ZA + MIX_A + MIX_B].astype(F32)
    ext_s = jnp.concatenate([state_pool[0], u_s], axis=1)
    zb_tm = proj_s3[:, :, COL_ZA + MIX_A + MIX_B:].transpose(1, 0, 2)
    ob_s = _pool_sample(ext_s.transpose(1, 0, 2), zb_tm, pw, pool_scale[0], bb=32,
                        pos0=PAST_LEN).transpose(1, 0, 2)
    x1_s, qx_s = _mix_out(x_sample.reshape(bs * ts, d), oa_s.reshape(-1, MIX_A),
                          ob_s.reshape(-1, MIX_B), wo, norm_cross[0], wcq, tm=256)
    nj = X_HEAD_DIM // LANES
    q_rows = qx_s.reshape(bs, ts, X_HEADS, nj, LANES).transpose(0, 3, 1, 2, 4).reshape(
        bs, nj * ts * X_HEADS, LANES)
    ctx_rows = _xattn_native(q_rows, _kv_rows(cache_mem_k[0]), _kv_rows(cache_mem_v[0]),
                             t_len=ts, bb=2)
    ctx_s = ctx_rows.reshape(bs, nj, ts, X_HEADS, LANES).transpose(0, 2, 3, 1, 4).reshape(bs * ts, d)
    y_s = _attn_out(x1_s, ctx_s, wco, norm_final, tm=256).reshape(bs, ts, d)
    conv_s = jnp.concatenate([state_conv[0], proj_s3[:, :, :QKV_W].astype(F32)], axis=1)[:, ts:]
    pool_s = ext_s[:, ts:]

    hd = (X_HEADS, X_HEAD_DIM)
    return (y_p, y_s, mk.reshape(1, bp, n_mem, *hd), mv.reshape(1, bp, n_mem, *hd),
            delta_p[None], conv_p[None], pool_p[None], delta_s[None], conv_s[None], pool_s[None])
```

```python
import functools
import math

import jax
import jax.numpy as jnp
from jax import lax
from jax.experimental import pallas as pl
from jax.experimental.pallas import tpu as pltpu

F32 = jnp.float32
BF16 = jnp.bfloat16

D_MODEL = 2048
MIX_A = D_MODEL // 2
MIX_B = D_MODEL - MIX_A
GDN_HEADS = 8
GDN_DK = MIX_A // GDN_HEADS
GDN_DV = MIX_A // GDN_HEADS
QK_W = GDN_HEADS * GDN_DK
QKV_W = 2 * QK_W + GDN_HEADS * GDN_DV
CONV_W = 4
POOL_WINDOWS = (2, 4, 8, 16)
POOL_GROUPS = len(POOL_WINDOWS)
POOL_CH = MIX_B // POOL_GROUPS
POOL_BUF = max(POOL_WINDOWS) - 1
X_HEADS = 4
X_HEAD_DIM = D_MODEL // X_HEADS
PAST_LEN = 16384
EPS = 1e-6
COL_ZA = QKV_W
COL_A = COL_ZA + GDN_HEADS * GDN_DV
COL_B = COL_A + GDN_HEADS
COL_U = COL_B + GDN_HEADS
COL_ZB = COL_U + MIX_B
IN_COLS = COL_ZB + MIX_B

W_MAIN = IN_COLS - 2 * GDN_HEADS
LANES = 128
SUBLANES = 8
BF16_ROWS = 16
GDN_CHUNK = 128
POOL_HALO = 16
VMEM_LIMIT = 56 * 1024 * 1024


def _cparams(sem):
    return pltpu.CompilerParams(dimension_semantics=sem, vmem_limit_bytes=VMEM_LIMIT)


def _dot(a, b):
    return jnp.dot(a, b, preferred_element_type=F32)


def _dot_nt(a, b):
    return lax.dot_general(a, b, (((1,), (1,)), ((), ())), preferred_element_type=F32)


def _dot_tn(a, b):
    return lax.dot_general(a, b, (((0,), (0,)), ((), ())), preferred_element_type=F32)


def _rms(x, g):
    return x * lax.rsqrt(jnp.mean(x * x, axis=-1, keepdims=True) + EPS) * g


def _silu(x):
    return x * jax.nn.sigmoid(x)


def _softplus(x):
    return jnp.maximum(x, 0.0) + jnp.log1p(jnp.exp(-jnp.abs(x)))


def _repack_kernel(wt_ref, gt_ref, main_ref, gate_ref):
    main_ref[...] = wt_ref[...].T.astype(main_ref.dtype)

    @pl.when(pl.program_id(1) == 0)
    def _():
        head = gt_ref[...].T
        lane = lax.broadcasted_iota(jnp.int32, head.shape, 1)
        gate_ref[...] = jnp.where(lane < COL_U - COL_A, head, 0.0).astype(gate_ref.dtype)


def _repack_w_in(wt, *, tk, tn):
    k_dim = wt.shape[1]
    n_gate = COL_U - COL_A
    src_row = lambda j: pl.multiple_of(j * tn + (j // (COL_A // tn)) * n_gate, n_gate)
    return pl.pallas_call(
        _repack_kernel,
        grid=(k_dim // tk, W_MAIN // tn),
        in_specs=[pl.BlockSpec((pl.Element(tn), pl.Element(tk)), lambda i, j: (src_row(j), i * tk)),
                  pl.BlockSpec((pl.Element(LANES), pl.Element(tk)), lambda i, j: (COL_A, i * tk))],
        out_specs=[pl.BlockSpec((tk, tn), lambda i, j: (i, j)),
                   pl.BlockSpec((tk, LANES), lambda i, j: (i, 0))],
        out_shape=[jax.ShapeDtypeStruct((k_dim, W_MAIN), BF16),
                   jax.ShapeDtypeStruct((k_dim, LANES), BF16)],
        compiler_params=_cparams(("parallel", "arbitrary")),
        name="repack_w_in",
    )(wt, wt)


def _norm_proj_kernel(x_ref, g_ref, w_ref, *rest, with_side):
    if with_side:
        ws_ref, out_ref, side_ref, h_scr = rest
    else:
        out_ref, h_scr = rest

    @pl.when(pl.program_id(1) == 0)
    def _():
        h = _rms(x_ref[...], g_ref[...]).astype(BF16)
        h_scr[...] = h
        if with_side:
            side_ref[...] = _dot(h, ws_ref[...])

    out_ref[...] = _dot(h_scr[...], w_ref[...].astype(BF16)).astype(out_ref.dtype)


def _norm_proj(x, g, w, w_side=None, *, tm, tn, out_dtype):
    n, d = x.shape
    ncol = w.shape[1]
    with_side = w_side is not None
    in_specs = [pl.BlockSpec((tm, d), lambda i, j: (i, 0)),
                pl.BlockSpec((1, d), lambda i, j: (0, 0)),
                pl.BlockSpec((d, tn), lambda i, j: (0, j))]
    out_specs = [pl.BlockSpec((tm, tn), lambda i, j: (i, j))]
    out_shape = [jax.ShapeDtypeStruct((n, ncol), out_dtype)]
    args = [x, g.reshape(1, d), w]
    if with_side:
        in_specs.append(pl.BlockSpec((d, LANES), lambda i, j: (0, 0)))
        out_specs.append(pl.BlockSpec((tm, LANES), lambda i, j: (i, 0)))
        out_shape.append(jax.ShapeDtypeStruct((n, LANES), F32))
        args.append(w_side)
    res = pl.pallas_call(
        functools.partial(_norm_proj_kernel, with_side=with_side),
        grid=(n // tm, ncol // tn),
        in_specs=in_specs, out_specs=out_specs, out_shape=out_shape,
        scratch_shapes=[pltpu.VMEM((tm, d), BF16)],
        compiler_params=_cparams(("parallel", "arbitrary")),
        name="norm_proj_side" if with_side else "norm_proj",
    )(*args)
    return res if with_side else res[0]


INV_BASE_LOG2 = 4
_GDN_WORK = (("k", 1, BF16), ("kb", 1, BF16), ("q", 1, BF16), ("kd", 1, BF16), ("vk", 2, BF16),
             ("dinc", 1, F32), ("low", 1, F32), ("x", 1, F32), ("p", 1, BF16), ("wu", 2, BF16))


def _split3(x):
    hi = x.astype(BF16)
    r1 = x - hi.astype(F32)
    mid = r1.astype(BF16)
    lo = (r1 - mid.astype(F32)).astype(BF16)
    return hi, mid, lo


def _gdn_prompt_kernel(q_ref, k_ref, v_ref, z_ref, ab_ref, cwq_ref, cwk_ref, cwv_ref, gp_ref,
                       gn_ref, o_ref, s_ref,
                       gc_s, gct_s, egc_s, ekd_s, beta_s, u_s, n_s, pw_s, qa_s, sv_s, *work_refs, group):
    nw = len(_GDN_WORK)
    work = [{name: ref for (name, _, _), ref in zip(_GDN_WORK, work_refs[j * nw:(j + 1) * nw])}
            for j in range(len(work_refs) // nw)]
    h = pl.program_id(1)
    t_len = q_ref.shape[1]
    c = GDN_CHUNK
    n_c = t_len // c
    row = lax.broadcasted_iota(jnp.int32, (c, c), 0)
    col = lax.broadcasted_iota(jnp.int32, (c, c), 1)

    @pl.when(h == 0)
    def _():
        gp = gp_ref[...]
        a_neg = -jnp.exp(gp[0:1])
        dt_b = gp[1:2]
        tri = jnp.where(row >= col, 1.0, 0.0).astype(BF16)

        def gate_chunk(ci, carry):
            rows = pl.ds(pl.multiple_of(ci * c, c), c)
            ab = ab_ref[0, rows, :]
            hi, mid, lo = _split3(a_neg * _softplus(ab + dt_b))
            gc = _dot(tri, hi) + _dot(tri, mid) + _dot(tri, lo)
            gc_s[rows, :] = gc
            gct_s[rows, :] = gc.T
            egc_s[rows, :] = jnp.exp(gc)
            ekd_s[rows, :] = jnp.exp(gc[c - 1:c, :] - gc)
            beta_s[rows, :] = jax.nn.sigmoid(ab)
            return carry

        lax.fori_loop(0, n_c, gate_chunk, 0)

    def l2n(x):
        return x * lax.rsqrt(jnp.sum(x * x, axis=-1, keepdims=True) + EPS)

    def operands(w, ci):
        r0 = ci * c
        rows = pl.ds(r0, c)

        def conv_silu(x_ref, cw_ref):
            cur = x_ref[0, rows, :].astype(F32)
            if ci > 0:
                tail = x_ref[0, pl.ds(r0 - BF16_ROWS, BF16_ROWS), :].astype(F32)[BF16_ROWS - SUBLANES:]
            else:
                tail = jnp.zeros((SUBLANES, cur.shape[1]), F32)
            ext = jnp.concatenate([tail, cur], axis=0)
            cw = cw_ref[...]
            y = cur * cw[CONV_W - 1:CONV_W]
            for s in range(1, CONV_W):
                y = y + ext[SUBLANES - s:SUBLANES - s + c] * cw[CONV_W - 1 - s:CONV_W - s]
            return _silu(y)

        def column(scr, idx):
            return jnp.sum(jnp.where(col == idx, scr[rows, :], 0.0), axis=-1, keepdims=True)

        q = l2n(conv_silu(q_ref, cwq_ref)) * (GDN_DK ** -0.5)
        k = l2n(conv_silu(k_ref, cwk_ref))
        v = conv_silu(v_ref, cwv_ref)
        gcc = column(gc_s, h)
        egc = column(egc_s, h)
        ekd = column(ekd_s, h)
        beta = column(beta_s, h + GDN_HEADS)
        grow = gct_s[pl.ds(r0 + h, 1), :]
        tri_i = row >= col
        w["dinc"][...] = jnp.where(tri_i, jnp.exp(jnp.where(tri_i, gcc - grow, 0.0)), 0.0)
        kb = k * beta
        w["k"][...] = k.astype(BF16)
        w["kb"][...] = kb.astype(BF16)
        w["q"][...] = q.astype(BF16)
        w["kd"][...] = (k * ekd).astype(BF16)
        w["vk"][:, :GDN_DV] = (v * beta).astype(BF16)
        w["vk"][:, GDN_DV:] = (kb * egc).astype(BF16)
        qa_s[ci, :, 0:GDN_DK] = (q * egc).astype(BF16)

    def gram(w, ci):
        r = _dot_nt(jnp.concatenate([w["kb"][...], w["q"][...]], axis=0), w["k"][...])
        d_incl = w["dinc"][...]
        low = r[:c] * jnp.where(row > col, d_incl, 0.0)
        qa_s[ci, :, GDN_DK:] = (r[c:] * d_incl).astype(BF16)
        w["low"][...] = low
        ld = jnp.where((row >> INV_BASE_LOG2) == (col >> INV_BASE_LOG2), low, 0.0)
        w["x"][...] = jnp.where(row == col, 1.0, 0.0) - ld
        w["p"][...] = ld.astype(BF16)

    def neumann(w, first, last):
        pb = w["p"][...]
        if first:
            w["p"][...] = _dot(pb, pb).astype(BF16)
        elif last:
            x = w["x"][...]
            w["x"][...] = x + _dot(x.astype(BF16), pb)
        else:
            x = w["x"][...]
            r = _dot(jnp.concatenate([x.astype(BF16), pb], axis=0), pb)
            w["x"][...] = x + r[:c]
            w["p"][...] = r[c:].astype(BF16)

    def merge_a(w, lg):
        pair = (row >> (lg + 1)) == (col >> (lg + 1))
        m = jnp.where(pair, (row >> lg) - (col >> lg), 0) > 0
        lm = jnp.where(m, w["low"][...], 0.0).astype(BF16)
        w["p"][...] = _dot(lm, w["x"][...].astype(BF16)).astype(BF16)

    def merge_b(w):
        x = w["x"][...]
        w["x"][...] = x - _dot(x.astype(BF16), w["p"][...])

    def solve(w, ci):
        uw = _dot(w["x"][...].astype(BF16), w["vk"][...])
        u_s[ci] = uw[:, :GDN_DV]
        uwb = uw.astype(BF16)
        w["wu"][...] = uwb
        pw_s[ci, c:2 * c, :] = uwb[:, GDN_DV:]

    def outer(w, ci):
        np_ = _dot_tn(w["kd"][...], w["wu"][...])
        n_s[ci] = np_[:, :GDN_DV]
        pw_s[ci, 0:c, :] = np_[:, GDN_DV:].astype(BF16)

    n_sq = INV_BASE_LOG2 - 1
    lgc = c.bit_length() - 1

    def prepare_stages(cis):
        over = lambda fn, *a: (lambda: [fn(work[ci], *[ci if x is cis else x for x in a])
                                        for ci in cis])
        stages = [over(operands, cis), over(gram, cis)]
        stages += [over(neumann, step == 0, step == n_sq) for step in range(n_sq + 1)]
        for lg in range(INV_BASE_LOG2, lgc):
            stages += [over(merge_a, lg), over(merge_b)]
        return stages + [over(solve, cis), over(outer, cis)]

    gn = gn_ref[...]
    lane1 = lax.broadcasted_iota(jnp.int32, (1, LANES), 1)

    def advance(ci, s_mat):
        sb = s_mat.astype(BF16)
        r = _dot(pw_s[ci], sb)
        g_last = jnp.sum(jnp.where(lane1 == h, gc_s[pl.ds(ci * c + c - 1, 1), :], 0.0),
                         axis=-1, keepdims=True)
        sv_s[ci, 0:c, :] = sb
        sv_s[ci, c:2 * c, :] = (u_s[ci] - r[c:2 * c]).astype(BF16)
        return s_mat * jnp.exp(g_last) - r[0:c] + n_s[ci]

    def emit(ci):
        rows = pl.ds(ci * c, c)
        o = _dot(qa_s[ci], sv_s[ci])
        zz = z_ref[0, rows, :].astype(F32)
        o_ref[0, rows, :] = (_rms(o, gn) * _silu(zz)).astype(o_ref.dtype)

    s_mat = jnp.zeros((GDN_DK, GDN_DV), F32)
    chain_todo, emit_todo = [], []
    for g in range(n_c // group):
        cis = list(range(g * group, (g + 1) * group))
        for stage in prepare_stages(cis):
            stage()
            if chain_todo:
                emit_todo.append(chain_todo[0])
                s_mat = advance(chain_todo.pop(0), s_mat)
        chain_todo += cis
    while chain_todo:
        if emit_todo:
            emit(emit_todo.pop(0))
        emit_todo.append(chain_todo[0])
        s_mat = advance(chain_todo.pop(0), s_mat)
    for ci in emit_todo:
        emit(ci)
    s_ref[0, 0] = s_mat


def _gdn_prompt(proj3, ab3, conv_w, gate_par, gdn_norm):
    bsz, t_len, _ = proj3.shape
    nh = GDN_HEADS
    blk = lambda off: pl.BlockSpec((1, t_len, LANES), lambda b, h: (b, 0, off + h))
    cw = lambda off: pl.BlockSpec((CONV_W, LANES), lambda b, h: (0, off + h))
    c = GDN_CHUNK
    n_c = t_len // c
    gate_scr = pltpu.VMEM((t_len, LANES), F32)
    group = 8
    work = [pltpu.VMEM((c, wide * LANES), dt) for _ in range(n_c) for _, wide, dt in _GDN_WORK]
    return pl.pallas_call(
        functools.partial(_gdn_prompt_kernel, group=group),
        grid=(bsz, nh),
        in_specs=[blk(0), blk(nh), blk(2 * nh), blk(3 * nh),
                  pl.BlockSpec((1, t_len, LANES), lambda b, h: (b, 0, 0)),
                  cw(0), cw(nh), cw(2 * nh),
                  pl.BlockSpec((SUBLANES, LANES), lambda b, h: (0, 0)),
                  pl.BlockSpec((1, LANES), lambda b, h: (0, 0))],
        out_specs=[pl.BlockSpec((1, t_len, LANES), lambda b, h: (b, 0, h)),
                   pl.BlockSpec((1, 1, GDN_DK, GDN_DV), lambda b, h: (b, h, 0, 0))],
        out_shape=[jax.ShapeDtypeStruct((bsz, t_len, MIX_A), BF16),
                   jax.ShapeDtypeStruct((bsz, nh, GDN_DK, GDN_DV), F32)],
        scratch_shapes=[gate_scr, gate_scr, gate_scr, gate_scr, gate_scr,
                        pltpu.VMEM((n_c, c, GDN_DV), F32), pltpu.VMEM((n_c, GDN_DK, GDN_DV), F32),
                        pltpu.VMEM((n_c, 2 * c, GDN_DK), BF16), pltpu.VMEM((n_c, c, GDN_DK + c), BF16),
                        pltpu.VMEM((n_c, GDN_DK + c, GDN_DV), BF16)] + work,
        compiler_params=_cparams(("parallel", "arbitrary")),
        name="gdn_prompt",
    )(proj3, proj3, proj3, proj3, ab3, conv_w, conv_w, conv_w, gate_par, gdn_norm.reshape(1, LANES))


def _gdn_sample_kernel(x_ref, ab_ref, hist_ref, cw_ref, gp_ref, gn_ref, s0_ref, o_ref, s_ref,
                       *, t_len):
    nh = GDN_HEADS
    g8 = SUBLANES
    n = nh * g8
    assert t_len + CONV_W - 1 <= g8
    cw = cw_ref[...]
    gp = gp_ref[...]
    gn = gn_ref[...]
    row8 = lax.broadcasted_iota(jnp.int32, (g8, QKV_W), 0)
    tpos = lax.broadcasted_iota(jnp.int32, (n, 1), 0) & (g8 - 1)
    rhead = lax.broadcasted_iota(jnp.int32, (n, GDN_DV), 0) >> 3
    ri = lax.broadcasted_iota(jnp.int32, (n, n), 0)
    ci = lax.broadcasted_iota(jnp.int32, (n, n), 1)
    tri = jnp.where((ri >> 3) == (ci >> 3), ri - ci, -1)
    eye = jnp.where(ri == ci, 1.0, 0.0)
    sel0 = jnp.where(lax.broadcasted_iota(jnp.int32, (n, LANES), 1) == 0, 1.0, 0.0).astype(BF16)
    valid = tpos < t_len
    a_neg = jnp.concatenate([jnp.broadcast_to(-jnp.exp(gp[0:1, h:h + 1]), (g8, 1)) for h in range(nh)], 0)
    dt_b = jnp.concatenate([jnp.broadcast_to(gp[1:2, h:h + 1], (g8, 1)) for h in range(nh)], 0)

    def l2n(x):
        return x * lax.rsqrt(jnp.sum(x * x, axis=-1, keepdims=True) + EPS)

    for b in range(x_ref.shape[0]):
        xb = x_ref[b]
        x = xb[:, :QKV_W].astype(F32)
        hst = hist_ref[b]
        y = x * cw[CONV_W - 1:CONV_W]
        for s in range(1, CONV_W):
            y = y + pltpu.roll(x, s, axis=0) * cw[CONV_W - 1 - s:CONV_W - s]
        for j in range(CONV_W - 1):
            hj = hst if j == 0 else pltpu.roll(hst, g8 - j, axis=0)
            y = y + jnp.where(row8 + j < CONV_W - 1, hj, 0.0) * cw[j:j + 1]
        qkv = jnp.where(row8 < t_len, _silu(y), 0.0)

        def stack(off):
            return jnp.concatenate([qkv[:, off + h * GDN_DK:off + (h + 1) * GDN_DK] for h in range(nh)], 0)

        q = l2n(stack(0)) * (GDN_DK ** -0.5)
        k = l2n(stack(QK_W))
        v = stack(2 * QK_W)
        z = jnp.concatenate([xb[:, 3 * QK_W + h * GDN_DV:3 * QK_W + (h + 1) * GDN_DV].astype(F32)
                             for h in range(nh)], 0)
        ab = ab_ref[b]
        alpha = jnp.concatenate([ab[:, h:h + 1] for h in range(nh)], 0)
        braw = jnp.concatenate([ab[:, nh + h:nh + h + 1] for h in range(nh)], 0)
        g = jnp.where(valid, a_neg * _softplus(alpha + dt_b), 0.0)
        beta = jnp.where(valid, jax.nn.sigmoid(braw), 0.0)
        gc = g
        s = 1
        while s < g8:
            gc = gc + jnp.where(tpos >= s, pltpu.roll(gc, s, axis=0), 0.0)
            s *= 2
        g_last = jnp.concatenate([jnp.broadcast_to(gc[h * g8 + g8 - 1:(h + 1) * g8, :], (g8, 1))
                                  for h in range(nh)], 0)
        xg = jnp.broadcast_to(gc, (n, LANES))
        hi = xg.astype(BF16)
        r1 = xg - hi.astype(F32)
        mid = r1.astype(BF16)
        lo = (r1 - mid.astype(F32)).astype(BF16)
        grow = _dot_nt(sel0, hi) + _dot_nt(sel0, mid) + _dot_nt(sel0, lo)
        d_incl = jnp.where(tri >= 0, jnp.exp(jnp.where(tri >= 0, gc - grow, 0.0)), 0.0)
        d_strict = jnp.where(tri > 0, d_incl, 0.0)
        egc = jnp.exp(gc)
        kb = k * beta
        kbf = k.astype(BF16)
        low = _dot_nt(kb.astype(BF16), kbf) * d_strict
        a_in = _dot_nt(q.astype(BF16), kbf) * d_incl
        t_inv = eye - low
        p = low
        covered = 2
        while covered < t_len:
            pb = p.astype(BF16)
            p = _dot(pb, pb)
            t_inv = t_inv + _dot(t_inv.astype(BF16), p.astype(BF16))
            covered *= 2
        t_inv = t_inv.astype(BF16)
        u = _dot(t_inv, (v * beta).astype(BF16))
        w = _dot(t_inv, (kb * egc).astype(BF16))
        qg = q * egc
        k_dec = (k * jnp.exp(g_last - gc)).astype(BF16)
        ws, qs = [], []
        for h in range(nh):
            hs = slice(h * g8, (h + 1) * g8)
            lhs = jnp.concatenate([w[hs], qg[hs]], 0).astype(BF16)
            r = _dot(lhs, s0_ref[b, h].astype(BF16))
            ws.append(r[:g8])
            qs.append(r[g8:])
        v_new = u - jnp.concatenate(ws, 0)
        o = jnp.concatenate(qs, 0) + _dot(a_in.astype(BF16), v_new.astype(BF16))
        for h in range(nh):
            vm = jnp.where(rhead == h, v_new, 0.0).astype(BF16)
            dec = jnp.exp(g_last[h * g8:h * g8 + 1, :])
            s_ref[b, h] = s0_ref[b, h] * dec + _dot_tn(k_dec, vm)
        o_ref[b] = (_rms(o, gn) * _silu(z)).astype(o_ref.dtype)


def _gdn_sample(proj3, ab3, hist, conv_w, gate_par, gdn_norm, s0, *, bb):
    bsz, t_len, _ = proj3.shape
    nh, g8 = GDN_HEADS, SUBLANES
    pad_t = lambda a, rows: jnp.pad(a, ((0, 0), (0, g8 - rows), (0, 0)))
    x8 = pad_t(proj3[:, :, :4 * QK_W], t_len)
    ab8 = pad_t(ab3, t_len)
    hist8 = pad_t(hist, CONV_W - 1)
    o, s_new = pl.pallas_call(
        functools.partial(_gdn_sample_kernel, t_len=t_len),
        grid=(bsz // bb,),
        in_specs=[pl.BlockSpec((bb, g8, 4 * QK_W), lambda i: (i, 0, 0)),
                  pl.BlockSpec((bb, g8, LANES), lambda i: (i, 0, 0)),
                  pl.BlockSpec((bb, g8, QKV_W), lambda i: (i, 0, 0)),
                  pl.BlockSpec((CONV_W, QKV_W), lambda i: (0, 0)),
                  pl.BlockSpec((SUBLANES, LANES), lambda i: (0, 0)),
                  pl.BlockSpec((1, LANES), lambda i: (0, 0)),
                  pl.BlockSpec((bb, nh, GDN_DK, GDN_DV), lambda i: (i, 0, 0, 0))],
        out_specs=[pl.BlockSpec((bb, nh * g8, GDN_DV), lambda i: (i, 0, 0)),
                   pl.BlockSpec((bb, nh, GDN_DK, GDN_DV), lambda i: (i, 0, 0, 0))],
        out_shape=[jax.ShapeDtypeStruct((bsz, nh * g8, GDN_DV), BF16),
                   jax.ShapeDtypeStruct((bsz, nh, GDN_DK, GDN_DV), F32)],
        compiler_params=_cparams(("parallel",)),
        name="gdn_sample",
    )(x8, ab8, hist8, conv_w, gate_par, gdn_norm.reshape(1, LANES), s0)
    o = o.reshape(bsz, nh, g8, GDN_DV)[:, :, :t_len].transpose(0, 2, 1, 3)
    return o.reshape(bsz, t_len, MIX_A), s_new


def _pool_finish(d_groups, pw_ref, ps_ref, zb):
    outs = [_dot(d.astype(BF16), pw_ref[gi]) for gi, d in enumerate(d_groups)]
    return jnp.concatenate(outs, axis=-1) * ps_ref[...] * _silu(zb)


def _pool_tile(u, halo, zb, pos, pw_ref, ps_ref):
    ext = jnp.concatenate([halo, u], axis=0)
    d_groups = []
    for gi, win in enumerate(POOL_WINDOWS):
        sl = slice(gi * POOL_CH, (gi + 1) * POOL_CH)
        lvl = ext[:, sl]
        s = 1
        while s < win:
            lvl = lvl + pltpu.roll(lvl, s, axis=0)
            s *= 2
        cnt = jnp.minimum(win, pos + 1).astype(F32)
        d_groups.append(lvl[POOL_HALO:, :] / cnt - u[:, sl])
    return _pool_finish(d_groups, pw_ref, ps_ref, zb)


def _mix_pool_kernel(x_ref, oa_ref, u_ref, halo_ref, zb_ref, pw_ref, ps_ref, wo_ref, nc_ref, wq_ref,
                     x1_ref, qx_ref, *, pos0):
    i = pl.program_id(1)
    tt = u_ref.shape[1]
    halo = jnp.where(i > 0, halo_ref[0].astype(F32), 0.0)
    pos = pos0 + i * tt + lax.broadcasted_iota(jnp.int32, (tt, 1), 0)
    ob = _pool_tile(u_ref[0].astype(F32), halo, zb_ref[0].astype(F32), pos, pw_ref, ps_ref)
    acc = _dot(oa_ref[0], wo_ref[0:MIX_A, :]) + _dot(ob.astype(BF16), wo_ref[MIX_A:D_MODEL, :])
    x1 = x_ref[0] + acc
    x1_ref[0] = x1
    qx_ref[0] = _dot(_rms(x1, nc_ref[...]).astype(BF16), wq_ref[...]).astype(qx_ref.dtype)


def _mix_pool(x3, oa3, proj3, pool_w, pool_scale, w_out, norm_cross, w_cq, *, tt, pos0):
    bsz, t_len, d = x3.shape
    ub, zb = COL_ZA // MIX_B + 1, COL_ZA // MIX_B + 2
    hb = tt // POOL_HALO
    const2 = lambda b, i: (0, 0)
    resident = dict(pipeline_mode=pl.Buffered(1))
    row = lambda w: pl.BlockSpec((1, tt, w), lambda b, i: (b, i, 0))
    return pl.pallas_call(
        functools.partial(_mix_pool_kernel, pos0=pos0),
        grid=(bsz, t_len // tt),
        in_specs=[row(d), row(MIX_A),
                  pl.BlockSpec((1, tt, MIX_B), lambda b, i: (b, i, ub)),
                  pl.BlockSpec((1, POOL_HALO, MIX_B), lambda b, i: (b, jnp.maximum(i * hb - 1, 0), ub)),
                  pl.BlockSpec((1, tt, MIX_B), lambda b, i: (b, i, zb)),
                  pl.BlockSpec((POOL_GROUPS, POOL_CH, POOL_CH), lambda b, i: (0, 0, 0), **resident),
                  pl.BlockSpec((1, MIX_B), const2, **resident),
                  pl.BlockSpec((d, d), const2, **resident),
                  pl.BlockSpec((1, d), const2, **resident),
                  pl.BlockSpec((d, d), const2, **resident)],
        out_specs=[row(d), row(d)],
        out_shape=[jax.ShapeDtypeStruct((bsz, t_len, d), F32), jax.ShapeDtypeStruct((bsz, t_len, d), BF16)],
        compiler_params=_cparams(("parallel", "arbitrary")),
        name="mix_pool",
    )(x3, oa3, proj3, proj3, proj3, pool_w, pool_scale.reshape(1, MIX_B), w_out,
      norm_cross.reshape(1, d), w_cq)


def _pool_sample_kernel(ext_ref, zb_ref, pw_ref, ps_ref, o_ref, *, pos0):
    t_len = zb_ref.shape[0]
    for t in range(t_len):
        e = POOL_BUF + t
        d_groups = []
        for gi, win in enumerate(POOL_WINDOWS):
            sl = slice(gi * POOL_CH, (gi + 1) * POOL_CH)
            tot = ext_ref[e, :, sl]
            for j in range(1, win):
                tot = tot + ext_ref[e - j, :, sl]
            cnt = float(min(win, pos0 + t + 1))
            d_groups.append(tot / cnt - ext_ref[e, :, sl])
        o_ref[t] = _pool_finish(d_groups, pw_ref, ps_ref, zb_ref[t].astype(F32)).astype(o_ref.dtype)


def _pool_sample(ext_tm, zb_tm, pool_w, pool_scale, *, bb, pos0):
    t_len, bsz, _ = zb_tm.shape
    return pl.pallas_call(
        functools.partial(_pool_sample_kernel, pos0=pos0),
        grid=(bsz // bb,),
        in_specs=[pl.BlockSpec((POOL_BUF + t_len, bb, MIX_B), lambda i: (0, i, 0)),
                  pl.BlockSpec((t_len, bb, MIX_B), lambda i: (0, i, 0)),
                  pl.BlockSpec((POOL_GROUPS, POOL_CH, POOL_CH), lambda i: (0, 0, 0)),
                  pl.BlockSpec((1, MIX_B), lambda i: (0, 0))],
        out_specs=pl.BlockSpec((t_len, bb, MIX_B), lambda i: (0, i, 0)),
        out_shape=jax.ShapeDtypeStruct((t_len, bsz, MIX_B), BF16),
        compiler_params=_cparams(("parallel",)),
        name="pool_sample",
    )(ext_tm, zb_tm, pool_w, pool_scale.reshape(1, MIX_B))


def _mix_out_kernel(x_ref, oa_ref, ob_ref, wo_ref, nc_ref, wq_ref, x1_ref, qx_ref):
    acc = _dot(oa_ref[...], wo_ref[0:MIX_A, :]) + _dot(ob_ref[...], wo_ref[MIX_A:D_MODEL, :])
    x1 = x_ref[...] + acc
    x1_ref[...] = x1
    qx_ref[...] = _dot(_rms(x1, nc_ref[...]).astype(BF16), wq_ref[...]).astype(qx_ref.dtype)


def _mix_out(x, oa, ob, w_out, norm_cross, w_cq, *, tm):
    n, d = x.shape
    const = lambda i: (0, 0)
    return pl.pallas_call(
        _mix_out_kernel,
        grid=(n // tm,),
        in_specs=[pl.BlockSpec((tm, d), lambda i: (i, 0)),
                  pl.BlockSpec((tm, MIX_A), lambda i: (i, 0)),
                  pl.BlockSpec((tm, MIX_B), lambda i: (i, 0)),
                  pl.BlockSpec((d, d), const), pl.BlockSpec((1, d), const), pl.BlockSpec((d, d), const)],
        out_specs=[pl.BlockSpec((tm, d), lambda i: (i, 0)), pl.BlockSpec((tm, d), lambda i: (i, 0))],
        out_shape=[jax.ShapeDtypeStruct((n, d), F32), jax.ShapeDtypeStruct((n, d), BF16)],
        compiler_params=_cparams(("parallel",)),
        name="mix_out",
    )(x, oa, ob, w_out, norm_cross.reshape(1, d), w_cq)


def _xattn_kernel(q_ref, k_ref, v_ref, ctx_ref):
    q = q_ref[0]
    scale = X_HEAD_DIM ** -0.5
    for h in range(X_HEADS):
        sl = slice(h * X_HEAD_DIM, (h + 1) * X_HEAD_DIM)
        s = _dot_nt(q[:, sl], k_ref[0, :, sl].astype(BF16)) * scale
        p = jnp.exp(s - jnp.max(s, axis=-1, keepdims=True))
        p = p / jnp.sum(p, axis=-1, keepdims=True)
        ctx_ref[0, :, sl] = _dot(p.astype(BF16), v_ref[0, :, sl].astype(BF16)).astype(ctx_ref.dtype)


def _xattn(qx3, mk3, mv3, *, tq):
    bsz, t_len, d = qx3.shape
    n_mem = mk3.shape[1]
    return pl.pallas_call(
        _xattn_kernel,
        grid=(bsz, t_len // tq),
        in_specs=[pl.BlockSpec((1, tq, d), lambda b, i: (b, i, 0)),
                  pl.BlockSpec((1, n_mem, d), lambda b, i: (b, 0, 0)),
                  pl.BlockSpec((1, n_mem, d), lambda b, i: (b, 0, 0))],
        out_specs=pl.BlockSpec((1, tq, d), lambda b, i: (b, i, 0)),
        out_shape=jax.ShapeDtypeStruct((bsz, t_len, d), BF16),
        compiler_params=_cparams(("parallel", "arbitrary")),
        name="xattn",
    )(qx3, mk3, mv3)


def _xattn_native_kernel(q_ref, k_ref, v_ref, o_ref, *, t_len):
    nj = X_HEAD_DIM // LANES
    grp = nj * X_HEADS
    th = t_len * X_HEADS
    scale = X_HEAD_DIM ** -0.5
    for b in range(q_ref.shape[0]):
        z = _dot_nt(q_ref[b], k_ref[b].astype(BF16))
        ncol = z.shape[1]
        r = lax.broadcasted_iota(jnp.int32, (th, ncol), 0) & (X_HEADS - 1)
        c = lax.broadcasted_iota(jnp.int32, (th, ncol), 1) & (grp - 1)
        s = None
        for j in range(nj):
            zj = jnp.where(c == r + j * X_HEADS, z[j * th:(j + 1) * th, :], 0.0)
            if j:
                zj = pltpu.roll(zj, ncol - j * X_HEADS, axis=1)
            s = zj if s is None else s + zj
        sm = jnp.where(c == r, s * scale, -jnp.inf)
        p = jnp.exp(sm - jnp.max(sm, axis=1, keepdims=True))
        p = p / jnp.sum(p, axis=1, keepdims=True)
        pp = jnp.concatenate([p if j == 0 else pltpu.roll(p, j * X_HEADS, axis=1)
                              for j in range(nj)], axis=0).astype(BF16)
        o_ref[b] = _dot(pp, v_ref[b].astype(BF16)).astype(o_ref.dtype)


def _xattn_native(q_rows, k_rows, v_rows, *, t_len, bb):
    bsz, nq, _ = q_rows.shape
    nk = k_rows.shape[1]
    return pl.pallas_call(
        functools.partial(_xattn_native_kernel, t_len=t_len),
        grid=(bsz // bb,),
        in_specs=[pl.BlockSpec((bb, nq, LANES), lambda i: (i, 0, 0)),
                  pl.BlockSpec((bb, nk, LANES), lambda i: (i, 0, 0)),
                  pl.BlockSpec((bb, nk, LANES), lambda i: (i, 0, 0))],
        out_specs=pl.BlockSpec((bb, nq, LANES), lambda i: (i, 0, 0)),
        out_shape=jax.ShapeDtypeStruct((bsz, nq, LANES), BF16),
        compiler_params=_cparams(("parallel",)),
        name="xattn_native",
    )(q_rows, k_rows, v_rows)


def _kv_rows(cache):
    bsz, n_mem, nh, dh = cache.shape
    nj = dh // LANES
    return cache.reshape(bsz, n_mem, nh, nj, LANES).transpose(0, 1, 3, 2, 4).reshape(
        bsz, n_mem * nj * nh, LANES)


def _attn_out_kernel(x1_ref, ctx_ref, wco_ref, nf_ref, y_ref):
    x2 = x1_ref[...] + _dot(ctx_ref[...], wco_ref[...])
    y_ref[...] = _rms(x2, nf_ref[...])


def _attn_out(x1, ctx, w_co, norm_final, *, tm):
    n, d = x1.shape
    const = lambda i: (0, 0)
    return pl.pallas_call(
        _attn_out_kernel,
        grid=(n // tm,),
        in_specs=[pl.BlockSpec((tm, d), lambda i: (i, 0)), pl.BlockSpec((tm, d), lambda i: (i, 0)),
                  pl.BlockSpec((d, d), const, pipeline_mode=pl.Buffered(1)),
                  pl.BlockSpec((1, d), const, pipeline_mode=pl.Buffered(1))],
        out_specs=pl.BlockSpec((tm, d), lambda i: (i, 0)),
        out_shape=jax.ShapeDtypeStruct((n, d), F32),
        compiler_params=_cparams(("parallel",)),
        name="attn_out",
    )(x1, ctx, w_co, norm_final.reshape(1, d))


def kernel(x_prompt, x_sample, mem_prompt, cache_mem_k, cache_mem_v, state_delta, state_conv,
           state_pool, norm_mix, w_in, conv_w, a_log, dt_bias, gdn_norm, pool_w, pool_scale,
           w_out, norm_mem, norm_cross, w_cq, w_ck, w_cv, w_co, norm_final):
    bp, tp, d = x_prompt.shape
    bs, ts, _ = x_sample.shape
    n_mem = mem_prompt.shape[1]

    w_main, w_gate = _repack_w_in(w_in[0].T, tk=512, tn=1024)
    wo, wcq, wco = (w[0].astype(BF16) for w in (w_out, w_cq, w_co))
    pw = pool_w[0].astype(BF16)
    gate_par = jnp.zeros((SUBLANES, LANES), F32)
    gate_par = gate_par.at[0, :GDN_HEADS].set(a_log[0]).at[1, :GDN_HEADS].set(dt_bias[0])
    cw = conv_w[0]

    mem2d = mem_prompt.reshape(bp * n_mem, d)
    mk = _norm_proj(mem2d, norm_mem[0], w_ck[0], tm=bp * n_mem, tn=1024, out_dtype=F32)
    mv = _norm_proj(mem2d, norm_mem[0], w_cv[0], tm=bp * n_mem, tn=1024, out_dtype=F32)

    proj_p, ab_p = _norm_proj(x_prompt.reshape(bp * tp, d), norm_mix[0], w_main, w_gate,
                              tm=1024, tn=1024, out_dtype=BF16)
    proj_p3 = proj_p.reshape(bp, tp, W_MAIN)
    oa_p, delta_p = _gdn_prompt(proj_p3, ab_p.reshape(bp, tp, LANES), cw, gate_par, gdn_norm[0])
    x1_p, qx_p = _mix_pool(x_prompt, oa_p, proj_p3, pw, pool_scale[0], wo, norm_cross[0], wcq,
                           tt=512, pos0=0)
    ctx_p = _xattn(qx_p, mk.reshape(bp, n_mem, d), mv.reshape(bp, n_mem, d), tq=512)
    y_p = _attn_out(x1_p.reshape(bp * tp, d), ctx_p.reshape(bp * tp, d), wco, norm_final,
                    tm=512).reshape(bp, tp, d)
    conv_p = proj_p3[:, tp - (CONV_W - 1):, :QKV_W].astype(F32)
    pool_p = proj_p3[:, tp - POOL_BUF:, COL_ZA + MIX_A:COL_ZA + MIX_A + MIX_B].astype(F32)

    proj_s, ab_s = _norm_proj(x_sample.reshape(bs * ts, d), norm_mix[0], w_main, w_gate,
                              tm=256, tn=1024, out_dtype=BF16)
    proj_s3 = proj_s.reshape(bs, ts, W_MAIN)
    oa_s, delta_s = _gdn_sample(proj_s3, ab_s.reshape(bs, ts, LANES), state_conv[0], cw, gate_par,
                                gdn_norm[0], state_delta[0], bb=4)
    u_s = proj_s3[:, :, COL_ZA + MIX_A:COL_ZA + MIX_A + MIX_B].astype(F32)
    ext_s = jnp.concatenate([state_pool[0], u_s], axis=1)
    zb_tm = proj_s3[:, :, COL_ZA + MIX_A + MIX_B:].transpose(1, 0, 2)
    ob_s = _pool_sample(ext_s.transpose(1, 0, 2), zb_tm, pw, pool_scale[0], bb=32,
                        pos0=PAST_LEN).transpose(1, 0, 2)
    x1_s, qx_s = _mix_out(x_sample.reshape(bs * ts, d), oa_s.reshape(-1, MIX_A),
                          ob_s.reshape(-1, MIX_B), wo, norm_cross[0], wcq, tm=256)
    nj = X_HEAD_DIM // LANES
    q_rows = qx_s.reshape(bs, ts, X_HEADS, nj, LANES).transpose(0, 3, 1, 2, 4).reshape(
        bs, nj * ts * X_HEADS, LANES)
    ctx_rows = _xattn_native(q_rows, _kv_rows(cache_mem_k[0]), _kv_rows(cache_mem_v[0]),
                             t_len=ts, bb=2)
    ctx_s = ctx_rows.reshape(bs, nj, ts, X_HEADS, LANES).transpose(0, 2, 3, 1, 4).reshape(bs * ts, d)
    y_s = _attn_out(x1_s, ctx_s, wco, norm_final, tm=256).reshape(bs, ts, d)
    conv_s = jnp.concatenate([state_conv[0], proj_s3[:, :, :QKV_W].astype(F32)], axis=1)[:, ts:]
    pool_s = ext_s[:, ts:]

    hd = (X_HEADS, X_HEAD_DIM)
    return (y_p, y_s, mk.reshape(1, bp, n_mem, *hd), mv.reshape(1, bp, n_mem, *hd),
            delta_p[None], conv_p[None], pool_p[None], delta_s[None], conv_s[None], pool_s[None])
```

```python
import functools
import math

import jax
import jax.numpy as jnp
from jax import lax
from jax.experimental import pallas as pl
from jax.experimental.pallas import tpu as pltpu

F32 = jnp.float32
BF16 = jnp.bfloat16

D_MODEL = 2048
MIX_A = D_MODEL // 2
MIX_B = D_MODEL - MIX_A
GDN_HEADS = 8
GDN_DK = MIX_A // GDN_HEADS
GDN_DV = MIX_A // GDN_HEADS
QK_W = GDN_HEADS * GDN_DK
QKV_W = 2 * QK_W + GDN_HEADS * GDN_DV
CONV_W = 4
POOL_WINDOWS = (2, 4, 8, 16)
POOL_GROUPS = len(POOL_WINDOWS)
POOL_CH = MIX_B // POOL_GROUPS
POOL_BUF = max(POOL_WINDOWS) - 1
X_HEADS = 4
X_HEAD_DIM = D_MODEL // X_HEADS
PAST_LEN = 16384
EPS = 1e-6
COL_ZA = QKV_W
COL_A = COL_ZA + GDN_HEADS * GDN_DV
COL_B = COL_A + GDN_HEADS
COL_U = COL_B + GDN_HEADS
COL_ZB = COL_U + MIX_B
IN_COLS = COL_ZB + MIX_B

W_MAIN = IN_COLS - 2 * GDN_HEADS
LANES = 128
SUBLANES = 8
BF16_ROWS = 16
GDN_CHUNK = 128
POOL_HALO = 16
VMEM_LIMIT = 56 * 1024 * 1024


def _cparams(sem):
    return pltpu.CompilerParams(dimension_semantics=sem, vmem_limit_bytes=VMEM_LIMIT)


def _dot(a, b):
    return jnp.dot(a, b, preferred_element_type=F32)


def _dot_nt(a, b):
    return lax.dot_general(a, b, (((1,), (1,)), ((), ())), preferred_element_type=F32)


def _dot_tn(a, b):
    return lax.dot_general(a, b, (((0,), (0,)), ((), ())), preferred_element_type=F32)


def _rms(x, g):
    return x * lax.rsqrt(jnp.mean(x * x, axis=-1, keepdims=True) + EPS) * g


def _silu(x):
    return x * jax.nn.sigmoid(x)


def _softplus(x):
    return jnp.maximum(x, 0.0) + jnp.log1p(jnp.exp(-jnp.abs(x)))


def _repack_kernel(wt_ref, gt_ref, main_ref, gate_ref):
    main_ref[...] = wt_ref[...].T.astype(main_ref.dtype)

    @pl.when(pl.program_id(1) == 0)
    def _():
        head = gt_ref[...].T
        lane = lax.broadcasted_iota(jnp.int32, head.shape, 1)
        gate_ref[...] = jnp.where(lane < COL_U - COL_A, head, 0.0).astype(gate_ref.dtype)


def _repack_w_in(wt, *, tk, tn):
    k_dim = wt.shape[1]
    n_gate = COL_U - COL_A
    src_row = lambda j: pl.multiple_of(j * tn + (j // (COL_A // tn)) * n_gate, n_gate)
    return pl.pallas_call(
        _repack_kernel,
        grid=(k_dim // tk, W_MAIN // tn),
        in_specs=[pl.BlockSpec((pl.Element(tn), pl.Element(tk)), lambda i, j: (src_row(j), i * tk)),
                  pl.BlockSpec((pl.Element(LANES), pl.Element(tk)), lambda i, j: (COL_A, i * tk))],
        out_specs=[pl.BlockSpec((tk, tn), lambda i, j: (i, j)),
                   pl.BlockSpec((tk, LANES), lambda i, j: (i, 0))],
        out_shape=[jax.ShapeDtypeStruct((k_dim, W_MAIN), BF16),
                   jax.ShapeDtypeStruct((k_dim, LANES), BF16)],
        compiler_params=_cparams(("parallel", "arbitrary")),
        name="repack_w_in",
    )(wt, wt)


def _norm_proj_kernel(x_ref, g_ref, w_ref, *rest, with_side):
    if with_side:
        ws_ref, out_ref, side_ref, h_scr = rest
    else:
        out_ref, h_scr = rest

    @pl.when(pl.program_id(1) == 0)
    def _():
        h = _rms(x_ref[...], g_ref[...]).astype(BF16)
        h_scr[...] = h
        if with_side:
            side_ref[...] = _dot(h, ws_ref[...])

    out_ref[...] = _dot(h_scr[...], w_ref[...].astype(BF16)).astype(out_ref.dtype)


def _norm_proj(x, g, w, w_side=None, *, tm, tn, out_dtype):
    n, d = x.shape
    ncol = w.shape[1]
    with_side = w_side is not None
    in_specs = [pl.BlockSpec((tm, d), lambda i, j: (i, 0)),
                pl.BlockSpec((1, d), lambda i, j: (0, 0)),
                pl.BlockSpec((d, tn), lambda i, j: (0, j))]
    out_specs = [pl.BlockSpec((tm, tn), lambda i, j: (i, j))]
    out_shape = [jax.ShapeDtypeStruct((n, ncol), out_dtype)]
    args = [x, g.reshape(1, d), w]
    if with_side:
        in_specs.append(pl.BlockSpec((d, LANES), lambda i, j: (0, 0)))
        out_specs.append(pl.BlockSpec((tm, LANES), lambda i, j: (i, 0)))
        out_shape.append(jax.ShapeDtypeStruct((n, LANES), F32))
        args.append(w_side)
    res = pl.pallas_call(
        functools.partial(_norm_proj_kernel, with_side=with_side),
        grid=(n // tm, ncol // tn),
        in_specs=in_specs, out_specs=out_specs, out_shape=out_shape,
        scratch_shapes=[pltpu.VMEM((tm, d), BF16)],
        compiler_params=_cparams(("parallel", "arbitrary")),
        name="norm_proj_side" if with_side else "norm_proj",
    )(*args)
    return res if with_side else res[0]


INV_BASE_LOG2 = 4
_GDN_WORK = (("k", 1, BF16), ("kb", 1, BF16), ("q", 1, BF16), ("kd", 1, BF16), ("vk", 2, BF16),
             ("dinc", 1, F32), ("low", 1, F32), ("x", 1, F32), ("p", 1, BF16), ("wu", 2, BF16))


def _split3(x):
    hi = x.astype(BF16)
    r1 = x - hi.astype(F32)
    mid = r1.astype(BF16)
    lo = (r1 - mid.astype(F32)).astype(BF16)
    return hi, mid, lo


def _gdn_prompt_kernel(q_ref, k_ref, v_ref, z_ref, ab_ref, cwq_ref, cwk_ref, cwv_ref, gp_ref,
                       gn_ref, o_ref, s_ref,
                       gc_s, gct_s, egc_s, ekd_s, beta_s, u_s, n_s, pw_s, qa_s, sv_s, *work_refs, group):
    nw = len(_GDN_WORK)
    work = [{name: ref for (name, _, _), ref in zip(_GDN_WORK, work_refs[j * nw:(j + 1) * nw])}
            for j in range(len(work_refs) // nw)]
    h = pl.program_id(1)
    t_len = q_ref.shape[1]
    c = GDN_CHUNK
    n_c = t_len // c
    row = lax.broadcasted_iota(jnp.int32, (c, c), 0)
    col = lax.broadcasted_iota(jnp.int32, (c, c), 1)

    @pl.when(h == 0)
    def _():
        gp = gp_ref[...]
        a_neg = -jnp.exp(gp[0:1])
        dt_b = gp[1:2]
        tri = jnp.where(row >= col, 1.0, 0.0).astype(BF16)

        def gate_chunk(ci, carry):
            rows = pl.ds(pl.multiple_of(ci * c, c), c)
            ab = ab_ref[0, rows, :]
            hi, mid, lo = _split3(a_neg * _softplus(ab + dt_b))
            gc = _dot(tri, hi) + _dot(tri, mid) + _dot(tri, lo)
            gc_s[rows, :] = gc
            gct_s[rows, :] = gc.T
            egc_s[rows, :] = jnp.exp(gc)
            ekd_s[rows, :] = jnp.exp(gc[c - 1:c, :] - gc)
            beta_s[rows, :] = jax.nn.sigmoid(ab)
            return carry

        lax.fori_loop(0, n_c, gate_chunk, 0)

    def l2n(x):
        return x * lax.rsqrt(jnp.sum(x * x, axis=-1, keepdims=True) + EPS)

    def operands(w, ci):
        r0 = ci * c
        rows = pl.ds(r0, c)

        def conv_silu(x_ref, cw_ref):
            cur = x_ref[0, rows, :].astype(F32)
            if ci > 0:
                tail = x_ref[0, pl.ds(r0 - BF16_ROWS, BF16_ROWS), :].astype(F32)[BF16_ROWS - SUBLANES:]
            else:
                tail = jnp.zeros((SUBLANES, cur.shape[1]), F32)
            ext = jnp.concatenate([tail, cur], axis=0)
            cw = cw_ref[...]
            y = cur * cw[CONV_W - 1:CONV_W]
            for s in range(1, CONV_W):
                y = y + ext[SUBLANES - s:SUBLANES - s + c] * cw[CONV_W - 1 - s:CONV_W - s]
            return _silu(y)

        def column(scr, idx):
            return jnp.sum(jnp.where(col == idx, scr[rows, :], 0.0), axis=-1, keepdims=True)

        q = l2n(conv_silu(q_ref, cwq_ref)) * (GDN_DK ** -0.5)
        k = l2n(conv_silu(k_ref, cwk_ref))
        v = conv_silu(v_ref, cwv_ref)
        gcc = column(gc_s, h)
        egc = column(egc_s, h)
        ekd = column(ekd_s, h)
        beta = column(beta_s, h + GDN_HEADS)
        grow = gct_s[pl.ds(r0 + h, 1), :]
        tri_i = row >= col
        w["dinc"][...] = jnp.where(tri_i, jnp.exp(jnp.where(tri_i, gcc - grow, 0.0)), 0.0)
        kb = k * beta
        w["k"][...] = k.astype(BF16)
        w["kb"][...] = kb.astype(BF16)
        w["q"][...] = q.astype(BF16)
        w["kd"][...] = (k * ekd).astype(BF16)
        w["vk"][:, :GDN_DV] = (v * beta).astype(BF16)
        w["vk"][:, GDN_DV:] = (kb * egc).astype(BF16)
        qa_s[ci, :, 0:GDN_DK] = (q * egc).astype(BF16)

    def gram(w, ci):
        r = _dot_nt(jnp.concatenate([w["kb"][...], w["q"][...]], axis=0), w["k"][...])
        d_incl = w["dinc"][...]
        low = r[:c] * jnp.where(row > col, d_incl, 0.0)
        qa_s[ci, :, GDN_DK:] = (r[c:] * d_incl).astype(BF16)
        w["low"][...] = low
        ld = jnp.where((row >> INV_BASE_LOG2) == (col >> INV_BASE_LOG2), low, 0.0)
        w["x"][...] = jnp.where(row == col, 1.0, 0.0) - ld
        w["p"][...] = ld.astype(BF16)

    def neumann(w, first, last):
        pb = w["p"][...]
        if first:
            w["p"][...] = _dot(pb, pb).astype(BF16)
        elif last:
            x = w["x"][...]
            w["x"][...] = x + _dot(x.astype(BF16), pb)
        else:
            x = w["x"][...]
            r = _dot(jnp.concatenate([x.astype(BF16), pb], axis=0), pb)
            w["x"][...] = x + r[:c]
            w["p"][...] = r[c:].astype(BF16)

    def merge_a(w, lg):
        pair = (row >> (lg + 1)) == (col >> (lg + 1))
        m = jnp.where(pair, (row >> lg) - (col >> lg), 0) > 0
        lm = jnp.where(m, w["low"][...], 0.0).astype(BF16)
        w["p"][...] = _dot(lm, w["x"][...].astype(BF16)).astype(BF16)

    def merge_b(w):
        x = w["x"][...]
        w["x"][...] = x - _dot(x.astype(BF16), w["p"][...])

    def solve(w, ci):
        uw = _dot(w["x"][...].astype(BF16), w["vk"][...])
        u_s[ci] = uw[:, :GDN_DV]
        uwb = uw.astype(BF16)
        w["wu"][...] = uwb
        pw_s[ci, c:2 * c, :] = uwb[:, GDN_DV:]

    def outer(w, ci):
        np_ = _dot_tn(w["kd"][...], w["wu"][...])
        n_s[ci] = np_[:, :GDN_DV]
        pw_s[ci, 0:c, :] = np_[:, GDN_DV:].astype(BF16)

    n_sq = INV_BASE_LOG2 - 1
    lgc = c.bit_length() - 1

    def prepare_stages(cis):
        over = lambda fn, *a: (lambda: [fn(work[ci], *[ci if x is cis else x for x in a])
                                        for ci in cis])
        stages = [over(operands, cis), over(gram, cis)]
        stages += [over(neumann, step == 0, step == n_sq) for step in range(n_sq + 1)]
        for lg in range(INV_BASE_LOG2, lgc):
            stages += [over(merge_a, lg), over(merge_b)]
        return stages + [over(solve, cis), over(outer, cis)]

    gn = gn_ref[...]
    lane1 = lax.broadcasted_iota(jnp.int32, (1, LANES), 1)

    def advance(ci, s_mat):
        sb = s_mat.astype(BF16)
        r = _dot(pw_s[ci], sb)
        g_last = jnp.sum(jnp.where(lane1 == h, gc_s[pl.ds(ci * c + c - 1, 1), :], 0.0),
                         axis=-1, keepdims=True)
        sv_s[ci, 0:c, :] = sb
        sv_s[ci, c:2 * c, :] = (u_s[ci] - r[c:2 * c]).astype(BF16)
        return s_mat * jnp.exp(g_last) - r[0:c] + n_s[ci]

    def emit(ci):
        rows = pl.ds(ci * c, c)
        o = _dot(qa_s[ci], sv_s[ci])
        zz = z_ref[0, rows, :].astype(F32)
        o_ref[0, rows, :] = (_rms(o, gn) * _silu(zz)).astype(o_ref.dtype)

    s_mat = jnp.zeros((GDN_DK, GDN_DV), F32)
    chain_todo, emit_todo = [], []
    for g in range(n_c // group):
        cis = list(range(g * group, (g + 1) * group))
        for stage in prepare_stages(cis):
            stage()
            if chain_todo:
                emit_todo.append(chain_todo[0])
                s_mat = advance(chain_todo.pop(0), s_mat)
        chain_todo += cis
    while chain_todo:
        if emit_todo:
            emit(emit_todo.pop(0))
        emit_todo.append(chain_todo[0])
        s_mat = advance(chain_todo.pop(0), s_mat)
    for ci in emit_todo:
        emit(ci)
    s_ref[0, 0] = s_mat


def _gdn_prompt(proj3, ab3, conv_w, gate_par, gdn_norm):
    bsz, t_len, _ = proj3.shape
    nh = GDN_HEADS
    blk = lambda off: pl.BlockSpec((1, t_len, LANES), lambda b, h: (b, 0, off + h))
    cw = lambda off: pl.BlockSpec((CONV_W, LANES), lambda b, h: (0, off + h))
    c = GDN_CHUNK
    n_c = t_len // c
    gate_scr = pltpu.VMEM((t_len, LANES), F32)
    group = 8
    work = [pltpu.VMEM((c, wide * LANES), dt) for _ in range(n_c) for _, wide, dt in _GDN_WORK]
    return pl.pallas_call(
        functools.partial(_gdn_prompt_kernel, group=group),
        grid=(bsz, nh),
        in_specs=[blk(0), blk(nh), blk(2 * nh), blk(3 * nh),
                  pl.BlockSpec((1, t_len, LANES), lambda b, h: (b, 0, 0)),
                  cw(0), cw(nh), cw(2 * nh),
                  pl.BlockSpec((SUBLANES, LANES), lambda b, h: (0, 0)),
                  pl.BlockSpec((1, LANES), lambda b, h: (0, 0))],
        out_specs=[pl.BlockSpec((1, t_len, LANES), lambda b, h: (b, 0, h)),
                   pl.BlockSpec((1, 1, GDN_DK, GDN_DV), lambda b, h: (b, h, 0, 0))],
        out_shape=[jax.ShapeDtypeStruct((bsz, t_len, MIX_A), BF16),
                   jax.ShapeDtypeStruct((bsz, nh, GDN_DK, GDN_DV), F32)],
        scratch_shapes=[gate_scr, gate_scr, gate_scr, gate_scr, gate_scr,
                        pltpu.VMEM((n_c, c, GDN_DV), F32), pltpu.VMEM((n_c, GDN_DK, GDN_DV), F32),
                        pltpu.VMEM((n_c, 2 * c, GDN_DK), BF16), pltpu.VMEM((n_c, c, GDN_DK + c), BF16),
                        pltpu.VMEM((n_c, GDN_DK + c, GDN_DV), BF16)] + work,
        compiler_params=_cparams(("parallel", "arbitrary")),
        name="gdn_prompt",
    )(proj3, proj3, proj3, proj3, ab3, conv_w, conv_w, conv_w, gate_par, gdn_norm.reshape(1, LANES))


def _gdn_sample_kernel(x_ref, ab_ref, hist_ref, cw_ref, gp_ref, gn_ref, s0_ref, o_ref, s_ref,
                       *, t_len):
    nh = GDN_HEADS
    g8 = SUBLANES
    n = nh * g8
    assert t_len + CONV_W - 1 <= g8
    cw = cw_ref[...]
    gp = gp_ref[...]
    gn = gn_ref[...]
    row8 = lax.broadcasted_iota(jnp.int32, (g8, QKV_W), 0)
    tpos = lax.broadcasted_iota(jnp.int32, (n, 1), 0) & (g8 - 1)
    rhead = lax.broadcasted_iota(jnp.int32, (n, GDN_DV), 0) >> 3
    ri = lax.broadcasted_iota(jnp.int32, (n, n), 0)
    ci = lax.broadcasted_iota(jnp.int32, (n, n), 1)
    tri = jnp.where((ri >> 3) == (ci >> 3), ri - ci, -1)
    eye = jnp.where(ri == ci, 1.0, 0.0)
    sel0 = jnp.where(lax.broadcasted_iota(jnp.int32, (n, LANES), 1) == 0, 1.0, 0.0).astype(BF16)
    valid = tpos < t_len
    a_neg = jnp.concatenate([jnp.broadcast_to(-jnp.exp(gp[0:1, h:h + 1]), (g8, 1)) for h in range(nh)], 0)
    dt_b = jnp.concatenate([jnp.broadcast_to(gp[1:2, h:h + 1], (g8, 1)) for h in range(nh)], 0)

    def l2n(x):
        return x * lax.rsqrt(jnp.sum(x * x, axis=-1, keepdims=True) + EPS)

    def operands(b):
        xb = x_ref[b]
        x = xb[:, :QKV_W].astype(F32)
        hst = hist_ref[b]
        y = x * cw[CONV_W - 1:CONV_W]
        for s in range(1, CONV_W):
            y = y + pltpu.roll(x, s, axis=0) * cw[CONV_W - 1 - s:CONV_W - s]
        for j in range(CONV_W - 1):
            hj = hst if j == 0 else pltpu.roll(hst, g8 - j, axis=0)
            y = y + jnp.where(row8 + j < CONV_W - 1, hj, 0.0) * cw[j:j + 1]
        qkv = jnp.where(row8 < t_len, _silu(y), 0.0)

        def stack(off):
            return jnp.concatenate([qkv[:, off + h * GDN_DK:off + (h + 1) * GDN_DK] for h in range(nh)], 0)

        q = l2n(stack(0)) * (GDN_DK ** -0.5)
        k = l2n(stack(QK_W))
        v = stack(2 * QK_W)
        ab = ab_ref[b]
        alpha = jnp.concatenate([ab[:, h:h + 1] for h in range(nh)], 0)
        braw = jnp.concatenate([ab[:, nh + h:nh + h + 1] for h in range(nh)], 0)
        g = jnp.where(valid, a_neg * _softplus(alpha + dt_b), 0.0)
        beta = jnp.where(valid, jax.nn.sigmoid(braw), 0.0)
        gc = g
        s = 1
        while s < g8:
            gc = gc + jnp.where(tpos >= s, pltpu.roll(gc, s, axis=0), 0.0)
            s *= 2
        g_last = jnp.concatenate([jnp.broadcast_to(gc[h * g8 + g8 - 1:(h + 1) * g8, :], (g8, 1))
                                  for h in range(nh)], 0)
        egc = jnp.exp(gc)
        kb = k * beta
        return dict(gc=gc, g_last=g_last, kbf=k.astype(BF16), kbb=kb.astype(BF16), qb=q.astype(BF16),
                    vb=(v * beta).astype(BF16), kg=(kb * egc).astype(BF16), qg=q * egc,
                    k_dec=(k * jnp.exp(g_last - gc)).astype(BF16))

    def gram(st):
        hi, mid, lo = _split3(jnp.broadcast_to(st["gc"], (n, LANES)))
        grow = _dot_nt(sel0, hi) + _dot_nt(sel0, mid) + _dot_nt(sel0, lo)
        d_incl = jnp.where(tri >= 0, jnp.exp(jnp.where(tri >= 0, st["gc"] - grow, 0.0)), 0.0)
        st["low"] = _dot_nt(st["kbb"], st["kbf"]) * jnp.where(tri > 0, d_incl, 0.0)
        st["a_in"] = (_dot_nt(st["qb"], st["kbf"]) * d_incl).astype(BF16)
        st["t_inv"] = eye - st["low"]
        st["p"] = st["low"].astype(BF16)

    def neumann(st):
        p = _dot(st["p"], st["p"])
        st["p"] = p.astype(BF16)
        st["t_inv"] = st["t_inv"] + _dot(st["t_inv"].astype(BF16), st["p"])

    def solve(st):
        tb = st["t_inv"].astype(BF16)
        st["u"] = _dot(tb, st["vb"])
        st["w"] = _dot(tb, st["kg"])

    def apply_state(st, b):
        ws, qs = [], []
        for h in range(nh):
            hs = slice(h * g8, (h + 1) * g8)
            lhs = jnp.concatenate([st["w"][hs], st["qg"][hs]], 0).astype(BF16)
            r = _dot(lhs, s0_ref[b, h].astype(BF16))
            ws.append(r[:g8])
            qs.append(r[g8:])
        st["v_new"] = st["u"] - jnp.concatenate(ws, 0)
        st["oq"] = jnp.concatenate(qs, 0)

    def finish(st, b):
        v_new = st["v_new"]
        o = st["oq"] + _dot(st["a_in"], v_new.astype(BF16))
        for h in range(nh):
            vm = jnp.where(rhead == h, v_new, 0.0).astype(BF16)
            dec = jnp.exp(st["g_last"][h * g8:h * g8 + 1, :])
            s_ref[b, h] = s0_ref[b, h] * dec + _dot_tn(st["k_dec"], vm)
        z = jnp.concatenate([x_ref[b][:, 3 * QK_W + h * GDN_DV:3 * QK_W + (h + 1) * GDN_DV].astype(F32)
                             for h in range(nh)], 0)
        o_ref[b] = (_rms(o, gn) * _silu(z)).astype(o_ref.dtype)

    bs_ = range(x_ref.shape[0])
    sts = [operands(b) for b in bs_]
    for st in sts:
        gram(st)
    covered = 2
    while covered < t_len:
        for st in sts:
            neumann(st)
        covered *= 2
    for st in sts:
        solve(st)
    for b in bs_:
        apply_state(sts[b], b)
    for b in bs_:
        finish(sts[b], b)


def _gdn_sample(proj3, ab3, hist, conv_w, gate_par, gdn_norm, s0, *, bb):
    bsz, t_len, _ = proj3.shape
    nh, g8 = GDN_HEADS, SUBLANES
    pad_t = lambda a, rows: jnp.pad(a, ((0, 0), (0, g8 - rows), (0, 0)))
    x8 = pad_t(proj3[:, :, :4 * QK_W], t_len)
    ab8 = pad_t(ab3, t_len)
    hist8 = pad_t(hist, CONV_W - 1)
    o, s_new = pl.pallas_call(
        functools.partial(_gdn_sample_kernel, t_len=t_len),
        grid=(bsz // bb,),
        in_specs=[pl.BlockSpec((bb, g8, 4 * QK_W), lambda i: (i, 0, 0)),
                  pl.BlockSpec((bb, g8, LANES), lambda i: (i, 0, 0)),
                  pl.BlockSpec((bb, g8, QKV_W), lambda i: (i, 0, 0)),
                  pl.BlockSpec((CONV_W, QKV_W), lambda i: (0, 0)),
                  pl.BlockSpec((SUBLANES, LANES), lambda i: (0, 0)),
                  pl.BlockSpec((1, LANES), lambda i: (0, 0)),
                  pl.BlockSpec((bb, nh, GDN_DK, GDN_DV), lambda i: (i, 0, 0, 0))],
        out_specs=[pl.BlockSpec((bb, nh * g8, GDN_DV), lambda i: (i, 0, 0)),
                   pl.BlockSpec((bb, nh, GDN_DK, GDN_DV), lambda i: (i, 0, 0, 0))],
        out_shape=[jax.ShapeDtypeStruct((bsz, nh * g8, GDN_DV), BF16),
                   jax.ShapeDtypeStruct((bsz, nh, GDN_DK, GDN_DV), F32)],
        compiler_params=_cparams(("parallel",)),
        name="gdn_sample",
    )(x8, ab8, hist8, conv_w, gate_par, gdn_norm.reshape(1, LANES), s0)
    o = o.reshape(bsz, nh, g8, GDN_DV)[:, :, :t_len].transpose(0, 2, 1, 3)
    return o.reshape(bsz, t_len, MIX_A), s_new


def _pool_finish(d_groups, pw_ref, ps_ref, zb):
    outs = [_dot(d.astype(BF16), pw_ref[gi]) for gi, d in enumerate(d_groups)]
    return jnp.concatenate(outs, axis=-1) * ps_ref[...] * _silu(zb)


def _pool_tile(u, halo, zb, pos, pw_ref, ps_ref):
    ext = jnp.concatenate([halo, u], axis=0)
    d_groups = []
    for gi, win in enumerate(POOL_WINDOWS):
        sl = slice(gi * POOL_CH, (gi + 1) * POOL_CH)
        lvl = ext[:, sl]
        s = 1
        while s < win:
            lvl = lvl + pltpu.roll(lvl, s, axis=0)
            s *= 2
        cnt = jnp.minimum(win, pos + 1).astype(F32)
        d_groups.append(lvl[POOL_HALO:, :] / cnt - u[:, sl])
    return _pool_finish(d_groups, pw_ref, ps_ref, zb)


def _mix_pool_kernel(x_ref, oa_ref, u_ref, halo_ref, zb_ref, pw_ref, ps_ref, wo_ref, nc_ref, wq_ref,
                     x1_ref, qx_ref, *, pos0):
    i = pl.program_id(1)
    tt = u_ref.shape[1]
    halo = jnp.where(i > 0, halo_ref[0].astype(F32), 0.0)
    pos = pos0 + i * tt + lax.broadcasted_iota(jnp.int32, (tt, 1), 0)
    ob = _pool_tile(u_ref[0].astype(F32), halo, zb_ref[0].astype(F32), pos, pw_ref, ps_ref)
    acc = _dot(oa_ref[0], wo_ref[0:MIX_A, :]) + _dot(ob.astype(BF16), wo_ref[MIX_A:D_MODEL, :])
    x1 = x_ref[0] + acc
    x1_ref[0] = x1
    qx_ref[0] = _dot(_rms(x1, nc_ref[...]).astype(BF16), wq_ref[...]).astype(qx_ref.dtype)


def _mix_pool(x3, oa3, proj3, pool_w, pool_scale, w_out, norm_cross, w_cq, *, tt, pos0):
    bsz, t_len, d = x3.shape
    ub, zb = COL_ZA // MIX_B + 1, COL_ZA // MIX_B + 2
    hb = tt // POOL_HALO
    const2 = lambda b, i: (0, 0)
    resident = dict(pipeline_mode=pl.Buffered(1))
    row = lambda w: pl.BlockSpec((1, tt, w), lambda b, i: (b, i, 0))
    return pl.pallas_call(
        functools.partial(_mix_pool_kernel, pos0=pos0),
        grid=(bsz, t_len // tt),
        in_specs=[row(d), row(MIX_A),
                  pl.BlockSpec((1, tt, MIX_B), lambda b, i: (b, i, ub)),
                  pl.BlockSpec((1, POOL_HALO, MIX_B), lambda b, i: (b, jnp.maximum(i * hb - 1, 0), ub)),
                  pl.BlockSpec((1, tt, MIX_B), lambda b, i: (b, i, zb)),
                  pl.BlockSpec((POOL_GROUPS, POOL_CH, POOL_CH), lambda b, i: (0, 0, 0), **resident),
                  pl.BlockSpec((1, MIX_B), const2, **resident),
                  pl.BlockSpec((d, d), const2, **resident),
                  pl.BlockSpec((1, d), const2, **resident),
                  pl.BlockSpec((d, d), const2, **resident)],
        out_specs=[row(d), row(d)],
        out_shape=[jax.ShapeDtypeStruct((bsz, t_len, d), F32), jax.ShapeDtypeStruct((bsz, t_len, d), BF16)],
        compiler_params=_cparams(("parallel", "arbitrary")),
        name="mix_pool",
    )(x3, oa3, proj3, proj3, proj3, pool_w, pool_scale.reshape(1, MIX_B), w_out,
      norm_cross.reshape(1, d), w_cq)


def _pool_sample_kernel(ext_ref, zb_ref, pw_ref, ps_ref, o_ref, *, pos0):
    t_len = zb_ref.shape[0]
    for t in range(t_len):
        e = POOL_BUF + t
        d_groups = []
        for gi, win in enumerate(POOL_WINDOWS):
            sl = slice(gi * POOL_CH, (gi + 1) * POOL_CH)
            tot = ext_ref[e, :, sl]
            for j in range(1, win):
                tot = tot + ext_ref[e - j, :, sl]
            cnt = float(min(win, pos0 + t + 1))
            d_groups.append(tot / cnt - ext_ref[e, :, sl])
        o_ref[t] = _pool_finish(d_groups, pw_ref, ps_ref, zb_ref[t].astype(F32)).astype(o_ref.dtype)


def _pool_sample(ext_tm, zb_tm, pool_w, pool_scale, *, bb, pos0):
    t_len, bsz, _ = zb_tm.shape
    return pl.pallas_call(
        functools.partial(_pool_sample_kernel, pos0=pos0),
        grid=(bsz // bb,),
        in_specs=[pl.BlockSpec((POOL_BUF + t_len, bb, MIX_B), lambda i: (0, i, 0)),
                  pl.BlockSpec((t_len, bb, MIX_B), lambda i: (0, i, 0)),
                  pl.BlockSpec((POOL_GROUPS, POOL_CH, POOL_CH), lambda i: (0, 0, 0)),
                  pl.BlockSpec((1, MIX_B), lambda i: (0, 0))],
        out_specs=pl.BlockSpec((t_len, bb, MIX_B), lambda i: (0, i, 0)),
        out_shape=jax.ShapeDtypeStruct((t_len, bsz, MIX_B), BF16),
        compiler_params=_cparams(("parallel",)),
        name="pool_sample",
    )(ext_tm, zb_tm, pool_w, pool_scale.reshape(1, MIX_B))


def _mix_out_kernel(x_ref, oa_ref, ob_ref, wo_ref, nc_ref, wq_ref, x1_ref, qx_ref):
    acc = _dot(oa_ref[...], wo_ref[0:MIX_A, :]) + _dot(ob_ref[...], wo_ref[MIX_A:D_MODEL, :])
    x1 = x_ref[...] + acc
    x1_ref[...] = x1
    qx_ref[...] = _dot(_rms(x1, nc_ref[...]).astype(BF16), wq_ref[...]).astype(qx_ref.dtype)


def _mix_out(x, oa, ob, w_out, norm_cross, w_cq, *, tm):
    n, d = x.shape
    const = lambda i: (0, 0)
    return pl.pallas_call(
        _mix_out_kernel,
        grid=(n // tm,),
        in_specs=[pl.BlockSpec((tm, d), lambda i: (i, 0)),
                  pl.BlockSpec((tm, MIX_A), lambda i: (i, 0)),
                  pl.BlockSpec((tm, MIX_B), lambda i: (i, 0)),
                  pl.BlockSpec((d, d), const), pl.BlockSpec((1, d), const), pl.BlockSpec((d, d), const)],
        out_specs=[pl.BlockSpec((tm, d), lambda i: (i, 0)), pl.BlockSpec((tm, d), lambda i: (i, 0))],
        out_shape=[jax.ShapeDtypeStruct((n, d), F32), jax.ShapeDtypeStruct((n, d), BF16)],
        compiler_params=_cparams(("parallel",)),
        name="mix_out",
    )(x, oa, ob, w_out, norm_cross.reshape(1, d), w_cq)


def _xattn_kernel(q_ref, k_ref, v_ref, ctx_ref):
    q = q_ref[0]
    scale = X_HEAD_DIM ** -0.5
    for h in range(X_HEADS):
        sl = slice(h * X_HEAD_DIM, (h + 1) * X_HEAD_DIM)
        s = _dot_nt(q[:, sl], k_ref[0, :, sl].astype(BF16)) * scale
        p = jnp.exp(s - jnp.max(s, axis=-1, keepdims=True))
        p = p / jnp.sum(p, axis=-1, keepdims=True)
        ctx_ref[0, :, sl] = _dot(p.astype(BF16), v_ref[0, :, sl].astype(BF16)).astype(ctx_ref.dtype)


def _xattn(qx3, mk3, mv3, *, tq):
    bsz, t_len, d = qx3.shape
    n_mem = mk3.shape[1]
    return pl.pallas_call(
        _xattn_kernel,
        grid=(bsz, t_len // tq),
        in_specs=[pl.BlockSpec((1, tq, d), lambda b, i: (b, i, 0)),
                  pl.BlockSpec((1, n_mem, d), lambda b, i: (b, 0, 0)),
                  pl.BlockSpec((1, n_mem, d), lambda b, i: (b, 0, 0))],
        out_specs=pl.BlockSpec((1, tq, d), lambda b, i: (b, i, 0)),
        out_shape=jax.ShapeDtypeStruct((bsz, t_len, d), BF16),
        compiler_params=_cparams(("parallel", "arbitrary")),
        name="xattn",
    )(qx3, mk3, mv3)


def _xattn_native_kernel(q_ref, k_ref, v_ref, o_ref, *, t_len):
    nj = X_HEAD_DIM // LANES
    grp = nj * X_HEADS
    th = t_len * X_HEADS
    scale = X_HEAD_DIM ** -0.5
    ncol = k_ref.shape[1]
    r = lax.broadcasted_iota(jnp.int32, (th, ncol), 0) & (X_HEADS - 1)
    c = lax.broadcasted_iota(jnp.int32, (th, ncol), 1) & (grp - 1)

    def probs(z):
        s = None
        for j in range(nj):
            zj = jnp.where(c == r + j * X_HEADS, z[j * th:(j + 1) * th, :], 0.0)
            if j:
                zj = pltpu.roll(zj, ncol - j * X_HEADS, axis=1)
            s = zj if s is None else s + zj
        sm = jnp.where(c == r, s * scale, -jnp.inf)
        p = jnp.exp(sm - jnp.max(sm, axis=1, keepdims=True))
        p = p / jnp.sum(p, axis=1, keepdims=True)
        return jnp.concatenate([p if j == 0 else pltpu.roll(p, j * X_HEADS, axis=1)
                                for j in range(nj)], axis=0).astype(BF16)

    bs_ = range(q_ref.shape[0])
    zs = [_dot_nt(q_ref[b], k_ref[b].astype(BF16)) for b in bs_]
    pps = [probs(z) for z in zs]
    for b in bs_:
        o_ref[b] = _dot(pps[b], v_ref[b].astype(BF16)).astype(o_ref.dtype)


def _xattn_native(q_rows, k_rows, v_rows, *, t_len, bb):
    bsz, nq, _ = q_rows.shape
    nk = k_rows.shape[1]
    return pl.pallas_call(
        functools.partial(_xattn_native_kernel, t_len=t_len),
        grid=(bsz // bb,),
        in_specs=[pl.BlockSpec((bb, nq, LANES), lambda i: (i, 0, 0)),
                  pl.BlockSpec((bb, nk, LANES), lambda i: (i, 0, 0)),
                  pl.BlockSpec((bb, nk, LANES), lambda i: (i, 0, 0))],
        out_specs=pl.BlockSpec((bb, nq, LANES), lambda i: (i, 0, 0)),
        out_shape=jax.ShapeDtypeStruct((bsz, nq, LANES), BF16),
        compiler_params=_cparams(("parallel",)),
        name="xattn_native",
    )(q_rows, k_rows, v_rows)


def _kv_rows(cache):
    bsz, n_mem, nh, dh = cache.shape
    nj = dh // LANES
    return cache.reshape(bsz, n_mem, nh, nj, LANES).transpose(0, 1, 3, 2, 4).reshape(
        bsz, n_mem * nj * nh, LANES)


def _attn_out_kernel(x1_ref, ctx_ref, wco_ref, nf_ref, y_ref):
    x2 = x1_ref[...] + _dot(ctx_ref[...], wco_ref[...])
    y_ref[...] = _rms(x2, nf_ref[...])


def _attn_out(x1, ctx, w_co, norm_final, *, tm):
    n, d = x1.shape
    const = lambda i: (0, 0)
    return pl.pallas_call(
        _attn_out_kernel,
        grid=(n // tm,),
        in_specs=[pl.BlockSpec((tm, d), lambda i: (i, 0)), pl.BlockSpec((tm, d), lambda i: (i, 0)),
                  pl.BlockSpec((d, d), const, pipeline_mode=pl.Buffered(1)),
                  pl.BlockSpec((1, d), const, pipeline_mode=pl.Buffered(1))],
        out_specs=pl.BlockSpec((tm, d), lambda i: (i, 0)),
        out_shape=jax.ShapeDtypeStruct((n, d), F32),
        compiler_params=_cparams(("parallel",)),
        name="attn_out",
    )(x1, ctx, w_co, norm_final.reshape(1, d))


def kernel(x_prompt, x_sample, mem_prompt, cache_mem_k, cache_mem_v, state_delta, state_conv,
           state_pool, norm_mix, w_in, conv_w, a_log, dt_bias, gdn_norm, pool_w, pool_scale,
           w_out, norm_mem, norm_cross, w_cq, w_ck, w_cv, w_co, norm_final):
    bp, tp, d = x_prompt.shape
    bs, ts, _ = x_sample.shape
    n_mem = mem_prompt.shape[1]

    w_main, w_gate = _repack_w_in(w_in[0].T, tk=512, tn=1024)
    wo, wcq, wco = (w[0].astype(BF16) for w in (w_out, w_cq, w_co))
    pw = pool_w[0].astype(BF16)
    gate_par = jnp.zeros((SUBLANES, LANES), F32)
    gate_par = gate_par.at[0, :GDN_HEADS].set(a_log[0]).at[1, :GDN_HEADS].set(dt_bias[0])
    cw = conv_w[0]

    mem2d = mem_prompt.reshape(bp * n_mem, d)
    mk = _norm_proj(mem2d, norm_mem[0], w_ck[0], tm=bp * n_mem, tn=1024, out_dtype=F32)
    mv = _norm_proj(mem2d, norm_mem[0], w_cv[0], tm=bp * n_mem, tn=1024, out_dtype=F32)

    proj_p, ab_p = _norm_proj(x_prompt.reshape(bp * tp, d), norm_mix[0], w_main, w_gate,
                              tm=1024, tn=1024, out_dtype=BF16)
    proj_p3 = proj_p.reshape(bp, tp, W_MAIN)
    oa_p, delta_p = _gdn_prompt(proj_p3, ab_p.reshape(bp, tp, LANES), cw, gate_par, gdn_norm[0])
    x1_p, qx_p = _mix_pool(x_prompt, oa_p, proj_p3, pw, pool_scale[0], wo, norm_cross[0], wcq,
                           tt=512, pos0=0)
    ctx_p = _xattn(qx_p, mk.reshape(bp, n_mem, d), mv.reshape(bp, n_mem, d), tq=512)
    y_p = _attn_out(x1_p.reshape(bp * tp, d), ctx_p.reshape(bp * tp, d), wco, norm_final,
                    tm=512).reshape(bp, tp, d)
    conv_p = proj_p3[:, tp - (CONV_W - 1):, :QKV_W].astype(F32)
    pool_p = proj_p3[:, tp - POOL_BUF:, COL_ZA + MIX_A:COL_ZA + MIX_A + MIX_B].astype(F32)

    proj_s, ab_s = _norm_proj(x_sample.reshape(bs * ts, d), norm_mix[0], w_main, w_gate,
                              tm=256, tn=1024, out_dtype=BF16)
    proj_s3 = proj_s.reshape(bs, ts, W_MAIN)
    oa_s, delta_s = _gdn_sample(proj_s3, ab_s.reshape(bs, ts, LANES), state_conv[0], cw, gate_par,
                                gdn_norm[0], state_delta[0], bb=4)
    u_s = proj_s3[:, :, COL_ZA + MIX_A:COL_ZA + MIX_A + MIX_B].astype(F32)
    ext_s = jnp.concatenate([state_pool[0], u_s], axis=1)
    zb_tm = proj_s3[:, :, COL_ZA + MIX_A + MIX_B:].transpose(1, 0, 2)
    ob_s = _pool_sample(ext_s.transpose(1, 0, 2), zb_tm, pw, pool_scale[0], bb=32,
                        pos0=PAST_LEN).transpose(1, 0, 2)
    x1_s, qx_s = _mix_out(x_sample.reshape(bs * ts, d), oa_s.reshape(-1, MIX_A),
                          ob_s.reshape(-1, MIX_B), wo, norm_cross[0], wcq, tm=256)
    nj = X_HEAD_DIM // LANES
    q_rows = qx_s.reshape(bs, ts, X_HEADS, nj, LANES).transpose(0, 3, 1, 2, 4).reshape(
        bs, nj * ts * X_HEADS, LANES)
    ctx_rows = _xattn_native(q_rows, _kv_rows(cache_mem_k[0]), _kv_rows(cache_mem_v[0]),
                             t_len=ts, bb=2)
    ctx_s = ctx_rows.reshape(bs, nj, ts, X_HEADS, LANES).transpose(0, 2, 3, 1, 4).reshape(bs * ts, d)
    y_s = _attn_out(x1_s, ctx_s, wco, norm_final, tm=256).reshape(bs, ts, d)
    conv_s = jnp.concatenate([state_conv[0], proj_s3[:, :, :QKV_W].astype(F32)], axis=1)[:, ts:]
    pool_s = ext_s[:, ts:]

    hd = (X_HEADS, X_HEAD_DIM)
    return (y_p, y_s, mk.reshape(1, bp, n_mem, *hd), mv.reshape(1, bp, n_mem, *hd),
            delta_p[None], conv_p[None], pool_p[None], delta_s[None], conv_s[None], pool_s[None])
```

```python
import functools
import math

import jax
import jax.numpy as jnp
from jax import lax
from jax.experimental import pallas as pl
from jax.experimental.pallas import tpu as pltpu

F32 = jnp.float32
BF16 = jnp.bfloat16

D_MODEL = 2048
MIX_A = D_MODEL // 2
MIX_B = D_MODEL - MIX_A
GDN_HEADS = 8
GDN_DK = MIX_A // GDN_HEADS
GDN_DV = MIX_A // GDN_HEADS
QK_W = GDN_HEADS * GDN_DK
QKV_W = 2 * QK_W + GDN_HEADS * GDN_DV
CONV_W = 4
POOL_WINDOWS = (2, 4, 8, 16)
POOL_GROUPS = len(POOL_WINDOWS)
POOL_CH = MIX_B // POOL_GROUPS
POOL_BUF = max(POOL_WINDOWS) - 1
X_HEADS = 4
X_HEAD_DIM = D_MODEL // X_HEADS
PAST_LEN = 16384
EPS = 1e-6
COL_ZA = QKV_W
COL_A = COL_ZA + GDN_HEADS * GDN_DV
COL_B = COL_A + GDN_HEADS
COL_U = COL_B + GDN_HEADS
COL_ZB = COL_U + MIX_B
IN_COLS = COL_ZB + MIX_B

W_MAIN = IN_COLS - 2 * GDN_HEADS
LANES = 128
SUBLANES = 8
BF16_ROWS = 16
GDN_CHUNK = 128
POOL_HALO = 16
VMEM_LIMIT = 56 * 1024 * 1024


def _cparams(sem):
    return pltpu.CompilerParams(dimension_semantics=sem, vmem_limit_bytes=VMEM_LIMIT)


def _dot(a, b):
    return jnp.dot(a, b, preferred_element_type=F32)


def _dot_nt(a, b):
    return lax.dot_general(a, b, (((1,), (1,)), ((), ())), preferred_element_type=F32)


def _dot_tn(a, b):
    return lax.dot_general(a, b, (((0,), (0,)), ((), ())), preferred_element_type=F32)


def _rms(x, g):
    return x * lax.rsqrt(jnp.mean(x * x, axis=-1, keepdims=True) + EPS) * g


def _silu(x):
    return x * jax.nn.sigmoid(x)


def _softplus(x):
    return jnp.maximum(x, 0.0) + jnp.log1p(jnp.exp(-jnp.abs(x)))


def _repack_kernel(wt_ref, gt_ref, main_ref, gate_ref):
    main_ref[...] = wt_ref[...].T.astype(main_ref.dtype)

    @pl.when(pl.program_id(1) == 0)
    def _():
        head = gt_ref[...].T
        lane = lax.broadcasted_iota(jnp.int32, head.shape, 1)
        gate_ref[...] = jnp.where(lane < COL_U - COL_A, head, 0.0).astype(gate_ref.dtype)


def _repack_w_in(wt, *, tk, tn):
    k_dim = wt.shape[1]
    n_gate = COL_U - COL_A
    src_row = lambda j: pl.multiple_of(j * tn + (j // (COL_A // tn)) * n_gate, n_gate)
    return pl.pallas_call(
        _repack_kernel,
        grid=(k_dim // tk, W_MAIN // tn),
        in_specs=[pl.BlockSpec((pl.Element(tn), pl.Element(tk)), lambda i, j: (src_row(j), i * tk)),
                  pl.BlockSpec((pl.Element(LANES), pl.Element(tk)), lambda i, j: (COL_A, i * tk))],
        out_specs=[pl.BlockSpec((tk, tn), lambda i, j: (i, j)),
                   pl.BlockSpec((tk, LANES), lambda i, j: (i, 0))],
        out_shape=[jax.ShapeDtypeStruct((k_dim, W_MAIN), BF16),
                   jax.ShapeDtypeStruct((k_dim, LANES), BF16)],
        compiler_params=_cparams(("parallel", "arbitrary")),
        name="repack_w_in",
    )(wt, wt)


def _norm_proj_kernel(x_ref, g_ref, w_ref, *rest, with_side):
    if with_side:
        ws_ref, out_ref, side_ref, h_scr = rest
    else:
        out_ref, h_scr = rest

    @pl.when(pl.program_id(1) == 0)
    def _():
        h = _rms(x_ref[...], g_ref[...]).astype(BF16)
        h_scr[...] = h
        if with_side:
            side_ref[...] = _dot(h, ws_ref[...])

    out_ref[...] = _dot(h_scr[...], w_ref[...].astype(BF16)).astype(out_ref.dtype)


def _norm_proj(x, g, w, w_side=None, *, tm, tn, out_dtype):
    n, d = x.shape
    ncol = w.shape[1]
    with_side = w_side is not None
    in_specs = [pl.BlockSpec((tm, d), lambda i, j: (i, 0)),
                pl.BlockSpec((1, d), lambda i, j: (0, 0)),
                pl.BlockSpec((d, tn), lambda i, j: (0, j))]
    out_specs = [pl.BlockSpec((tm, tn), lambda i, j: (i, j))]
    out_shape = [jax.ShapeDtypeStruct((n, ncol), out_dtype)]
    args = [x, g.reshape(1, d), w]
    if with_side:
        in_specs.append(pl.BlockSpec((d, LANES), lambda i, j: (0, 0)))
        out_specs.append(pl.BlockSpec((tm, LANES), lambda i, j: (i, 0)))
        out_shape.append(jax.ShapeDtypeStruct((n, LANES), F32))
        args.append(w_side)
    res = pl.pallas_call(
        functools.partial(_norm_proj_kernel, with_side=with_side),
        grid=(n // tm, ncol // tn),
        in_specs=in_specs, out_specs=out_specs, out_shape=out_shape,
        scratch_shapes=[pltpu.VMEM((tm, d), BF16)],
        compiler_params=_cparams(("parallel", "arbitrary")),
        name="norm_proj_side" if with_side else "norm_proj",
    )(*args)
    return res if with_side else res[0]


INV_BASE_LOG2 = 4
_GDN_WORK = (("k", 1, BF16), ("kb", 1, BF16), ("q", 1, BF16), ("kd", 1, BF16), ("vk", 2, BF16),
             ("dinc", 1, F32), ("low", 1, F32), ("x", 1, F32), ("p", 1, BF16), ("wu", 2, BF16))


def _split3(x):
    hi = x.astype(BF16)
    r1 = x - hi.astype(F32)
    mid = r1.astype(BF16)
    lo = (r1 - mid.astype(F32)).astype(BF16)
    return hi, mid, lo


def _gdn_prompt_kernel(q_ref, k_ref, v_ref, z_ref, ab_ref, cwq_ref, cwk_ref, cwv_ref, gp_ref,
                       gn_ref, o_ref, s_ref,
                       gc_s, gct_s, egc_s, ekd_s, beta_s, u_s, n_s, pw_s, qa_s, sv_s, *work_refs, group):
    nw = len(_GDN_WORK)
    work = [{name: ref for (name, _, _), ref in zip(_GDN_WORK, work_refs[j * nw:(j + 1) * nw])}
            for j in range(len(work_refs) // nw)]
    h = pl.program_id(1)
    t_len = q_ref.shape[1]
    c = GDN_CHUNK
    n_c = t_len // c
    row = lax.broadcasted_iota(jnp.int32, (c, c), 0)
    col = lax.broadcasted_iota(jnp.int32, (c, c), 1)

    @pl.when(h == 0)
    def _():
        gp = gp_ref[...]
        a_neg = -jnp.exp(gp[0:1])
        dt_b = gp[1:2]
        tri = jnp.where(row >= col, 1.0, 0.0).astype(BF16)

        def gate_chunk(ci, carry):
            rows = pl.ds(pl.multiple_of(ci * c, c), c)
            ab = ab_ref[0, rows, :]
            hi, mid, lo = _split3(a_neg * _softplus(ab + dt_b))
            gc = _dot(tri, hi) + _dot(tri, mid) + _dot(tri, lo)
            gc_s[rows, :] = gc
            gct_s[rows, :] = gc.T
            egc_s[rows, :] = jnp.exp(gc)
            ekd_s[rows, :] = jnp.exp(gc[c - 1:c, :] - gc)
            beta_s[rows, :] = jax.nn.sigmoid(ab)
            return carry

        lax.fori_loop(0, n_c, gate_chunk, 0)

    def l2n(x):
        return x * lax.rsqrt(jnp.sum(x * x, axis=-1, keepdims=True) + EPS)

    def operands(w, ci):
        r0 = ci * c
        rows = pl.ds(r0, c)

        def conv_silu(x_ref, cw_ref):
            cur = x_ref[0, rows, :].astype(F32)
            if ci > 0:
                tail = x_ref[0, pl.ds(r0 - BF16_ROWS, BF16_ROWS), :].astype(F32)[BF16_ROWS - SUBLANES:]
            else:
                tail = jnp.zeros((SUBLANES, cur.shape[1]), F32)
            ext = jnp.concatenate([tail, cur], axis=0)
            cw = cw_ref[...]
            y = cur * cw[CONV_W - 1:CONV_W]
            for s in range(1, CONV_W):
                y = y + ext[SUBLANES - s:SUBLANES - s + c] * cw[CONV_W - 1 - s:CONV_W - s]
            return _silu(y)

        def column(scr, idx):
            return jnp.sum(jnp.where(col == idx, scr[rows, :], 0.0), axis=-1, keepdims=True)

        q = l2n(conv_silu(q_ref, cwq_ref)) * (GDN_DK ** -0.5)
        k = l2n(conv_silu(k_ref, cwk_ref))
        v = conv_silu(v_ref, cwv_ref)
        gcc = column(gc_s, h)
        egc = column(egc_s, h)
        ekd = column(ekd_s, h)
        beta = column(beta_s, h + GDN_HEADS)
        grow = gct_s[pl.ds(r0 + h, 1), :]
        tri_i = row >= col
        w["dinc"][...] = jnp.where(tri_i, jnp.exp(jnp.where(tri_i, gcc - grow, 0.0)), 0.0)
        kb = k * beta
        w["k"][...] = k.astype(BF16)
        w["kb"][...] = kb.astype(BF16)
        w["q"][...] = q.astype(BF16)
        w["kd"][...] = (k * ekd).astype(BF16)
        w["vk"][:, :GDN_DV] = (v * beta).astype(BF16)
        w["vk"][:, GDN_DV:] = (kb * egc).astype(BF16)
        qa_s[ci, :, 0:GDN_DK] = (q * egc).astype(BF16)

    def gram(w, ci):
        r = _dot_nt(jnp.concatenate([w["kb"][...], w["q"][...]], axis=0), w["k"][...])
        d_incl = w["dinc"][...]
        low = r[:c] * jnp.where(row > col, d_incl, 0.0)
        qa_s[ci, :, GDN_DK:] = (r[c:] * d_incl).astype(BF16)
        w["low"][...] = low
        ld = jnp.where((row >> INV_BASE_LOG2) == (col >> INV_BASE_LOG2), low, 0.0)
        w["x"][...] = jnp.where(row == col, 1.0, 0.0) - ld
        w["p"][...] = ld.astype(BF16)

    def neumann(w, first, last):
        pb = w["p"][...]
        if first:
            w["p"][...] = _dot(pb, pb).astype(BF16)
        elif last:
            x = w["x"][...]
            w["x"][...] = x + _dot(x.astype(BF16), pb)
        else:
            x = w["x"][...]
            r = _dot(jnp.concatenate([x.astype(BF16), pb], axis=0), pb)
            w["x"][...] = x + r[:c]
            w["p"][...] = r[c:].astype(BF16)

    def merge_a(w, lg):
        pair = (row >> (lg + 1)) == (col >> (lg + 1))
        m = jnp.where(pair, (row >> lg) - (col >> lg), 0) > 0
        lm = jnp.where(m, w["low"][...], 0.0).astype(BF16)
        w["p"][...] = _dot(lm, w["x"][...].astype(BF16)).astype(BF16)

    def merge_b(w):
        x = w["x"][...]
        w["x"][...] = x - _dot(x.astype(BF16), w["p"][...])

    def solve(w, ci):
        uw = _dot(w["x"][...].astype(BF16), w["vk"][...])
        u_s[ci] = uw[:, :GDN_DV]
        uwb = uw.astype(BF16)
        w["wu"][...] = uwb
        pw_s[ci, c:2 * c, :] = uwb[:, GDN_DV:]

    def outer(w, ci):
        np_ = _dot_tn(w["kd"][...], w["wu"][...])
        n_s[ci] = np_[:, :GDN_DV]
        pw_s[ci, 0:c, :] = np_[:, GDN_DV:].astype(BF16)

    n_sq = INV_BASE_LOG2 - 1
    lgc = c.bit_length() - 1

    def prepare_stages(cis):
        over = lambda fn, *a: (lambda: [fn(work[ci], *[ci if x is cis else x for x in a])
                                        for ci in cis])
        stages = [over(operands, cis), over(gram, cis)]
        stages += [over(neumann, step == 0, step == n_sq) for step in range(n_sq + 1)]
        for lg in range(INV_BASE_LOG2, lgc):
            stages += [over(merge_a, lg), over(merge_b)]
        return stages + [over(solve, cis), over(outer, cis)]

    gn = gn_ref[...]
    lane1 = lax.broadcasted_iota(jnp.int32, (1, LANES), 1)

    def advance(ci, s_mat):
        sb = s_mat.astype(BF16)
        r = _dot(pw_s[ci], sb)
        g_last = jnp.sum(jnp.where(lane1 == h, gc_s[pl.ds(ci * c + c - 1, 1), :], 0.0),
                         axis=-1, keepdims=True)
        sv_s[ci, 0:c, :] = sb
        sv_s[ci, c:2 * c, :] = (u_s[ci] - r[c:2 * c]).astype(BF16)
        return s_mat * jnp.exp(g_last) - r[0:c] + n_s[ci]

    def emit(ci):
        rows = pl.ds(ci * c, c)
        o = _dot(qa_s[ci], sv_s[ci])
        zz = z_ref[0, rows, :].astype(F32)
        o_ref[0, rows, :] = (_rms(o, gn) * _silu(zz)).astype(o_ref.dtype)

    s_mat = jnp.zeros((GDN_DK, GDN_DV), F32)
    chain_todo, emit_todo = [], []
    for g in range(n_c // group):
        cis = list(range(g * group, (g + 1) * group))
        for stage in prepare_stages(cis):
            stage()
            if chain_todo:
                emit_todo.append(chain_todo[0])
                s_mat = advance(chain_todo.pop(0), s_mat)
        chain_todo += cis
    while chain_todo:
        if emit_todo:
            emit(emit_todo.pop(0))
        emit_todo.append(chain_todo[0])
        s_mat = advance(chain_todo.pop(0), s_mat)
    for ci in emit_todo:
        emit(ci)
    s_ref[0, 0] = s_mat


def _gdn_prompt(proj3, ab3, conv_w, gate_par, gdn_norm):
    bsz, t_len, _ = proj3.shape
    nh = GDN_HEADS
    blk = lambda off: pl.BlockSpec((1, t_len, LANES), lambda b, h: (b, 0, off + h))
    cw = lambda off: pl.BlockSpec((CONV_W, LANES), lambda b, h: (0, off + h))
    c = GDN_CHUNK
    n_c = t_len // c
    gate_scr = pltpu.VMEM((t_len, LANES), F32)
    group = 8
    work = [pltpu.VMEM((c, wide * LANES), dt) for _ in range(n_c) for _, wide, dt in _GDN_WORK]
    return pl.pallas_call(
        functools.partial(_gdn_prompt_kernel, group=group),
        grid=(bsz, nh),
        in_specs=[blk(0), blk(nh), blk(2 * nh), blk(3 * nh),
                  pl.BlockSpec((1, t_len, LANES), lambda b, h: (b, 0, 0)),
                  cw(0), cw(nh), cw(2 * nh),
                  pl.BlockSpec((SUBLANES, LANES), lambda b, h: (0, 0)),
                  pl.BlockSpec((1, LANES), lambda b, h: (0, 0))],
        out_specs=[pl.BlockSpec((1, t_len, LANES), lambda b, h: (b, 0, h)),
                   pl.BlockSpec((1, 1, GDN_DK, GDN_DV), lambda b, h: (b, h, 0, 0))],
        out_shape=[jax.ShapeDtypeStruct((bsz, t_len, MIX_A), BF16),
                   jax.ShapeDtypeStruct((bsz, nh, GDN_DK, GDN_DV), F32)],
        scratch_shapes=[gate_scr, gate_scr, gate_scr, gate_scr, gate_scr,
                        pltpu.VMEM((n_c, c, GDN_DV), F32), pltpu.VMEM((n_c, GDN_DK, GDN_DV), F32),
                        pltpu.VMEM((n_c, 2 * c, GDN_DK), BF16), pltpu.VMEM((n_c, c, GDN_DK + c), BF16),
                        pltpu.VMEM((n_c, GDN_DK + c, GDN_DV), BF16)] + work,
        compiler_params=_cparams(("parallel", "arbitrary")),
        name="gdn_prompt",
    )(proj3, proj3, proj3, proj3, ab3, conv_w, conv_w, conv_w, gate_par, gdn_norm.reshape(1, LANES))


def _gdn_sample_kernel(x_ref, ab_ref, hist_ref, cw_ref, gp_ref, gn_ref, s0_ref, o_ref, s_ref,
                       *, t_len):
    nh = GDN_HEADS
    g8 = SUBLANES
    n = nh * g8
    assert t_len + CONV_W - 1 <= g8
    cw = cw_ref[...]
    gp = gp_ref[...]
    gn = gn_ref[...]
    row8 = lax.broadcasted_iota(jnp.int32, (g8, QKV_W), 0)
    tpos = lax.broadcasted_iota(jnp.int32, (n, 1), 0) & (g8 - 1)
    rhead = lax.broadcasted_iota(jnp.int32, (n, GDN_DV), 0) >> 3
    ri = lax.broadcasted_iota(jnp.int32, (n, n), 0)
    ci = lax.broadcasted_iota(jnp.int32, (n, n), 1)
    tri = jnp.where((ri >> 3) == (ci >> 3), ri - ci, -1)
    eye = jnp.where(ri == ci, 1.0, 0.0)
    sel0 = jnp.where(lax.broadcasted_iota(jnp.int32, (n, LANES), 1) == 0, 1.0, 0.0).astype(BF16)
    valid = tpos < t_len
    a_neg = jnp.concatenate([jnp.broadcast_to(-jnp.exp(gp[0:1, h:h + 1]), (g8, 1)) for h in range(nh)], 0)
    dt_b = jnp.concatenate([jnp.broadcast_to(gp[1:2, h:h + 1], (g8, 1)) for h in range(nh)], 0)

    def l2n(x):
        return x * lax.rsqrt(jnp.sum(x * x, axis=-1, keepdims=True) + EPS)

    def operands(b):
        xb = x_ref[b]
        x = xb[:, :QKV_W].astype(F32)
        hst = hist_ref[b]
        y = x * cw[CONV_W - 1:CONV_W]
        for s in range(1, CONV_W):
            y = y + pltpu.roll(x, s, axis=0) * cw[CONV_W - 1 - s:CONV_W - s]
        for j in range(CONV_W - 1):
            hj = hst if j == 0 else pltpu.roll(hst, g8 - j, axis=0)
            y = y + jnp.where(row8 + j < CONV_W - 1, hj, 0.0) * cw[j:j + 1]
        qkv = jnp.where(row8 < t_len, _silu(y), 0.0)

        def stack(off):
            return jnp.concatenate([qkv[:, off + h * GDN_DK:off + (h + 1) * GDN_DK] for h in range(nh)], 0)

        q = l2n(stack(0)) * (GDN_DK ** -0.5)
        k = l2n(stack(QK_W))
        v = stack(2 * QK_W)
        ab = ab_ref[b]
        alpha = jnp.concatenate([ab[:, h:h + 1] for h in range(nh)], 0)
        braw = jnp.concatenate([ab[:, nh + h:nh + h + 1] for h in range(nh)], 0)
        g = jnp.where(valid, a_neg * _softplus(alpha + dt_b), 0.0)
        beta = jnp.where(valid, jax.nn.sigmoid(braw), 0.0)
        gc = g
        s = 1
        while s < g8:
            gc = gc + jnp.where(tpos >= s, pltpu.roll(gc, s, axis=0), 0.0)
            s *= 2
        g_last = jnp.concatenate([jnp.broadcast_to(gc[h * g8 + g8 - 1:(h + 1) * g8, :], (g8, 1))
                                  for h in range(nh)], 0)
        egc = jnp.exp(gc)
        kb = k * beta
        return dict(gc=gc, g_last=g_last, kbf=k.astype(BF16), kbb=kb.astype(BF16), qb=q.astype(BF16),
                    vb=(v * beta).astype(BF16), kg=(kb * egc).astype(BF16), qg=q * egc,
                    k_dec=(k * jnp.exp(g_last - gc)).astype(BF16))

    def gram(st):
        hi, mid, lo = _split3(jnp.broadcast_to(st["gc"], (n, LANES)))
        grow = _dot_nt(sel0, hi) + _dot_nt(sel0, mid) + _dot_nt(sel0, lo)
        d_incl = jnp.where(tri >= 0, jnp.exp(jnp.where(tri >= 0, st["gc"] - grow, 0.0)), 0.0)
        st["low"] = _dot_nt(st["kbb"], st["kbf"]) * jnp.where(tri > 0, d_incl, 0.0)
        st["a_in"] = (_dot_nt(st["qb"], st["kbf"]) * d_incl).astype(BF16)
        st["t_inv"] = eye - st["low"]
        st["p"] = st["low"].astype(BF16)

    def neumann(st):
        p = _dot(st["p"], st["p"])
        st["p"] = p.astype(BF16)
        st["t_inv"] = st["t_inv"] + _dot(st["t_inv"].astype(BF16), st["p"])

    def solve(st):
        tb = st["t_inv"].astype(BF16)
        st["u"] = _dot(tb, st["vb"])
        st["w"] = _dot(tb, st["kg"])

    def apply_state(st, b):
        ws, qs = [], []
        for h in range(nh):
            hs = slice(h * g8, (h + 1) * g8)
            lhs = jnp.concatenate([st["w"][hs], st["qg"][hs]], 0).astype(BF16)
            r = _dot(lhs, s0_ref[b, h].astype(BF16))
            ws.append(r[:g8])
            qs.append(r[g8:])
        st["v_new"] = st["u"] - jnp.concatenate(ws, 0)
        st["oq"] = jnp.concatenate(qs, 0)

    def finish(st, b):
        v_new = st["v_new"]
        o = st["oq"] + _dot(st["a_in"], v_new.astype(BF16))
        for h in range(nh):
            vm = jnp.where(rhead == h, v_new, 0.0).astype(BF16)
            dec = jnp.exp(st["g_last"][h * g8:h * g8 + 1, :])
            s_ref[b, h] = s0_ref[b, h] * dec + _dot_tn(st["k_dec"], vm)
        z = jnp.concatenate([x_ref[b][:, 3 * QK_W + h * GDN_DV:3 * QK_W + (h + 1) * GDN_DV].astype(F32)
                             for h in range(nh)], 0)
        o_ref[b] = (_rms(o, gn) * _silu(z)).astype(o_ref.dtype)

    bs_ = range(x_ref.shape[0])
    sts = [operands(b) for b in bs_]
    for st in sts:
        gram(st)
    covered = 2
    while covered < t_len:
        for st in sts:
            neumann(st)
        covered *= 2
    for st in sts:
        solve(st)
    for b in bs_:
        apply_state(sts[b], b)
    for b in bs_:
        finish(sts[b], b)


def _gdn_sample(proj3, ab3, hist, conv_w, gate_par, gdn_norm, s0, *, bb):
    bsz, t_len, _ = proj3.shape
    nh, g8 = GDN_HEADS, SUBLANES
    pad_t = lambda a, rows: jnp.pad(a, ((0, 0), (0, g8 - rows), (0, 0)))
    x8 = pad_t(proj3[:, :, :4 * QK_W], t_len)
    ab8 = pad_t(ab3, t_len)
    hist8 = pad_t(hist, CONV_W - 1)
    o, s_new = pl.pallas_call(
        functools.partial(_gdn_sample_kernel, t_len=t_len),
        grid=(bsz // bb,),
        in_specs=[pl.BlockSpec((bb, g8, 4 * QK_W), lambda i: (i, 0, 0)),
                  pl.BlockSpec((bb, g8, LANES), lambda i: (i, 0, 0)),
                  pl.BlockSpec((bb, g8, QKV_W), lambda i: (i, 0, 0)),
                  pl.BlockSpec((CONV_W, QKV_W), lambda i: (0, 0)),
                  pl.BlockSpec((SUBLANES, LANES), lambda i: (0, 0)),
                  pl.BlockSpec((1, LANES), lambda i: (0, 0)),
                  pl.BlockSpec((bb, nh, GDN_DK, GDN_DV), lambda i: (i, 0, 0, 0))],
        out_specs=[pl.BlockSpec((bb, nh * g8, GDN_DV), lambda i: (i, 0, 0)),
                   pl.BlockSpec((bb, nh, GDN_DK, GDN_DV), lambda i: (i, 0, 0, 0))],
        out_shape=[jax.ShapeDtypeStruct((bsz, nh * g8, GDN_DV), BF16),
                   jax.ShapeDtypeStruct((bsz, nh, GDN_DK, GDN_DV), F32)],
        compiler_params=_cparams(("parallel",)),
        name="gdn_sample",
    )(x8, ab8, hist8, conv_w, gate_par, gdn_norm.reshape(1, LANES), s0)
    o = o.reshape(bsz, nh, g8, GDN_DV)[:, :, :t_len].transpose(0, 2, 1, 3)
    return o.reshape(bsz, t_len, MIX_A), s_new


def _pool_finish(d_groups, pw_ref, ps_ref, zb):
    outs = [_dot(d.astype(BF16), pw_ref[gi]) for gi, d in enumerate(d_groups)]
    return jnp.concatenate(outs, axis=-1) * ps_ref[...] * _silu(zb)


def _pool_tile(u, halo, zb, pos, pw_ref, ps_ref):
    ext = jnp.concatenate([halo, u], axis=0)
    d_groups = []
    for gi, win in enumerate(POOL_WINDOWS):
        sl = slice(gi * POOL_CH, (gi + 1) * POOL_CH)
        lvl = ext[:, sl]
        s = 1
        while s < win:
            lvl = lvl + pltpu.roll(lvl, s, axis=0)
            s *= 2
        cnt = jnp.minimum(win, pos + 1).astype(F32)
        d_groups.append(lvl[POOL_HALO:, :] / cnt - u[:, sl])
    return _pool_finish(d_groups, pw_ref, ps_ref, zb)


def _mix_pool_kernel(x_ref, oa_ref, u_ref, halo_ref, zb_ref, pw_ref, ps_ref, wo_ref, nc_ref, wq_ref,
                     x1_ref, qx_ref, *, pos0):
    i = pl.program_id(1)
    tt = u_ref.shape[1]
    halo = jnp.where(i > 0, halo_ref[0].astype(F32), 0.0)
    pos = pos0 + i * tt + lax.broadcasted_iota(jnp.int32, (tt, 1), 0)
    ob = _pool_tile(u_ref[0].astype(F32), halo, zb_ref[0].astype(F32), pos, pw_ref, ps_ref)
    acc = _dot(oa_ref[0], wo_ref[0:MIX_A, :]) + _dot(ob.astype(BF16), wo_ref[MIX_A:D_MODEL, :])
    x1 = x_ref[0] + acc
    x1_ref[0] = x1
    qx_ref[0] = _dot(_rms(x1, nc_ref[...]).astype(BF16), wq_ref[...]).astype(qx_ref.dtype)


def _mix_pool(x3, oa3, proj3, pool_w, pool_scale, w_out, norm_cross, w_cq, *, tt, pos0):
    bsz, t_len, d = x3.shape
    ub, zb = COL_ZA // MIX_B + 1, COL_ZA // MIX_B + 2
    hb = tt // POOL_HALO
    const2 = lambda b, i: (0, 0)
    resident = dict(pipeline_mode=pl.Buffered(1))
    row = lambda w: pl.BlockSpec((1, tt, w), lambda b, i: (b, i, 0))
    return pl.pallas_call(
        functools.partial(_mix_pool_kernel, pos0=pos0),
        grid=(bsz, t_len // tt),
        in_specs=[row(d), row(MIX_A),
                  pl.BlockSpec((1, tt, MIX_B), lambda b, i: (b, i, ub)),
                  pl.BlockSpec((1, POOL_HALO, MIX_B), lambda b, i: (b, jnp.maximum(i * hb - 1, 0), ub)),
                  pl.BlockSpec((1, tt, MIX_B), lambda b, i: (b, i, zb)),
                  pl.BlockSpec((POOL_GROUPS, POOL_CH, POOL_CH), lambda b, i: (0, 0, 0), **resident),
                  pl.BlockSpec((1, MIX_B), const2, **resident),
                  pl.BlockSpec((d, d), const2, **resident),
                  pl.BlockSpec((1, d), const2, **resident),
                  pl.BlockSpec((d, d), const2, **resident)],
        out_specs=[row(d), row(d)],
        out_shape=[jax.ShapeDtypeStruct((bsz, t_len, d), F32), jax.ShapeDtypeStruct((bsz, t_len, d), BF16)],
        compiler_params=_cparams(("parallel", "arbitrary")),
        name="mix_pool",
    )(x3, oa3, proj3, proj3, proj3, pool_w, pool_scale.reshape(1, MIX_B), w_out,
      norm_cross.reshape(1, d), w_cq)


def _pool_sample_kernel(ext_ref, zb_ref, pw_ref, ps_ref, o_ref, *, pos0):
    t_len = zb_ref.shape[0]
    for t in range(t_len):
        e = POOL_BUF + t
        d_groups = []
        for gi, win in enumerate(POOL_WINDOWS):
            sl = slice(gi * POOL_CH, (gi + 1) * POOL_CH)
            tot = ext_ref[e, :, sl]
            for j in range(1, win):
                tot = tot + ext_ref[e - j, :, sl]
            cnt = float(min(win, pos0 + t + 1))
            d_groups.append(tot / cnt - ext_ref[e, :, sl])
        o_ref[t] = _pool_finish(d_groups, pw_ref, ps_ref, zb_ref[t].astype(F32)).astype(o_ref.dtype)


def _pool_sample(ext_tm, zb_tm, pool_w, pool_scale, *, bb, pos0):
    t_len, bsz, _ = zb_tm.shape
    return pl.pallas_call(
        functools.partial(_pool_sample_kernel, pos0=pos0),
        grid=(bsz // bb,),
        in_specs=[pl.BlockSpec((POOL_BUF + t_len, bb, MIX_B), lambda i: (0, i, 0)),
                  pl.BlockSpec((t_len, bb, MIX_B), lambda i: (0, i, 0)),
                  pl.BlockSpec((POOL_GROUPS, POOL_CH, POOL_CH), lambda i: (0, 0, 0)),
                  pl.BlockSpec((1, MIX_B), lambda i: (0, 0))],
        out_specs=pl.BlockSpec((t_len, bb, MIX_B), lambda i: (0, i, 0)),
        out_shape=jax.ShapeDtypeStruct((t_len, bsz, MIX_B), BF16),
        compiler_params=_cparams(("parallel",)),
        name="pool_sample",
    )(ext_tm, zb_tm, pool_w, pool_scale.reshape(1, MIX_B))


def _mix_out_kernel(x_ref, oa_ref, ob_ref, wo_ref, nc_ref, wq_ref, x1_ref, qx_ref):
    acc = _dot(oa_ref[...], wo_ref[0:MIX_A, :]) + _dot(ob_ref[...], wo_ref[MIX_A:D_MODEL, :])
    x1 = x_ref[...] + acc
    x1_ref[...] = x1
    qx_ref[...] = _dot(_rms(x1, nc_ref[...]).astype(BF16), wq_ref[...]).astype(qx_ref.dtype)


def _mix_out(x, oa, ob, w_out, norm_cross, w_cq, *, tm):
    n, d = x.shape
    const = lambda i: (0, 0)
    return pl.pallas_call(
        _mix_out_kernel,
        grid=(n // tm,),
        in_specs=[pl.BlockSpec((tm, d), lambda i: (i, 0)),
                  pl.BlockSpec((tm, MIX_A), lambda i: (i, 0)),
                  pl.BlockSpec((tm, MIX_B), lambda i: (i, 0)),
                  pl.BlockSpec((d, d), const), pl.BlockSpec((1, d), const), pl.BlockSpec((d, d), const)],
        out_specs=[pl.BlockSpec((tm, d), lambda i: (i, 0)), pl.BlockSpec((tm, d), lambda i: (i, 0))],
        out_shape=[jax.ShapeDtypeStruct((n, d), F32), jax.ShapeDtypeStruct((n, d), BF16)],
        compiler_params=_cparams(("parallel",)),
        name="mix_out",
    )(x, oa, ob, w_out, norm_cross.reshape(1, d), w_cq)


def _xattn_out_kernel(q_ref, k_ref, v_ref, x1_ref, wco_ref, nf_ref, y_ref):
    q = q_ref[0]
    scale = X_HEAD_DIM ** -0.5
    sls = [slice(h * X_HEAD_DIM, (h + 1) * X_HEAD_DIM) for h in range(X_HEADS)]
    ss = [_dot_nt(q[:, sl], k_ref[0, :, sl].astype(BF16)) * scale for sl in sls]
    ps = []
    for s in ss:
        p = jnp.exp(s - jnp.max(s, axis=-1, keepdims=True))
        ps.append((p / jnp.sum(p, axis=-1, keepdims=True)).astype(BF16))
    ctx = jnp.concatenate([_dot(p, v_ref[0, :, sl].astype(BF16)).astype(BF16)
                           for p, sl in zip(ps, sls)], axis=-1)
    y_ref[0] = _rms(x1_ref[0] + _dot(ctx, wco_ref[...]), nf_ref[...])


def _xattn_out(qx3, mk3, mv3, x13, w_co, norm_final, *, tq):
    bsz, t_len, d = qx3.shape
    n_mem = mk3.shape[1]
    row = pl.BlockSpec((1, tq, d), lambda b, i: (b, i, 0))
    mem = pl.BlockSpec((1, n_mem, d), lambda b, i: (b, 0, 0))
    return pl.pallas_call(
        _xattn_out_kernel,
        grid=(bsz, t_len // tq),
        in_specs=[row, mem, mem, row,
                  pl.BlockSpec((d, d), lambda b, i: (0, 0), pipeline_mode=pl.Buffered(1)),
                  pl.BlockSpec((1, d), lambda b, i: (0, 0), pipeline_mode=pl.Buffered(1))],
        out_specs=row,
        out_shape=jax.ShapeDtypeStruct((bsz, t_len, d), F32),
        compiler_params=_cparams(("parallel", "arbitrary")),
        name="xattn_out",
    )(qx3, mk3, mv3, x13, w_co, norm_final.reshape(1, d))


def _xattn_native_kernel(q_ref, k_ref, v_ref, o_ref, *, t_len):
    nj = X_HEAD_DIM // LANES
    grp = nj * X_HEADS
    th = t_len * X_HEADS
    scale = X_HEAD_DIM ** -0.5
    ncol = k_ref.shape[1]
    r = lax.broadcasted_iota(jnp.int32, (th, ncol), 0) & (X_HEADS - 1)
    c = lax.broadcasted_iota(jnp.int32, (th, ncol), 1) & (grp - 1)

    def probs(z):
        s = None
        for j in range(nj):
            zj = jnp.where(c == r + j * X_HEADS, z[j * th:(j + 1) * th, :], 0.0)
            if j:
                zj = pltpu.roll(zj, ncol - j * X_HEADS, axis=1)
            s = zj if s is None else s + zj
        sm = jnp.where(c == r, s * scale, -jnp.inf)
        p = jnp.exp(sm - jnp.max(sm, axis=1, keepdims=True))
        p = p / jnp.sum(p, axis=1, keepdims=True)
        return jnp.concatenate([p if j == 0 else pltpu.roll(p, j * X_HEADS, axis=1)
                                for j in range(nj)], axis=0).astype(BF16)

    bs_ = range(q_ref.shape[0])
    zs = [_dot_nt(q_ref[b], k_ref[b].astype(BF16)) for b in bs_]
    pps = [probs(z) for z in zs]
    for b in bs_:
        o_ref[b] = _dot(pps[b], v_ref[b].astype(BF16)).astype(o_ref.dtype)


def _xattn_native(q_rows, k_rows, v_rows, *, t_len, bb):
    bsz, nq, _ = q_rows.shape
    nk = k_rows.shape[1]
    return pl.pallas_call(
        functools.partial(_xattn_native_kernel, t_len=t_len),
        grid=(bsz // bb,),
        in_specs=[pl.BlockSpec((bb, nq, LANES), lambda i: (i, 0, 0)),
                  pl.BlockSpec((bb, nk, LANES), lambda i: (i, 0, 0)),
                  pl.BlockSpec((bb, nk, LANES), lambda i: (i, 0, 0))],
        out_specs=pl.BlockSpec((bb, nq, LANES), lambda i: (i, 0, 0)),
        out_shape=jax.ShapeDtypeStruct((bsz, nq, LANES), BF16),
        compiler_params=_cparams(("parallel",)),
        name="xattn_native",
    )(q_rows, k_rows, v_rows)


def _kv_rows(cache):
    bsz, n_mem, nh, dh = cache.shape
    nj = dh // LANES
    return cache.reshape(bsz, n_mem, nh, nj, LANES).transpose(0, 1, 3, 2, 4).reshape(
        bsz, n_mem * nj * nh, LANES)


def _attn_out_kernel(x1_ref, ctx_ref, wco_ref, nf_ref, y_ref):
    x2 = x1_ref[...] + _dot(ctx_ref[...], wco_ref[...])
    y_ref[...] = _rms(x2, nf_ref[...])


def _attn_out(x1, ctx, w_co, norm_final, *, tm):
    n, d = x1.shape
    const = lambda i: (0, 0)
    return pl.pallas_call(
        _attn_out_kernel,
        grid=(n // tm,),
        in_specs=[pl.BlockSpec((tm, d), lambda i: (i, 0)), pl.BlockSpec((tm, d), lambda i: (i, 0)),
                  pl.BlockSpec((d, d), const, pipeline_mode=pl.Buffered(1)),
                  pl.BlockSpec((1, d), const, pipeline_mode=pl.Buffered(1))],
        out_specs=pl.BlockSpec((tm, d), lambda i: (i, 0)),
        out_shape=jax.ShapeDtypeStruct((n, d), F32),
        compiler_params=_cparams(("parallel",)),
        name="attn_out",
    )(x1, ctx, w_co, norm_final.reshape(1, d))


def kernel(x_prompt, x_sample, mem_prompt, cache_mem_k, cache_mem_v, state_delta, state_conv,
           state_pool, norm_mix, w_in, conv_w, a_log, dt_bias, gdn_norm, pool_w, pool_scale,
           w_out, norm_mem, norm_cross, w_cq, w_ck, w_cv, w_co, norm_final):
    bp, tp, d = x_prompt.shape
    bs, ts, _ = x_sample.shape
    n_mem = mem_prompt.shape[1]

    w_main, w_gate = _repack_w_in(w_in[0].T, tk=512, tn=1024)
    wo, wcq, wco = (w[0].astype(BF16) for w in (w_out, w_cq, w_co))
    pw = pool_w[0].astype(BF16)
    gate_par = jnp.zeros((SUBLANES, LANES), F32)
    gate_par = gate_par.at[0, :GDN_HEADS].set(a_log[0]).at[1, :GDN_HEADS].set(dt_bias[0])
    cw = conv_w[0]

    mem2d = mem_prompt.reshape(bp * n_mem, d)
    mk = _norm_proj(mem2d, norm_mem[0], w_ck[0], tm=bp * n_mem, tn=1024, out_dtype=F32)
    mv = _norm_proj(mem2d, norm_mem[0], w_cv[0], tm=bp * n_mem, tn=1024, out_dtype=F32)

    proj_p, ab_p = _norm_proj(x_prompt.reshape(bp * tp, d), norm_mix[0], w_main, w_gate,
                              tm=1024, tn=1024, out_dtype=BF16)
    proj_p3 = proj_p.reshape(bp, tp, W_MAIN)
    oa_p, delta_p = _gdn_prompt(proj_p3, ab_p.reshape(bp, tp, LANES), cw, gate_par, gdn_norm[0])
    x1_p, qx_p = _mix_pool(x_prompt, oa_p, proj_p3, pw, pool_scale[0], wo, norm_cross[0], wcq,
                           tt=512, pos0=0)
    y_p = _xattn_out(qx_p, mk.reshape(bp, n_mem, d), mv.reshape(bp, n_mem, d), x1_p, wco,
                     norm_final, tq=512)
    conv_p = proj_p3[:, tp - (CONV_W - 1):, :QKV_W].astype(F32)
    pool_p = proj_p3[:, tp - POOL_BUF:, COL_ZA + MIX_A:COL_ZA + MIX_A + MIX_B].astype(F32)

    proj_s, ab_s = _norm_proj(x_sample.reshape(bs * ts, d), norm_mix[0], w_main, w_gate,
                              tm=bs * ts, tn=1024, out_dtype=BF16)
    proj_s3 = proj_s.reshape(bs, ts, W_MAIN)
    oa_s, delta_s = _gdn_sample(proj_s3, ab_s.reshape(bs, ts, LANES), state_conv[0], cw, gate_par,
                                gdn_norm[0], state_delta[0], bb=8)
    u_s = proj_s3[:, :, COL_ZA + MIX_A:COL_ZA + MIX_A + MIX_B].astype(F32)
    ext_s = jnp.concatenate([state_pool[0], u_s], axis=1)
    zb_tm = proj_s3[:, :, COL_ZA + MIX_A + MIX_B:].transpose(1, 0, 2)
    ob_s = _pool_sample(ext_s.transpose(1, 0, 2), zb_tm, pw, pool_scale[0], bb=32,
                        pos0=PAST_LEN).transpose(1, 0, 2)
    x1_s, qx_s = _mix_out(x_sample.reshape(bs * ts, d), oa_s.reshape(-1, MIX_A),
                          ob_s.reshape(-1, MIX_B), wo, norm_cross[0], wcq, tm=256)
    nj = X_HEAD_DIM // LANES
    q_rows = qx_s.reshape(bs, ts, X_HEADS, nj, LANES).transpose(0, 3, 1, 2, 4).reshape(
        bs, nj * ts * X_HEADS, LANES)
    ctx_rows = _xattn_native(q_rows, _kv_rows(cache_mem_k[0]), _kv_rows(cache_mem_v[0]),
                             t_len=ts, bb=4)
    ctx_s = ctx_rows.reshape(bs, nj, ts, X_HEADS, LANES).transpose(0, 2, 3, 1, 4).reshape(bs * ts, d)
    y_s = _attn_out(x1_s, ctx_s, wco, norm_final, tm=256).reshape(bs, ts, d)
    conv_s = jnp.concatenate([state_conv[0], proj_s3[:, :, :QKV_W].astype(F32)], axis=1)[:, ts:]
    pool_s = ext_s[:, ts:]

    hd = (X_HEADS, X_HEAD_DIM)
    return (y_p, y_s, mk.reshape(1, bp, n_mem, *hd), mv.reshape(1, bp, n_mem, *hd),
            delta_p[None], conv_p[None], pool_p[None], delta_s[None], conv_s[None], pool_s[None])
```

```python
import functools
import math

import jax
import jax.numpy as jnp
from jax import lax
from jax.experimental import pallas as pl
from jax.experimental.pallas import tpu as pltpu

F32 = jnp.float32
BF16 = jnp.bfloat16

D_MODEL = 2048
MIX_A = D_MODEL // 2
MIX_B = D_MODEL - MIX_A
GDN_HEADS = 8
GDN_DK = MIX_A // GDN_HEADS
GDN_DV = MIX_A // GDN_HEADS
QK_W = GDN_HEADS * GDN_DK
QKV_W = 2 * QK_W + GDN_HEADS * GDN_DV
CONV_W = 4
POOL_WINDOWS = (2, 4, 8, 16)
POOL_GROUPS = len(POOL_WINDOWS)
POOL_CH = MIX_B // POOL_GROUPS
POOL_BUF = max(POOL_WINDOWS) - 1
X_HEADS = 4
X_HEAD_DIM = D_MODEL // X_HEADS
PAST_LEN = 16384
EPS = 1e-6
COL_ZA = QKV_W
COL_A = COL_ZA + GDN_HEADS * GDN_DV
COL_B = COL_A + GDN_HEADS
COL_U = COL_B + GDN_HEADS
COL_ZB = COL_U + MIX_B
IN_COLS = COL_ZB + MIX_B

W_MAIN = IN_COLS - 2 * GDN_HEADS
LANES = 128
SUBLANES = 8
BF16_ROWS = 16
GDN_CHUNK = 128
POOL_HALO = 16
VMEM_LIMIT = 56 * 1024 * 1024


def _cparams(sem):
    return pltpu.CompilerParams(dimension_semantics=sem, vmem_limit_bytes=VMEM_LIMIT)


def _dot(a, b):
    return jnp.dot(a, b, preferred_element_type=F32)


def _dot_nt(a, b):
    return lax.dot_general(a, b, (((1,), (1,)), ((), ())), preferred_element_type=F32)


def _dot_tn(a, b):
    return lax.dot_general(a, b, (((0,), (0,)), ((), ())), preferred_element_type=F32)


def _rms(x, g):
    return x * lax.rsqrt(jnp.mean(x * x, axis=-1, keepdims=True) + EPS) * g


def _silu(x):
    return x * jax.nn.sigmoid(x)


def _softplus(x):
    return jnp.maximum(x, 0.0) + jnp.log1p(jnp.exp(-jnp.abs(x)))


def _repack_kernel(wt_ref, gt_ref, main_ref, gate_ref):
    main_ref[...] = wt_ref[...].T.astype(main_ref.dtype)

    @pl.when(pl.program_id(1) == 0)
    def _():
        head = gt_ref[...].T
        lane = lax.broadcasted_iota(jnp.int32, head.shape, 1)
        gate_ref[...] = jnp.where(lane < COL_U - COL_A, head, 0.0).astype(gate_ref.dtype)


def _repack_w_in(wt, *, tk, tn):
    k_dim = wt.shape[1]
    n_gate = COL_U - COL_A
    src_row = lambda j: pl.multiple_of(j * tn + (j // (COL_A // tn)) * n_gate, n_gate)
    return pl.pallas_call(
        _repack_kernel,
        grid=(k_dim // tk, W_MAIN // tn),
        in_specs=[pl.BlockSpec((pl.Element(tn), pl.Element(tk)), lambda i, j: (src_row(j), i * tk)),
                  pl.BlockSpec((pl.Element(LANES), pl.Element(tk)), lambda i, j: (COL_A, i * tk))],
        out_specs=[pl.BlockSpec((tk, tn), lambda i, j: (i, j)),
                   pl.BlockSpec((tk, LANES), lambda i, j: (i, 0))],
        out_shape=[jax.ShapeDtypeStruct((k_dim, W_MAIN), BF16),
                   jax.ShapeDtypeStruct((k_dim, LANES), BF16)],
        compiler_params=_cparams(("parallel", "arbitrary")),
        name="repack_w_in",
    )(wt, wt)


def _norm_proj_kernel(x_ref, g_ref, w_ref, *rest, with_side):
    if with_side:
        ws_ref, out_ref, side_ref, h_scr = rest
    else:
        out_ref, h_scr = rest

    @pl.when(pl.program_id(1) == 0)
    def _():
        h = _rms(x_ref[...], g_ref[...]).astype(BF16)
        h_scr[...] = h
        if with_side:
            side_ref[...] = _dot(h, ws_ref[...])

    out_ref[...] = _dot(h_scr[...], w_ref[...].astype(BF16)).astype(out_ref.dtype)


def _norm_proj(x, g, w, w_side=None, *, tm, tn, out_dtype):
    n, d = x.shape
    ncol = w.shape[1]
    with_side = w_side is not None
    in_specs = [pl.BlockSpec((tm, d), lambda i, j: (i, 0)),
                pl.BlockSpec((1, d), lambda i, j: (0, 0)),
                pl.BlockSpec((d, tn), lambda i, j: (0, j))]
    out_specs = [pl.BlockSpec((tm, tn), lambda i, j: (i, j))]
    out_shape = [jax.ShapeDtypeStruct((n, ncol), out_dtype)]
    args = [x, g.reshape(1, d), w]
    if with_side:
        in_specs.append(pl.BlockSpec((d, LANES), lambda i, j: (0, 0)))
        out_specs.append(pl.BlockSpec((tm, LANES), lambda i, j: (i, 0)))
        out_shape.append(jax.ShapeDtypeStruct((n, LANES), F32))
        args.append(w_side)
    res = pl.pallas_call(
        functools.partial(_norm_proj_kernel, with_side=with_side),
        grid=(n // tm, ncol // tn),
        in_specs=in_specs, out_specs=out_specs, out_shape=out_shape,
        scratch_shapes=[pltpu.VMEM((tm, d), BF16)],
        compiler_params=_cparams(("parallel", "arbitrary")),
        name="norm_proj_side" if with_side else "norm_proj",
    )(*args)
    return res if with_side else res[0]


INV_BASE_LOG2 = 3
_GDN_WORK = (("k", 1, BF16, 0), ("kb", 1, BF16, 0), ("q", 1, BF16, 0), ("kd", 1, BF16, 0),
             ("vk", 2, BF16, 0), ("dinc", 1, F32, 0), ("low", 1, F32, 0), ("x", 1, F32, 0),
             ("p", 1, BF16, 0), ("wu", 2, BF16, 0),
             ("cq", 1, F32, SUBLANES), ("ck", 1, F32, SUBLANES), ("cv", 1, F32, SUBLANES))


def _split3(x):
    hi = x.astype(BF16)
    r1 = x - hi.astype(F32)
    mid = r1.astype(BF16)
    lo = (r1 - mid.astype(F32)).astype(BF16)
    return hi, mid, lo


def _gdn_prompt_kernel(q_ref, k_ref, v_ref, z_ref, ab_ref, cwq_ref, cwk_ref, cwv_ref, gp_ref,
                       gn_ref, o_ref, s_ref,
                       gc_s, gct_s, egc_s, ekd_s, beta_s, u_s, n_s, pw_s, qa_s, sv_s, *work_refs, group):
    nw = len(_GDN_WORK)
    work = [{spec[0]: ref for spec, ref in zip(_GDN_WORK, work_refs[j * nw:(j + 1) * nw])}
            for j in range(len(work_refs) // nw)]
    h = pl.program_id(1)
    t_len = q_ref.shape[1]
    c = GDN_CHUNK
    n_c = t_len // c
    row = lax.broadcasted_iota(jnp.int32, (c, c), 0)
    col = lax.broadcasted_iota(jnp.int32, (c, c), 1)

    @pl.when(h == 0)
    def _():
        gp = gp_ref[...]
        a_neg = -jnp.exp(gp[0:1])
        dt_b = gp[1:2]
        tri = jnp.where(row >= col, 1.0, 0.0).astype(BF16)

        def gate_chunk(ci, carry):
            rows = pl.ds(pl.multiple_of(ci * c, c), c)
            ab = ab_ref[0, rows, :]
            hi, mid, lo = _split3(a_neg * _softplus(ab + dt_b))
            gc = _dot(tri, hi) + _dot(tri, mid) + _dot(tri, lo)
            gc_s[rows, :] = gc
            gct_s[rows, :] = gc.T
            egc_s[rows, :] = jnp.exp(gc)
            ekd_s[rows, :] = jnp.exp(gc[c - 1:c, :] - gc)
            beta_s[rows, :] = jax.nn.sigmoid(ab)
            return carry

        lax.fori_loop(0, n_c, gate_chunk, 0)

    def l2n(x):
        return x * lax.rsqrt(jnp.sum(x * x, axis=-1, keepdims=True) + EPS)

    def operands(w, ci):
        r0 = ci * c
        rows = pl.ds(r0, c)

        def conv_silu(x_ref, cw_ref, stage):
            cur = x_ref[0, rows, :].astype(F32)
            if ci > 0:
                tail = x_ref[0, pl.ds(r0 - BF16_ROWS, BF16_ROWS), :].astype(F32)[BF16_ROWS - SUBLANES:]
            else:
                tail = jnp.zeros((SUBLANES, cur.shape[1]), F32)
            stage[0:SUBLANES, :] = tail
            stage[SUBLANES:SUBLANES + c, :] = cur
            cw = cw_ref[...]
            y = cur * cw[CONV_W - 1:CONV_W]
            for s in range(1, CONV_W):
                y = y + stage[SUBLANES - s:SUBLANES - s + c, :] * cw[CONV_W - 1 - s:CONV_W - s]
            return _silu(y)

        def column(scr, idx):
            return jnp.sum(jnp.where(col == idx, scr[rows, :], 0.0), axis=-1, keepdims=True)

        q = l2n(conv_silu(q_ref, cwq_ref, w["cq"])) * (GDN_DK ** -0.5)
        k = l2n(conv_silu(k_ref, cwk_ref, w["ck"]))
        v = conv_silu(v_ref, cwv_ref, w["cv"])
        gcc = column(gc_s, h)
        egc = column(egc_s, h)
        ekd = column(ekd_s, h)
        beta = column(beta_s, h + GDN_HEADS)
        grow = gct_s[pl.ds(r0 + h, 1), :]
        tri_i = row >= col
        w["dinc"][...] = jnp.where(tri_i, jnp.exp(jnp.where(tri_i, gcc - grow, 0.0)), 0.0)
        kb = k * beta
        w["k"][...] = k.astype(BF16)
        w["kb"][...] = kb.astype(BF16)
        w["q"][...] = q.astype(BF16)
        w["kd"][...] = (k * ekd).astype(BF16)
        w["vk"][:, :GDN_DV] = (v * beta).astype(BF16)
        w["vk"][:, GDN_DV:] = (kb * egc).astype(BF16)
        qa_s[ci, :, 0:GDN_DK] = (q * egc).astype(BF16)

    def gram(w, ci):
        r = _dot_nt(jnp.concatenate([w["kb"][...], w["q"][...]], axis=0), w["k"][...])
        d_incl = w["dinc"][...]
        low = r[:c] * jnp.where(row > col, d_incl, 0.0)
        qa_s[ci, :, GDN_DK:] = (r[c:] * d_incl).astype(BF16)
        w["low"][...] = low
        ld = jnp.where((row >> INV_BASE_LOG2) == (col >> INV_BASE_LOG2), low, 0.0)
        w["x"][...] = jnp.where(row == col, 1.0, 0.0) - ld
        w["p"][...] = ld.astype(BF16)

    def neumann(w, first, last):
        pb = w["p"][...]
        if first:
            w["p"][...] = _dot(pb, pb).astype(BF16)
        elif last:
            x = w["x"][...]
            w["x"][...] = x + _dot(x.astype(BF16), pb)
        else:
            x = w["x"][...]
            r = _dot(jnp.concatenate([x.astype(BF16), pb], axis=0), pb)
            w["x"][...] = x + r[:c]
            w["p"][...] = r[c:].astype(BF16)

    def merge_a(w, lg):
        pair = (row >> (lg + 1)) == (col >> (lg + 1))
        m = jnp.where(pair, (row >> lg) - (col >> lg), 0) > 0
        lm = jnp.where(m, w["low"][...], 0.0).astype(BF16)
        w["p"][...] = _dot(lm, w["x"][...].astype(BF16)).astype(BF16)

    def merge_b(w):
        x = w["x"][...]
        w["x"][...] = x - _dot(x.astype(BF16), w["p"][...])

    def solve(w, ci):
        uw = _dot(w["x"][...].astype(BF16), w["vk"][...])
        u_s[ci] = uw[:, :GDN_DV]
        uwb = uw.astype(BF16)
        w["wu"][...] = uwb
        pw_s[ci, c:2 * c, :] = uwb[:, GDN_DV:]

    def outer(w, ci):
        np_ = _dot_tn(w["kd"][...], w["wu"][...])
        n_s[ci] = np_[:, :GDN_DV]
        pw_s[ci, 0:c, :] = np_[:, GDN_DV:].astype(BF16)

    n_sq = INV_BASE_LOG2 - 1
    lgc = c.bit_length() - 1

    def prepare_stages(cis):
        over = lambda fn, *a: (lambda: [fn(work[ci], *[ci if x is cis else x for x in a])
                                        for ci in cis])
        stages = [over(operands, cis), over(gram, cis)]
        stages += [over(neumann, step == 0, step == n_sq) for step in range(n_sq + 1)]
        for lg in range(INV_BASE_LOG2, lgc):
            stages += [over(merge_a, lg), over(merge_b)]
        return stages + [over(solve, cis), over(outer, cis)]

    gn = gn_ref[...]
    lane1 = lax.broadcasted_iota(jnp.int32, (1, LANES), 1)

    def advance(ci, s_mat):
        sb = s_mat.astype(BF16)
        r = _dot(pw_s[ci], sb)
        g_last = jnp.sum(jnp.where(lane1 == h, gc_s[pl.ds(ci * c + c - 1, 1), :], 0.0),
                         axis=-1, keepdims=True)
        sv_s[ci, 0:c, :] = sb
        sv_s[ci, c:2 * c, :] = (u_s[ci] - r[c:2 * c]).astype(BF16)
        return s_mat * jnp.exp(g_last) - r[0:c] + n_s[ci]

    def emit(ci):
        rows = pl.ds(ci * c, c)
        o = _dot(qa_s[ci], sv_s[ci])
        zz = z_ref[0, rows, :].astype(F32)
        o_ref[0, rows, :] = (_rms(o, gn) * _silu(zz)).astype(o_ref.dtype)

    s_mat = jnp.zeros((GDN_DK, GDN_DV), F32)
    chain_todo, emit_todo = [], []
    for g in range(n_c // group):
        cis = list(range(g * group, (g + 1) * group))
        for stage in prepare_stages(cis):
            stage()
            if chain_todo:
                emit_todo.append(chain_todo[0])
                s_mat = advance(chain_todo.pop(0), s_mat)
        chain_todo += cis
    while chain_todo:
        if emit_todo:
            emit(emit_todo.pop(0))
        emit_todo.append(chain_todo[0])
        s_mat = advance(chain_todo.pop(0), s_mat)
    for ci in emit_todo:
        emit(ci)
    s_ref[0, 0] = s_mat


def _gdn_prompt(proj3, ab3, conv_w, gate_par, gdn_norm):
    bsz, t_len, _ = proj3.shape
    nh = GDN_HEADS
    blk = lambda off: pl.BlockSpec((1, t_len, LANES), lambda b, h: (b, 0, off + h))
    cw = lambda off: pl.BlockSpec((CONV_W, LANES), lambda b, h: (0, off + h))
    c = GDN_CHUNK
    n_c = t_len // c
    gate_scr = pltpu.VMEM((t_len, LANES), F32)
    group = 8
    work = [pltpu.VMEM((c + extra, wide * LANES), dt)
            for _ in range(n_c) for _, wide, dt, extra in _GDN_WORK]
    return pl.pallas_call(
        functools.partial(_gdn_prompt_kernel, group=group),
        grid=(bsz, nh),
        in_specs=[blk(0), blk(nh), blk(2 * nh), blk(3 * nh),
                  pl.BlockSpec((1, t_len, LANES), lambda b, h: (b, 0, 0)),
                  cw(0), cw(nh), cw(2 * nh),
                  pl.BlockSpec((SUBLANES, LANES), lambda b, h: (0, 0)),
                  pl.BlockSpec((1, LANES), lambda b, h: (0, 0))],
        out_specs=[pl.BlockSpec((1, t_len, LANES), lambda b, h: (b, 0, h)),
                   pl.BlockSpec((1, 1, GDN_DK, GDN_DV), lambda b, h: (b, h, 0, 0))],
        out_shape=[jax.ShapeDtypeStruct((bsz, t_len, MIX_A), BF16),
                   jax.ShapeDtypeStruct((bsz, nh, GDN_DK, GDN_DV), F32)],
        scratch_shapes=[gate_scr, gate_scr, gate_scr, gate_scr, gate_scr,
                        pltpu.VMEM((n_c, c, GDN_DV), F32), pltpu.VMEM((n_c, GDN_DK, GDN_DV), F32),
                        pltpu.VMEM((n_c, 2 * c, GDN_DK), BF16), pltpu.VMEM((n_c, c, GDN_DK + c), BF16),
                        pltpu.VMEM((n_c, GDN_DK + c, GDN_DV), BF16)] + work,
        compiler_params=_cparams(("parallel", "arbitrary")),
        name="gdn_prompt",
    )(proj3, proj3, proj3, proj3, ab3, conv_w, conv_w, conv_w, gate_par, gdn_norm.reshape(1, LANES))


def _gdn_sample_kernel(x_ref, ab_ref, hist_ref, cw_ref, gp_ref, gn_ref, s0_ref, o_ref, s_ref,
                       *, t_len):
    nh = GDN_HEADS
    g8 = SUBLANES
    n = nh * g8
    assert t_len + CONV_W - 1 <= g8
    cw = cw_ref[...]
    gp = gp_ref[...]
    gn = gn_ref[...]
    row8 = lax.broadcasted_iota(jnp.int32, (g8, QKV_W), 0)
    tpos = lax.broadcasted_iota(jnp.int32, (n, 1), 0) & (g8 - 1)
    rhead = lax.broadcasted_iota(jnp.int32, (n, GDN_DV), 0) >> 3
    ri = lax.broadcasted_iota(jnp.int32, (n, n), 0)
    ci = lax.broadcasted_iota(jnp.int32, (n, n), 1)
    tri = jnp.where((ri >> 3) == (ci >> 3), ri - ci, -1)
    eye = jnp.where(ri == ci, 1.0, 0.0)
    sel0 = jnp.where(lax.broadcasted_iota(jnp.int32, (n, LANES), 1) == 0, 1.0, 0.0).astype(BF16)
    valid = tpos < t_len
    a_neg = jnp.concatenate([jnp.broadcast_to(-jnp.exp(gp[0:1, h:h + 1]), (g8, 1)) for h in range(nh)], 0)
    dt_b = jnp.concatenate([jnp.broadcast_to(gp[1:2, h:h + 1], (g8, 1)) for h in range(nh)], 0)

    def l2n(x):
        return x * lax.rsqrt(jnp.sum(x * x, axis=-1, keepdims=True) + EPS)

    def operands(b):
        xb = x_ref[b]
        x = xb[:, :QKV_W].astype(F32)
        hst = hist_ref[b]
        y = x * cw[CONV_W - 1:CONV_W]
        for s in range(1, CONV_W):
            y = y + pltpu.roll(x, s, axis=0) * cw[CONV_W - 1 - s:CONV_W - s]
        for j in range(CONV_W - 1):
            hj = hst if j == 0 else pltpu.roll(hst, g8 - j, axis=0)
            y = y + jnp.where(row8 + j < CONV_W - 1, hj, 0.0) * cw[j:j + 1]
        qkv = jnp.where(row8 < t_len, _silu(y), 0.0)

        def stack(off):
            return jnp.concatenate([qkv[:, off + h * GDN_DK:off + (h + 1) * GDN_DK] for h in range(nh)], 0)

        q = l2n(stack(0)) * (GDN_DK ** -0.5)
        k = l2n(stack(QK_W))
        v = stack(2 * QK_W)
        ab = ab_ref[b]
        alpha = jnp.concatenate([ab[:, h:h + 1] for h in range(nh)], 0)
        braw = jnp.concatenate([ab[:, nh + h:nh + h + 1] for h in range(nh)], 0)
        g = jnp.where(valid, a_neg * _softplus(alpha + dt_b), 0.0)
        beta = jnp.where(valid, jax.nn.sigmoid(braw), 0.0)
        gc = g
        s = 1
        while s < g8:
            gc = gc + jnp.where(tpos >= s, pltpu.roll(gc, s, axis=0), 0.0)
            s *= 2
        g_last = jnp.concatenate([jnp.broadcast_to(gc[h * g8 + g8 - 1:(h + 1) * g8, :], (g8, 1))
                                  for h in range(nh)], 0)
        egc = jnp.exp(gc)
        kb = k * beta
        return dict(gc=gc, g_last=g_last, kbf=k.astype(BF16), kbb=kb.astype(BF16), qb=q.astype(BF16),
                    vb=(v * beta).astype(BF16), kg=(kb * egc).astype(BF16), qg=q * egc,
                    k_dec=(k * jnp.exp(g_last - gc)).astype(BF16))

    def gram(st):
        hi, mid, lo = _split3(jnp.broadcast_to(st["gc"], (n, LANES)))
        grow = _dot_nt(sel0, hi) + _dot_nt(sel0, mid) + _dot_nt(sel0, lo)
        d_incl = jnp.where(tri >= 0, jnp.exp(jnp.where(tri >= 0, st["gc"] - grow, 0.0)), 0.0)
        st["low"] = _dot_nt(st["kbb"], st["kbf"]) * jnp.where(tri > 0, d_incl, 0.0)
        st["a_in"] = (_dot_nt(st["qb"], st["kbf"]) * d_incl).astype(BF16)
        st["t_inv"] = eye - st["low"]
        st["p"] = st["low"].astype(BF16)

    def neumann(st):
        p = _dot(st["p"], st["p"])
        st["p"] = p.astype(BF16)
        st["t_inv"] = st["t_inv"] + _dot(st["t_inv"].astype(BF16), st["p"])

    def solve(st):
        tb = st["t_inv"].astype(BF16)
        st["u"] = _dot(tb, st["vb"])
        st["w"] = _dot(tb, st["kg"])

    def apply_state(st, b):
        ws, qs = [], []
        for h in range(nh):
            hs = slice(h * g8, (h + 1) * g8)
            lhs = jnp.concatenate([st["w"][hs], st["qg"][hs]], 0).astype(BF16)
            r = _dot(lhs, s0_ref[b, h].astype(BF16))
            ws.append(r[:g8])
            qs.append(r[g8:])
        st["v_new"] = st["u"] - jnp.concatenate(ws, 0)
        st["oq"] = jnp.concatenate(qs, 0)

    def finish(st, b):
        v_new = st["v_new"]
        o = st["oq"] + _dot(st["a_in"], v_new.astype(BF16))
        for h in range(nh):
            vm = jnp.where(rhead == h, v_new, 0.0).astype(BF16)
            dec = jnp.exp(st["g_last"][h * g8:h * g8 + 1, :])
            s_ref[b, h] = s0_ref[b, h] * dec + _dot_tn(st["k_dec"], vm)
        z = jnp.concatenate([x_ref[b][:, 3 * QK_W + h * GDN_DV:3 * QK_W + (h + 1) * GDN_DV].astype(F32)
                             for h in range(nh)], 0)
        o_ref[b] = (_rms(o, gn) * _silu(z)).astype(o_ref.dtype)

    bs_ = range(x_ref.shape[0])
    sts = [operands(b) for b in bs_]
    for st in sts:
        gram(st)
    covered = 2
    while covered < t_len:
        for st in sts:
            neumann(st)
        covered *= 2
    for st in sts:
        solve(st)
    for b in bs_:
        apply_state(sts[b], b)
    for b in bs_:
        finish(sts[b], b)


def _gdn_sample(proj3, ab3, hist, conv_w, gate_par, gdn_norm, s0, *, bb):
    bsz, t_len, _ = proj3.shape
    nh, g8 = GDN_HEADS, SUBLANES
    pad_t = lambda a, rows: jnp.pad(a, ((0, 0), (0, g8 - rows), (0, 0)))
    x8 = pad_t(proj3[:, :, :4 * QK_W], t_len)
    ab8 = pad_t(ab3, t_len)
    hist8 = pad_t(hist, CONV_W - 1)
    o, s_new = pl.pallas_call(
        functools.partial(_gdn_sample_kernel, t_len=t_len),
        grid=(bsz // bb,),
        in_specs=[pl.BlockSpec((bb, g8, 4 * QK_W), lambda i: (i, 0, 0)),
                  pl.BlockSpec((bb, g8, LANES), lambda i: (i, 0, 0)),
                  pl.BlockSpec((bb, g8, QKV_W), lambda i: (i, 0, 0)),
                  pl.BlockSpec((CONV_W, QKV_W), lambda i: (0, 0)),
                  pl.BlockSpec((SUBLANES, LANES), lambda i: (0, 0)),
                  pl.BlockSpec((1, LANES), lambda i: (0, 0)),
                  pl.BlockSpec((bb, nh, GDN_DK, GDN_DV), lambda i: (i, 0, 0, 0))],
        out_specs=[pl.BlockSpec((bb, nh * g8, GDN_DV), lambda i: (i, 0, 0)),
                   pl.BlockSpec((bb, nh, GDN_DK, GDN_DV), lambda i: (i, 0, 0, 0))],
        out_shape=[jax.ShapeDtypeStruct((bsz, nh * g8, GDN_DV), BF16),
                   jax.ShapeDtypeStruct((bsz, nh, GDN_DK, GDN_DV), F32)],
        compiler_params=_cparams(("parallel",)),
        name="gdn_sample",
    )(x8, ab8, hist8, conv_w, gate_par, gdn_norm.reshape(1, LANES), s0)
    o = o.reshape(bsz, nh, g8, GDN_DV)[:, :, :t_len].transpose(0, 2, 1, 3)
    return o.reshape(bsz, t_len, MIX_A), s_new


def _pool_finish(d_groups, pw_ref, ps_ref, zb):
    outs = [_dot(d.astype(BF16), pw_ref[gi]) for gi, d in enumerate(d_groups)]
    return jnp.concatenate(outs, axis=-1) * ps_ref[...] * _silu(zb)


def _pool_tile(u, halo, zb, pos, pw_ref, ps_ref):
    ext = jnp.concatenate([halo, u], axis=0)
    d_groups = []
    for gi, win in enumerate(POOL_WINDOWS):
        sl = slice(gi * POOL_CH, (gi + 1) * POOL_CH)
        lvl = ext[:, sl]
        s = 1
        while s < win:
            lvl = lvl + pltpu.roll(lvl, s, axis=0)
            s *= 2
        cnt = jnp.minimum(win, pos + 1).astype(F32)
        d_groups.append(lvl[POOL_HALO:, :] / cnt - u[:, sl])
    return _pool_finish(d_groups, pw_ref, ps_ref, zb)


def _mix_pool_kernel(x_ref, oa_ref, u_ref, halo_ref, zb_ref, pw_ref, ps_ref, wo_ref, nc_ref, wq_ref,
                     x1_ref, qx_ref, *, pos0):
    i = pl.program_id(1)
    tt = u_ref.shape[1]
    halo = jnp.where(i > 0, halo_ref[0].astype(F32), 0.0)
    pos = pos0 + i * tt + lax.broadcasted_iota(jnp.int32, (tt, 1), 0)
    ob = _pool_tile(u_ref[0].astype(F32), halo, zb_ref[0].astype(F32), pos, pw_ref, ps_ref)
    acc = _dot(oa_ref[0], wo_ref[0:MIX_A, :]) + _dot(ob.astype(BF16), wo_ref[MIX_A:D_MODEL, :])
    x1 = x_ref[0] + acc
    x1_ref[0] = x1
    qx_ref[0] = _dot(_rms(x1, nc_ref[...]).astype(BF16), wq_ref[...]).astype(qx_ref.dtype)


def _mix_pool(x3, oa3, proj3, pool_w, pool_scale, w_out, norm_cross, w_cq, *, tt, pos0):
    bsz, t_len, d = x3.shape
    ub, zb = COL_ZA // MIX_B + 1, COL_ZA // MIX_B + 2
    hb = tt // POOL_HALO
    const2 = lambda b, i: (0, 0)
    resident = dict(pipeline_mode=pl.Buffered(1))
    row = lambda w: pl.BlockSpec((1, tt, w), lambda b, i: (b, i, 0))
    return pl.pallas_call(
        functools.partial(_mix_pool_kernel, pos0=pos0),
        grid=(bsz, t_len // tt),
        in_specs=[row(d), row(MIX_A),
                  pl.BlockSpec((1, tt, MIX_B), lambda b, i: (b, i, ub)),
                  pl.BlockSpec((1, POOL_HALO, MIX_B), lambda b, i: (b, jnp.maximum(i * hb - 1, 0), ub)),
                  pl.BlockSpec((1, tt, MIX_B), lambda b, i: (b, i, zb)),
                  pl.BlockSpec((POOL_GROUPS, POOL_CH, POOL_CH), lambda b, i: (0, 0, 0), **resident),
                  pl.BlockSpec((1, MIX_B), const2, **resident),
                  pl.BlockSpec((d, d), const2, **resident),
                  pl.BlockSpec((1, d), const2, **resident),
                  pl.BlockSpec((d, d), const2, **resident)],
        out_specs=[row(d), row(d)],
        out_shape=[jax.ShapeDtypeStruct((bsz, t_len, d), F32), jax.ShapeDtypeStruct((bsz, t_len, d), BF16)],
        compiler_params=_cparams(("parallel", "arbitrary")),
        name="mix_pool",
    )(x3, oa3, proj3, proj3, proj3, pool_w, pool_scale.reshape(1, MIX_B), w_out,
      norm_cross.reshape(1, d), w_cq)


def _pool_sample_kernel(ext_ref, zb_ref, pw_ref, ps_ref, o_ref, *, pos0):
    t_len = zb_ref.shape[0]
    for t in range(t_len):
        e = POOL_BUF + t
        d_groups = []
        for gi, win in enumerate(POOL_WINDOWS):
            sl = slice(gi * POOL_CH, (gi + 1) * POOL_CH)
            tot = ext_ref[e, :, sl]
            for j in range(1, win):
                tot = tot + ext_ref[e - j, :, sl]
            cnt = float(min(win, pos0 + t + 1))
            d_groups.append(tot / cnt - ext_ref[e, :, sl])
        o_ref[t] = _pool_finish(d_groups, pw_ref, ps_ref, zb_ref[t].astype(F32)).astype(o_ref.dtype)


def _pool_sample(ext_tm, zb_tm, pool_w, pool_scale, *, bb, pos0):
    t_len, bsz, _ = zb_tm.shape
    return pl.pallas_call(
        functools.partial(_pool_sample_kernel, pos0=pos0),
        grid=(bsz // bb,),
        in_specs=[pl.BlockSpec((POOL_BUF + t_len, bb, MIX_B), lambda i: (0, i, 0)),
                  pl.BlockSpec((t_len, bb, MIX_B), lambda i: (0, i, 0)),
                  pl.BlockSpec((POOL_GROUPS, POOL_CH, POOL_CH), lambda i: (0, 0, 0)),
                  pl.BlockSpec((1, MIX_B), lambda i: (0, 0))],
        out_specs=pl.BlockSpec((t_len, bb, MIX_B), lambda i: (0, i, 0)),
        out_shape=jax.ShapeDtypeStruct((t_len, bsz, MIX_B), BF16),
        compiler_params=_cparams(("parallel",)),
        name="pool_sample",
    )(ext_tm, zb_tm, pool_w, pool_scale.reshape(1, MIX_B))


def _mix_out_kernel(x_ref, oa_ref, ob_ref, wo_ref, nc_ref, wq_ref, x1_ref, qx_ref):
    acc = _dot(oa_ref[...], wo_ref[0:MIX_A, :]) + _dot(ob_ref[...], wo_ref[MIX_A:D_MODEL, :])
    x1 = x_ref[...] + acc
    x1_ref[...] = x1
    qx_ref[...] = _dot(_rms(x1, nc_ref[...]).astype(BF16), wq_ref[...]).astype(qx_ref.dtype)


def _mix_out(x, oa, ob, w_out, norm_cross, w_cq, *, tm):
    n, d = x.shape
    const = lambda i: (0, 0)
    return pl.pallas_call(
        _mix_out_kernel,
        grid=(n // tm,),
        in_specs=[pl.BlockSpec((tm, d), lambda i: (i, 0)),
                  pl.BlockSpec((tm, MIX_A), lambda i: (i, 0)),
                  pl.BlockSpec((tm, MIX_B), lambda i: (i, 0)),
                  pl.BlockSpec((d, d), const), pl.BlockSpec((1, d), const), pl.BlockSpec((d, d), const)],
        out_specs=[pl.BlockSpec((tm, d), lambda i: (i, 0)), pl.BlockSpec((tm, d), lambda i: (i, 0))],
        out_shape=[jax.ShapeDtypeStruct((n, d), F32), jax.ShapeDtypeStruct((n, d), BF16)],
        compiler_params=_cparams(("parallel",)),
        name="mix_out",
    )(x, oa, ob, w_out, norm_cross.reshape(1, d), w_cq)


def _xattn_out_kernel(q_ref, k_ref, v_ref, x1_ref, wco_ref, nf_ref, y_ref):
    q = q_ref[0]
    scale = X_HEAD_DIM ** -0.5
    sls = [slice(h * X_HEAD_DIM, (h + 1) * X_HEAD_DIM) for h in range(X_HEADS)]
    ss = [_dot_nt(q[:, sl], k_ref[0, :, sl].astype(BF16)) * scale for sl in sls]
    ps = []
    for s in ss:
        p = jnp.exp(s - jnp.max(s, axis=-1, keepdims=True))
        ps.append((p / jnp.sum(p, axis=-1, keepdims=True)).astype(BF16))
    ctx = jnp.concatenate([_dot(p, v_ref[0, :, sl].astype(BF16)).astype(BF16)
                           for p, sl in zip(ps, sls)], axis=-1)
    y_ref[0] = _rms(x1_ref[0] + _dot(ctx, wco_ref[...]), nf_ref[...])


def _xattn_out(qx3, mk3, mv3, x13, w_co, norm_final, *, tq):
    bsz, t_len, d = qx3.shape
    n_mem = mk3.shape[1]
    row = pl.BlockSpec((1, tq, d), lambda b, i: (b, i, 0))
    mem = pl.BlockSpec((1, n_mem, d), lambda b, i: (b, 0, 0))
    return pl.pallas_call(
        _xattn_out_kernel,
        grid=(bsz, t_len // tq),
        in_specs=[row, mem, mem, row,
                  pl.BlockSpec((d, d), lambda b, i: (0, 0), pipeline_mode=pl.Buffered(1)),
                  pl.BlockSpec((1, d), lambda b, i: (0, 0), pipeline_mode=pl.Buffered(1))],
        out_specs=row,
        out_shape=jax.ShapeDtypeStruct((bsz, t_len, d), F32),
        compiler_params=_cparams(("parallel", "arbitrary")),
        name="xattn_out",
    )(qx3, mk3, mv3, x13, w_co, norm_final.reshape(1, d))


def _xattn_native_kernel(q_ref, k_ref, v_ref, o_ref, *, t_len):
    nj = X_HEAD_DIM // LANES
    grp = nj * X_HEADS
    th = t_len * X_HEADS
    scale = X_HEAD_DIM ** -0.5
    ncol = k_ref.shape[1]
    r = lax.broadcasted_iota(jnp.int32, (th, ncol), 0) & (X_HEADS - 1)
    c = lax.broadcasted_iota(jnp.int32, (th, ncol), 1) & (grp - 1)

    def probs(z):
        s = None
        for j in range(nj):
            zj = jnp.where(c == r + j * X_HEADS, z[j * th:(j + 1) * th, :], 0.0)
            if j:
                zj = pltpu.roll(zj, ncol - j * X_HEADS, axis=1)
            s = zj if s is None else s + zj
        sm = jnp.where(c == r, s * scale, -jnp.inf)
        p = jnp.exp(sm - jnp.max(sm, axis=1, keepdims=True))
        p = p / jnp.sum(p, axis=1, keepdims=True)
        return jnp.concatenate([p if j == 0 else pltpu.roll(p, j * X_HEADS, axis=1)
                                for j in range(nj)], axis=0).astype(BF16)

    bs_ = range(q_ref.shape[0])
    zs = [_dot_nt(q_ref[b], k_ref[b].astype(BF16)) for b in bs_]
    pps = [probs(z) for z in zs]
    for b in bs_:
        o_ref[b] = _dot(pps[b], v_ref[b].astype(BF16)).astype(o_ref.dtype)


def _xattn_native(q_rows, k_rows, v_rows, *, t_len, bb):
    bsz, nq, _ = q_rows.shape
    nk = k_rows.shape[1]
    return pl.pallas_call(
        functools.partial(_xattn_native_kernel, t_len=t_len),
        grid=(bsz // bb,),
        in_specs=[pl.BlockSpec((bb, nq, LANES), lambda i: (i, 0, 0)),
                  pl.BlockSpec((bb, nk, LANES), lambda i: (i, 0, 0)),
                  pl.BlockSpec((bb, nk, LANES), lambda i: (i, 0, 0))],
        out_specs=pl.BlockSpec((bb, nq, LANES), lambda i: (i, 0, 0)),
        out_shape=jax.ShapeDtypeStruct((bsz, nq, LANES), BF16),
        compiler_params=_cparams(("parallel",)),
        name="xattn_native",
    )(q_rows, k_rows, v_rows)


def _kv_rows(cache):
    bsz, n_mem, nh, dh = cache.shape
    nj = dh // LANES
    return cache.reshape(bsz, n_mem, nh, nj, LANES).transpose(0, 1, 3, 2, 4).reshape(
        bsz, n_mem * nj * nh, LANES)


def _attn_out_kernel(x1_ref, ctx_ref, wco_ref, nf_ref, y_ref):
    x2 = x1_ref[...] + _dot(ctx_ref[...], wco_ref[...])
    y_ref[...] = _rms(x2, nf_ref[...])


def _attn_out(x1, ctx, w_co, norm_final, *, tm):
    n, d = x1.shape
    const = lambda i: (0, 0)
    return pl.pallas_call(
        _attn_out_kernel,
        grid=(n // tm,),
        in_specs=[pl.BlockSpec((tm, d), lambda i: (i, 0)), pl.BlockSpec((tm, d), lambda i: (i, 0)),
                  pl.BlockSpec((d, d), const, pipeline_mode=pl.Buffered(1)),
                  pl.BlockSpec((1, d), const, pipeline_mode=pl.Buffered(1))],
        out_specs=pl.BlockSpec((tm, d), lambda i: (i, 0)),
        out_shape=jax.ShapeDtypeStruct((n, d), F32),
        compiler_params=_cparams(("parallel",)),
        name="attn_out",
    )(x1, ctx, w_co, norm_final.reshape(1, d))


def kernel(x_prompt, x_sample, mem_prompt, cache_mem_k, cache_mem_v, state_delta, state_conv,
           state_pool, norm_mix, w_in, conv_w, a_log, dt_bias, gdn_norm, pool_w, pool_scale,
           w_out, norm_mem, norm_cross, w_cq, w_ck, w_cv, w_co, norm_final):
    bp, tp, d = x_prompt.shape
    bs, ts, _ = x_sample.shape
    n_mem = mem_prompt.shape[1]

    w_main, w_gate = _repack_w_in(w_in[0].T, tk=512, tn=1024)
    wo, wcq, wco = (w[0].astype(BF16) for w in (w_out, w_cq, w_co))
    pw = pool_w[0].astype(BF16)
    gate_par = jnp.zeros((SUBLANES, LANES), F32)
    gate_par = gate_par.at[0, :GDN_HEADS].set(a_log[0]).at[1, :GDN_HEADS].set(dt_bias[0])
    cw = conv_w[0]

    mem2d = mem_prompt.reshape(bp * n_mem, d)
    mk = _norm_proj(mem2d, norm_mem[0], w_ck[0], tm=bp * n_mem, tn=1024, out_dtype=F32)
    mv = _norm_proj(mem2d, norm_mem[0], w_cv[0], tm=bp * n_mem, tn=1024, out_dtype=F32)

    proj_p, ab_p = _norm_proj(x_prompt.reshape(bp * tp, d), norm_mix[0], w_main, w_gate,
                              tm=1024, tn=1024, out_dtype=BF16)
    proj_p3 = proj_p.reshape(bp, tp, W_MAIN)
    oa_p, delta_p = _gdn_prompt(proj_p3, ab_p.reshape(bp, tp, LANES), cw, gate_par, gdn_norm[0])
    x1_p, qx_p = _mix_pool(x_prompt, oa_p, proj_p3, pw, pool_scale[0], wo, norm_cross[0], wcq,
                           tt=512, pos0=0)
    y_p = _xattn_out(qx_p, mk.reshape(bp, n_mem, d), mv.reshape(bp, n_mem, d), x1_p, wco,
                     norm_final, tq=512)
    conv_p = proj_p3[:, tp - (CONV_W - 1):, :QKV_W].astype(F32)
    pool_p = proj_p3[:, tp - POOL_BUF:, COL_ZA + MIX_A:COL_ZA + MIX_A + MIX_B].astype(F32)

    proj_s, ab_s = _norm_proj(x_sample.reshape(bs * ts, d), norm_mix[0], w_main, w_gate,
                              tm=bs * ts, tn=1024, out_dtype=BF16)
    proj_s3 = proj_s.reshape(bs, ts, W_MAIN)
    oa_s, delta_s = _gdn_sample(proj_s3, ab_s.reshape(bs, ts, LANES), state_conv[0], cw, gate_par,
                                gdn_norm[0], state_delta[0], bb=8)
    u_s = proj_s3[:, :, COL_ZA + MIX_A:COL_ZA + MIX_A + MIX_B].astype(F32)
    ext_s = jnp.concatenate([state_pool[0], u_s], axis=1)
    zb_tm = proj_s3[:, :, COL_ZA + MIX_A + MIX_B:].transpose(1, 0, 2)
    ob_s = _pool_sample(ext_s.transpose(1, 0, 2), zb_tm, pw, pool_scale[0], bb=32,
                        pos0=PAST_LEN).transpose(1, 0, 2)
    x1_s, qx_s = _mix_out(x_sample.reshape(bs * ts, d), oa_s.reshape(-1, MIX_A),
                          ob_s.reshape(-1, MIX_B), wo, norm_cross[0], wcq, tm=256)
    nj = X_HEAD_DIM // LANES
    q_rows = qx_s.reshape(bs, ts, X_HEADS, nj, LANES).transpose(0, 3, 1, 2, 4).reshape(
        bs, nj * ts * X_HEADS, LANES)
    ctx_rows = _xattn_native(q_rows, _kv_rows(cache_mem_k[0]), _kv_rows(cache_mem_v[0]),
                             t_len=ts, bb=4)
    ctx_s = ctx_rows.reshape(bs, nj, ts, X_HEADS, LANES).transpose(0, 2, 3, 1, 4).reshape(bs * ts, d)
    y_s = _attn_out(x1_s, ctx_s, wco, norm_final, tm=256).reshape(bs, ts, d)
    conv_s = jnp.concatenate([state_conv[0], proj_s3[:, :, :QKV_W].astype(F32)], axis=1)[:, ts:]
    pool_s = ext_s[:, ts:]

    hd = (X_HEADS, X_HEAD_DIM)
    return (y_p, y_s, mk.reshape(1, bp, n_mem, *hd), mv.reshape(1, bp, n_mem, *hd),
            delta_p[None], conv_p[None], pool_p[None], delta_s[None], conv_s[None], pool_s[None])
```

```python
import functools

import jax
import jax.numpy as jnp
from jax import lax
from jax.experimental import pallas as pl
from jax.experimental.pallas import tpu as pltpu

F32 = jnp.float32
BF16 = jnp.bfloat16

D_MODEL = 2048
MIX_A = D_MODEL // 2
MIX_B = D_MODEL - MIX_A
GDN_HEADS = 8
GDN_DK = MIX_A // GDN_HEADS
GDN_DV = MIX_A // GDN_HEADS
QK_W = GDN_HEADS * GDN_DK
QKV_W = 2 * QK_W + GDN_HEADS * GDN_DV
CONV_W = 4
POOL_WINDOWS = (2, 4, 8, 16)
POOL_GROUPS = len(POOL_WINDOWS)
POOL_CH = MIX_B // POOL_GROUPS
POOL_BUF = max(POOL_WINDOWS) - 1
X_HEADS = 4
X_HEAD_DIM = D_MODEL // X_HEADS
PAST_LEN = 16384
EPS = 1e-6
COL_ZA = QKV_W
COL_A = COL_ZA + GDN_HEADS * GDN_DV
COL_B = COL_A + GDN_HEADS
COL_U = COL_B + GDN_HEADS
COL_ZB = COL_U + MIX_B
IN_COLS = COL_ZB + MIX_B

W_MAIN = IN_COLS - 2 * GDN_HEADS
LANES = 128
SUBLANES = 8
BF16_ROWS = 16
GDN_CHUNK = 128
POOL_HALO = 16
VMEM_LIMIT = 56 * 1024 * 1024


def _cparams(sem):
    return pltpu.CompilerParams(dimension_semantics=sem, vmem_limit_bytes=VMEM_LIMIT)


def _dot(a, b):
    return jnp.dot(a, b, preferred_element_type=F32)


def _dot_nt(a, b):
    return lax.dot_general(a, b, (((1,), (1,)), ((), ())), preferred_element_type=F32)


def _dot_tn(a, b):
    return lax.dot_general(a, b, (((0,), (0,)), ((), ())), preferred_element_type=F32)


def _rms(x, g):
    return x * lax.rsqrt(jnp.mean(x * x, axis=-1, keepdims=True) + EPS) * g


def _silu(x):
    return x * jax.nn.sigmoid(x)


def _softplus(x):
    return jnp.maximum(x, 0.0) + jnp.log1p(jnp.exp(-jnp.abs(x)))


def _repack_kernel(wt_ref, gt_ref, main_ref, gate_ref):
    main_ref[...] = wt_ref[...].T.astype(main_ref.dtype)

    @pl.when(pl.program_id(1) == 0)
    def _():
        head = gt_ref[...].T
        lane = lax.broadcasted_iota(jnp.int32, head.shape, 1)
        gate_ref[...] = jnp.where(lane < COL_U - COL_A, head, 0.0).astype(gate_ref.dtype)


def _repack_w_in(wt, *, tk, tn):
    k_dim = wt.shape[1]
    n_gate = COL_U - COL_A
    src_row = lambda j: pl.multiple_of(j * tn + (j // (COL_A // tn)) * n_gate, n_gate)
    return pl.pallas_call(
        _repack_kernel,
        grid=(k_dim // tk, W_MAIN // tn),
        in_specs=[pl.BlockSpec((pl.Element(tn), pl.Element(tk)), lambda i, j: (src_row(j), i * tk)),
                  pl.BlockSpec((pl.Element(LANES), pl.Element(tk)), lambda i, j: (COL_A, i * tk))],
        out_specs=[pl.BlockSpec((tk, tn), lambda i, j: (i, j)),
                   pl.BlockSpec((tk, LANES), lambda i, j: (i, 0))],
        out_shape=[jax.ShapeDtypeStruct((k_dim, W_MAIN), BF16),
                   jax.ShapeDtypeStruct((k_dim, LANES), BF16)],
        compiler_params=_cparams(("parallel", "arbitrary")),
        name="repack_w_in",
    )(wt, wt)


def _norm_proj_kernel(x_ref, g_ref, w_ref, *rest, with_side):
    if with_side:
        ws_ref, out_ref, side_ref, h_scr = rest
    else:
        out_ref, h_scr = rest

    @pl.when(pl.program_id(1) == 0)
    def _():
        h = _rms(x_ref[...], g_ref[...]).astype(BF16)
        h_scr[...] = h
        if with_side:
            side_ref[...] = _dot(h, ws_ref[...])

    out_ref[...] = _dot(h_scr[...], w_ref[...].astype(BF16)).astype(out_ref.dtype)


def _norm_proj(x, g, w, w_side=None, *, tm, tn, out_dtype):
    n, d = x.shape
    ncol = w.shape[1]
    with_side = w_side is not None
    in_specs = [pl.BlockSpec((tm, d), lambda i, j: (i, 0)),
                pl.BlockSpec((1, d), lambda i, j: (0, 0)),
                pl.BlockSpec((d, tn), lambda i, j: (0, j))]
    out_specs = [pl.BlockSpec((tm, tn), lambda i, j: (i, j))]
    out_shape = [jax.ShapeDtypeStruct((n, ncol), out_dtype)]
    args = [x, g.reshape(1, d), w]
    if with_side:
        in_specs.append(pl.BlockSpec((d, LANES), lambda i, j: (0, 0)))
        out_specs.append(pl.BlockSpec((tm, LANES), lambda i, j: (i, 0)))
        out_shape.append(jax.ShapeDtypeStruct((n, LANES), F32))
        args.append(w_side)
    res = pl.pallas_call(
        functools.partial(_norm_proj_kernel, with_side=with_side),
        grid=(n // tm, ncol // tn),
        in_specs=in_specs, out_specs=out_specs, out_shape=out_shape,
        scratch_shapes=[pltpu.VMEM((tm, d), BF16)],
        compiler_params=_cparams(("parallel", "arbitrary")),
        name="norm_proj_side" if with_side else "norm_proj",
    )(*args)
    return res if with_side else res[0]


INV_BASE_LOG2 = 3
_GDN_WORK = (("k", 1, BF16, 0), ("kb", 1, BF16, 0), ("q", 1, BF16, 0), ("kd", 1, BF16, 0),
             ("vk", 2, BF16, 0), ("dinc", 1, F32, 0), ("low", 1, F32, 0), ("x", 1, F32, 0),
             ("p", 1, BF16, 0), ("wu", 2, BF16, 0),
             ("cq", 1, F32, SUBLANES), ("ck", 1, F32, SUBLANES), ("cv", 1, F32, SUBLANES))


def _split3(x):
    hi = x.astype(BF16)
    r1 = x - hi.astype(F32)
    mid = r1.astype(BF16)
    lo = (r1 - mid.astype(F32)).astype(BF16)
    return hi, mid, lo


def _gdn_prompt_kernel(q_ref, k_ref, v_ref, z_ref, ab_ref, cwq_ref, cwk_ref, cwv_ref, gp_ref,
                       gn_ref, xq_ref, xk_ref, xv_ref, o_ref, s_ref, xo_ref,
                       gc_s, gct_s, egc_s, ekd_s, beta_s, u_s, n_s, pw_s, qa_s, sv_s, *work_refs,
                       group, side_t):
    nw = len(_GDN_WORK)
    work = [{spec[0]: ref for spec, ref in zip(_GDN_WORK, work_refs[j * nw:(j + 1) * nw])}
            for j in range(len(work_refs) // nw)]
    h = pl.program_id(1)
    t_len = q_ref.shape[1]
    c = GDN_CHUNK
    n_c = t_len // c
    row = lax.broadcasted_iota(jnp.int32, (c, c), 0)
    col = lax.broadcasted_iota(jnp.int32, (c, c), 1)

    @pl.when((h == 0) & (pl.program_id(2) == 0))
    def _():
        gp = gp_ref[...]
        a_neg = -jnp.exp(gp[0:1])
        dt_b = gp[1:2]
        tri = jnp.where(row >= col, 1.0, 0.0).astype(BF16)

        def gate_chunk(ci, carry):
            rows = pl.ds(pl.multiple_of(ci * c, c), c)
            ab = ab_ref[0, rows, :]
            hi, mid, lo = _split3(a_neg * _softplus(ab + dt_b))
            gc = _dot(tri, hi) + _dot(tri, mid) + _dot(tri, lo)
            gc_s[rows, :] = gc
            gct_s[rows, :] = gc.T
            egc_s[rows, :] = jnp.exp(gc)
            ekd_s[rows, :] = jnp.exp(gc[c - 1:c, :] - gc)
            beta_s[rows, :] = jax.nn.sigmoid(ab)
            return carry

        lax.fori_loop(0, n_c, gate_chunk, 0)

    def l2n(x):
        return x * lax.rsqrt(jnp.sum(x * x, axis=-1, keepdims=True) + EPS)

    def operands(w, ci):
        r0 = ci * c
        rows = pl.ds(r0, c)

        def conv_silu(x_ref, cw_ref, stage):
            cur = x_ref[0, rows, :].astype(F32)
            if ci > 0:
                tail = x_ref[0, pl.ds(r0 - BF16_ROWS, BF16_ROWS), :].astype(F32)[BF16_ROWS - SUBLANES:]
            else:
                tail = jnp.zeros((SUBLANES, cur.shape[1]), F32)
            stage[0:SUBLANES, :] = tail
            stage[SUBLANES:SUBLANES + c, :] = cur
            cw = cw_ref[...]
            y = cur * cw[CONV_W - 1:CONV_W]
            for s in range(1, CONV_W):
                y = y + stage[SUBLANES - s:SUBLANES - s + c, :] * cw[CONV_W - 1 - s:CONV_W - s]
            return _silu(y)

        def column(scr, idx):
            return jnp.sum(jnp.where(col == idx, scr[rows, :], 0.0), axis=-1, keepdims=True)

        q = l2n(conv_silu(q_ref, cwq_ref, w["cq"])) * (GDN_DK ** -0.5)
        k = l2n(conv_silu(k_ref, cwk_ref, w["ck"]))
        v = conv_silu(v_ref, cwv_ref, w["cv"])
        gcc = column(gc_s, h)
        egc = column(egc_s, h)
        ekd = column(ekd_s, h)
        beta = column(beta_s, h + GDN_HEADS)
        grow = gct_s[pl.ds(r0 + h, 1), :]
        tri_i = row >= col
        w["dinc"][...] = jnp.where(tri_i, jnp.exp(jnp.where(tri_i, gcc - grow, 0.0)), 0.0)
        kb = k * beta
        w["k"][...] = k.astype(BF16)
        w["kb"][...] = kb.astype(BF16)
        w["q"][...] = q.astype(BF16)
        w["kd"][...] = (k * ekd).astype(BF16)
        w["vk"][:, :GDN_DV] = (v * beta).astype(BF16)
        w["vk"][:, GDN_DV:] = (kb * egc).astype(BF16)
        qa_s[ci, :, 0:GDN_DK] = (q * egc).astype(BF16)

    def gram(w, ci):
        r = _dot_nt(jnp.concatenate([w["kb"][...], w["q"][...]], axis=0), w["k"][...])
        d_incl = w["dinc"][...]
        low = r[:c] * jnp.where(row > col, d_incl, 0.0)
        qa_s[ci, :, GDN_DK:] = (r[c:] * d_incl).astype(BF16)
        w["low"][...] = low
        ld = jnp.where((row >> INV_BASE_LOG2) == (col >> INV_BASE_LOG2), low, 0.0)
        w["x"][...] = jnp.where(row == col, 1.0, 0.0) - ld
        w["p"][...] = ld.astype(BF16)

    def neumann(w, first, last):
        pb = w["p"][...]
        if first:
            w["p"][...] = _dot(pb, pb).astype(BF16)
        elif last:
            x = w["x"][...]
            w["x"][...] = x + _dot(x.astype(BF16), pb)
        else:
            x = w["x"][...]
            r = _dot(jnp.concatenate([x.astype(BF16), pb], axis=0), pb)
            w["x"][...] = x + r[:c]
            w["p"][...] = r[c:].astype(BF16)

    def merge_a(w, lg):
        pair = (row >> (lg + 1)) == (col >> (lg + 1))
        m = jnp.where(pair, (row >> lg) - (col >> lg), 0) > 0
        lm = jnp.where(m, w["low"][...], 0.0).astype(BF16)
        w["p"][...] = _dot(lm, w["x"][...].astype(BF16)).astype(BF16)

    def merge_b(w):
        x = w["x"][...]
        w["x"][...] = x - _dot(x.astype(BF16), w["p"][...])

    def solve(w, ci):
        uw = _dot(w["x"][...].astype(BF16), w["vk"][...])
        u_s[ci] = uw[:, :GDN_DV]
        uwb = uw.astype(BF16)
        w["wu"][...] = uwb
        pw_s[ci, c:2 * c, :] = uwb[:, GDN_DV:]

    def outer(w, ci):
        np_ = _dot_tn(w["kd"][...], w["wu"][...])
        n_s[ci] = np_[:, :GDN_DV]
        pw_s[ci, 0:c, :] = np_[:, GDN_DV:].astype(BF16)

    n_sq = INV_BASE_LOG2 - 1
    lgc = c.bit_length() - 1

    def prepare_stages(cis):
        over = lambda fn, *a: (lambda: [fn(work[ci], *[ci if x is cis else x for x in a])
                                        for ci in cis])
        stages = [over(operands, cis), over(gram, cis)]
        stages += [over(neumann, step == 0, step == n_sq) for step in range(n_sq + 1)]
        for lg in range(INV_BASE_LOG2, lgc):
            stages += [over(merge_a, lg), over(merge_b)]
        return stages + [over(solve, cis), over(outer, cis)]

    gn = gn_ref[...]
    lane1 = lax.broadcasted_iota(jnp.int32, (1, LANES), 1)

    def advance(ci, s_mat):
        sb = s_mat.astype(BF16)
        r = _dot(pw_s[ci], sb)
        g_last = jnp.sum(jnp.where(lane1 == h, gc_s[pl.ds(ci * c + c - 1, 1), :], 0.0),
                         axis=-1, keepdims=True)
        sv_s[ci, 0:c, :] = sb
        sv_s[ci, c:2 * c, :] = (u_s[ci] - r[c:2 * c]).astype(BF16)
        return s_mat * jnp.exp(g_last) - r[0:c] + n_s[ci]

    def emit(ci):
        rows = pl.ds(ci * c, c)
        o = _dot(qa_s[ci], sv_s[ci])
        zz = z_ref[0, rows, :].astype(F32)
        o_ref[0, rows, :] = (_rms(o, gn) * _silu(zz)).astype(o_ref.dtype)

    n_g = n_c // group
    groups = [list(range(g * group, (g + 1) * group)) for g in range(n_g)]

    def substep(g):
        side = _xattn_native_stages(xq_ref, xk_ref, xv_ref, xo_ref, side_t)
        chain_todo = list(groups[g - 1]) if g > 0 else []
        emit_todo = []
        s_mat = s_ref[0, 0] if g > 1 else jnp.zeros((GDN_DK, GDN_DV), F32)
        for stage in prepare_stages(groups[g]):
            stage()
            if side:
                side.pop(0)()
            if chain_todo:
                emit_todo.append(chain_todo[0])
                s_mat = advance(chain_todo.pop(0), s_mat)
        assert not side and not chain_todo
        if g == n_g - 1:
            chain_todo = list(groups[g])
            while chain_todo:
                if emit_todo:
                    emit(emit_todo.pop(0))
                emit_todo.append(chain_todo[0])
                s_mat = advance(chain_todo.pop(0), s_mat)
        for ci in emit_todo:
            emit(ci)
        if g > 0:
            s_ref[0, 0] = s_mat

    for g in range(n_g):
        pl.when(pl.program_id(2) == g)(functools.partial(substep, g))


def _gdn_prompt(proj3, ab3, conv_w, gate_par, gdn_norm, xq_rows, xk_rows, xv_rows, *, side_t):
    bsz, t_len, _ = proj3.shape
    nh = GDN_HEADS
    c = GDN_CHUNK
    n_c = t_len // c
    group = 8
    n_g = n_c // group
    xb, nq, _ = xq_rows.shape
    nk = xk_rows.shape[1]
    steps = bsz * nh * n_g
    assert xb % steps == 0
    xbb = xb // steps
    blk = lambda off: pl.BlockSpec((1, t_len, LANES), lambda b, h, g: (b, 0, off + h))
    cw = lambda off: pl.BlockSpec((CONV_W, LANES), lambda b, h, g: (0, off + h))
    side = lambda rows: pl.BlockSpec((xbb, rows, LANES), lambda b, h, g: ((b * nh + h) * n_g + g, 0, 0))
    gate_scr = pltpu.VMEM((t_len, LANES), F32)
    work = [pltpu.VMEM((c + extra, wide * LANES), dt)
            for _ in range(n_c) for _, wide, dt, extra in _GDN_WORK]
    return pl.pallas_call(
        functools.partial(_gdn_prompt_kernel, group=group, side_t=side_t),
        grid=(bsz, nh, n_g),
        in_specs=[blk(0), blk(nh), blk(2 * nh), blk(3 * nh),
                  pl.BlockSpec((1, t_len, LANES), lambda b, h, g: (b, 0, 0)),
                  cw(0), cw(nh), cw(2 * nh),
                  pl.BlockSpec((SUBLANES, LANES), lambda b, h, g: (0, 0)),
                  pl.BlockSpec((1, LANES), lambda b, h, g: (0, 0)),
                  side(nq), side(nk), side(nk)],
        out_specs=[pl.BlockSpec((1, t_len, LANES), lambda b, h, g: (b, 0, h)),
                   pl.BlockSpec((1, 1, GDN_DK, GDN_DV), lambda b, h, g: (b, h, 0, 0)),
                   side(nq)],
        out_shape=[jax.ShapeDtypeStruct((bsz, t_len, MIX_A), BF16),
                   jax.ShapeDtypeStruct((bsz, nh, GDN_DK, GDN_DV), F32),
                   jax.ShapeDtypeStruct((xb, nq, LANES), BF16)],
        scratch_shapes=[gate_scr, gate_scr, gate_scr, gate_scr, gate_scr,
                        pltpu.VMEM((n_c, c, GDN_DV), F32), pltpu.VMEM((n_c, GDN_DK, GDN_DV), F32),
                        pltpu.VMEM((n_c, 2 * c, GDN_DK), BF16), pltpu.VMEM((n_c, c, GDN_DK + c), BF16),
                        pltpu.VMEM((n_c, GDN_DK + c, GDN_DV), BF16)] + work,
        compiler_params=_cparams(("parallel", "arbitrary", "arbitrary")),
        name="gdn_prompt",
    )(proj3, proj3, proj3, proj3, ab3, conv_w, conv_w, conv_w, gate_par, gdn_norm.reshape(1, LANES),
      xq_rows, xk_rows, xv_rows)


def _gdn_sample_kernel(x_ref, ab_ref, hist_ref, cw_ref, gp_ref, gn_ref, s0_ref, o_ref, s_ref,
                       *, t_len):
    nh = GDN_HEADS
    g8 = SUBLANES
    n = nh * g8
    assert t_len + CONV_W - 1 <= g8
    cw = cw_ref[...]
    gp = gp_ref[...]
    gn = gn_ref[...]
    row8 = lax.broadcasted_iota(jnp.int32, (g8, QKV_W), 0)
    tpos = lax.broadcasted_iota(jnp.int32, (n, 1), 0) & (g8 - 1)
    rhead = lax.broadcasted_iota(jnp.int32, (n, GDN_DV), 0) >> 3
    ri = lax.broadcasted_iota(jnp.int32, (n, n), 0)
    ci = lax.broadcasted_iota(jnp.int32, (n, n), 1)
    tri = jnp.where((ri >> 3) == (ci >> 3), ri - ci, -1)
    eye = jnp.where(ri == ci, 1.0, 0.0)
    sel0 = jnp.where(lax.broadcasted_iota(jnp.int32, (n, LANES), 1) == 0, 1.0, 0.0).astype(BF16)
    valid = tpos < t_len
    a_neg = jnp.concatenate([jnp.broadcast_to(-jnp.exp(gp[0:1, h:h + 1]), (g8, 1)) for h in range(nh)], 0)
    dt_b = jnp.concatenate([jnp.broadcast_to(gp[1:2, h:h + 1], (g8, 1)) for h in range(nh)], 0)

    def l2n(x):
        return x * lax.rsqrt(jnp.sum(x * x, axis=-1, keepdims=True) + EPS)

    def operands(b):
        xb = x_ref[b]
        x = xb[:, :QKV_W].astype(F32)
        hst = hist_ref[b]
        y = x * cw[CONV_W - 1:CONV_W]
        for s in range(1, CONV_W):
            y = y + pltpu.roll(x, s, axis=0) * cw[CONV_W - 1 - s:CONV_W - s]
        for j in range(CONV_W - 1):
            hj = hst if j == 0 else pltpu.roll(hst, g8 - j, axis=0)
            y = y + jnp.where(row8 + j < CONV_W - 1, hj, 0.0) * cw[j:j + 1]
        qkv = jnp.where(row8 < t_len, _silu(y), 0.0)

        def stack(off):
            return jnp.concatenate([qkv[:, off + h * GDN_DK:off + (h + 1) * GDN_DK] for h in range(nh)], 0)

        q = l2n(stack(0)) * (GDN_DK ** -0.5)
        k = l2n(stack(QK_W))
        v = stack(2 * QK_W)
        ab = ab_ref[b]
        alpha = jnp.concatenate([ab[:, h:h + 1] for h in range(nh)], 0)
        braw = jnp.concatenate([ab[:, nh + h:nh + h + 1] for h in range(nh)], 0)
        g = jnp.where(valid, a_neg * _softplus(alpha + dt_b), 0.0)
        beta = jnp.where(valid, jax.nn.sigmoid(braw), 0.0)
        gc = g
        s = 1
        while s < g8:
            gc = gc + jnp.where(tpos >= s, pltpu.roll(gc, s, axis=0), 0.0)
            s *= 2
        g_last = jnp.concatenate([jnp.broadcast_to(gc[h * g8 + g8 - 1:(h + 1) * g8, :], (g8, 1))
                                  for h in range(nh)], 0)
        egc = jnp.exp(gc)
        kb = k * beta
        return dict(gc=gc, g_last=g_last, kbf=k.astype(BF16), kbb=kb.astype(BF16), qb=q.astype(BF16),
                    vb=(v * beta).astype(BF16), kg=(kb * egc).astype(BF16), qg=q * egc,
                    k_dec=(k * jnp.exp(g_last - gc)).astype(BF16))

    def gram(st):
        hi, mid, lo = _split3(jnp.broadcast_to(st["gc"], (n, LANES)))
        grow = _dot_nt(sel0, hi) + _dot_nt(sel0, mid) + _dot_nt(sel0, lo)
        d_incl = jnp.where(tri >= 0, jnp.exp(jnp.where(tri >= 0, st["gc"] - grow, 0.0)), 0.0)
        st["low"] = _dot_nt(st["kbb"], st["kbf"]) * jnp.where(tri > 0, d_incl, 0.0)
        st["a_in"] = (_dot_nt(st["qb"], st["kbf"]) * d_incl).astype(BF16)
        st["t_inv"] = eye - st["low"]
        st["p"] = st["low"].astype(BF16)

    def neumann(st):
        p = _dot(st["p"], st["p"])
        st["p"] = p.astype(BF16)
        st["t_inv"] = st["t_inv"] + _dot(st["t_inv"].astype(BF16), st["p"])

    def solve(st):
        tb = st["t_inv"].astype(BF16)
        st["u"] = _dot(tb, st["vb"])
        st["w"] = _dot(tb, st["kg"])

    def apply_state(st, b):
        ws, qs = [], []
        for h in range(nh):
            hs = slice(h * g8, (h + 1) * g8)
            lhs = jnp.concatenate([st["w"][hs], st["qg"][hs]], 0).astype(BF16)
            r = _dot(lhs, s0_ref[b, h].astype(BF16))
            ws.append(r[:g8])
            qs.append(r[g8:])
        st["v_new"] = st["u"] - jnp.concatenate(ws, 0)
        st["oq"] = jnp.concatenate(qs, 0)

    def finish(st, b):
        v_new = st["v_new"]
        o = st["oq"] + _dot(st["a_in"], v_new.astype(BF16))
        for h in range(nh):
            vm = jnp.where(rhead == h, v_new, 0.0).astype(BF16)
            dec = jnp.exp(st["g_last"][h * g8:h * g8 + 1, :])
            s_ref[b, h] = s0_ref[b, h] * dec + _dot_tn(st["k_dec"], vm)
        z = jnp.concatenate([x_ref[b][:, 3 * QK_W + h * GDN_DV:3 * QK_W + (h + 1) * GDN_DV].astype(F32)
                             for h in range(nh)], 0)
        o_ref[b] = (_rms(o, gn) * _silu(z)).astype(o_ref.dtype)

    bs_ = range(x_ref.shape[0])
    sts = [operands(b) for b in bs_]
    for st in sts:
        gram(st)
    covered = 2
    while covered < t_len:
        for st in sts:
            neumann(st)
        covered *= 2
    for st in sts:
        solve(st)
    for b in bs_:
        apply_state(sts[b], b)
    for b in bs_:
        finish(sts[b], b)


def _gdn_sample(proj3, ab3, hist, conv_w, gate_par, gdn_norm, s0, *, bb):
    bsz, t_len, _ = proj3.shape
    nh, g8 = GDN_HEADS, SUBLANES
    pad_t = lambda a, rows: jnp.pad(a, ((0, 0), (0, g8 - rows), (0, 0)))
    x8 = pad_t(proj3[:, :, :4 * QK_W], t_len)
    ab8 = pad_t(ab3, t_len)
    hist8 = pad_t(hist, CONV_W - 1)
    o, s_new = pl.pallas_call(
        functools.partial(_gdn_sample_kernel, t_len=t_len),
        grid=(bsz // bb,),
        in_specs=[pl.BlockSpec((bb, g8, 4 * QK_W), lambda i: (i, 0, 0)),
                  pl.BlockSpec((bb, g8, LANES), lambda i: (i, 0, 0)),
                  pl.BlockSpec((bb, g8, QKV_W), lambda i: (i, 0, 0)),
                  pl.BlockSpec((CONV_W, QKV_W), lambda i: (0, 0)),
                  pl.BlockSpec((SUBLANES, LANES), lambda i: (0, 0)),
                  pl.BlockSpec((1, LANES), lambda i: (0, 0)),
                  pl.BlockSpec((bb, nh, GDN_DK, GDN_DV), lambda i: (i, 0, 0, 0))],
        out_specs=[pl.BlockSpec((bb, nh * g8, GDN_DV), lambda i: (i, 0, 0)),
                   pl.BlockSpec((bb, nh, GDN_DK, GDN_DV), lambda i: (i, 0, 0, 0))],
        out_shape=[jax.ShapeDtypeStruct((bsz, nh * g8, GDN_DV), BF16),
                   jax.ShapeDtypeStruct((bsz, nh, GDN_DK, GDN_DV), F32)],
        compiler_params=_cparams(("parallel",)),
        name="gdn_sample",
    )(x8, ab8, hist8, conv_w, gate_par, gdn_norm.reshape(1, LANES), s0)
    o = o.reshape(bsz, nh, g8, GDN_DV)[:, :, :t_len].transpose(0, 2, 1, 3)
    return o.reshape(bsz, t_len, MIX_A), s_new


def _pool_finish(d_groups, pw_ref, ps_ref, zb):
    outs = [_dot(d.astype(BF16), pw_ref[gi]) for gi, d in enumerate(d_groups)]
    return jnp.concatenate(outs, axis=-1) * ps_ref[...] * _silu(zb)


def _pool_tile(u, halo, zb, pos, pw_ref, ps_ref):
    ext = jnp.concatenate([halo, u], axis=0)
    d_groups = []
    for gi, win in enumerate(POOL_WINDOWS):
        sl = slice(gi * POOL_CH, (gi + 1) * POOL_CH)
        lvl = ext[:, sl]
        s = 1
        while s < win:
            lvl = lvl + pltpu.roll(lvl, s, axis=0)
            s *= 2
        cnt = jnp.minimum(win, pos + 1).astype(F32)
        d_groups.append(lvl[POOL_HALO:, :] / cnt - u[:, sl])
    return _pool_finish(d_groups, pw_ref, ps_ref, zb)


def _mix_pool_kernel(x_ref, oa_ref, u_ref, halo_ref, zb_ref, pw_ref, ps_ref, wo_ref, nc_ref, wq_ref,
                     x1_ref, qx_ref, *, pos0):
    i = pl.program_id(1)
    tt = u_ref.shape[1]
    halo = jnp.where(i > 0, halo_ref[0].astype(F32), 0.0)
    pos = pos0 + i * tt + lax.broadcasted_iota(jnp.int32, (tt, 1), 0)
    ob = _pool_tile(u_ref[0].astype(F32), halo, zb_ref[0].astype(F32), pos, pw_ref, ps_ref)
    acc = _dot(oa_ref[0], wo_ref[0:MIX_A, :]) + _dot(ob.astype(BF16), wo_ref[MIX_A:D_MODEL, :])
    x1 = x_ref[0] + acc
    x1_ref[0] = x1
    qx_ref[0] = _dot(_rms(x1, nc_ref[...]).astype(BF16), wq_ref[...]).astype(qx_ref.dtype)


def _mix_pool(x3, oa3, proj3, pool_w, pool_scale, w_out, norm_cross, w_cq, *, tt, pos0):
    bsz, t_len, d = x3.shape
    ub, zb = COL_ZA // MIX_B + 1, COL_ZA // MIX_B + 2
    hb = tt // POOL_HALO
    const2 = lambda b, i: (0, 0)
    resident = dict(pipeline_mode=pl.Buffered(1))
    row = lambda w: pl.BlockSpec((1, tt, w), lambda b, i: (b, i, 0))
    return pl.pallas_call(
        functools.partial(_mix_pool_kernel, pos0=pos0),
        grid=(bsz, t_len // tt),
        in_specs=[row(d), row(MIX_A),
                  pl.BlockSpec((1, tt, MIX_B), lambda b, i: (b, i, ub)),
                  pl.BlockSpec((1, POOL_HALO, MIX_B), lambda b, i: (b, jnp.maximum(i * hb - 1, 0), ub)),
                  pl.BlockSpec((1, tt, MIX_B), lambda b, i: (b, i, zb)),
                  pl.BlockSpec((POOL_GROUPS, POOL_CH, POOL_CH), lambda b, i: (0, 0, 0), **resident),
                  pl.BlockSpec((1, MIX_B), const2, **resident),
                  pl.BlockSpec((d, d), const2, **resident),
                  pl.BlockSpec((1, d), const2, **resident),
                  pl.BlockSpec((d, d), const2, **resident)],
        out_specs=[row(d), row(d)],
        out_shape=[jax.ShapeDtypeStruct((bsz, t_len, d), F32), jax.ShapeDtypeStruct((bsz, t_len, d), BF16)],
        compiler_params=_cparams(("parallel", "arbitrary")),
        name="mix_pool",
    )(x3, oa3, proj3, proj3, proj3, pool_w, pool_scale.reshape(1, MIX_B), w_out,
      norm_cross.reshape(1, d), w_cq)


def _pool_sample_kernel(ext_ref, zb_ref, pw_ref, ps_ref, o_ref, *, pos0):
    t_len = zb_ref.shape[0]
    for t in range(t_len):
        e = POOL_BUF + t
        d_groups = []
        for gi, win in enumerate(POOL_WINDOWS):
            sl = slice(gi * POOL_CH, (gi + 1) * POOL_CH)
            tot = ext_ref[e, :, sl]
            for j in range(1, win):
                tot = tot + ext_ref[e - j, :, sl]
            cnt = float(min(win, pos0 + t + 1))
            d_groups.append(tot / cnt - ext_ref[e, :, sl])
        o_ref[t] = _pool_finish(d_groups, pw_ref, ps_ref, zb_ref[t].astype(F32)).astype(o_ref.dtype)


def _pool_sample(ext_tm, zb_tm, pool_w, pool_scale, *, bb, pos0):
    t_len, bsz, _ = zb_tm.shape
    return pl.pallas_call(
        functools.partial(_pool_sample_kernel, pos0=pos0),
        grid=(bsz // bb,),
        in_specs=[pl.BlockSpec((POOL_BUF + t_len, bb, MIX_B), lambda i: (0, i, 0)),
                  pl.BlockSpec((t_len, bb, MIX_B), lambda i: (0, i, 0)),
                  pl.BlockSpec((POOL_GROUPS, POOL_CH, POOL_CH), lambda i: (0, 0, 0)),
                  pl.BlockSpec((1, MIX_B), lambda i: (0, 0))],
        out_specs=pl.BlockSpec((t_len, bb, MIX_B), lambda i: (0, i, 0)),
        out_shape=jax.ShapeDtypeStruct((t_len, bsz, MIX_B), BF16),
        compiler_params=_cparams(("parallel",)),
        name="pool_sample",
    )(ext_tm, zb_tm, pool_w, pool_scale.reshape(1, MIX_B))


def _mix_out_kernel(x_ref, oa_ref, ob_ref, wo_ref, nc_ref, wq_ref, x1_ref, qx_ref):
    acc = _dot(oa_ref[...], wo_ref[0:MIX_A, :]) + _dot(ob_ref[...], wo_ref[MIX_A:D_MODEL, :])
    x1 = x_ref[...] + acc
    x1_ref[...] = x1
    qx_ref[...] = _dot(_rms(x1, nc_ref[...]).astype(BF16), wq_ref[...]).astype(qx_ref.dtype)


def _mix_out(x, oa, ob, w_out, norm_cross, w_cq, *, tm):
    n, d = x.shape
    const = lambda i: (0, 0)
    return pl.pallas_call(
        _mix_out_kernel,
        grid=(n // tm,),
        in_specs=[pl.BlockSpec((tm, d), lambda i: (i, 0)),
                  pl.BlockSpec((tm, MIX_A), lambda i: (i, 0)),
                  pl.BlockSpec((tm, MIX_B), lambda i: (i, 0)),
                  pl.BlockSpec((d, d), const), pl.BlockSpec((1, d), const), pl.BlockSpec((d, d), const)],
        out_specs=[pl.BlockSpec((tm, d), lambda i: (i, 0)), pl.BlockSpec((tm, d), lambda i: (i, 0))],
        out_shape=[jax.ShapeDtypeStruct((n, d), F32), jax.ShapeDtypeStruct((n, d), BF16)],
        compiler_params=_cparams(("parallel",)),
        name="mix_out",
    )(x, oa, ob, w_out, norm_cross.reshape(1, d), w_cq)


def _xattn_out_kernel(q_ref, k_ref, v_ref, x1_ref, wco_ref, nf_ref, y_ref):
    q = q_ref[0]
    scale = X_HEAD_DIM ** -0.5
    sls = [slice(h * X_HEAD_DIM, (h + 1) * X_HEAD_DIM) for h in range(X_HEADS)]
    ss = [_dot_nt(q[:, sl], k_ref[0, :, sl].astype(BF16)) * scale for sl in sls]
    ps = []
    for s in ss:
        p = jnp.exp(s - jnp.max(s, axis=-1, keepdims=True))
        ps.append((p / jnp.sum(p, axis=-1, keepdims=True)).astype(BF16))
    ctx = jnp.concatenate([_dot(p, v_ref[0, :, sl].astype(BF16)).astype(BF16)
                           for p, sl in zip(ps, sls)], axis=-1)
    y_ref[0] = _rms(x1_ref[0] + _dot(ctx, wco_ref[...]), nf_ref[...])


def _xattn_out(qx3, mk3, mv3, x13, w_co, norm_final, *, tq):
    bsz, t_len, d = qx3.shape
    n_mem = mk3.shape[1]
    row = pl.BlockSpec((1, tq, d), lambda b, i: (b, i, 0))
    mem = pl.BlockSpec((1, n_mem, d), lambda b, i: (b, 0, 0))
    return pl.pallas_call(
        _xattn_out_kernel,
        grid=(bsz, t_len // tq),
        in_specs=[row, mem, mem, row,
                  pl.BlockSpec((d, d), lambda b, i: (0, 0), pipeline_mode=pl.Buffered(1)),
                  pl.BlockSpec((1, d), lambda b, i: (0, 0), pipeline_mode=pl.Buffered(1))],
        out_specs=row,
        out_shape=jax.ShapeDtypeStruct((bsz, t_len, d), F32),
        compiler_params=_cparams(("parallel", "arbitrary")),
        name="xattn_out",
    )(qx3, mk3, mv3, x13, w_co, norm_final.reshape(1, d))


def _xattn_native_stages(q_ref, k_ref, v_ref, o_ref, t_len):
    nj = X_HEAD_DIM // LANES
    grp = nj * X_HEADS
    th = t_len * X_HEADS
    scale = X_HEAD_DIM ** -0.5
    ncol = k_ref.shape[1]
    r = lax.broadcasted_iota(jnp.int32, (th, ncol), 0) & (X_HEADS - 1)
    c = lax.broadcasted_iota(jnp.int32, (th, ncol), 1) & (grp - 1)

    def probs(z):
        s = None
        for j in range(nj):
            zj = jnp.where(c == r + j * X_HEADS, z[j * th:(j + 1) * th, :], 0.0)
            if j:
                zj = pltpu.roll(zj, ncol - j * X_HEADS, axis=1)
            s = zj if s is None else s + zj
        sm = jnp.where(c == r, s * scale, -jnp.inf)
        p = jnp.exp(sm - jnp.max(sm, axis=1, keepdims=True))
        p = p / jnp.sum(p, axis=1, keepdims=True)
        return jnp.concatenate([p if j == 0 else pltpu.roll(p, j * X_HEADS, axis=1)
                                for j in range(nj)], axis=0).astype(BF16)

    bs_ = range(q_ref.shape[0])
    vals = {}

    def scores():
        vals["z"] = [_dot_nt(q_ref[b], k_ref[b].astype(BF16)) for b in bs_]

    def probabilities():
        vals["p"] = [probs(z) for z in vals["z"]]

    def context():
        for b in bs_:
            o_ref[b] = _dot(vals["p"][b], v_ref[b].astype(BF16)).astype(o_ref.dtype)

    return [scores, probabilities, context]


def _kv_rows(cache):
    bsz, n_mem, nh, dh = cache.shape
    nj = dh // LANES
    return cache.reshape(bsz, n_mem, nh, nj, LANES).transpose(0, 1, 3, 2, 4).reshape(
        bsz, n_mem * nj * nh, LANES)


def _attn_out_kernel(x1_ref, ctx_ref, wco_ref, nf_ref, y_ref):
    x2 = x1_ref[...] + _dot(ctx_ref[...], wco_ref[...])
    y_ref[...] = _rms(x2, nf_ref[...])


def _attn_out(x1, ctx, w_co, norm_final, *, tm):
    n, d = x1.shape
    const = lambda i: (0, 0)
    return pl.pallas_call(
        _attn_out_kernel,
        grid=(n // tm,),
        in_specs=[pl.BlockSpec((tm, d), lambda i: (i, 0)), pl.BlockSpec((tm, d), lambda i: (i, 0)),
                  pl.BlockSpec((d, d), const, pipeline_mode=pl.Buffered(1)),
                  pl.BlockSpec((1, d), const, pipeline_mode=pl.Buffered(1))],
        out_specs=pl.BlockSpec((tm, d), lambda i: (i, 0)),
        out_shape=jax.ShapeDtypeStruct((n, d), F32),
        compiler_params=_cparams(("parallel",)),
        name="attn_out",
    )(x1, ctx, w_co, norm_final.reshape(1, d))


def kernel(x_prompt, x_sample, mem_prompt, cache_mem_k, cache_mem_v, state_delta, state_conv,
           state_pool, norm_mix, w_in, conv_w, a_log, dt_bias, gdn_norm, pool_w, pool_scale,
           w_out, norm_mem, norm_cross, w_cq, w_ck, w_cv, w_co, norm_final):
    bp, tp, d = x_prompt.shape
    bs, ts, _ = x_sample.shape
    n_mem = mem_prompt.shape[1]

    w_main, w_gate = _repack_w_in(w_in[0].T, tk=512, tn=1024)
    wo, wcq, wco = (w[0].astype(BF16) for w in (w_out, w_cq, w_co))
    pw = pool_w[0].astype(BF16)
    gate_par = jnp.zeros((SUBLANES, LANES), F32)
    gate_par = gate_par.at[0, :GDN_HEADS].set(a_log[0]).at[1, :GDN_HEADS].set(dt_bias[0])
    cw = conv_w[0]

    mem2d = mem_prompt.reshape(bp * n_mem, d)
    mk = _norm_proj(mem2d, norm_mem[0], w_ck[0], tm=bp * n_mem, tn=1024, out_dtype=F32)
    mv = _norm_proj(mem2d, norm_mem[0], w_cv[0], tm=bp * n_mem, tn=1024, out_dtype=F32)

    proj_s, ab_s = _norm_proj(x_sample.reshape(bs * ts, d), norm_mix[0], w_main, w_gate,
                              tm=bs * ts, tn=1024, out_dtype=BF16)
    proj_s3 = proj_s.reshape(bs, ts, W_MAIN)
    oa_s, delta_s = _gdn_sample(proj_s3, ab_s.reshape(bs, ts, LANES), state_conv[0], cw, gate_par,
                                gdn_norm[0], state_delta[0], bb=8)
    u_s = proj_s3[:, :, COL_ZA + MIX_A:COL_ZA + MIX_A + MIX_B].astype(F32)
    ext_s = jnp.concatenate([state_pool[0], u_s], axis=1)
    zb_tm = proj_s3[:, :, COL_ZA + MIX_A + MIX_B:].transpose(1, 0, 2)
    ob_s = _pool_sample(ext_s.transpose(1, 0, 2), zb_tm, pw, pool_scale[0], bb=32,
                        pos0=PAST_LEN).transpose(1, 0, 2)
    x1_s, qx_s = _mix_out(x_sample.reshape(bs * ts, d), oa_s.reshape(-1, MIX_A),
                          ob_s.reshape(-1, MIX_B), wo, norm_cross[0], wcq, tm=256)
    nj = X_HEAD_DIM // LANES
    q_rows = qx_s.reshape(bs, ts, X_HEADS, nj, LANES).transpose(0, 3, 1, 2, 4).reshape(
        bs, nj * ts * X_HEADS, LANES)

    proj_p, ab_p = _norm_proj(x_prompt.reshape(bp * tp, d), norm_mix[0], w_main, w_gate,
                              tm=1024, tn=1024, out_dtype=BF16)
    proj_p3 = proj_p.reshape(bp, tp, W_MAIN)
    oa_p, delta_p, ctx_rows = _gdn_prompt(
        proj_p3, ab_p.reshape(bp, tp, LANES), cw, gate_par, gdn_norm[0],
        q_rows, _kv_rows(cache_mem_k[0]), _kv_rows(cache_mem_v[0]), side_t=ts)
    x1_p, qx_p = _mix_pool(x_prompt, oa_p, proj_p3, pw, pool_scale[0], wo, norm_cross[0], wcq,
                           tt=512, pos0=0)
    y_p = _xattn_out(qx_p, mk.reshape(bp, n_mem, d), mv.reshape(bp, n_mem, d), x1_p, wco,
                     norm_final, tq=512)
    conv_p = proj_p3[:, tp - (CONV_W - 1):, :QKV_W].astype(F32)
    pool_p = proj_p3[:, tp - POOL_BUF:, COL_ZA + MIX_A:COL_ZA + MIX_A + MIX_B].astype(F32)

    ctx_s = ctx_rows.reshape(bs, nj, ts, X_HEADS, LANES).transpose(0, 2, 3, 1, 4).reshape(bs * ts, d)
    y_s = _attn_out(x1_s, ctx_s, wco, norm_final, tm=256).reshape(bs, ts, d)
    conv_s = jnp.concatenate([state_conv[0], proj_s3[:, :, :QKV_W].astype(F32)], axis=1)[:, ts:]
    pool_s = ext_s[:, ts:]

    hd = (X_HEADS, X_HEAD_DIM)
    return (y_p, y_s, mk.reshape(1, bp, n_mem, *hd), mv.reshape(1, bp, n_mem, *hd),
            delta_p[None], conv_p[None], pool_p[None], delta_s[None], conv_s[None], pool_s[None])
```

```python
import functools

import jax
import jax.numpy as jnp
from jax import lax
from jax.experimental import pallas as pl
from jax.experimental.pallas import tpu as pltpu

F32 = jnp.float32
BF16 = jnp.bfloat16

D_MODEL = 2048
MIX_A = D_MODEL // 2
MIX_B = D_MODEL - MIX_A
GDN_HEADS = 8
GDN_DK = MIX_A // GDN_HEADS
GDN_DV = MIX_A // GDN_HEADS
QK_W = GDN_HEADS * GDN_DK
QKV_W = 2 * QK_W + GDN_HEADS * GDN_DV
CONV_W = 4
POOL_WINDOWS = (2, 4, 8, 16)
POOL_GROUPS = len(POOL_WINDOWS)
POOL_CH = MIX_B // POOL_GROUPS
POOL_BUF = max(POOL_WINDOWS) - 1
X_HEADS = 4
X_HEAD_DIM = D_MODEL // X_HEADS
PAST_LEN = 16384
EPS = 1e-6
COL_ZA = QKV_W
COL_A = COL_ZA + GDN_HEADS * GDN_DV
COL_B = COL_A + GDN_HEADS
COL_U = COL_B + GDN_HEADS
COL_ZB = COL_U + MIX_B
IN_COLS = COL_ZB + MIX_B

W_MAIN = IN_COLS - 2 * GDN_HEADS
LANES = 128
SUBLANES = 8
BF16_ROWS = 16
GDN_CHUNK = 128
POOL_HALO = 16
VMEM_LIMIT = 56 * 1024 * 1024


def _cparams(sem):
    return pltpu.CompilerParams(dimension_semantics=sem, vmem_limit_bytes=VMEM_LIMIT)


def _dot(a, b):
    return jnp.dot(a, b, preferred_element_type=F32)


def _dot_nt(a, b):
    return lax.dot_general(a, b, (((1,), (1,)), ((), ())), preferred_element_type=F32)


def _dot_tn(a, b):
    return lax.dot_general(a, b, (((0,), (0,)), ((), ())), preferred_element_type=F32)


def _rms(x, g):
    return x * lax.rsqrt(jnp.mean(x * x, axis=-1, keepdims=True) + EPS) * g


def _silu(x):
    return x * jax.nn.sigmoid(x)


def _softplus(x):
    return jnp.maximum(x, 0.0) + jnp.log1p(jnp.exp(-jnp.abs(x)))


def _repack_kernel(wt_ref, gt_ref, main_ref, gate_ref):
    main_ref[...] = wt_ref[...].T.astype(main_ref.dtype)

    @pl.when(pl.program_id(1) == 0)
    def _():
        head = gt_ref[...].T
        lane = lax.broadcasted_iota(jnp.int32, head.shape, 1)
        gate_ref[...] = jnp.where(lane < COL_U - COL_A, head, 0.0).astype(gate_ref.dtype)


def _repack_w_in(wt, *, tk, tn):
    k_dim = wt.shape[1]
    n_gate = COL_U - COL_A
    src_row = lambda j: pl.multiple_of(j * tn + (j // (COL_A // tn)) * n_gate, n_gate)
    return pl.pallas_call(
        _repack_kernel,
        grid=(k_dim // tk, W_MAIN // tn),
        in_specs=[pl.BlockSpec((pl.Element(tn), pl.Element(tk)), lambda i, j: (src_row(j), i * tk)),
                  pl.BlockSpec((pl.Element(LANES), pl.Element(tk)), lambda i, j: (COL_A, i * tk))],
        out_specs=[pl.BlockSpec((tk, tn), lambda i, j: (i, j)),
                   pl.BlockSpec((tk, LANES), lambda i, j: (i, 0))],
        out_shape=[jax.ShapeDtypeStruct((k_dim, W_MAIN), BF16),
                   jax.ShapeDtypeStruct((k_dim, LANES), BF16)],
        compiler_params=_cparams(("parallel", "arbitrary")),
        name="repack_w_in",
    )(wt, wt)


def _norm_proj_kernel(x_ref, g_ref, w_ref, *rest, with_side, attn_t):
    rest = list(rest)
    ws_ref = rest.pop(0) if with_side else None
    xq_ref, xk_ref, xv_ref = (rest.pop(0), rest.pop(0), rest.pop(0)) if attn_t else (None,) * 3
    out_ref = rest.pop(0)
    side_ref = rest.pop(0) if with_side else None
    xo_ref = rest.pop(0) if attn_t else None
    h_scr = rest.pop(0)

    @pl.when(pl.program_id(1) == 0)
    def _():
        h = _rms(x_ref[...], g_ref[...]).astype(BF16)
        h_scr[...] = h
        if with_side:
            side_ref[...] = _dot(h, ws_ref[...])

    out_ref[...] = _dot(h_scr[...], w_ref[...].astype(BF16)).astype(out_ref.dtype)
    if attn_t:
        for stage in _xattn_native_stages(xq_ref, xk_ref, xv_ref, xo_ref, attn_t):
            stage()


def _norm_proj(x, g, w, w_side=None, *, tm, tn, out_dtype, attn=None):
    n, d = x.shape
    ncol = w.shape[1]
    with_side = w_side is not None
    in_specs = [pl.BlockSpec((tm, d), lambda i, j: (i, 0)),
                pl.BlockSpec((1, d), lambda i, j: (0, 0)),
                pl.BlockSpec((d, tn), lambda i, j: (0, j))]
    out_specs = [pl.BlockSpec((tm, tn), lambda i, j: (i, j))]
    out_shape = [jax.ShapeDtypeStruct((n, ncol), out_dtype)]
    args = [x, g.reshape(1, d), w]
    if with_side:
        in_specs.append(pl.BlockSpec((d, LANES), lambda i, j: (0, 0)))
        out_specs.append(pl.BlockSpec((tm, LANES), lambda i, j: (i, 0)))
        out_shape.append(jax.ShapeDtypeStruct((n, LANES), F32))
        args.append(w_side)
    if attn is not None:
        xq, xk, xv, attn_t = attn
        n_j = ncol // tn
        steps = (n // tm) * n_j
        assert xq.shape[0] % steps == 0
        xbb = xq.shape[0] // steps
        share = lambda rows: pl.BlockSpec((xbb, rows, LANES), lambda i, j: (i * n_j + j, 0, 0))
        in_specs += [share(xq.shape[1]), share(xk.shape[1]), share(xv.shape[1])]
        out_specs.append(share(xq.shape[1]))
        out_shape.append(jax.ShapeDtypeStruct(xq.shape, BF16))
        args += [xq, xk, xv]
    res = pl.pallas_call(
        functools.partial(_norm_proj_kernel, with_side=with_side,
                          attn_t=attn[3] if attn is not None else 0),
        grid=(n // tm, ncol // tn),
        in_specs=in_specs, out_specs=out_specs, out_shape=out_shape,
        scratch_shapes=[pltpu.VMEM((tm, d), BF16)],
        compiler_params=_cparams(("parallel", "arbitrary")),
        name=("norm_proj_attn" if attn is not None else "norm_proj_side") if with_side else "norm_proj",
    )(*args)
    return res if len(res) > 1 else res[0]


INV_BASE_LOG2 = 3
_GDN_WORK = (("k", 1, BF16, 0), ("kb", 1, BF16, 0), ("q", 1, BF16, 0), ("kd", 1, BF16, 0),
             ("vk", 2, BF16, 0), ("dinc", 1, F32, 0), ("low", 1, F32, 0), ("x", 1, F32, 0),
             ("p", 1, BF16, 0), ("wu", 2, BF16, 0),
             ("cq", 1, F32, SUBLANES), ("ck", 1, F32, SUBLANES), ("cv", 1, F32, SUBLANES))


def _split3(x):
    hi = x.astype(BF16)
    r1 = x - hi.astype(F32)
    mid = r1.astype(BF16)
    lo = (r1 - mid.astype(F32)).astype(BF16)
    return hi, mid, lo


def _gdn_prompt_kernel(q_ref, k_ref, v_ref, z_ref, ab_ref, cwq_ref, cwk_ref, cwv_ref, gp_ref,
                       gn_ref, o_ref, s_ref,
                       gc_s, gct_s, egc_s, ekd_s, beta_s, u_s, n_s, pw_s, qa_s, sv_s, *work_refs, group):
    nw = len(_GDN_WORK)
    work = [{spec[0]: ref for spec, ref in zip(_GDN_WORK, work_refs[j * nw:(j + 1) * nw])}
            for j in range(len(work_refs) // nw)]
    h = pl.program_id(1)
    t_len = q_ref.shape[1]
    c = GDN_CHUNK
    n_c = t_len // c
    row = lax.broadcasted_iota(jnp.int32, (c, c), 0)
    col = lax.broadcasted_iota(jnp.int32, (c, c), 1)

    @pl.when(h == 0)
    def _():
        gp = gp_ref[...]
        a_neg = -jnp.exp(gp[0:1])
        dt_b = gp[1:2]
        tri = jnp.where(row >= col, 1.0, 0.0).astype(BF16)

        def gate_chunk(ci, carry):
            rows = pl.ds(pl.multiple_of(ci * c, c), c)
            ab = ab_ref[0, rows, :]
            hi, mid, lo = _split3(a_neg * _softplus(ab + dt_b))
            gc = _dot(tri, hi) + _dot(tri, mid) + _dot(tri, lo)
            gc_s[rows, :] = gc
            gct_s[rows, :] = gc.T
            egc_s[rows, :] = jnp.exp(gc)
            ekd_s[rows, :] = jnp.exp(gc[c - 1:c, :] - gc)
            beta_s[rows, :] = jax.nn.sigmoid(ab)
            return carry

        lax.fori_loop(0, n_c, gate_chunk, 0)

    def l2n(x):
        return x * lax.rsqrt(jnp.sum(x * x, axis=-1, keepdims=True) + EPS)

    def operands(w, ci):
        r0 = ci * c
        rows = pl.ds(r0, c)

        def conv_silu(x_ref, cw_ref, stage):
            cur = x_ref[0, rows, :].astype(F32)
            if ci > 0:
                tail = x_ref[0, pl.ds(r0 - BF16_ROWS, BF16_ROWS), :].astype(F32)[BF16_ROWS - SUBLANES:]
            else:
                tail = jnp.zeros((SUBLANES, cur.shape[1]), F32)
            stage[0:SUBLANES, :] = tail
            stage[SUBLANES:SUBLANES + c, :] = cur
            cw = cw_ref[...]
            y = cur * cw[CONV_W - 1:CONV_W]
            for s in range(1, CONV_W):
                y = y + stage[SUBLANES - s:SUBLANES - s + c, :] * cw[CONV_W - 1 - s:CONV_W - s]
            return _silu(y)

        def column(scr, idx):
            return jnp.sum(jnp.where(col == idx, scr[rows, :], 0.0), axis=-1, keepdims=True)

        q = l2n(conv_silu(q_ref, cwq_ref, w["cq"])) * (GDN_DK ** -0.5)
        k = l2n(conv_silu(k_ref, cwk_ref, w["ck"]))
        v = conv_silu(v_ref, cwv_ref, w["cv"])
        gcc = column(gc_s, h)
        egc = column(egc_s, h)
        ekd = column(ekd_s, h)
        beta = column(beta_s, h + GDN_HEADS)
        grow = gct_s[pl.ds(r0 + h, 1), :]
        tri_i = row >= col
        w["dinc"][...] = jnp.where(tri_i, jnp.exp(jnp.where(tri_i, gcc - grow, 0.0)), 0.0)
        kb = k * beta
        w["k"][...] = k.astype(BF16)
        w["kb"][...] = kb.astype(BF16)
        w["q"][...] = q.astype(BF16)
        w["kd"][...] = (k * ekd).astype(BF16)
        w["vk"][:, :GDN_DV] = (v * beta).astype(BF16)
        w["vk"][:, GDN_DV:] = (kb * egc).astype(BF16)
        qa_s[ci, :, 0:GDN_DK] = (q * egc).astype(BF16)

    def gram(w, ci):
        r = _dot_nt(jnp.concatenate([w["kb"][...], w["q"][...]], axis=0), w["k"][...])
        d_incl = w["dinc"][...]
        low = r[:c] * jnp.where(row > col, d_incl, 0.0)
        qa_s[ci, :, GDN_DK:] = (r[c:] * d_incl).astype(BF16)
        w["low"][...] = low
        ld = jnp.where((row >> INV_BASE_LOG2) == (col >> INV_BASE_LOG2), low, 0.0)
        w["x"][...] = jnp.where(row == col, 1.0, 0.0) - ld
        w["p"][...] = ld.astype(BF16)

    def neumann(w, first, last):
        pb = w["p"][...]
        if first:
            w["p"][...] = _dot(pb, pb).astype(BF16)
        elif last:
            x = w["x"][...]
            w["x"][...] = x + _dot(x.astype(BF16), pb)
        else:
            x = w["x"][...]
            r = _dot(jnp.concatenate([x.astype(BF16), pb], axis=0), pb)
            w["x"][...] = x + r[:c]
            w["p"][...] = r[c:].astype(BF16)

    def merge_a(w, lg):
        pair = (row >> (lg + 1)) == (col >> (lg + 1))
        m = jnp.where(pair, (row >> lg) - (col >> lg), 0) > 0
        lm = jnp.where(m, w["low"][...], 0.0).astype(BF16)
        w["p"][...] = _dot(lm, w["x"][...].astype(BF16)).astype(BF16)

    def merge_b(w):
        x = w["x"][...]
        w["x"][...] = x - _dot(x.astype(BF16), w["p"][...])

    def solve(w, ci):
        uw = _dot(w["x"][...].astype(BF16), w["vk"][...])
        u_s[ci] = uw[:, :GDN_DV]
        uwb = uw.astype(BF16)
        w["wu"][...] = uwb
        pw_s[ci, c:2 * c, :] = uwb[:, GDN_DV:]

    def outer(w, ci):
        np_ = _dot_tn(w["kd"][...], w["wu"][...])
        n_s[ci] = np_[:, :GDN_DV]
        pw_s[ci, 0:c, :] = np_[:, GDN_DV:].astype(BF16)

    n_sq = INV_BASE_LOG2 - 1
    lgc = c.bit_length() - 1

    def prepare_stages(cis):
        over = lambda fn, *a: (lambda: [fn(work[ci], *[ci if x is cis else x for x in a])
                                        for ci in cis])
        stages = [over(operands, cis), over(gram, cis)]
        stages += [over(neumann, step == 0, step == n_sq) for step in range(n_sq + 1)]
        for lg in range(INV_BASE_LOG2, lgc):
            stages += [over(merge_a, lg), over(merge_b)]
        return stages + [over(solve, cis), over(outer, cis)]

    gn = gn_ref[...]
    lane1 = lax.broadcasted_iota(jnp.int32, (1, LANES), 1)

    def advance(ci, s_mat):
        sb = s_mat.astype(BF16)
        r = _dot(pw_s[ci], sb)
        g_last = jnp.sum(jnp.where(lane1 == h, gc_s[pl.ds(ci * c + c - 1, 1), :], 0.0),
                         axis=-1, keepdims=True)
        sv_s[ci, 0:c, :] = sb
        sv_s[ci, c:2 * c, :] = (u_s[ci] - r[c:2 * c]).astype(BF16)
        return s_mat * jnp.exp(g_last) - r[0:c] + n_s[ci]

    def emit(ci):
        rows = pl.ds(ci * c, c)
        o = _dot(qa_s[ci], sv_s[ci])
        zz = z_ref[0, rows, :].astype(F32)
        o_ref[0, rows, :] = (_rms(o, gn) * _silu(zz)).astype(o_ref.dtype)

    s_mat = jnp.zeros((GDN_DK, GDN_DV), F32)
    chain_todo, emit_todo = [], []
    for g in range(n_c // group):
        cis = list(range(g * group, (g + 1) * group))
        for stage in prepare_stages(cis):
            stage()
            if chain_todo:
                emit_todo.append(chain_todo[0])
                s_mat = advance(chain_todo.pop(0), s_mat)
        chain_todo += cis
    while chain_todo:
        if emit_todo:
            emit(emit_todo.pop(0))
        emit_todo.append(chain_todo[0])
        s_mat = advance(chain_todo.pop(0), s_mat)
    for ci in emit_todo:
        emit(ci)
    s_ref[0, 0] = s_mat


def _gdn_prompt(proj3, ab3, conv_w, gate_par, gdn_norm):
    bsz, t_len, _ = proj3.shape
    nh = GDN_HEADS
    blk = lambda off: pl.BlockSpec((1, t_len, LANES), lambda b, h: (b, 0, off + h))
    cw = lambda off: pl.BlockSpec((CONV_W, LANES), lambda b, h: (0, off + h))
    c = GDN_CHUNK
    n_c = t_len // c
    gate_scr = pltpu.VMEM((t_len, LANES), F32)
    group = 8
    work = [pltpu.VMEM((c + extra, wide * LANES), dt)
            for _ in range(n_c) for _, wide, dt, extra in _GDN_WORK]
    return pl.pallas_call(
        functools.partial(_gdn_prompt_kernel, group=group),
        grid=(bsz, nh),
        in_specs=[blk(0), blk(nh), blk(2 * nh), blk(3 * nh),
                  pl.BlockSpec((1, t_len, LANES), lambda b, h: (b, 0, 0)),
                  cw(0), cw(nh), cw(2 * nh),
                  pl.BlockSpec((SUBLANES, LANES), lambda b, h: (0, 0)),
                  pl.BlockSpec((1, LANES), lambda b, h: (0, 0))],
        out_specs=[pl.BlockSpec((1, t_len, LANES), lambda b, h: (b, 0, h)),
                   pl.BlockSpec((1, 1, GDN_DK, GDN_DV), lambda b, h: (b, h, 0, 0))],
        out_shape=[jax.ShapeDtypeStruct((bsz, t_len, MIX_A), BF16),
                   jax.ShapeDtypeStruct((bsz, nh, GDN_DK, GDN_DV), F32)],
        scratch_shapes=[gate_scr, gate_scr, gate_scr, gate_scr, gate_scr,
                        pltpu.VMEM((n_c, c, GDN_DV), F32), pltpu.VMEM((n_c, GDN_DK, GDN_DV), F32),
                        pltpu.VMEM((n_c, 2 * c, GDN_DK), BF16), pltpu.VMEM((n_c, c, GDN_DK + c), BF16),
                        pltpu.VMEM((n_c, GDN_DK + c, GDN_DV), BF16)] + work,
        compiler_params=_cparams(("parallel", "arbitrary")),
        name="gdn_prompt",
    )(proj3, proj3, proj3, proj3, ab3, conv_w, conv_w, conv_w, gate_par, gdn_norm.reshape(1, LANES))


def _gdn_sample_kernel(x_ref, ab_ref, hist_ref, cw_ref, gp_ref, gn_ref, s0_ref, o_ref, s_ref,
                       *, t_len):
    nh = GDN_HEADS
    g8 = SUBLANES
    n = nh * g8
    assert t_len + CONV_W - 1 <= g8
    cw = cw_ref[...]
    gp = gp_ref[...]
    gn = gn_ref[...]
    row8 = lax.broadcasted_iota(jnp.int32, (g8, QKV_W), 0)
    tpos = lax.broadcasted_iota(jnp.int32, (n, 1), 0) & (g8 - 1)
    rhead = lax.broadcasted_iota(jnp.int32, (n, GDN_DV), 0) >> 3
    ri = lax.broadcasted_iota(jnp.int32, (n, n), 0)
    ci = lax.broadcasted_iota(jnp.int32, (n, n), 1)
    tri = jnp.where((ri >> 3) == (ci >> 3), ri - ci, -1)
    eye = jnp.where(ri == ci, 1.0, 0.0)
    sel0 = jnp.where(lax.broadcasted_iota(jnp.int32, (n, LANES), 1) == 0, 1.0, 0.0).astype(BF16)
    valid = tpos < t_len
    a_neg = jnp.concatenate([jnp.broadcast_to(-jnp.exp(gp[0:1, h:h + 1]), (g8, 1)) for h in range(nh)], 0)
    dt_b = jnp.concatenate([jnp.broadcast_to(gp[1:2, h:h + 1], (g8, 1)) for h in range(nh)], 0)

    def l2n(x):
        return x * lax.rsqrt(jnp.sum(x * x, axis=-1, keepdims=True) + EPS)

    def operands(b):
        xb = x_ref[b]
        x = xb[:, :QKV_W].astype(F32)
        hst = hist_ref[b]
        y = x * cw[CONV_W - 1:CONV_W]
        for s in range(1, CONV_W):
            y = y + pltpu.roll(x, s, axis=0) * cw[CONV_W - 1 - s:CONV_W - s]
        for j in range(CONV_W - 1):
            hj = hst if j == 0 else pltpu.roll(hst, g8 - j, axis=0)
            y = y + jnp.where(row8 + j < CONV_W - 1, hj, 0.0) * cw[j:j + 1]
        qkv = jnp.where(row8 < t_len, _silu(y), 0.0)

        def stack(off):
            return jnp.concatenate([qkv[:, off + h * GDN_DK:off + (h + 1) * GDN_DK] for h in range(nh)], 0)

        q = l2n(stack(0)) * (GDN_DK ** -0.5)
        k = l2n(stack(QK_W))
        v = stack(2 * QK_W)
        ab = ab_ref[b]
        alpha = jnp.concatenate([ab[:, h:h + 1] for h in range(nh)], 0)
        braw = jnp.concatenate([ab[:, nh + h:nh + h + 1] for h in range(nh)], 0)
        g = jnp.where(valid, a_neg * _softplus(alpha + dt_b), 0.0)
        beta = jnp.where(valid, jax.nn.sigmoid(braw), 0.0)
        gc = g
        s = 1
        while s < g8:
            gc = gc + jnp.where(tpos >= s, pltpu.roll(gc, s, axis=0), 0.0)
            s *= 2
        g_last = jnp.concatenate([jnp.broadcast_to(gc[h * g8 + g8 - 1:(h + 1) * g8, :], (g8, 1))
                                  for h in range(nh)], 0)
        egc = jnp.exp(gc)
        kb = k * beta
        return dict(gc=gc, g_last=g_last, kbf=k.astype(BF16), kbb=kb.astype(BF16), qb=q.astype(BF16),
                    vb=(v * beta).astype(BF16), kg=(kb * egc).astype(BF16), qg=q * egc,
                    k_dec=(k * jnp.exp(g_last - gc)).astype(BF16))

    def gram(st):
        hi, mid, lo = _split3(jnp.broadcast_to(st["gc"], (n, LANES)))
        grow = _dot_nt(sel0, hi) + _dot_nt(sel0, mid) + _dot_nt(sel0, lo)
        d_incl = jnp.where(tri >= 0, jnp.exp(jnp.where(tri >= 0, st["gc"] - grow, 0.0)), 0.0)
        st["low"] = _dot_nt(st["kbb"], st["kbf"]) * jnp.where(tri > 0, d_incl, 0.0)
        st["a_in"] = (_dot_nt(st["qb"], st["kbf"]) * d_incl).astype(BF16)
        st["t_inv"] = eye - st["low"]
        st["p"] = st["low"].astype(BF16)

    def neumann(st):
        p = _dot(st["p"], st["p"])
        st["p"] = p.astype(BF16)
        st["t_inv"] = st["t_inv"] + _dot(st["t_inv"].astype(BF16), st["p"])

    def solve(st):
        tb = st["t_inv"].astype(BF16)
        st["u"] = _dot(tb, st["vb"])
        st["w"] = _dot(tb, st["kg"])

    def apply_state(st, b):
        ws, qs = [], []
        for h in range(nh):
            hs = slice(h * g8, (h + 1) * g8)
            lhs = jnp.concatenate([st["w"][hs], st["qg"][hs]], 0).astype(BF16)
            r = _dot(lhs, s0_ref[b, h].astype(BF16))
            ws.append(r[:g8])
            qs.append(r[g8:])
        st["v_new"] = st["u"] - jnp.concatenate(ws, 0)
        st["oq"] = jnp.concatenate(qs, 0)

    def finish(st, b):
        v_new = st["v_new"]
        o = st["oq"] + _dot(st["a_in"], v_new.astype(BF16))
        for h in range(nh):
            vm = jnp.where(rhead == h, v_new, 0.0).astype(BF16)
            dec = jnp.exp(st["g_last"][h * g8:h * g8 + 1, :])
            s_ref[b, h] = s0_ref[b, h] * dec + _dot_tn(st["k_dec"], vm)
        z = jnp.concatenate([x_ref[b][:, 3 * QK_W + h * GDN_DV:3 * QK_W + (h + 1) * GDN_DV].astype(F32)
                             for h in range(nh)], 0)
        o_ref[b] = (_rms(o, gn) * _silu(z)).astype(o_ref.dtype)

    bs_ = range(x_ref.shape[0])
    sts = [operands(b) for b in bs_]
    for st in sts:
        gram(st)
    covered = 2
    while covered < t_len:
        for st in sts:
            neumann(st)
        covered *= 2
    for st in sts:
        solve(st)
    for b in bs_:
        apply_state(sts[b], b)
    for b in bs_:
        finish(sts[b], b)


def _gdn_sample(proj3, ab3, hist, conv_w, gate_par, gdn_norm, s0, *, bb):
    bsz, t_len, _ = proj3.shape
    nh, g8 = GDN_HEADS, SUBLANES
    pad_t = lambda a, rows: jnp.pad(a, ((0, 0), (0, g8 - rows), (0, 0)))
    x8 = pad_t(proj3[:, :, :4 * QK_W], t_len)
    ab8 = pad_t(ab3, t_len)
    hist8 = pad_t(hist, CONV_W - 1)
    o, s_new = pl.pallas_call(
        functools.partial(_gdn_sample_kernel, t_len=t_len),
        grid=(bsz // bb,),
        in_specs=[pl.BlockSpec((bb, g8, 4 * QK_W), lambda i: (i, 0, 0)),
                  pl.BlockSpec((bb, g8, LANES), lambda i: (i, 0, 0)),
                  pl.BlockSpec((bb, g8, QKV_W), lambda i: (i, 0, 0)),
                  pl.BlockSpec((CONV_W, QKV_W), lambda i: (0, 0)),
                  pl.BlockSpec((SUBLANES, LANES), lambda i: (0, 0)),
                  pl.BlockSpec((1, LANES), lambda i: (0, 0)),
                  pl.BlockSpec((bb, nh, GDN_DK, GDN_DV), lambda i: (i, 0, 0, 0))],
        out_specs=[pl.BlockSpec((bb, nh * g8, GDN_DV), lambda i: (i, 0, 0)),
                   pl.BlockSpec((bb, nh, GDN_DK, GDN_DV), lambda i: (i, 0, 0, 0))],
        out_shape=[jax.ShapeDtypeStruct((bsz, nh * g8, GDN_DV), BF16),
                   jax.ShapeDtypeStruct((bsz, nh, GDN_DK, GDN_DV), F32)],
        compiler_params=_cparams(("parallel",)),
        name="gdn_sample",
    )(x8, ab8, hist8, conv_w, gate_par, gdn_norm.reshape(1, LANES), s0)
    o = o.reshape(bsz, nh, g8, GDN_DV)[:, :, :t_len].transpose(0, 2, 1, 3)
    return o.reshape(bsz, t_len, MIX_A), s_new


def _pool_finish(d_groups, pw_ref, ps_ref, zb):
    outs = [_dot(d.astype(BF16), pw_ref[gi]) for gi, d in enumerate(d_groups)]
    return jnp.concatenate(outs, axis=-1) * ps_ref[...] * _silu(zb)


def _pool_tile(u, halo, zb, pos, pw_ref, ps_ref):
    ext = jnp.concatenate([halo, u], axis=0)
    d_groups = []
    for gi, win in enumerate(POOL_WINDOWS):
        sl = slice(gi * POOL_CH, (gi + 1) * POOL_CH)
        lvl = ext[:, sl]
        s = 1
        while s < win:
            lvl = lvl + pltpu.roll(lvl, s, axis=0)
            s *= 2
        cnt = jnp.minimum(win, pos + 1).astype(F32)
        d_groups.append(lvl[POOL_HALO:, :] / cnt - u[:, sl])
    return _pool_finish(d_groups, pw_ref, ps_ref, zb)


def _mix_pool_kernel(x_ref, oa_ref, u_ref, halo_ref, zb_ref, pw_ref, ps_ref, wo_ref, nc_ref, wq_ref,
                     x1_ref, qx_ref, *, pos0):
    i = pl.program_id(1)
    tt = u_ref.shape[1]
    halo = jnp.where(i > 0, halo_ref[0].astype(F32), 0.0)
    pos = pos0 + i * tt + lax.broadcasted_iota(jnp.int32, (tt, 1), 0)
    ob = _pool_tile(u_ref[0].astype(F32), halo, zb_ref[0].astype(F32), pos, pw_ref, ps_ref)
    acc = _dot(oa_ref[0], wo_ref[0:MIX_A, :]) + _dot(ob.astype(BF16), wo_ref[MIX_A:D_MODEL, :])
    x1 = x_ref[0] + acc
    x1_ref[0] = x1
    qx_ref[0] = _dot(_rms(x1, nc_ref[...]).astype(BF16), wq_ref[...]).astype(qx_ref.dtype)


def _mix_pool(x3, oa3, proj3, pool_w, pool_scale, w_out, norm_cross, w_cq, *, tt, pos0):
    bsz, t_len, d = x3.shape
    ub, zb = COL_ZA // MIX_B + 1, COL_ZA // MIX_B + 2
    hb = tt // POOL_HALO
    const2 = lambda b, i: (0, 0)
    resident = dict(pipeline_mode=pl.Buffered(1))
    row = lambda w: pl.BlockSpec((1, tt, w), lambda b, i: (b, i, 0))
    return pl.pallas_call(
        functools.partial(_mix_pool_kernel, pos0=pos0),
        grid=(bsz, t_len // tt),
        in_specs=[row(d), row(MIX_A),
                  pl.BlockSpec((1, tt, MIX_B), lambda b, i: (b, i, ub)),
                  pl.BlockSpec((1, POOL_HALO, MIX_B), lambda b, i: (b, jnp.maximum(i * hb - 1, 0), ub)),
                  pl.BlockSpec((1, tt, MIX_B), lambda b, i: (b, i, zb)),
                  pl.BlockSpec((POOL_GROUPS, POOL_CH, POOL_CH), lambda b, i: (0, 0, 0), **resident),
                  pl.BlockSpec((1, MIX_B), const2, **resident),
                  pl.BlockSpec((d, d), const2, **resident),
                  pl.BlockSpec((1, d), const2, **resident),
                  pl.BlockSpec((d, d), const2, **resident)],
        out_specs=[row(d), row(d)],
        out_shape=[jax.ShapeDtypeStruct((bsz, t_len, d), F32), jax.ShapeDtypeStruct((bsz, t_len, d), BF16)],
        compiler_params=_cparams(("parallel", "arbitrary")),
        name="mix_pool",
    )(x3, oa3, proj3, proj3, proj3, pool_w, pool_scale.reshape(1, MIX_B), w_out,
      norm_cross.reshape(1, d), w_cq)


def _pool_sample_kernel(ext_ref, zb_ref, pw_ref, ps_ref, o_ref, *, pos0):
    t_len = zb_ref.shape[0]
    for t in range(t_len):
        e = POOL_BUF + t
        d_groups = []
        for gi, win in enumerate(POOL_WINDOWS):
            sl = slice(gi * POOL_CH, (gi + 1) * POOL_CH)
            tot = ext_ref[e, :, sl]
            for j in range(1, win):
                tot = tot + ext_ref[e - j, :, sl]
            cnt = float(min(win, pos0 + t + 1))
            d_groups.append(tot / cnt - ext_ref[e, :, sl])
        o_ref[t] = _pool_finish(d_groups, pw_ref, ps_ref, zb_ref[t].astype(F32)).astype(o_ref.dtype)


def _pool_sample(ext_tm, zb_tm, pool_w, pool_scale, *, bb, pos0):
    t_len, bsz, _ = zb_tm.shape
    return pl.pallas_call(
        functools.partial(_pool_sample_kernel, pos0=pos0),
        grid=(bsz // bb,),
        in_specs=[pl.BlockSpec((POOL_BUF + t_len, bb, MIX_B), lambda i: (0, i, 0)),
                  pl.BlockSpec((t_len, bb, MIX_B), lambda i: (0, i, 0)),
                  pl.BlockSpec((POOL_GROUPS, POOL_CH, POOL_CH), lambda i: (0, 0, 0)),
                  pl.BlockSpec((1, MIX_B), lambda i: (0, 0))],
        out_specs=pl.BlockSpec((t_len, bb, MIX_B), lambda i: (0, i, 0)),
        out_shape=jax.ShapeDtypeStruct((t_len, bsz, MIX_B), BF16),
        compiler_params=_cparams(("parallel",)),
        name="pool_sample",
    )(ext_tm, zb_tm, pool_w, pool_scale.reshape(1, MIX_B))


def _mix_out_kernel(x_ref, oa_ref, ob_ref, wo_ref, nc_ref, wq_ref, x1_ref, qx_ref):
    acc = _dot(oa_ref[...], wo_ref[0:MIX_A, :]) + _dot(ob_ref[...], wo_ref[MIX_A:D_MODEL, :])
    x1 = x_ref[...] + acc
    x1_ref[...] = x1
    qx_ref[...] = _dot(_rms(x1, nc_ref[...]).astype(BF16), wq_ref[...]).astype(qx_ref.dtype)


def _mix_out(x, oa, ob, w_out, norm_cross, w_cq, *, tm):
    n, d = x.shape
    const = lambda i: (0, 0)
    return pl.pallas_call(
        _mix_out_kernel,
        grid=(n // tm,),
        in_specs=[pl.BlockSpec((tm, d), lambda i: (i, 0)),
                  pl.BlockSpec((tm, MIX_A), lambda i: (i, 0)),
                  pl.BlockSpec((tm, MIX_B), lambda i: (i, 0)),
                  pl.BlockSpec((d, d), const), pl.BlockSpec((1, d), const), pl.BlockSpec((d, d), const)],
        out_specs=[pl.BlockSpec((tm, d), lambda i: (i, 0)), pl.BlockSpec((tm, d), lambda i: (i, 0))],
        out_shape=[jax.ShapeDtypeStruct((n, d), F32), jax.ShapeDtypeStruct((n, d), BF16)],
        compiler_params=_cparams(("parallel",)),
        name="mix_out",
    )(x, oa, ob, w_out, norm_cross.reshape(1, d), w_cq)


def _xattn_out_kernel(q_ref, k_ref, v_ref, x1_ref, wco_ref, nf_ref, y_ref):
    q = q_ref[0]
    scale = X_HEAD_DIM ** -0.5
    sls = [slice(h * X_HEAD_DIM, (h + 1) * X_HEAD_DIM) for h in range(X_HEADS)]
    ss = [_dot_nt(q[:, sl], k_ref[0, :, sl].astype(BF16)) * scale for sl in sls]
    ps = []
    for s in ss:
        p = jnp.exp(s - jnp.max(s, axis=-1, keepdims=True))
        ps.append((p / jnp.sum(p, axis=-1, keepdims=True)).astype(BF16))
    ctx = jnp.concatenate([_dot(p, v_ref[0, :, sl].astype(BF16)).astype(BF16)
                           for p, sl in zip(ps, sls)], axis=-1)
    y_ref[0] = _rms(x1_ref[0] + _dot(ctx, wco_ref[...]), nf_ref[...])


def _xattn_out(qx3, mk3, mv3, x13, w_co, norm_final, *, tq):
    bsz, t_len, d = qx3.shape
    n_mem = mk3.shape[1]
    row = pl.BlockSpec((1, tq, d), lambda b, i: (b, i, 0))
    mem = pl.BlockSpec((1, n_mem, d), lambda b, i: (b, 0, 0))
    return pl.pallas_call(
        _xattn_out_kernel,
        grid=(bsz, t_len // tq),
        in_specs=[row, mem, mem, row,
                  pl.BlockSpec((d, d), lambda b, i: (0, 0), pipeline_mode=pl.Buffered(1)),
                  pl.BlockSpec((1, d), lambda b, i: (0, 0), pipeline_mode=pl.Buffered(1))],
        out_specs=row,
        out_shape=jax.ShapeDtypeStruct((bsz, t_len, d), F32),
        compiler_params=_cparams(("parallel", "arbitrary")),
        name="xattn_out",
    )(qx3, mk3, mv3, x13, w_co, norm_final.reshape(1, d))


def _xattn_native_stages(q_ref, k_ref, v_ref, o_ref, t_len):
    nj = X_HEAD_DIM // LANES
    grp = nj * X_HEADS
    th = t_len * X_HEADS
    scale = X_HEAD_DIM ** -0.5
    ncol = k_ref.shape[1]
    r = lax.broadcasted_iota(jnp.int32, (th, ncol), 0) & (X_HEADS - 1)
    c = lax.broadcasted_iota(jnp.int32, (th, ncol), 1) & (grp - 1)

    def probs(z):
        s = None
        for j in range(nj):
            zj = jnp.where(c == r + j * X_HEADS, z[j * th:(j + 1) * th, :], 0.0)
            if j:
                zj = pltpu.roll(zj, ncol - j * X_HEADS, axis=1)
            s = zj if s is None else s + zj
        sm = jnp.where(c == r, s * scale, -jnp.inf)
        p = jnp.exp(sm - jnp.max(sm, axis=1, keepdims=True))
        p = p / jnp.sum(p, axis=1, keepdims=True)
        return jnp.concatenate([p if j == 0 else pltpu.roll(p, j * X_HEADS, axis=1)
                                for j in range(nj)], axis=0).astype(BF16)

    bs_ = range(q_ref.shape[0])
    vals = {}

    def scores():
        vals["z"] = [_dot_nt(q_ref[b], k_ref[b].astype(BF16)) for b in bs_]

    def probabilities():
        vals["p"] = [probs(z) for z in vals["z"]]

    def context():
        for b in bs_:
            o_ref[b] = _dot(vals["p"][b], v_ref[b].astype(BF16)).astype(o_ref.dtype)

    return [scores, probabilities, context]


def _kv_rows(cache):
    bsz, n_mem, nh, dh = cache.shape
    nj = dh // LANES
    return cache.reshape(bsz, n_mem, nh, nj, LANES).transpose(0, 1, 3, 2, 4).reshape(
        bsz, n_mem * nj * nh, LANES)


def _attn_out_kernel(x1_ref, ctx_ref, wco_ref, nf_ref, y_ref):
    x2 = x1_ref[...] + _dot(ctx_ref[...], wco_ref[...])
    y_ref[...] = _rms(x2, nf_ref[...])


def _attn_out(x1, ctx, w_co, norm_final, *, tm):
    n, d = x1.shape
    const = lambda i: (0, 0)
    return pl.pallas_call(
        _attn_out_kernel,
        grid=(n // tm,),
        in_specs=[pl.BlockSpec((tm, d), lambda i: (i, 0)), pl.BlockSpec((tm, d), lambda i: (i, 0)),
                  pl.BlockSpec((d, d), const, pipeline_mode=pl.Buffered(1)),
                  pl.BlockSpec((1, d), const, pipeline_mode=pl.Buffered(1))],
        out_specs=pl.BlockSpec((tm, d), lambda i: (i, 0)),
        out_shape=jax.ShapeDtypeStruct((n, d), F32),
        compiler_params=_cparams(("parallel",)),
        name="attn_out",
    )(x1, ctx, w_co, norm_final.reshape(1, d))


def kernel(x_prompt, x_sample, mem_prompt, cache_mem_k, cache_mem_v, state_delta, state_conv,
           state_pool, norm_mix, w_in, conv_w, a_log, dt_bias, gdn_norm, pool_w, pool_scale,
           w_out, norm_mem, norm_cross, w_cq, w_ck, w_cv, w_co, norm_final):
    bp, tp, d = x_prompt.shape
    bs, ts, _ = x_sample.shape
    n_mem = mem_prompt.shape[1]

    w_main, w_gate = _repack_w_in(w_in[0].T, tk=512, tn=1024)
    wo, wcq, wco = (w[0].astype(BF16) for w in (w_out, w_cq, w_co))
    pw = pool_w[0].astype(BF16)
    gate_par = jnp.zeros((SUBLANES, LANES), F32)
    gate_par = gate_par.at[0, :GDN_HEADS].set(a_log[0]).at[1, :GDN_HEADS].set(dt_bias[0])
    cw = conv_w[0]

    mem2d = mem_prompt.reshape(bp * n_mem, d)
    mk = _norm_proj(mem2d, norm_mem[0], w_ck[0], tm=bp * n_mem, tn=1024, out_dtype=F32)
    mv = _norm_proj(mem2d, norm_mem[0], w_cv[0], tm=bp * n_mem, tn=1024, out_dtype=F32)

    proj_s, ab_s = _norm_proj(x_sample.reshape(bs * ts, d), norm_mix[0], w_main, w_gate,
                              tm=bs * ts, tn=1024, out_dtype=BF16)
    proj_s3 = proj_s.reshape(bs, ts, W_MAIN)
    oa_s, delta_s = _gdn_sample(proj_s3, ab_s.reshape(bs, ts, LANES), state_conv[0], cw, gate_par,
                                gdn_norm[0], state_delta[0], bb=8)
    u_s = proj_s3[:, :, COL_ZA + MIX_A:COL_ZA + MIX_A + MIX_B].astype(F32)
    ext_s = jnp.concatenate([state_pool[0], u_s], axis=1)
    zb_tm = proj_s3[:, :, COL_ZA + MIX_A + MIX_B:].transpose(1, 0, 2)
    ob_s = _pool_sample(ext_s.transpose(1, 0, 2), zb_tm, pw, pool_scale[0], bb=32,
                        pos0=PAST_LEN).transpose(1, 0, 2)
    x1_s, qx_s = _mix_out(x_sample.reshape(bs * ts, d), oa_s.reshape(-1, MIX_A),
                          ob_s.reshape(-1, MIX_B), wo, norm_cross[0], wcq, tm=256)
    nj = X_HEAD_DIM // LANES
    q_rows = qx_s.reshape(bs, ts, X_HEADS, nj, LANES).transpose(0, 3, 1, 2, 4).reshape(
        bs, nj * ts * X_HEADS, LANES)

    proj_p, ab_p, ctx_rows = _norm_proj(
        x_prompt.reshape(bp * tp, d), norm_mix[0], w_main, w_gate, tm=1024, tn=768, out_dtype=BF16,
        attn=(q_rows, _kv_rows(cache_mem_k[0]), _kv_rows(cache_mem_v[0]), ts))
    proj_p3 = proj_p.reshape(bp, tp, W_MAIN)
    oa_p, delta_p = _gdn_prompt(proj_p3, ab_p.reshape(bp, tp, LANES), cw, gate_par, gdn_norm[0])
    x1_p, qx_p = _mix_pool(x_prompt, oa_p, proj_p3, pw, pool_scale[0], wo, norm_cross[0], wcq,
                           tt=512, pos0=0)
    y_p = _xattn_out(qx_p, mk.reshape(bp, n_mem, d), mv.reshape(bp, n_mem, d), x1_p, wco,
                     norm_final, tq=512)
    conv_p = proj_p3[:, tp - (CONV_W - 1):, :QKV_W].astype(F32)
    pool_p = proj_p3[:, tp - POOL_BUF:, COL_ZA + MIX_A:COL_ZA + MIX_A + MIX_B].astype(F32)

    ctx_s = ctx_rows.reshape(bs, nj, ts, X_HEADS, LANES).transpose(0, 2, 3, 1, 4).reshape(bs * ts, d)
    y_s = _attn_out(x1_s, ctx_s, wco, norm_final, tm=256).reshape(bs, ts, d)
    conv_s = jnp.concatenate([state_conv[0], proj_s3[:, :, :QKV_W].astype(F32)], axis=1)[:, ts:]
    pool_s = ext_s[:, ts:]

    hd = (X_HEADS, X_HEAD_DIM)
    return (y_p, y_s, mk.reshape(1, bp, n_mem, *hd), mv.reshape(1, bp, n_mem, *hd),
            delta_p[None], conv_p[None], pool_p[None], delta_s[None], conv_s[None], pool_s[None])
```

```python
import functools

import jax
import jax.numpy as jnp
from jax import lax
from jax.experimental import pallas as pl
from jax.experimental.pallas import tpu as pltpu

F32 = jnp.float32
BF16 = jnp.bfloat16

D_MODEL = 2048
MIX_A = D_MODEL // 2
MIX_B = D_MODEL - MIX_A
GDN_HEADS = 8
GDN_DK = MIX_A // GDN_HEADS
GDN_DV = MIX_A // GDN_HEADS
QK_W = GDN_HEADS * GDN_DK
QKV_W = 2 * QK_W + GDN_HEADS * GDN_DV
CONV_W = 4
POOL_WINDOWS = (2, 4, 8, 16)
POOL_GROUPS = len(POOL_WINDOWS)
POOL_CH = MIX_B // POOL_GROUPS
POOL_BUF = max(POOL_WINDOWS) - 1
X_HEADS = 4
X_HEAD_DIM = D_MODEL // X_HEADS
PAST_LEN = 16384
EPS = 1e-6
COL_ZA = QKV_W
COL_A = COL_ZA + GDN_HEADS * GDN_DV
COL_B = COL_A + GDN_HEADS
COL_U = COL_B + GDN_HEADS
COL_ZB = COL_U + MIX_B
IN_COLS = COL_ZB + MIX_B

W_MAIN = IN_COLS - 2 * GDN_HEADS
LANES = 128
SUBLANES = 8
BF16_ROWS = 16
GDN_CHUNK = 128
POOL_HALO = 16
VMEM_LIMIT = 56 * 1024 * 1024


def _cparams(sem):
    return pltpu.CompilerParams(dimension_semantics=sem, vmem_limit_bytes=VMEM_LIMIT)


def _dot(a, b):
    return jnp.dot(a, b, preferred_element_type=F32)


def _dot_nt(a, b):
    return lax.dot_general(a, b, (((1,), (1,)), ((), ())), preferred_element_type=F32)


def _dot_tn(a, b):
    return lax.dot_general(a, b, (((0,), (0,)), ((), ())), preferred_element_type=F32)


def _rms(x, g):
    return x * lax.rsqrt(jnp.mean(x * x, axis=-1, keepdims=True) + EPS) * g


def _silu(x):
    return x * jax.nn.sigmoid(x)


def _softplus(x):
    return jnp.maximum(x, 0.0) + jnp.log1p(jnp.exp(-jnp.abs(x)))


def _repack_kernel(wt_ref, gt_ref, main_ref, gate_ref):
    main_ref[...] = wt_ref[...].T.astype(main_ref.dtype)

    @pl.when(pl.program_id(1) == 0)
    def _():
        head = gt_ref[...].T
        lane = lax.broadcasted_iota(jnp.int32, head.shape, 1)
        gate_ref[...] = jnp.where(lane < COL_U - COL_A, head, 0.0).astype(gate_ref.dtype)


def _repack_w_in(wt, *, tk, tn):
    k_dim = wt.shape[1]
    n_gate = COL_U - COL_A
    src_row = lambda j: pl.multiple_of(j * tn + (j // (COL_A // tn)) * n_gate, n_gate)
    return pl.pallas_call(
        _repack_kernel,
        grid=(k_dim // tk, W_MAIN // tn),
        in_specs=[pl.BlockSpec((pl.Element(tn), pl.Element(tk)), lambda i, j: (src_row(j), i * tk)),
                  pl.BlockSpec((pl.Element(LANES), pl.Element(tk)), lambda i, j: (COL_A, i * tk))],
        out_specs=[pl.BlockSpec((tk, tn), lambda i, j: (i, j)),
                   pl.BlockSpec((tk, LANES), lambda i, j: (i, 0))],
        out_shape=[jax.ShapeDtypeStruct((k_dim, W_MAIN), BF16),
                   jax.ShapeDtypeStruct((k_dim, LANES), BF16)],
        compiler_params=_cparams(("parallel", "arbitrary")),
        name="repack_w_in",
    )(wt, wt)


def _norm_proj_kernel(x_ref, g_ref, w_ref, *rest, with_side, attn_t):
    rest = list(rest)
    ws_ref = rest.pop(0) if with_side else None
    xq_ref, xk_ref, xv_ref = (rest.pop(0), rest.pop(0), rest.pop(0)) if attn_t else (None,) * 3
    out_ref = rest.pop(0)
    side_ref = rest.pop(0) if with_side else None
    xo_ref = rest.pop(0) if attn_t else None
    h_scr = rest.pop(0)

    @pl.when(pl.program_id(1) == 0)
    def _():
        h = _rms(x_ref[...], g_ref[...]).astype(BF16)
        h_scr[...] = h
        if with_side:
            side_ref[...] = _dot(h, ws_ref[...])

    stages = _xattn_native_stages(xq_ref, xk_ref, xv_ref, xo_ref, attn_t) if attn_t else []
    for stage in stages[:1]:
        stage()
    out_ref[...] = _dot(h_scr[...], w_ref[...].astype(BF16)).astype(out_ref.dtype)
    for stage in stages[1:]:
        stage()


def _norm_proj(x, g, w, w_side=None, *, tm, tn, out_dtype, attn=None):
    n, d = x.shape
    ncol = w.shape[1]
    with_side = w_side is not None
    in_specs = [pl.BlockSpec((tm, d), lambda i, j: (i, 0)),
                pl.BlockSpec((1, d), lambda i, j: (0, 0)),
                pl.BlockSpec((d, tn), lambda i, j: (0, j))]
    out_specs = [pl.BlockSpec((tm, tn), lambda i, j: (i, j))]
    out_shape = [jax.ShapeDtypeStruct((n, ncol), out_dtype)]
    args = [x, g.reshape(1, d), w]
    if with_side:
        in_specs.append(pl.BlockSpec((d, LANES), lambda i, j: (0, 0)))
        out_specs.append(pl.BlockSpec((tm, LANES), lambda i, j: (i, 0)))
        out_shape.append(jax.ShapeDtypeStruct((n, LANES), F32))
        args.append(w_side)
    if attn is not None:
        xq, xk, xv, attn_t = attn
        n_j = ncol // tn
        steps = (n // tm) * n_j
        assert xq.shape[0] % steps == 0
        xbb = xq.shape[0] // steps
        share = lambda rows: pl.BlockSpec((xbb, rows, LANES), lambda i, j: (i * n_j + j, 0, 0))
        in_specs += [share(xq.shape[1]), share(xk.shape[1]), share(xv.shape[1])]
        out_specs.append(share(xq.shape[1]))
        out_shape.append(jax.ShapeDtypeStruct(xq.shape, BF16))
        args += [xq, xk, xv]
    res = pl.pallas_call(
        functools.partial(_norm_proj_kernel, with_side=with_side,
                          attn_t=attn[3] if attn is not None else 0),
        grid=(n // tm, ncol // tn),
        in_specs=in_specs, out_specs=out_specs, out_shape=out_shape,
        scratch_shapes=[pltpu.VMEM((tm, d), BF16)],
        compiler_params=_cparams(("parallel", "arbitrary")),
        name=("norm_proj_attn" if attn is not None else "norm_proj_side") if with_side else "norm_proj",
    )(*args)
    return res if len(res) > 1 else res[0]


INV_BASE_LOG2 = 3
GATE_LANES = 2 * GDN_HEADS
GATE_PACK = LANES // GATE_LANES
_GDN_WORK = (("k", 1, BF16, 0), ("kb", 1, BF16, 0), ("q", 1, BF16, 0), ("kd", 1, BF16, 0),
             ("vk", 2, BF16, 0), ("dinc", 1, F32, 0), ("low", 1, F32, 0), ("x", 1, F32, 0),
             ("p", 1, BF16, 0), ("wu", 2, BF16, 0),
             ("cq", 1, F32, SUBLANES), ("ck", 1, F32, SUBLANES), ("cv", 1, F32, SUBLANES))


def _split3(x):
    hi = x.astype(BF16)
    r1 = x - hi.astype(F32)
    mid = r1.astype(BF16)
    lo = (r1 - mid.astype(F32)).astype(BF16)
    return hi, mid, lo


def _gdn_prompt_kernel(q_ref, k_ref, v_ref, z_ref, ab_ref, cwq_ref, cwk_ref, cwv_ref, gp_ref,
                       gn_ref, o_ref, s_ref,
                       gc_s, gct_s, egc_s, ekd_s, beta_s, u_s, n_s, pw_s, qa_s, sv_s, *work_refs, group):
    nw = len(_GDN_WORK)
    work = [{spec[0]: ref for spec, ref in zip(_GDN_WORK, work_refs[j * nw:(j + 1) * nw])}
            for j in range(len(work_refs) // nw)]
    h = pl.program_id(1)
    t_len = q_ref.shape[1]
    c = GDN_CHUNK
    n_c = t_len // c
    row = lax.broadcasted_iota(jnp.int32, (c, c), 0)
    col = lax.broadcasted_iota(jnp.int32, (c, c), 1)

    @pl.when(h == 0)
    def _():
        gp = gp_ref[...]
        slot = col[0:1] >> (GATE_LANES.bit_length() - 1)
        par = gp
        for j in range(1, GATE_PACK):
            par = par + jnp.where(slot == j, pltpu.roll(gp, GATE_LANES * j, axis=1), 0.0)
        a_neg = -jnp.exp(par[0:1])
        dt_b = par[1:2]
        tri = jnp.where(row >= col, 1.0, 0.0).astype(BF16)
        for grp in range(n_c // GATE_PACK):
            ab = ab_ref[0, pl.ds(grp * GATE_PACK * c, c), :]
            for j in range(1, GATE_PACK):
                nxt = ab_ref[0, pl.ds((grp * GATE_PACK + j) * c, c), :]
                ab = jnp.where(slot == j, pltpu.roll(nxt, GATE_LANES * j, axis=1), ab)
            hi, mid, lo = _split3(a_neg * _softplus(ab + dt_b))
            gc = _dot(tri, hi) + _dot(tri, mid) + _dot(tri, lo)
            rows = pl.ds(grp * c, c)
            gc_s[rows, :] = gc
            gct_s[rows, :] = gc.T
            egc_s[rows, :] = jnp.exp(gc)
            ekd_s[rows, :] = jnp.exp(gc[c - 1:c, :] - gc)
            beta_s[rows, :] = jax.nn.sigmoid(ab)

    def l2n(x):
        return x * lax.rsqrt(jnp.sum(x * x, axis=-1, keepdims=True) + EPS)

    def operands(w, ci):
        r0 = ci * c
        rows = pl.ds(r0, c)

        def conv_silu(x_ref, cw_ref, stage):
            cur = x_ref[0, rows, :].astype(F32)
            if ci > 0:
                tail = x_ref[0, pl.ds(r0 - BF16_ROWS, BF16_ROWS), :].astype(F32)[BF16_ROWS - SUBLANES:]
            else:
                tail = jnp.zeros((SUBLANES, cur.shape[1]), F32)
            stage[0:SUBLANES, :] = tail
            stage[SUBLANES:SUBLANES + c, :] = cur
            cw = cw_ref[...]
            y = cur * cw[CONV_W - 1:CONV_W]
            for s in range(1, CONV_W):
                y = y + stage[SUBLANES - s:SUBLANES - s + c, :] * cw[CONV_W - 1 - s:CONV_W - s]
            return _silu(y)

        g_rows = pl.ds((ci // GATE_PACK) * c, c)
        g_lane = (ci % GATE_PACK) * GATE_LANES + h

        def column(scr, idx):
            return jnp.sum(jnp.where(col == idx, scr[g_rows, :], 0.0), axis=-1, keepdims=True)

        q = l2n(conv_silu(q_ref, cwq_ref, w["cq"])) * (GDN_DK ** -0.5)
        k = l2n(conv_silu(k_ref, cwk_ref, w["ck"]))
        v = conv_silu(v_ref, cwv_ref, w["cv"])
        gcc = column(gc_s, g_lane)
        egc = column(egc_s, g_lane)
        ekd = column(ekd_s, g_lane)
        beta = column(beta_s, g_lane + GDN_HEADS)
        grow = gct_s[pl.ds((ci // GATE_PACK) * c + g_lane, 1), :]
        tri_i = row >= col
        w["dinc"][...] = jnp.where(tri_i, jnp.exp(jnp.where(tri_i, gcc - grow, 0.0)), 0.0)
        kb = k * beta
        w["k"][...] = k.astype(BF16)
        w["kb"][...] = kb.astype(BF16)
        w["q"][...] = q.astype(BF16)
        w["kd"][...] = (k * ekd).astype(BF16)
        w["vk"][:, :GDN_DV] = (v * beta).astype(BF16)
        w["vk"][:, GDN_DV:] = (kb * egc).astype(BF16)
        qa_s[ci, :, 0:GDN_DK] = (q * egc).astype(BF16)

    def gram(w, ci):
        r = _dot_nt(jnp.concatenate([w["kb"][...], w["q"][...]], axis=0), w["k"][...])
        d_incl = w["dinc"][...]
        low = r[:c] * jnp.where(row > col, d_incl, 0.0)
        qa_s[ci, :, GDN_DK:] = (r[c:] * d_incl).astype(BF16)
        w["low"][...] = low
        ld = jnp.where((row >> INV_BASE_LOG2) == (col >> INV_BASE_LOG2), low, 0.0)
        w["x"][...] = jnp.where(row == col, 1.0, 0.0) - ld
        w["p"][...] = ld.astype(BF16)

    def neumann(w, first, last):
        pb = w["p"][...]
        if first:
            w["p"][...] = _dot(pb, pb).astype(BF16)
        elif last:
            x = w["x"][...]
            w["x"][...] = x + _dot(x.astype(BF16), pb)
        else:
            x = w["x"][...]
            r = _dot(jnp.concatenate([x.astype(BF16), pb], axis=0), pb)
            w["x"][...] = x + r[:c]
            w["p"][...] = r[c:].astype(BF16)

    def merge_a(w, lg):
        pair = (row >> (lg + 1)) == (col >> (lg + 1))
        m = jnp.where(pair, (row >> lg) - (col >> lg), 0) > 0
        lm = jnp.where(m, w["low"][...], 0.0).astype(BF16)
        w["p"][...] = _dot(lm, w["x"][...].astype(BF16)).astype(BF16)

    def merge_b(w):
        x = w["x"][...]
        w["x"][...] = x - _dot(x.astype(BF16), w["p"][...])

    def solve(w, ci):
        uw = _dot(w["x"][...].astype(BF16), w["vk"][...])
        u_s[ci] = uw[:, :GDN_DV]
        uwb = uw.astype(BF16)
        w["wu"][...] = uwb
        pw_s[ci, c:2 * c, :] = uwb[:, GDN_DV:]

    def outer(w, ci):
        np_ = _dot_tn(w["kd"][...], w["wu"][...])
        n_s[ci] = np_[:, :GDN_DV]
        pw_s[ci, 0:c, :] = np_[:, GDN_DV:].astype(BF16)

    n_sq = INV_BASE_LOG2 - 1
    lgc = c.bit_length() - 1

    def prepare_stages(cis):
        over = lambda fn, *a: (lambda: [fn(work[ci], *[ci if x is cis else x for x in a])
                                        for ci in cis])
        stages = [over(operands, cis), over(gram, cis)]
        stages += [over(neumann, step == 0, step == n_sq) for step in range(n_sq + 1)]
        for lg in range(INV_BASE_LOG2, lgc):
            stages += [over(merge_a, lg), over(merge_b)]
        return stages + [over(solve, cis), over(outer, cis)]

    gn = gn_ref[...]
    lane1 = lax.broadcasted_iota(jnp.int32, (1, LANES), 1)

    def advance(ci, s_mat):
        sb = s_mat.astype(BF16)
        r = _dot(pw_s[ci], sb)
        g_last = jnp.sum(jnp.where(lane1 == (ci % GATE_PACK) * GATE_LANES + h,
                                   gc_s[pl.ds((ci // GATE_PACK) * c + c - 1, 1), :], 0.0),
                         axis=-1, keepdims=True)
        sv_s[ci, 0:c, :] = sb
        sv_s[ci, c:2 * c, :] = (u_s[ci] - r[c:2 * c]).astype(BF16)
        return s_mat * jnp.exp(g_last) - r[0:c] + n_s[ci]

    def emit(ci):
        rows = pl.ds(ci * c, c)
        o = _dot(qa_s[ci], sv_s[ci])
        zz = z_ref[0, rows, :].astype(F32)
        o_ref[0, rows, :] = (_rms(o, gn) * _silu(zz)).astype(o_ref.dtype)

    s_mat = jnp.zeros((GDN_DK, GDN_DV), F32)
    chain_todo, emit_todo = [], []
    for g in range(n_c // group):
        cis = list(range(g * group, (g + 1) * group))
        for stage in prepare_stages(cis):
            stage()
            if chain_todo:
                emit_todo.append(chain_todo[0])
                s_mat = advance(chain_todo.pop(0), s_mat)
        chain_todo += cis
    while chain_todo:
        if emit_todo:
            emit(emit_todo.pop(0))
        emit_todo.append(chain_todo[0])
        s_mat = advance(chain_todo.pop(0), s_mat)
    for ci in emit_todo:
        emit(ci)
    s_ref[0, 0] = s_mat


def _gdn_prompt(proj3, ab3, conv_w, gate_par, gdn_norm):
    bsz, t_len, _ = proj3.shape
    nh = GDN_HEADS
    blk = lambda off: pl.BlockSpec((1, t_len, LANES), lambda b, h: (b, 0, off + h))
    cw = lambda off: pl.BlockSpec((CONV_W, LANES), lambda b, h: (0, off + h))
    c = GDN_CHUNK
    n_c = t_len // c
    assert n_c % GATE_PACK == 0
    gate_scr = pltpu.VMEM((n_c // GATE_PACK * c, LANES), F32)
    group = 8
    work = [pltpu.VMEM((c + extra, wide * LANES), dt)
            for _ in range(n_c) for _, wide, dt, extra in _GDN_WORK]
    return pl.pallas_call(
        functools.partial(_gdn_prompt_kernel, group=group),
        grid=(bsz, nh),
        in_specs=[blk(0), blk(nh), blk(2 * nh), blk(3 * nh),
                  pl.BlockSpec((1, t_len, LANES), lambda b, h: (b, 0, 0)),
                  cw(0), cw(nh), cw(2 * nh),
                  pl.BlockSpec((SUBLANES, LANES), lambda b, h: (0, 0)),
                  pl.BlockSpec((1, LANES), lambda b, h: (0, 0))],
        out_specs=[pl.BlockSpec((1, t_len, LANES), lambda b, h: (b, 0, h)),
                   pl.BlockSpec((1, 1, GDN_DK, GDN_DV), lambda b, h: (b, h, 0, 0))],
        out_shape=[jax.ShapeDtypeStruct((bsz, t_len, MIX_A), BF16),
                   jax.ShapeDtypeStruct((bsz, nh, GDN_DK, GDN_DV), F32)],
        scratch_shapes=[gate_scr, gate_scr, gate_scr, gate_scr, gate_scr,
                        pltpu.VMEM((n_c, c, GDN_DV), F32), pltpu.VMEM((n_c, GDN_DK, GDN_DV), F32),
                        pltpu.VMEM((n_c, 2 * c, GDN_DK), BF16), pltpu.VMEM((n_c, c, GDN_DK + c), BF16),
                        pltpu.VMEM((n_c, GDN_DK + c, GDN_DV), BF16)] + work,
        compiler_params=_cparams(("parallel", "arbitrary")),
        name="gdn_prompt",
    )(proj3, proj3, proj3, proj3, ab3, conv_w, conv_w, conv_w, gate_par, gdn_norm.reshape(1, LANES))


def _gdn_sample_kernel(x_ref, ab_ref, hist_ref, cw_ref, gp_ref, gn_ref, s0_ref, o_ref, s_ref,
                       *, t_len):
    nh = GDN_HEADS
    g8 = SUBLANES
    n = nh * g8
    assert t_len + CONV_W - 1 <= g8
    cw = cw_ref[...]
    gp = gp_ref[...]
    gn = gn_ref[...]
    row8 = lax.broadcasted_iota(jnp.int32, (g8, QKV_W), 0)
    tpos = lax.broadcasted_iota(jnp.int32, (n, 1), 0) & (g8 - 1)
    rhead = lax.broadcasted_iota(jnp.int32, (n, GDN_DV), 0) >> 3
    ri = lax.broadcasted_iota(jnp.int32, (n, n), 0)
    ci = lax.broadcasted_iota(jnp.int32, (n, n), 1)
    tri = jnp.where((ri >> 3) == (ci >> 3), ri - ci, -1)
    eye = jnp.where(ri == ci, 1.0, 0.0)
    sel0 = jnp.where(lax.broadcasted_iota(jnp.int32, (n, LANES), 1) == 0, 1.0, 0.0).astype(BF16)
    valid = tpos < t_len
    a_neg = jnp.concatenate([jnp.broadcast_to(-jnp.exp(gp[0:1, h:h + 1]), (g8, 1)) for h in range(nh)], 0)
    dt_b = jnp.concatenate([jnp.broadcast_to(gp[1:2, h:h + 1], (g8, 1)) for h in range(nh)], 0)

    def l2n(x):
        return x * lax.rsqrt(jnp.sum(x * x, axis=-1, keepdims=True) + EPS)

    def operands(b):
        xb = x_ref[b]
        x = xb[:, :QKV_W].astype(F32)
        hst = hist_ref[b]
        y = x * cw[CONV_W - 1:CONV_W]
        for s in range(1, CONV_W):
            y = y + pltpu.roll(x, s, axis=0) * cw[CONV_W - 1 - s:CONV_W - s]
        for j in range(CONV_W - 1):
            hj = hst if j == 0 else pltpu.roll(hst, g8 - j, axis=0)
            y = y + jnp.where(row8 + j < CONV_W - 1, hj, 0.0) * cw[j:j + 1]
        qkv = jnp.where(row8 < t_len, _silu(y), 0.0)

        def stack(off):
            return jnp.concatenate([qkv[:, off + h * GDN_DK:off + (h + 1) * GDN_DK] for h in range(nh)], 0)

        q = l2n(stack(0)) * (GDN_DK ** -0.5)
        k = l2n(stack(QK_W))
        v = stack(2 * QK_W)
        ab = ab_ref[b]
        alpha = jnp.concatenate([ab[:, h:h + 1] for h in range(nh)], 0)
        braw = jnp.concatenate([ab[:, nh + h:nh + h + 1] for h in range(nh)], 0)
        g = jnp.where(valid, a_neg * _softplus(alpha + dt_b), 0.0)
        beta = jnp.where(valid, jax.nn.sigmoid(braw), 0.0)
        gc = g
        s = 1
        while s < g8:
            gc = gc + jnp.where(tpos >= s, pltpu.roll(gc, s, axis=0), 0.0)
            s *= 2
        g_last = jnp.concatenate([jnp.broadcast_to(gc[h * g8 + g8 - 1:(h + 1) * g8, :], (g8, 1))
                                  for h in range(nh)], 0)
        egc = jnp.exp(gc)
        kb = k * beta
        return dict(gc=gc, g_last=g_last, kbf=k.astype(BF16), kbb=kb.astype(BF16), qb=q.astype(BF16),
                    vb=(v * beta).astype(BF16), kg=(kb * egc).astype(BF16), qg=q * egc,
                    k_dec=(k * jnp.exp(g_last - gc)).astype(BF16))

    def gram(st):
        hi, mid, lo = _split3(jnp.broadcast_to(st["gc"], (n, LANES)))
        grow = _dot_nt(sel0, hi) + _dot_nt(sel0, mid) + _dot_nt(sel0, lo)
        d_incl = jnp.where(tri >= 0, jnp.exp(jnp.where(tri >= 0, st["gc"] - grow, 0.0)), 0.0)
        st["low"] = _dot_nt(st["kbb"], st["kbf"]) * jnp.where(tri > 0, d_incl, 0.0)
        st["a_in"] = (_dot_nt(st["qb"], st["kbf"]) * d_incl).astype(BF16)
        st["t_inv"] = eye - st["low"]
        st["p"] = st["low"].astype(BF16)

    def neumann(st):
        p = _dot(st["p"], st["p"])
        st["p"] = p.astype(BF16)
        st["t_inv"] = st["t_inv"] + _dot(st["t_inv"].astype(BF16), st["p"])

    def solve(st):
        tb = st["t_inv"].astype(BF16)
        st["u"] = _dot(tb, st["vb"])
        st["w"] = _dot(tb, st["kg"])

    def apply_state(st, b):
        ws, qs = [], []
        for h in range(nh):
            hs = slice(h * g8, (h + 1) * g8)
            lhs = jnp.concatenate([st["w"][hs], st["qg"][hs]], 0).astype(BF16)
            r = _dot(lhs, s0_ref[b, h].astype(BF16))
            ws.append(r[:g8])
            qs.append(r[g8:])
        st["v_new"] = st["u"] - jnp.concatenate(ws, 0)
        st["oq"] = jnp.concatenate(qs, 0)

    def finish(st, b):
        v_new = st["v_new"]
        o = st["oq"] + _dot(st["a_in"], v_new.astype(BF16))
        for h in range(nh):
            vm = jnp.where(rhead == h, v_new, 0.0).astype(BF16)
            dec = jnp.exp(st["g_last"][h * g8:h * g8 + 1, :])
            s_ref[b, h] = s0_ref[b, h] * dec + _dot_tn(st["k_dec"], vm)
        z = jnp.concatenate([x_ref[b][:, 3 * QK_W + h * GDN_DV:3 * QK_W + (h + 1) * GDN_DV].astype(F32)
                             for h in range(nh)], 0)
        o_ref[b] = (_rms(o, gn) * _silu(z)).astype(o_ref.dtype)

    bs_ = range(x_ref.shape[0])
    sts = [operands(b) for b in bs_]
    for st in sts:
        gram(st)
    covered = 2
    while covered < t_len:
        for st in sts:
            neumann(st)
        covered *= 2
    for st in sts:
        solve(st)
    for b in bs_:
        apply_state(sts[b], b)
    for b in bs_:
        finish(sts[b], b)


def _gdn_sample(proj3, ab3, hist, conv_w, gate_par, gdn_norm, s0, *, bb):
    bsz, t_len, _ = proj3.shape
    nh, g8 = GDN_HEADS, SUBLANES
    pad_t = lambda a, rows: jnp.pad(a, ((0, 0), (0, g8 - rows), (0, 0)))
    x8 = pad_t(proj3[:, :, :4 * QK_W], t_len)
    ab8 = pad_t(ab3, t_len)
    hist8 = pad_t(hist, CONV_W - 1)
    o, s_new = pl.pallas_call(
        functools.partial(_gdn_sample_kernel, t_len=t_len),
        grid=(bsz // bb,),
        in_specs=[pl.BlockSpec((bb, g8, 4 * QK_W), lambda i: (i, 0, 0)),
                  pl.BlockSpec((bb, g8, LANES), lambda i: (i, 0, 0)),
                  pl.BlockSpec((bb, g8, QKV_W), lambda i: (i, 0, 0)),
                  pl.BlockSpec((CONV_W, QKV_W), lambda i: (0, 0)),
                  pl.BlockSpec((SUBLANES, LANES), lambda i: (0, 0)),
                  pl.BlockSpec((1, LANES), lambda i: (0, 0)),
                  pl.BlockSpec((bb, nh, GDN_DK, GDN_DV), lambda i: (i, 0, 0, 0))],
        out_specs=[pl.BlockSpec((bb, nh * g8, GDN_DV), lambda i: (i, 0, 0)),
                   pl.BlockSpec((bb, nh, GDN_DK, GDN_DV), lambda i: (i, 0, 0, 0))],
        out_shape=[jax.ShapeDtypeStruct((bsz, nh * g8, GDN_DV), BF16),
                   jax.ShapeDtypeStruct((bsz, nh, GDN_DK, GDN_DV), F32)],
        compiler_params=_cparams(("parallel",)),
        name="gdn_sample",
    )(x8, ab8, hist8, conv_w, gate_par, gdn_norm.reshape(1, LANES), s0)
    o = o.reshape(bsz, nh, g8, GDN_DV)[:, :, :t_len].transpose(0, 2, 1, 3)
    return o.reshape(bsz, t_len, MIX_A), s_new


def _pool_finish(d_groups, pw_ref, ps_ref, zb):
    outs = [_dot(d.astype(BF16), pw_ref[gi]) for gi, d in enumerate(d_groups)]
    return jnp.concatenate(outs, axis=-1) * ps_ref[...] * _silu(zb)


def _pool_tile(u, halo, zb, pos, pw_ref, ps_ref):
    ext = jnp.concatenate([halo, u], axis=0)
    d_groups = []
    for gi, win in enumerate(POOL_WINDOWS):
        sl = slice(gi * POOL_CH, (gi + 1) * POOL_CH)
        lvl = ext[:, sl]
        s = 1
        while s < win:
            lvl = lvl + pltpu.roll(lvl, s, axis=0)
            s *= 2
        cnt = jnp.minimum(win, pos + 1).astype(F32)
        d_groups.append(lvl[POOL_HALO:, :] / cnt - u[:, sl])
    return _pool_finish(d_groups, pw_ref, ps_ref, zb)


def _mix_pool_kernel(x_ref, oa_ref, u_ref, halo_ref, zb_ref, pw_ref, ps_ref, wo_ref, nc_ref, wq_ref,
                     x1_ref, qx_ref, *, pos0):
    i = pl.program_id(1)
    tt = u_ref.shape[1]
    halo = jnp.where(i > 0, halo_ref[0].astype(F32), 0.0)
    pos = pos0 + i * tt + lax.broadcasted_iota(jnp.int32, (tt, 1), 0)
    ob = _pool_tile(u_ref[0].astype(F32), halo, zb_ref[0].astype(F32), pos, pw_ref, ps_ref)
    acc = _dot(oa_ref[0], wo_ref[0:MIX_A, :]) + _dot(ob.astype(BF16), wo_ref[MIX_A:D_MODEL, :])
    x1 = x_ref[0] + acc
    x1_ref[0] = x1
    qx_ref[0] = _dot(_rms(x1, nc_ref[...]).astype(BF16), wq_ref[...]).astype(qx_ref.dtype)


def _mix_pool(x3, oa3, proj3, pool_w, pool_scale, w_out, norm_cross, w_cq, *, tt, pos0):
    bsz, t_len, d = x3.shape
    ub, zb = COL_ZA // MIX_B + 1, COL_ZA // MIX_B + 2
    hb = tt // POOL_HALO
    const2 = lambda b, i: (0, 0)
    resident = dict(pipeline_mode=pl.Buffered(1))
    row = lambda w: pl.BlockSpec((1, tt, w), lambda b, i: (b, i, 0))
    return pl.pallas_call(
        functools.partial(_mix_pool_kernel, pos0=pos0),
        grid=(bsz, t_len // tt),
        in_specs=[row(d), row(MIX_A),
                  pl.BlockSpec((1, tt, MIX_B), lambda b, i: (b, i, ub)),
                  pl.BlockSpec((1, POOL_HALO, MIX_B), lambda b, i: (b, jnp.maximum(i * hb - 1, 0), ub)),
                  pl.BlockSpec((1, tt, MIX_B), lambda b, i: (b, i, zb)),
                  pl.BlockSpec((POOL_GROUPS, POOL_CH, POOL_CH), lambda b, i: (0, 0, 0), **resident),
                  pl.BlockSpec((1, MIX_B), const2, **resident),
                  pl.BlockSpec((d, d), const2, **resident),
                  pl.BlockSpec((1, d), const2, **resident),
                  pl.BlockSpec((d, d), const2, **resident)],
        out_specs=[row(d), row(d)],
        out_shape=[jax.ShapeDtypeStruct((bsz, t_len, d), F32), jax.ShapeDtypeStruct((bsz, t_len, d), BF16)],
        compiler_params=_cparams(("parallel", "arbitrary")),
        name="mix_pool",
    )(x3, oa3, proj3, proj3, proj3, pool_w, pool_scale.reshape(1, MIX_B), w_out,
      norm_cross.reshape(1, d), w_cq)


def _pool_sample_kernel(ext_ref, zb_ref, pw_ref, ps_ref, o_ref, *, pos0):
    t_len = zb_ref.shape[0]
    for t in range(t_len):
        e = POOL_BUF + t
        d_groups = []
        for gi, win in enumerate(POOL_WINDOWS):
            sl = slice(gi * POOL_CH, (gi + 1) * POOL_CH)
            tot = ext_ref[e, :, sl]
            for j in range(1, win):
                tot = tot + ext_ref[e - j, :, sl]
            cnt = float(min(win, pos0 + t + 1))
            d_groups.append(tot / cnt - ext_ref[e, :, sl])
        o_ref[t] = _pool_finish(d_groups, pw_ref, ps_ref, zb_ref[t].astype(F32)).astype(o_ref.dtype)


def _pool_sample(ext_tm, zb_tm, pool_w, pool_scale, *, bb, pos0):
    t_len, bsz, _ = zb_tm.shape
    return pl.pallas_call(
        functools.partial(_pool_sample_kernel, pos0=pos0),
        grid=(bsz // bb,),
        in_specs=[pl.BlockSpec((POOL_BUF + t_len, bb, MIX_B), lambda i: (0, i, 0)),
                  pl.BlockSpec((t_len, bb, MIX_B), lambda i: (0, i, 0)),
                  pl.BlockSpec((POOL_GROUPS, POOL_CH, POOL_CH), lambda i: (0, 0, 0)),
                  pl.BlockSpec((1, MIX_B), lambda i: (0, 0))],
        out_specs=pl.BlockSpec((t_len, bb, MIX_B), lambda i: (0, i, 0)),
        out_shape=jax.ShapeDtypeStruct((t_len, bsz, MIX_B), BF16),
        compiler_params=_cparams(("parallel",)),
        name="pool_sample",
    )(ext_tm, zb_tm, pool_w, pool_scale.reshape(1, MIX_B))


def _mix_out_kernel(x_ref, oa_ref, ob_ref, wo_ref, nc_ref, wq_ref, x1_ref, qx_ref):
    acc = _dot(oa_ref[...], wo_ref[0:MIX_A, :]) + _dot(ob_ref[...], wo_ref[MIX_A:D_MODEL, :])
    x1 = x_ref[...] + acc
    x1_ref[...] = x1
    qx_ref[...] = _dot(_rms(x1, nc_ref[...]).astype(BF16), wq_ref[...]).astype(qx_ref.dtype)


def _mix_out(x, oa, ob, w_out, norm_cross, w_cq, *, tm):
    n, d = x.shape
    const = lambda i: (0, 0)
    return pl.pallas_call(
        _mix_out_kernel,
        grid=(n // tm,),
        in_specs=[pl.BlockSpec((tm, d), lambda i: (i, 0)),
                  pl.BlockSpec((tm, MIX_A), lambda i: (i, 0)),
                  pl.BlockSpec((tm, MIX_B), lambda i: (i, 0)),
                  pl.BlockSpec((d, d), const), pl.BlockSpec((1, d), const), pl.BlockSpec((d, d), const)],
        out_specs=[pl.BlockSpec((tm, d), lambda i: (i, 0)), pl.BlockSpec((tm, d), lambda i: (i, 0))],
        out_shape=[jax.ShapeDtypeStruct((n, d), F32), jax.ShapeDtypeStruct((n, d), BF16)],
        compiler_params=_cparams(("parallel",)),
        name="mix_out",
    )(x, oa, ob, w_out, norm_cross.reshape(1, d), w_cq)


def _xattn_out_kernel(q_ref, k_ref, v_ref, x1_ref, wco_ref, nf_ref, y_ref):
    q = q_ref[0]
    scale = X_HEAD_DIM ** -0.5
    sls = [slice(h * X_HEAD_DIM, (h + 1) * X_HEAD_DIM) for h in range(X_HEADS)]
    ss = [_dot_nt(q[:, sl], k_ref[0, :, sl].astype(BF16)) * scale for sl in sls]
    ps = []
    for s in ss:
        p = jnp.exp(s - jnp.max(s, axis=-1, keepdims=True))
        ps.append((p / jnp.sum(p, axis=-1, keepdims=True)).astype(BF16))
    ctx = jnp.concatenate([_dot(p, v_ref[0, :, sl].astype(BF16)).astype(BF16)
                           for p, sl in zip(ps, sls)], axis=-1)
    y_ref[0] = _rms(x1_ref[0] + _dot(ctx, wco_ref[...]), nf_ref[...])


def _xattn_out(qx3, mk3, mv3, x13, w_co, norm_final, *, tq):
    bsz, t_len, d = qx3.shape
    n_mem = mk3.shape[1]
    row = pl.BlockSpec((1, tq, d), lambda b, i: (b, i, 0))
    mem = pl.BlockSpec((1, n_mem, d), lambda b, i: (b, 0, 0))
    return pl.pallas_call(
        _xattn_out_kernel,
        grid=(bsz, t_len // tq),
        in_specs=[row, mem, mem, row,
                  pl.BlockSpec((d, d), lambda b, i: (0, 0), pipeline_mode=pl.Buffered(1)),
                  pl.BlockSpec((1, d), lambda b, i: (0, 0), pipeline_mode=pl.Buffered(1))],
        out_specs=row,
        out_shape=jax.ShapeDtypeStruct((bsz, t_len, d), F32),
        compiler_params=_cparams(("parallel", "arbitrary")),
        name="xattn_out",
    )(qx3, mk3, mv3, x13, w_co, norm_final.reshape(1, d))


def _xattn_native_stages(q_ref, k_ref, v_ref, o_ref, t_len):
    nj = X_HEAD_DIM // LANES
    grp = nj * X_HEADS
    th = t_len * X_HEADS
    scale = X_HEAD_DIM ** -0.5
    ncol = k_ref.shape[1]
    r = lax.broadcasted_iota(jnp.int32, (th, ncol), 0) & (X_HEADS - 1)
    c = lax.broadcasted_iota(jnp.int32, (th, ncol), 1) & (grp - 1)

    def probs(z):
        s = None
        for j in range(nj):
            zj = jnp.where(c == r + j * X_HEADS, z[j * th:(j + 1) * th, :], 0.0)
            if j:
                zj = pltpu.roll(zj, ncol - j * X_HEADS, axis=1)
            s = zj if s is None else s + zj
        sm = jnp.where(c == r, s * scale, -jnp.inf)
        p = jnp.exp(sm - jnp.max(sm, axis=1, keepdims=True))
        p = p / jnp.sum(p, axis=1, keepdims=True)
        return jnp.concatenate([p if j == 0 else pltpu.roll(p, j * X_HEADS, axis=1)
                                for j in range(nj)], axis=0).astype(BF16)

    bs_ = range(q_ref.shape[0])
    vals = {}

    def scores():
        vals["z"] = [_dot_nt(q_ref[b], k_ref[b].astype(BF16)) for b in bs_]

    def probabilities():
        vals["p"] = [probs(z) for z in vals["z"]]

    def context():
        for b in bs_:
            o_ref[b] = _dot(vals["p"][b], v_ref[b].astype(BF16)).astype(o_ref.dtype)

    return [scores, probabilities, context]


def _kv_rows(cache):
    bsz, n_mem, nh, dh = cache.shape
    nj = dh // LANES
    return cache.reshape(bsz, n_mem, nh, nj, LANES).transpose(0, 1, 3, 2, 4).reshape(
        bsz, n_mem * nj * nh, LANES)


def _attn_out_kernel(x1_ref, ctx_ref, wco_ref, nf_ref, y_ref):
    x2 = x1_ref[...] + _dot(ctx_ref[...], wco_ref[...])
    y_ref[...] = _rms(x2, nf_ref[...])


def _attn_out(x1, ctx, w_co, norm_final, *, tm):
    n, d = x1.shape
    const = lambda i: (0, 0)
    return pl.pallas_call(
        _attn_out_kernel,
        grid=(n // tm,),
        in_specs=[pl.BlockSpec((tm, d), lambda i: (i, 0)), pl.BlockSpec((tm, d), lambda i: (i, 0)),
                  pl.BlockSpec((d, d), const, pipeline_mode=pl.Buffered(1)),
                  pl.BlockSpec((1, d), const, pipeline_mode=pl.Buffered(1))],
        out_specs=pl.BlockSpec((tm, d), lambda i: (i, 0)),
        out_shape=jax.ShapeDtypeStruct((n, d), F32),
        compiler_params=_cparams(("parallel",)),
        name="attn_out",
    )(x1, ctx, w_co, norm_final.reshape(1, d))


def kernel(x_prompt, x_sample, mem_prompt, cache_mem_k, cache_mem_v, state_delta, state_conv,
           state_pool, norm_mix, w_in, conv_w, a_log, dt_bias, gdn_norm, pool_w, pool_scale,
           w_out, norm_mem, norm_cross, w_cq, w_ck, w_cv, w_co, norm_final):
    bp, tp, d = x_prompt.shape
    bs, ts, _ = x_sample.shape
    n_mem = mem_prompt.shape[1]

    w_main, w_gate = _repack_w_in(w_in[0].T, tk=512, tn=1024)
    wo, wcq, wco = (w[0].astype(BF16) for w in (w_out, w_cq, w_co))
    pw = pool_w[0].astype(BF16)
    gate_par = jnp.zeros((SUBLANES, LANES), F32)
    gate_par = gate_par.at[0, :GDN_HEADS].set(a_log[0]).at[1, :GDN_HEADS].set(dt_bias[0])
    cw = conv_w[0]

    mem2d = mem_prompt.reshape(bp * n_mem, d)
    mk = _norm_proj(mem2d, norm_mem[0], w_ck[0], tm=bp * n_mem, tn=1024, out_dtype=F32)
    mv = _norm_proj(mem2d, norm_mem[0], w_cv[0], tm=bp * n_mem, tn=1024, out_dtype=F32)

    proj_s, ab_s = _norm_proj(x_sample.reshape(bs * ts, d), norm_mix[0], w_main, w_gate,
                              tm=bs * ts, tn=1024, out_dtype=BF16)
    proj_s3 = proj_s.reshape(bs, ts, W_MAIN)
    oa_s, delta_s = _gdn_sample(proj_s3, ab_s.reshape(bs, ts, LANES), state_conv[0], cw, gate_par,
                                gdn_norm[0], state_delta[0], bb=8)
    u_s = proj_s3[:, :, COL_ZA + MIX_A:COL_ZA + MIX_A + MIX_B].astype(F32)
    ext_s = jnp.concatenate([state_pool[0], u_s], axis=1)
    zb_tm = proj_s3[:, :, COL_ZA + MIX_A + MIX_B:].transpose(1, 0, 2)
    ob_s = _pool_sample(ext_s.transpose(1, 0, 2), zb_tm, pw, pool_scale[0], bb=32,
                        pos0=PAST_LEN).transpose(1, 0, 2)
    x1_s, qx_s = _mix_out(x_sample.reshape(bs * ts, d), oa_s.reshape(-1, MIX_A),
                          ob_s.reshape(-1, MIX_B), wo, norm_cross[0], wcq, tm=256)
    nj = X_HEAD_DIM // LANES
    q_rows = qx_s.reshape(bs, ts, X_HEADS, nj, LANES).transpose(0, 3, 1, 2, 4).reshape(
        bs, nj * ts * X_HEADS, LANES)

    proj_p, ab_p, ctx_rows = _norm_proj(
        x_prompt.reshape(bp * tp, d), norm_mix[0], w_main, w_gate, tm=1024, tn=768, out_dtype=BF16,
        attn=(q_rows, _kv_rows(cache_mem_k[0]), _kv_rows(cache_mem_v[0]), ts))
    proj_p3 = proj_p.reshape(bp, tp, W_MAIN)
    oa_p, delta_p = _gdn_prompt(proj_p3, ab_p.reshape(bp, tp, LANES), cw, gate_par, gdn_norm[0])
    x1_p, qx_p = _mix_pool(x_prompt, oa_p, proj_p3, pw, pool_scale[0], wo, norm_cross[0], wcq,
                           tt=512, pos0=0)
    y_p = _xattn_out(qx_p, mk.reshape(bp, n_mem, d), mv.reshape(bp, n_mem, d), x1_p, wco,
                     norm_final, tq=512)
    conv_p = proj_p3[:, tp - (CONV_W - 1):, :QKV_W].astype(F32)
    pool_p = proj_p3[:, tp - POOL_BUF:, COL_ZA + MIX_A:COL_ZA + MIX_A + MIX_B].astype(F32)

    ctx_s = ctx_rows.reshape(bs, nj, ts, X_HEADS, LANES).transpose(0, 2, 3, 1, 4).reshape(bs * ts, d)
    y_s = _attn_out(x1_s, ctx_s, wco, norm_final, tm=256).reshape(bs, ts, d)
    conv_s = jnp.concatenate([state_conv[0], proj_s3[:, :, :QKV_W].astype(F32)], axis=1)[:, ts:]
    pool_s = ext_s[:, ts:]

    hd = (X_HEADS, X_HEAD_DIM)
    return (y_p, y_s, mk.reshape(1, bp, n_mem, *hd), mv.reshape(1, bp, n_mem, *hd),
            delta_p[None], conv_p[None], pool_p[None], delta_s[None], conv_s[None], pool_s[None])
```

```python
import functools

import jax
import jax.numpy as jnp
from jax import lax
from jax.experimental import pallas as pl
from jax.experimental.pallas import tpu as pltpu

F32 = jnp.float32
BF16 = jnp.bfloat16

D_MODEL = 2048
MIX_A = D_MODEL // 2
MIX_B = D_MODEL - MIX_A
GDN_HEADS = 8
GDN_DK = MIX_A // GDN_HEADS
GDN_DV = MIX_A // GDN_HEADS
QK_W = GDN_HEADS * GDN_DK
QKV_W = 2 * QK_W + GDN_HEADS * GDN_DV
CONV_W = 4
POOL_WINDOWS = (2, 4, 8, 16)
POOL_GROUPS = len(POOL_WINDOWS)
POOL_CH = MIX_B // POOL_GROUPS
POOL_BUF = max(POOL_WINDOWS) - 1
X_HEADS = 4
X_HEAD_DIM = D_MODEL // X_HEADS
PAST_LEN = 16384
EPS = 1e-6
COL_ZA = QKV_W
COL_A = COL_ZA + GDN_HEADS * GDN_DV
COL_B = COL_A + GDN_HEADS
COL_U = COL_B + GDN_HEADS
COL_ZB = COL_U + MIX_B
IN_COLS = COL_ZB + MIX_B

W_MAIN = IN_COLS - 2 * GDN_HEADS
LANES = 128
SUBLANES = 8
BF16_ROWS = 16
GDN_CHUNK = 128
POOL_HALO = 16
VMEM_LIMIT = 56 * 1024 * 1024


def _cparams(sem):
    return pltpu.CompilerParams(dimension_semantics=sem, vmem_limit_bytes=VMEM_LIMIT)


def _dot(a, b):
    return jnp.dot(a, b, preferred_element_type=F32)


def _dot_nt(a, b):
    return lax.dot_general(a, b, (((1,), (1,)), ((), ())), preferred_element_type=F32)


def _dot_tn(a, b):
    return lax.dot_general(a, b, (((0,), (0,)), ((), ())), preferred_element_type=F32)


def _rms(x, g):
    return x * lax.rsqrt(jnp.mean(x * x, axis=-1, keepdims=True) + EPS) * g


def _silu(x):
    return x * jax.nn.sigmoid(x)


def _softplus(x):
    return jnp.maximum(x, 0.0) + jnp.log1p(jnp.exp(-jnp.abs(x)))


def _repack_kernel(wt_ref, gt_ref, main_ref, gate_ref):
    main_ref[...] = wt_ref[...].T.astype(main_ref.dtype)

    @pl.when(pl.program_id(1) == 0)
    def _():
        head = gt_ref[...].T
        lane = lax.broadcasted_iota(jnp.int32, head.shape, 1)
        gate_ref[...] = jnp.where(lane < COL_U - COL_A, head, 0.0).astype(gate_ref.dtype)


def _repack_w_in(wt, *, tk, tn):
    k_dim = wt.shape[1]
    n_gate = COL_U - COL_A
    src_row = lambda j: pl.multiple_of(j * tn + (j // (COL_A // tn)) * n_gate, n_gate)
    return pl.pallas_call(
        _repack_kernel,
        grid=(k_dim // tk, W_MAIN // tn),
        in_specs=[pl.BlockSpec((pl.Element(tn), pl.Element(tk)), lambda i, j: (src_row(j), i * tk)),
                  pl.BlockSpec((pl.Element(LANES), pl.Element(tk)), lambda i, j: (COL_A, i * tk))],
        out_specs=[pl.BlockSpec((tk, tn), lambda i, j: (i, j)),
                   pl.BlockSpec((tk, LANES), lambda i, j: (i, 0))],
        out_shape=[jax.ShapeDtypeStruct((k_dim, W_MAIN), BF16),
                   jax.ShapeDtypeStruct((k_dim, LANES), BF16)],
        compiler_params=_cparams(("parallel", "arbitrary")),
        name="repack_w_in",
    )(wt, wt)


def _norm_proj_kernel(x_ref, g_ref, w_ref, *rest, with_side, attn_t):
    rest = list(rest)
    ws_ref = rest.pop(0) if with_side else None
    xq_ref, xk_ref, xv_ref = (rest.pop(0), rest.pop(0), rest.pop(0)) if attn_t else (None,) * 3
    out_ref = rest.pop(0)
    side_ref = rest.pop(0) if with_side else None
    xo_ref = rest.pop(0) if attn_t else None
    h_scr = rest.pop(0)

    @pl.when(pl.program_id(1) == 0)
    def _():
        h = _rms(x_ref[...], g_ref[...]).astype(BF16)
        h_scr[...] = h
        if with_side:
            side_ref[...] = _dot(h, ws_ref[...])

    stages = _xattn_native_stages(xq_ref, xk_ref, xv_ref, xo_ref, attn_t) if attn_t else []
    for stage in stages[:1]:
        stage()
    out_ref[...] = _dot(h_scr[...], w_ref[...].astype(BF16)).astype(out_ref.dtype)
    for stage in stages[1:]:
        stage()


def _norm_proj(x, g, w, w_side=None, *, tm, tn, out_dtype, attn=None):
    n, d = x.shape
    ncol = w.shape[1]
    with_side = w_side is not None
    in_specs = [pl.BlockSpec((tm, d), lambda i, j: (i, 0)),
                pl.BlockSpec((1, d), lambda i, j: (0, 0)),
                pl.BlockSpec((d, tn), lambda i, j: (0, j))]
    out_specs = [pl.BlockSpec((tm, tn), lambda i, j: (i, j))]
    out_shape = [jax.ShapeDtypeStruct((n, ncol), out_dtype)]
    args = [x, g.reshape(1, d), w]
    if with_side:
        in_specs.append(pl.BlockSpec((d, LANES), lambda i, j: (0, 0)))
        out_specs.append(pl.BlockSpec((tm, LANES), lambda i, j: (i, 0)))
        out_shape.append(jax.ShapeDtypeStruct((n, LANES), F32))
        args.append(w_side)
    if attn is not None:
        xq, xk, xv, attn_t = attn
        n_j = ncol // tn
        steps = (n // tm) * n_j
        assert xq.shape[0] % steps == 0
        xbb = xq.shape[0] // steps
        share = lambda rows: pl.BlockSpec((xbb, rows, LANES), lambda i, j: (i * n_j + j, 0, 0))
        in_specs += [share(xq.shape[1]), share(xk.shape[1]), share(xv.shape[1])]
        out_specs.append(share(xq.shape[1]))
        out_shape.append(jax.ShapeDtypeStruct(xq.shape, BF16))
        args += [xq, xk, xv]
    res = pl.pallas_call(
        functools.partial(_norm_proj_kernel, with_side=with_side,
                          attn_t=attn[3] if attn is not None else 0),
        grid=(n // tm, ncol // tn),
        in_specs=in_specs, out_specs=out_specs, out_shape=out_shape,
        scratch_shapes=[pltpu.VMEM((tm, d), BF16)],
        compiler_params=_cparams(("parallel", "arbitrary")),
        name=("norm_proj_attn" if attn is not None else "norm_proj_side") if with_side else "norm_proj",
    )(*args)
    return res if len(res) > 1 else res[0]


INV_BASE_LOG2 = 3
GATE_LANES = 2 * GDN_HEADS
GATE_PACK = LANES // GATE_LANES
_GDN_WORK = (("k", 1, BF16, 0), ("kb", 1, BF16, 0), ("q", 1, BF16, 0), ("kd", 1, BF16, 0),
             ("vk", 2, BF16, 0), ("dinc", 1, F32, 0), ("low", 1, F32, 0), ("x", 1, F32, 0),
             ("p", 1, BF16, 0), ("wu", 2, BF16, 0),
             ("cq", 1, F32, SUBLANES), ("ck", 1, F32, SUBLANES), ("cv", 1, F32, SUBLANES))


def _split3(x):
    hi = x.astype(BF16)
    r1 = x - hi.astype(F32)
    mid = r1.astype(BF16)
    lo = (r1 - mid.astype(F32)).astype(BF16)
    return hi, mid, lo


def _gdn_prompt_kernel(q_ref, k_ref, v_ref, z_ref, ab_ref, cwq_ref, cwk_ref, cwv_ref, gp_ref,
                       gn_ref, o_ref, s_ref,
                       gc_s, gct_s, egc_s, ekd_s, beta_s, u_s, n_s, pw_s, qa_s, sv_s, *work_refs, group):
    nw = len(_GDN_WORK)
    work = [{spec[0]: ref for spec, ref in zip(_GDN_WORK, work_refs[j * nw:(j + 1) * nw])}
            for j in range(len(work_refs) // nw)]
    t_len = q_ref.shape[1]
    c = GDN_CHUNK
    n_c = t_len // c
    hps = q_ref.shape[2] // GDN_DK
    head_of = lambda u: pl.program_id(1) * hps + u // n_c
    lanes_of = lambda u: slice((u // n_c) * GDN_DK, (u // n_c + 1) * GDN_DK)
    row = lax.broadcasted_iota(jnp.int32, (c, c), 0)
    col = lax.broadcasted_iota(jnp.int32, (c, c), 1)

    @pl.when(pl.program_id(1) == 0)
    def _():
        gp = gp_ref[...]
        slot = col[0:1] >> (GATE_LANES.bit_length() - 1)
        par = gp
        for j in range(1, GATE_PACK):
            par = par + jnp.where(slot == j, pltpu.roll(gp, GATE_LANES * j, axis=1), 0.0)
        a_neg = -jnp.exp(par[0:1])
        dt_b = par[1:2]
        tri = jnp.where(row >= col, 1.0, 0.0).astype(BF16)
        for grp in range(n_c // GATE_PACK):
            ab = ab_ref[0, pl.ds(grp * GATE_PACK * c, c), :]
            for j in range(1, GATE_PACK):
                nxt = ab_ref[0, pl.ds((grp * GATE_PACK + j) * c, c), :]
                ab = jnp.where(slot == j, pltpu.roll(nxt, GATE_LANES * j, axis=1), ab)
            hi, mid, lo = _split3(a_neg * _softplus(ab + dt_b))
            gc = _dot(tri, hi) + _dot(tri, mid) + _dot(tri, lo)
            rows = pl.ds(grp * c, c)
            gc_s[rows, :] = gc
            gct_s[rows, :] = gc.T
            egc_s[rows, :] = jnp.exp(gc)
            ekd_s[rows, :] = jnp.exp(gc[c - 1:c, :] - gc)
            beta_s[rows, :] = jax.nn.sigmoid(ab)

    def l2n(x):
        return x * lax.rsqrt(jnp.sum(x * x, axis=-1, keepdims=True) + EPS)

    def operands(w, u):
        ci, hl, h = u % n_c, lanes_of(u), head_of(u)
        r0 = ci * c
        rows = pl.ds(r0, c)

        def conv_silu(x_ref, cw_ref, stage):
            cur = x_ref[0, rows, hl].astype(F32)
            if ci > 0:
                tail = x_ref[0, pl.ds(r0 - BF16_ROWS, BF16_ROWS), hl].astype(F32)[BF16_ROWS - SUBLANES:]
            else:
                tail = jnp.zeros((SUBLANES, cur.shape[1]), F32)
            stage[0:SUBLANES, :] = tail
            stage[SUBLANES:SUBLANES + c, :] = cur
            cw = cw_ref[:, hl]
            y = cur * cw[CONV_W - 1:CONV_W]
            for s in range(1, CONV_W):
                y = y + stage[SUBLANES - s:SUBLANES - s + c, :] * cw[CONV_W - 1 - s:CONV_W - s]
            return _silu(y)

        g_rows = pl.ds((ci // GATE_PACK) * c, c)
        g_lane = (ci % GATE_PACK) * GATE_LANES + h

        def column(scr, idx):
            return jnp.sum(jnp.where(col == idx, scr[g_rows, :], 0.0), axis=-1, keepdims=True)

        q = l2n(conv_silu(q_ref, cwq_ref, w["cq"])) * (GDN_DK ** -0.5)
        k = l2n(conv_silu(k_ref, cwk_ref, w["ck"]))
        v = conv_silu(v_ref, cwv_ref, w["cv"])
        gcc = column(gc_s, g_lane)
        egc = column(egc_s, g_lane)
        ekd = column(ekd_s, g_lane)
        beta = column(beta_s, g_lane + GDN_HEADS)
        grow = gct_s[pl.ds((ci // GATE_PACK) * c + g_lane, 1), :]
        tri_i = row >= col
        w["dinc"][...] = jnp.where(tri_i, jnp.exp(jnp.where(tri_i, gcc - grow, 0.0)), 0.0)
        kb = k * beta
        w["k"][...] = k.astype(BF16)
        w["kb"][...] = kb.astype(BF16)
        w["q"][...] = q.astype(BF16)
        w["kd"][...] = (k * ekd).astype(BF16)
        w["vk"][:, :GDN_DV] = (v * beta).astype(BF16)
        w["vk"][:, GDN_DV:] = (kb * egc).astype(BF16)
        qa_s[u, :, 0:GDN_DK] = (q * egc).astype(BF16)

    def gram(w, u):
        r = _dot_nt(jnp.concatenate([w["kb"][...], w["q"][...]], axis=0), w["k"][...])
        d_incl = w["dinc"][...]
        low = r[:c] * jnp.where(row > col, d_incl, 0.0)
        qa_s[u, :, GDN_DK:] = (r[c:] * d_incl).astype(BF16)
        w["low"][...] = low
        ld = jnp.where((row >> INV_BASE_LOG2) == (col >> INV_BASE_LOG2), low, 0.0)
        w["x"][...] = jnp.where(row == col, 1.0, 0.0) - ld
        w["p"][...] = ld.astype(BF16)

    def neumann(w, first, last):
        pb = w["p"][...]
        if first:
            w["p"][...] = _dot(pb, pb).astype(BF16)
        elif last:
            x = w["x"][...]
            w["x"][...] = x + _dot(x.astype(BF16), pb)
        else:
            x = w["x"][...]
            r = _dot(jnp.concatenate([x.astype(BF16), pb], axis=0), pb)
            w["x"][...] = x + r[:c]
            w["p"][...] = r[c:].astype(BF16)

    def merge_a(w, lg):
        pair = (row >> (lg + 1)) == (col >> (lg + 1))
        m = jnp.where(pair, (row >> lg) - (col >> lg), 0) > 0
        lm = jnp.where(m, w["low"][...], 0.0).astype(BF16)
        w["p"][...] = _dot(lm, w["x"][...].astype(BF16)).astype(BF16)

    def merge_b(w):
        x = w["x"][...]
        w["x"][...] = x - _dot(x.astype(BF16), w["p"][...])

    def solve(w, ci):
        uw = _dot(w["x"][...].astype(BF16), w["vk"][...])
        u_s[ci] = uw[:, :GDN_DV]
        uwb = uw.astype(BF16)
        w["wu"][...] = uwb
        pw_s[ci, c:2 * c, :] = uwb[:, GDN_DV:]

    def outer(w, ci):
        np_ = _dot_tn(w["kd"][...], w["wu"][...])
        n_s[ci] = np_[:, :GDN_DV]
        pw_s[ci, 0:c, :] = np_[:, GDN_DV:].astype(BF16)

    n_sq = INV_BASE_LOG2 - 1
    lgc = c.bit_length() - 1

    def prepare_stages(cis):
        over = lambda fn, *a: (lambda: [fn(work[ci], *[ci if x is cis else x for x in a])
                                        for ci in cis])
        stages = [over(operands, cis), over(gram, cis)]
        stages += [over(neumann, step == 0, step == n_sq) for step in range(n_sq + 1)]
        for lg in range(INV_BASE_LOG2, lgc):
            stages += [over(merge_a, lg), over(merge_b)]
        return stages + [over(solve, cis), over(outer, cis)]

    gn = gn_ref[...]
    lane1 = lax.broadcasted_iota(jnp.int32, (1, LANES), 1)

    def advance(u, s_mat):
        ci = u % n_c
        if ci == 0:
            s_mat = jnp.zeros((GDN_DK, GDN_DV), F32)
        sb = s_mat.astype(BF16)
        r = _dot(pw_s[u], sb)
        g_last = jnp.sum(jnp.where(lane1 == (ci % GATE_PACK) * GATE_LANES + head_of(u),
                                   gc_s[pl.ds((ci // GATE_PACK) * c + c - 1, 1), :], 0.0),
                         axis=-1, keepdims=True)
        sv_s[u, 0:c, :] = sb
        sv_s[u, c:2 * c, :] = (u_s[u] - r[c:2 * c]).astype(BF16)
        s_next = s_mat * jnp.exp(g_last) - r[0:c] + n_s[u]
        if ci == n_c - 1:
            s_ref[0, u // n_c] = s_next
        return s_next

    def emit(u):
        rows, hl = pl.ds((u % n_c) * c, c), lanes_of(u)
        o = _dot(qa_s[u], sv_s[u])
        zz = z_ref[0, rows, hl].astype(F32)
        o_ref[0, rows, hl] = (_rms(o, gn) * _silu(zz)).astype(o_ref.dtype)

    s_mat = None
    chain_todo, emit_todo = [], []
    for g in range(hps * n_c // group):
        us = list(range(g * group, (g + 1) * group))
        for stage in prepare_stages(us):
            stage()
            if chain_todo:
                emit_todo.append(chain_todo[0])
                s_mat = advance(chain_todo.pop(0), s_mat)
            elif emit_todo:
                emit(emit_todo.pop(0))
        chain_todo += us
    while chain_todo:
        if emit_todo:
            emit(emit_todo.pop(0))
        emit_todo.append(chain_todo[0])
        s_mat = advance(chain_todo.pop(0), s_mat)
    for u in emit_todo:
        emit(u)


def _gdn_prompt(proj3, ab3, conv_w, gate_par, gdn_norm):
    bsz, t_len, _ = proj3.shape
    nh = GDN_HEADS
    hps = 2
    hw = hps * GDN_DK
    blk = lambda off: pl.BlockSpec((1, t_len, hw), lambda b, p: (b, 0, off // hps + p))
    cw = lambda off: pl.BlockSpec((CONV_W, hw), lambda b, p: (0, off // hps + p))
    c = GDN_CHUNK
    n_c = t_len // c
    n_u = hps * n_c
    assert n_c % GATE_PACK == 0 and nh % hps == 0
    gate_scr = pltpu.VMEM((n_c // GATE_PACK * c, LANES), F32)
    group = 8
    work = [pltpu.VMEM((c + extra, wide * LANES), dt)
            for _ in range(n_u) for _, wide, dt, extra in _GDN_WORK]
    return pl.pallas_call(
        functools.partial(_gdn_prompt_kernel, group=group),
        grid=(bsz, nh // hps),
        in_specs=[blk(0), blk(nh), blk(2 * nh), blk(3 * nh),
                  pl.BlockSpec((1, t_len, LANES), lambda b, p: (b, 0, 0)),
                  cw(0), cw(nh), cw(2 * nh),
                  pl.BlockSpec((SUBLANES, LANES), lambda b, p: (0, 0)),
                  pl.BlockSpec((1, LANES), lambda b, p: (0, 0))],
        out_specs=[pl.BlockSpec((1, t_len, hw), lambda b, p: (b, 0, p)),
                   pl.BlockSpec((1, hps, GDN_DK, GDN_DV), lambda b, p: (b, p, 0, 0))],
        out_shape=[jax.ShapeDtypeStruct((bsz, t_len, MIX_A), BF16),
                   jax.ShapeDtypeStruct((bsz, nh, GDN_DK, GDN_DV), F32)],
        scratch_shapes=[gate_scr, gate_scr, gate_scr, gate_scr, gate_scr,
                        pltpu.VMEM((n_u, c, GDN_DV), F32), pltpu.VMEM((n_u, GDN_DK, GDN_DV), F32),
                        pltpu.VMEM((n_u, 2 * c, GDN_DK), BF16), pltpu.VMEM((n_u, c, GDN_DK + c), BF16),
                        pltpu.VMEM((n_u, GDN_DK + c, GDN_DV), BF16)] + work,
        compiler_params=_cparams(("parallel", "arbitrary")),
        name="gdn_prompt",
    )(proj3, proj3, proj3, proj3, ab3, conv_w, conv_w, conv_w, gate_par, gdn_norm.reshape(1, LANES))


def _gdn_sample_kernel(x_ref, ab_ref, hist_ref, cw_ref, gp_ref, gn_ref, s0_ref, o_ref, s_ref,
                       *, t_len):
    nh = GDN_HEADS
    g8 = SUBLANES
    n = nh * g8
    assert t_len + CONV_W - 1 <= g8
    cw = cw_ref[...]
    gp = gp_ref[...]
    gn = gn_ref[...]
    row8 = lax.broadcasted_iota(jnp.int32, (g8, QKV_W), 0)
    tpos = lax.broadcasted_iota(jnp.int32, (n, 1), 0) & (g8 - 1)
    rhead = lax.broadcasted_iota(jnp.int32, (n, GDN_DV), 0) >> 3
    ri = lax.broadcasted_iota(jnp.int32, (n, n), 0)
    ci = lax.broadcasted_iota(jnp.int32, (n, n), 1)
    tri = jnp.where((ri >> 3) == (ci >> 3), ri - ci, -1)
    eye = jnp.where(ri == ci, 1.0, 0.0)
    sel0 = jnp.where(lax.broadcasted_iota(jnp.int32, (n, LANES), 1) == 0, 1.0, 0.0).astype(BF16)
    valid = tpos < t_len
    a_neg = jnp.concatenate([jnp.broadcast_to(-jnp.exp(gp[0:1, h:h + 1]), (g8, 1)) for h in range(nh)], 0)
    dt_b = jnp.concatenate([jnp.broadcast_to(gp[1:2, h:h + 1], (g8, 1)) for h in range(nh)], 0)

    def l2n(x):
        return x * lax.rsqrt(jnp.sum(x * x, axis=-1, keepdims=True) + EPS)

    def operands(b):
        xb = x_ref[b]
        x = xb[:, :QKV_W].astype(F32)
        hst = hist_ref[b]
        y = x * cw[CONV_W - 1:CONV_W]
        for s in range(1, CONV_W):
            y = y + pltpu.roll(x, s, axis=0) * cw[CONV_W - 1 - s:CONV_W - s]
        for j in range(CONV_W - 1):
            hj = hst if j == 0 else pltpu.roll(hst, g8 - j, axis=0)
            y = y + jnp.where(row8 + j < CONV_W - 1, hj, 0.0) * cw[j:j + 1]
        qkv = jnp.where(row8 < t_len, _silu(y), 0.0)

        def stack(off):
            return jnp.concatenate([qkv[:, off + h * GDN_DK:off + (h + 1) * GDN_DK] for h in range(nh)], 0)

        q = l2n(stack(0)) * (GDN_DK ** -0.5)
        k = l2n(stack(QK_W))
        v = stack(2 * QK_W)
        ab = ab_ref[b]
        alpha = jnp.concatenate([ab[:, h:h + 1] for h in range(nh)], 0)
        braw = jnp.concatenate([ab[:, nh + h:nh + h + 1] for h in range(nh)], 0)
        g = jnp.where(valid, a_neg * _softplus(alpha + dt_b), 0.0)
        beta = jnp.where(valid, jax.nn.sigmoid(braw), 0.0)
        gc = g
        s = 1
        while s < g8:
            gc = gc + jnp.where(tpos >= s, pltpu.roll(gc, s, axis=0), 0.0)
            s *= 2
        g_last = jnp.concatenate([jnp.broadcast_to(gc[h * g8 + g8 - 1:(h + 1) * g8, :], (g8, 1))
                                  for h in range(nh)], 0)
        egc = jnp.exp(gc)
        kb = k * beta
        return dict(gc=gc, g_last=g_last, kbf=k.astype(BF16), kbb=kb.astype(BF16), qb=q.astype(BF16),
                    vb=(v * beta).astype(BF16), kg=(kb * egc).astype(BF16), qg=q * egc,
                    k_dec=(k * jnp.exp(g_last - gc)).astype(BF16))

    def gram(st):
        hi, mid, lo = _split3(jnp.broadcast_to(st["gc"], (n, LANES)))
        grow = _dot_nt(sel0, hi) + _dot_nt(sel0, mid) + _dot_nt(sel0, lo)
        d_incl = jnp.where(tri >= 0, jnp.exp(jnp.where(tri >= 0, st["gc"] - grow, 0.0)), 0.0)
        st["low"] = _dot_nt(st["kbb"], st["kbf"]) * jnp.where(tri > 0, d_incl, 0.0)
        st["a_in"] = (_dot_nt(st["qb"], st["kbf"]) * d_incl).astype(BF16)
        st["t_inv"] = eye - st["low"]
        st["p"] = st["low"].astype(BF16)

    def neumann(st):
        p = _dot(st["p"], st["p"])
        st["p"] = p.astype(BF16)
        st["t_inv"] = st["t_inv"] + _dot(st["t_inv"].astype(BF16), st["p"])

    def solve(st):
        tb = st["t_inv"].astype(BF16)
        st["u"] = _dot(tb, st["vb"])
        st["w"] = _dot(tb, st["kg"])

    def apply_state(st, b):
        ws, qs = [], []
        for h in range(nh):
            hs = slice(h * g8, (h + 1) * g8)
            lhs = jnp.concatenate([st["w"][hs], st["qg"][hs]], 0).astype(BF16)
            r = _dot(lhs, s0_ref[b, h].astype(BF16))
            ws.append(r[:g8])
            qs.append(r[g8:])
        st["v_new"] = st["u"] - jnp.concatenate(ws, 0)
        st["oq"] = jnp.concatenate(qs, 0)

    def finish(st, b):
        v_new = st["v_new"]
        o = st["oq"] + _dot(st["a_in"], v_new.astype(BF16))
        for h in range(nh):
            vm = jnp.where(rhead == h, v_new, 0.0).astype(BF16)
            dec = jnp.exp(st["g_last"][h * g8:h * g8 + 1, :])
            s_ref[b, h] = s0_ref[b, h] * dec + _dot_tn(st["k_dec"], vm)
        z = jnp.concatenate([x_ref[b][:, 3 * QK_W + h * GDN_DV:3 * QK_W + (h + 1) * GDN_DV].astype(F32)
                             for h in range(nh)], 0)
        o_ref[b] = (_rms(o, gn) * _silu(z)).astype(o_ref.dtype)

    bs_ = range(x_ref.shape[0])
    sts = [operands(b) for b in bs_]
    for st in sts:
        gram(st)
    covered = 2
    while covered < t_len:
        for st in sts:
            neumann(st)
        covered *= 2
    for st in sts:
        solve(st)
    for b in bs_:
        apply_state(sts[b], b)
    for b in bs_:
        finish(sts[b], b)


def _gdn_sample(proj3, ab3, hist, conv_w, gate_par, gdn_norm, s0, *, bb):
    bsz, t_len, _ = proj3.shape
    nh, g8 = GDN_HEADS, SUBLANES
    pad_t = lambda a, rows: jnp.pad(a, ((0, 0), (0, g8 - rows), (0, 0)))
    x8 = pad_t(proj3[:, :, :4 * QK_W], t_len)
    ab8 = pad_t(ab3, t_len)
    hist8 = pad_t(hist, CONV_W - 1)
    o, s_new = pl.pallas_call(
        functools.partial(_gdn_sample_kernel, t_len=t_len),
        grid=(bsz // bb,),
        in_specs=[pl.BlockSpec((bb, g8, 4 * QK_W), lambda i: (i, 0, 0)),
                  pl.BlockSpec((bb, g8, LANES), lambda i: (i, 0, 0)),
                  pl.BlockSpec((bb, g8, QKV_W), lambda i: (i, 0, 0)),
                  pl.BlockSpec((CONV_W, QKV_W), lambda i: (0, 0)),
                  pl.BlockSpec((SUBLANES, LANES), lambda i: (0, 0)),
                  pl.BlockSpec((1, LANES), lambda i: (0, 0)),
                  pl.BlockSpec((bb, nh, GDN_DK, GDN_DV), lambda i: (i, 0, 0, 0))],
        out_specs=[pl.BlockSpec((bb, nh * g8, GDN_DV), lambda i: (i, 0, 0)),
                   pl.BlockSpec((bb, nh, GDN_DK, GDN_DV), lambda i: (i, 0, 0, 0))],
        out_shape=[jax.ShapeDtypeStruct((bsz, nh * g8, GDN_DV), BF16),
                   jax.ShapeDtypeStruct((bsz, nh, GDN_DK, GDN_DV), F32)],
        compiler_params=_cparams(("parallel",)),
        name="gdn_sample",
    )(x8, ab8, hist8, conv_w, gate_par, gdn_norm.reshape(1, LANES), s0)
    o = o.reshape(bsz, nh, g8, GDN_DV)[:, :, :t_len].transpose(0, 2, 1, 3)
    return o.reshape(bsz, t_len, MIX_A), s_new


def _pool_finish(d_groups, pw_ref, ps_ref, zb):
    outs = [_dot(d.astype(BF16), pw_ref[gi]) for gi, d in enumerate(d_groups)]
    return jnp.concatenate(outs, axis=-1) * ps_ref[...] * _silu(zb)


def _pool_tile(u, halo, zb, pos, pw_ref, ps_ref):
    ext = jnp.concatenate([halo, u], axis=0)
    d_groups = []
    for gi, win in enumerate(POOL_WINDOWS):
        sl = slice(gi * POOL_CH, (gi + 1) * POOL_CH)
        lvl = ext[:, sl]
        s = 1
        while s < win:
            lvl = lvl + pltpu.roll(lvl, s, axis=0)
            s *= 2
        cnt = jnp.minimum(win, pos + 1).astype(F32)
        d_groups.append(lvl[POOL_HALO:, :] / cnt - u[:, sl])
    return _pool_finish(d_groups, pw_ref, ps_ref, zb)


def _mix_pool_kernel(x_ref, oa_ref, u_ref, halo_ref, zb_ref, pw_ref, ps_ref, wo_ref, nc_ref, wq_ref,
                     x1_ref, qx_ref, *, pos0):
    i = pl.program_id(1)
    tt = u_ref.shape[1]
    halo = jnp.where(i > 0, halo_ref[0].astype(F32), 0.0)
    pos = pos0 + i * tt + lax.broadcasted_iota(jnp.int32, (tt, 1), 0)
    ob = _pool_tile(u_ref[0].astype(F32), halo, zb_ref[0].astype(F32), pos, pw_ref, ps_ref)
    acc = _dot(oa_ref[0], wo_ref[0:MIX_A, :]) + _dot(ob.astype(BF16), wo_ref[MIX_A:D_MODEL, :])
    x1 = x_ref[0] + acc
    x1_ref[0] = x1
    qx_ref[0] = _dot(_rms(x1, nc_ref[...]).astype(BF16), wq_ref[...]).astype(qx_ref.dtype)


def _mix_pool(x3, oa3, proj3, pool_w, pool_scale, w_out, norm_cross, w_cq, *, tt, pos0):
    bsz, t_len, d = x3.shape
    ub, zb = COL_ZA // MIX_B + 1, COL_ZA // MIX_B + 2
    hb = tt // POOL_HALO
    const2 = lambda b, i: (0, 0)
    resident = dict(pipeline_mode=pl.Buffered(1))
    row = lambda w: pl.BlockSpec((1, tt, w), lambda b, i: (b, i, 0))
    return pl.pallas_call(
        functools.partial(_mix_pool_kernel, pos0=pos0),
        grid=(bsz, t_len // tt),
        in_specs=[row(d), row(MIX_A),
                  pl.BlockSpec((1, tt, MIX_B), lambda b, i: (b, i, ub)),
                  pl.BlockSpec((1, POOL_HALO, MIX_B), lambda b, i: (b, jnp.maximum(i * hb - 1, 0), ub)),
                  pl.BlockSpec((1, tt, MIX_B), lambda b, i: (b, i, zb)),
                  pl.BlockSpec((POOL_GROUPS, POOL_CH, POOL_CH), lambda b, i: (0, 0, 0), **resident),
                  pl.BlockSpec((1, MIX_B), const2, **resident),
                  pl.BlockSpec((d, d), const2, **resident),
                  pl.BlockSpec((1, d), const2, **resident),
                  pl.BlockSpec((d, d), const2, **resident)],
        out_specs=[row(d), row(d)],
        out_shape=[jax.ShapeDtypeStruct((bsz, t_len, d), F32), jax.ShapeDtypeStruct((bsz, t_len, d), BF16)],
        compiler_params=_cparams(("parallel", "arbitrary")),
        name="mix_pool",
    )(x3, oa3, proj3, proj3, proj3, pool_w, pool_scale.reshape(1, MIX_B), w_out,
      norm_cross.reshape(1, d), w_cq)


def _pool_sample_kernel(ext_ref, zb_ref, pw_ref, ps_ref, o_ref, *, pos0):
    t_len = zb_ref.shape[0]
    for t in range(t_len):
        e = POOL_BUF + t
        d_groups = []
        for gi, win in enumerate(POOL_WINDOWS):
            sl = slice(gi * POOL_CH, (gi + 1) * POOL_CH)
            tot = ext_ref[e, :, sl]
            for j in range(1, win):
                tot = tot + ext_ref[e - j, :, sl]
            cnt = float(min(win, pos0 + t + 1))
            d_groups.append(tot / cnt - ext_ref[e, :, sl])
        o_ref[t] = _pool_finish(d_groups, pw_ref, ps_ref, zb_ref[t].astype(F32)).astype(o_ref.dtype)


def _pool_sample(ext_tm, zb_tm, pool_w, pool_scale, *, bb, pos0):
    t_len, bsz, _ = zb_tm.shape
    return pl.pallas_call(
        functools.partial(_pool_sample_kernel, pos0=pos0),
        grid=(bsz // bb,),
        in_specs=[pl.BlockSpec((POOL_BUF + t_len, bb, MIX_B), lambda i: (0, i, 0)),
                  pl.BlockSpec((t_len, bb, MIX_B), lambda i: (0, i, 0)),
                  pl.BlockSpec((POOL_GROUPS, POOL_CH, POOL_CH), lambda i: (0, 0, 0)),
                  pl.BlockSpec((1, MIX_B), lambda i: (0, 0))],
        out_specs=pl.BlockSpec((t_len, bb, MIX_B), lambda i: (0, i, 0)),
        out_shape=jax.ShapeDtypeStruct((t_len, bsz, MIX_B), BF16),
        compiler_params=_cparams(("parallel",)),
        name="pool_sample",
    )(ext_tm, zb_tm, pool_w, pool_scale.reshape(1, MIX_B))


def _mix_out_kernel(x_ref, oa_ref, ob_ref, wo_ref, nc_ref, wq_ref, x1_ref, qx_ref):
    acc = _dot(oa_ref[...], wo_ref[0:MIX_A, :]) + _dot(ob_ref[...], wo_ref[MIX_A:D_MODEL, :])
    x1 = x_ref[...] + acc
    x1_ref[...] = x1
    qx_ref[...] = _dot(_rms(x1, nc_ref[...]).astype(BF16), wq_ref[...]).astype(qx_ref.dtype)


def _mix_out(x, oa, ob, w_out, norm_cross, w_cq, *, tm):
    n, d = x.shape
    const = lambda i: (0, 0)
    return pl.pallas_call(
        _mix_out_kernel,
        grid=(n // tm,),
        in_specs=[pl.BlockSpec((tm, d), lambda i: (i, 0)),
                  pl.BlockSpec((tm, MIX_A), lambda i: (i, 0)),
                  pl.BlockSpec((tm, MIX_B), lambda i: (i, 0)),
                  pl.BlockSpec((d, d), const), pl.BlockSpec((1, d), const), pl.BlockSpec((d, d), const)],
        out_specs=[pl.BlockSpec((tm, d), lambda i: (i, 0)), pl.BlockSpec((tm, d), lambda i: (i, 0))],
        out_shape=[jax.ShapeDtypeStruct((n, d), F32), jax.ShapeDtypeStruct((n, d), BF16)],
        compiler_params=_cparams(("parallel",)),
        name="mix_out",
    )(x, oa, ob, w_out, norm_cross.reshape(1, d), w_cq)


def _xattn_out_kernel(q_ref, k_ref, v_ref, x1_ref, wco_ref, nf_ref, y_ref):
    q = q_ref[0]
    scale = X_HEAD_DIM ** -0.5
    sls = [slice(h * X_HEAD_DIM, (h + 1) * X_HEAD_DIM) for h in range(X_HEADS)]
    ss = [_dot_nt(q[:, sl], k_ref[0, :, sl].astype(BF16)) * scale for sl in sls]
    ps = []
    for s in ss:
        p = jnp.exp(s - jnp.max(s, axis=-1, keepdims=True))
        ps.append((p / jnp.sum(p, axis=-1, keepdims=True)).astype(BF16))
    ctx = jnp.concatenate([_dot(p, v_ref[0, :, sl].astype(BF16)).astype(BF16)
                           for p, sl in zip(ps, sls)], axis=-1)
    y_ref[0] = _rms(x1_ref[0] + _dot(ctx, wco_ref[...]), nf_ref[...])


def _xattn_out(qx3, mk3, mv3, x13, w_co, norm_final, *, tq):
    bsz, t_len, d = qx3.shape
    n_mem = mk3.shape[1]
    row = pl.BlockSpec((1, tq, d), lambda b, i: (b, i, 0))
    mem = pl.BlockSpec((1, n_mem, d), lambda b, i: (b, 0, 0))
    return pl.pallas_call(
        _xattn_out_kernel,
        grid=(bsz, t_len // tq),
        in_specs=[row, mem, mem, row,
                  pl.BlockSpec((d, d), lambda b, i: (0, 0), pipeline_mode=pl.Buffered(1)),
                  pl.BlockSpec((1, d), lambda b, i: (0, 0), pipeline_mode=pl.Buffered(1))],
        out_specs=row,
        out_shape=jax.ShapeDtypeStruct((bsz, t_len, d), F32),
        compiler_params=_cparams(("parallel", "arbitrary")),
        name="xattn_out",
    )(qx3, mk3, mv3, x13, w_co, norm_final.reshape(1, d))


def _xattn_native_stages(q_ref, k_ref, v_ref, o_ref, t_len):
    nj = X_HEAD_DIM // LANES
    grp = nj * X_HEADS
    th = t_len * X_HEADS
    scale = X_HEAD_DIM ** -0.5
    ncol = k_ref.shape[1]
    r = lax.broadcasted_iota(jnp.int32, (th, ncol), 0) & (X_HEADS - 1)
    c = lax.broadcasted_iota(jnp.int32, (th, ncol), 1) & (grp - 1)

    def probs(z):
        s = None
        for j in range(nj):
            zj = jnp.where(c == r + j * X_HEADS, z[j * th:(j + 1) * th, :], 0.0)
            if j:
                zj = pltpu.roll(zj, ncol - j * X_HEADS, axis=1)
            s = zj if s is None else s + zj
        sm = jnp.where(c == r, s * scale, -jnp.inf)
        p = jnp.exp(sm - jnp.max(sm, axis=1, keepdims=True))
        p = p / jnp.sum(p, axis=1, keepdims=True)
        return jnp.concatenate([p if j == 0 else pltpu.roll(p, j * X_HEADS, axis=1)
                                for j in range(nj)], axis=0).astype(BF16)

    bs_ = range(q_ref.shape[0])
    vals = {}

    def scores():
        vals["z"] = [_dot_nt(q_ref[b], k_ref[b].astype(BF16)) for b in bs_]

    def probabilities():
        vals["p"] = [probs(z) for z in vals["z"]]

    def context():
        for b in bs_:
            o_ref[b] = _dot(vals["p"][b], v_ref[b].astype(BF16)).astype(o_ref.dtype)

    return [scores, probabilities, context]


def _kv_rows(cache):
    bsz, n_mem, nh, dh = cache.shape
    nj = dh // LANES
    return cache.reshape(bsz, n_mem, nh, nj, LANES).transpose(0, 1, 3, 2, 4).reshape(
        bsz, n_mem * nj * nh, LANES)


def _attn_out_kernel(x1_ref, ctx_ref, wco_ref, nf_ref, y_ref):
    x2 = x1_ref[...] + _dot(ctx_ref[...], wco_ref[...])
    y_ref[...] = _rms(x2, nf_ref[...])


def _attn_out(x1, ctx, w_co, norm_final, *, tm):
    n, d = x1.shape
    const = lambda i: (0, 0)
    return pl.pallas_call(
        _attn_out_kernel,
        grid=(n // tm,),
        in_specs=[pl.BlockSpec((tm, d), lambda i: (i, 0)), pl.BlockSpec((tm, d), lambda i: (i, 0)),
                  pl.BlockSpec((d, d), const, pipeline_mode=pl.Buffered(1)),
                  pl.BlockSpec((1, d), const, pipeline_mode=pl.Buffered(1))],
        out_specs=pl.BlockSpec((tm, d), lambda i: (i, 0)),
        out_shape=jax.ShapeDtypeStruct((n, d), F32),
        compiler_params=_cparams(("parallel",)),
        name="attn_out",
    )(x1, ctx, w_co, norm_final.reshape(1, d))


def kernel(x_prompt, x_sample, mem_prompt, cache_mem_k, cache_mem_v, state_delta, state_conv,
           state_pool, norm_mix, w_in, conv_w, a_log, dt_bias, gdn_norm, pool_w, pool_scale,
           w_out, norm_mem, norm_cross, w_cq, w_ck, w_cv, w_co, norm_final):
    bp, tp, d = x_prompt.shape
    bs, ts, _ = x_sample.shape
    n_mem = mem_prompt.shape[1]

    w_main, w_gate = _repack_w_in(w_in[0].T, tk=512, tn=1024)
    wo, wcq, wco = (w[0].astype(BF16) for w in (w_out, w_cq, w_co))
    pw = pool_w[0].astype(BF16)
    gate_par = jnp.zeros((SUBLANES, LANES), F32)
    gate_par = gate_par.at[0, :GDN_HEADS].set(a_log[0]).at[1, :GDN_HEADS].set(dt_bias[0])
    cw = conv_w[0]

    mem2d = mem_prompt.reshape(bp * n_mem, d)
    mk = _norm_proj(mem2d, norm_mem[0], w_ck[0], tm=bp * n_mem, tn=1024, out_dtype=F32)
    mv = _norm_proj(mem2d, norm_mem[0], w_cv[0], tm=bp * n_mem, tn=1024, out_dtype=F32)

    proj_s, ab_s = _norm_proj(x_sample.reshape(bs * ts, d), norm_mix[0], w_main, w_gate,
                              tm=bs * ts, tn=1024, out_dtype=BF16)
    proj_s3 = proj_s.reshape(bs, ts, W_MAIN)
    oa_s, delta_s = _gdn_sample(proj_s3, ab_s.reshape(bs, ts, LANES), state_conv[0], cw, gate_par,
                                gdn_norm[0], state_delta[0], bb=8)
    u_s = proj_s3[:, :, COL_ZA + MIX_A:COL_ZA + MIX_A + MIX_B].astype(F32)
    ext_s = jnp.concatenate([state_pool[0], u_s], axis=1)
    zb_tm = proj_s3[:, :, COL_ZA + MIX_A + MIX_B:].transpose(1, 0, 2)
    ob_s = _pool_sample(ext_s.transpose(1, 0, 2), zb_tm, pw, pool_scale[0], bb=32,
                        pos0=PAST_LEN).transpose(1, 0, 2)
    x1_s, qx_s = _mix_out(x_sample.reshape(bs * ts, d), oa_s.reshape(-1, MIX_A),
                          ob_s.reshape(-1, MIX_B), wo, norm_cross[0], wcq, tm=256)
    nj = X_HEAD_DIM // LANES
    q_rows = qx_s.reshape(bs, ts, X_HEADS, nj, LANES).transpose(0, 3, 1, 2, 4).reshape(
        bs, nj * ts * X_HEADS, LANES)

    proj_p, ab_p, ctx_rows = _norm_proj(
        x_prompt.reshape(bp * tp, d), norm_mix[0], w_main, w_gate, tm=1024, tn=768, out_dtype=BF16,
        attn=(q_rows, _kv_rows(cache_mem_k[0]), _kv_rows(cache_mem_v[0]), ts))
    proj_p3 = proj_p.reshape(bp, tp, W_MAIN)
    oa_p, delta_p = _gdn_prompt(proj_p3, ab_p.reshape(bp, tp, LANES), cw, gate_par, gdn_norm[0])
    x1_p, qx_p = _mix_pool(x_prompt, oa_p, proj_p3, pw, pool_scale[0], wo, norm_cross[0], wcq,
                           tt=512, pos0=0)
    y_p = _xattn_out(qx_p, mk.reshape(bp, n_mem, d), mv.reshape(bp, n_mem, d), x1_p, wco,
                     norm_final, tq=512)
    conv_p = proj_p3[:, tp - (CONV_W - 1):, :QKV_W].astype(F32)
    pool_p = proj_p3[:, tp - POOL_BUF:, COL_ZA + MIX_A:COL_ZA + MIX_A + MIX_B].astype(F32)

    ctx_s = ctx_rows.reshape(bs, nj, ts, X_HEADS, LANES).transpose(0, 2, 3, 1, 4).reshape(bs * ts, d)
    y_s = _attn_out(x1_s, ctx_s, wco, norm_final, tm=256).reshape(bs, ts, d)
    conv_s = jnp.concatenate([state_conv[0], proj_s3[:, :, :QKV_W].astype(F32)], axis=1)[:, ts:]
    pool_s = ext_s[:, ts:]

    hd = (X_HEADS, X_HEAD_DIM)
    return (y_p, y_s, mk.reshape(1, bp, n_mem, *hd), mv.reshape(1, bp, n_mem, *hd),
            delta_p[None], conv_p[None], pool_p[None], delta_s[None], conv_s[None], pool_s[None])
```

```python
import functools

import jax
import jax.numpy as jnp
from jax import lax
from jax.experimental import pallas as pl
from jax.experimental.pallas import tpu as pltpu

F32 = jnp.float32
BF16 = jnp.bfloat16

D_MODEL = 2048
MIX_A = D_MODEL // 2
MIX_B = D_MODEL - MIX_A
GDN_HEADS = 8
GDN_DK = MIX_A // GDN_HEADS
GDN_DV = MIX_A // GDN_HEADS
QK_W = GDN_HEADS * GDN_DK
QKV_W = 2 * QK_W + GDN_HEADS * GDN_DV
CONV_W = 4
POOL_WINDOWS = (2, 4, 8, 16)
POOL_GROUPS = len(POOL_WINDOWS)
POOL_CH = MIX_B // POOL_GROUPS
POOL_BUF = max(POOL_WINDOWS) - 1
X_HEADS = 4
X_HEAD_DIM = D_MODEL // X_HEADS
PAST_LEN = 16384
EPS = 1e-6
COL_ZA = QKV_W
COL_A = COL_ZA + GDN_HEADS * GDN_DV
COL_B = COL_A + GDN_HEADS
COL_U = COL_B + GDN_HEADS
COL_ZB = COL_U + MIX_B
IN_COLS = COL_ZB + MIX_B

W_MAIN = IN_COLS - 2 * GDN_HEADS
LANES = 128
SUBLANES = 8
BF16_ROWS = 16
GDN_CHUNK = 128
POOL_HALO = 16
VMEM_LIMIT = 56 * 1024 * 1024


def _cparams(sem):
    return pltpu.CompilerParams(dimension_semantics=sem, vmem_limit_bytes=VMEM_LIMIT)


def _dot(a, b):
    return jnp.dot(a, b, preferred_element_type=F32)


def _dot_nt(a, b):
    return lax.dot_general(a, b, (((1,), (1,)), ((), ())), preferred_element_type=F32)


def _dot_tn(a, b):
    return lax.dot_general(a, b, (((0,), (0,)), ((), ())), preferred_element_type=F32)


def _rms(x, g):
    return x * lax.rsqrt(jnp.mean(x * x, axis=-1, keepdims=True) + EPS) * g


def _silu(x):
    return x * jax.nn.sigmoid(x)


def _softplus(x):
    return jnp.maximum(x, 0.0) + jnp.log1p(jnp.exp(-jnp.abs(x)))


def _repack_kernel(wt_ref, gt_ref, main_ref, gate_ref):
    main_ref[...] = wt_ref[...].T.astype(main_ref.dtype)

    @pl.when(pl.program_id(1) == 0)
    def _():
        head = gt_ref[...].T
        lane = lax.broadcasted_iota(jnp.int32, head.shape, 1)
        gate_ref[...] = jnp.where(lane < COL_U - COL_A, head, 0.0).astype(gate_ref.dtype)


def _repack_w_in(wt, *, tk, tn):
    k_dim = wt.shape[1]
    n_gate = COL_U - COL_A
    src_row = lambda j: pl.multiple_of(j * tn + (j // (COL_A // tn)) * n_gate, n_gate)
    return pl.pallas_call(
        _repack_kernel,
        grid=(k_dim // tk, W_MAIN // tn),
        in_specs=[pl.BlockSpec((pl.Element(tn), pl.Element(tk)), lambda i, j: (src_row(j), i * tk)),
                  pl.BlockSpec((pl.Element(LANES), pl.Element(tk)), lambda i, j: (COL_A, i * tk))],
        out_specs=[pl.BlockSpec((tk, tn), lambda i, j: (i, j)),
                   pl.BlockSpec((tk, LANES), lambda i, j: (i, 0))],
        out_shape=[jax.ShapeDtypeStruct((k_dim, W_MAIN), BF16),
                   jax.ShapeDtypeStruct((k_dim, LANES), BF16)],
        compiler_params=_cparams(("parallel", "arbitrary")),
        name="repack_w_in",
    )(wt, wt)


def _norm_proj_kernel(x_ref, g_ref, w_ref, *rest, with_side, attn_t):
    rest = list(rest)
    ws_ref = rest.pop(0) if with_side else None
    xq_ref, xk_ref, xv_ref = (rest.pop(0), rest.pop(0), rest.pop(0)) if attn_t else (None,) * 3
    out_ref = rest.pop(0)
    side_ref = rest.pop(0) if with_side else None
    xo_ref = rest.pop(0) if attn_t else None
    h_scr = rest.pop(0)

    @pl.when(pl.program_id(1) == 0)
    def _():
        h = _rms(x_ref[...], g_ref[...]).astype(BF16)
        h_scr[...] = h
        if with_side:
            side_ref[...] = _dot(h, ws_ref[...])

    stages = _xattn_native_stages(xq_ref, xk_ref, xv_ref, xo_ref, attn_t) if attn_t else []
    for stage in stages[:1]:
        stage()
    out_ref[...] = _dot(h_scr[...], w_ref[...].astype(BF16)).astype(out_ref.dtype)
    for stage in stages[1:]:
        stage()


def _norm_proj(x, g, w, w_side=None, *, tm, tn, out_dtype, attn=None):
    n, d = x.shape
    ncol = w.shape[1]
    with_side = w_side is not None
    in_specs = [pl.BlockSpec((tm, d), lambda i, j: (i, 0)),
                pl.BlockSpec((1, d), lambda i, j: (0, 0)),
                pl.BlockSpec((d, tn), lambda i, j: (0, j))]
    out_specs = [pl.BlockSpec((tm, tn), lambda i, j: (i, j))]
    out_shape = [jax.ShapeDtypeStruct((n, ncol), out_dtype)]
    args = [x, g.reshape(1, d), w]
    if with_side:
        in_specs.append(pl.BlockSpec((d, LANES), lambda i, j: (0, 0)))
        out_specs.append(pl.BlockSpec((tm, LANES), lambda i, j: (i, 0)))
        out_shape.append(jax.ShapeDtypeStruct((n, LANES), F32))
        args.append(w_side)
    if attn is not None:
        xq, xk, xv, attn_t = attn
        n_j = ncol // tn
        steps = (n // tm) * n_j
        assert xq.shape[0] % steps == 0
        xbb = xq.shape[0] // steps
        share = lambda rows: pl.BlockSpec((xbb, rows, LANES), lambda i, j: (i * n_j + j, 0, 0))
        in_specs += [share(xq.shape[1]), share(xk.shape[1]), share(xv.shape[1])]
        out_specs.append(share(xq.shape[1]))
        out_shape.append(jax.ShapeDtypeStruct(xq.shape, BF16))
        args += [xq, xk, xv]
    res = pl.pallas_call(
        functools.partial(_norm_proj_kernel, with_side=with_side,
                          attn_t=attn[3] if attn is not None else 0),
        grid=(n // tm, ncol // tn),
        in_specs=in_specs, out_specs=out_specs, out_shape=out_shape,
        scratch_shapes=[pltpu.VMEM((tm, d), BF16)],
        compiler_params=_cparams(("parallel", "arbitrary")),
        name=("norm_proj_attn" if attn is not None else "norm_proj_side") if with_side else "norm_proj",
    )(*args)
    return res if len(res) > 1 else res[0]


INV_BASE_LOG2 = 3
GATE_LANES = 2 * GDN_HEADS
GATE_PACK = LANES // GATE_LANES
_GDN_WORK = (("k", 1, BF16, 0), ("kb", 1, BF16, 0), ("q", 1, BF16, 0), ("kd", 1, BF16, 0),
             ("vk", 2, BF16, 0), ("dinc", 1, F32, 0), ("low", 1, F32, 0), ("x", 1, F32, 0),
             ("p", 1, BF16, 0), ("wu", 2, BF16, 0),
             ("cq", 1, F32, SUBLANES), ("ck", 1, F32, SUBLANES), ("cv", 1, F32, SUBLANES))


def _split3(x):
    hi = x.astype(BF16)
    r1 = x - hi.astype(F32)
    mid = r1.astype(BF16)
    lo = (r1 - mid.astype(F32)).astype(BF16)
    return hi, mid, lo


def _gdn_prompt_kernel(q_ref, k_ref, v_ref, z_ref, ab_ref, cwq_ref, cwk_ref, cwv_ref, gp_ref,
                       gn_ref, o_ref, s_ref,
                       gc_s, gct_s, egc_s, ekd_s, beta_s, u_s, n_s, pw_s, qa_s, sv_s, *work_refs, group):
    nw = len(_GDN_WORK)
    work = [{spec[0]: ref for spec, ref in zip(_GDN_WORK, work_refs[j * nw:(j + 1) * nw])}
            for j in range(len(work_refs) // nw)]
    t_len = q_ref.shape[1]
    c = GDN_CHUNK
    n_c = t_len // c
    hps = q_ref.shape[2] // GDN_DK
    head_of = lambda u: pl.program_id(1) * hps + u // n_c
    lanes_of = lambda u: slice((u // n_c) * GDN_DK, (u // n_c + 1) * GDN_DK)
    row = lax.broadcasted_iota(jnp.int32, (c, c), 0)
    col = lax.broadcasted_iota(jnp.int32, (c, c), 1)

    @pl.when(pl.program_id(1) == 0)
    def _():
        gp = gp_ref[...]
        slot = col[0:1] >> (GATE_LANES.bit_length() - 1)
        par = gp
        for j in range(1, GATE_PACK):
            par = par + jnp.where(slot == j, pltpu.roll(gp, GATE_LANES * j, axis=1), 0.0)
        a_neg = -jnp.exp(par[0:1])
        dt_b = par[1:2]
        tri = jnp.where(row >= col, 1.0, 0.0).astype(BF16)
        for grp in range(n_c // GATE_PACK):
            ab = ab_ref[0, pl.ds(grp * GATE_PACK * c, c), :]
            for j in range(1, GATE_PACK):
                nxt = ab_ref[0, pl.ds((grp * GATE_PACK + j) * c, c), :]
                ab = jnp.where(slot == j, pltpu.roll(nxt, GATE_LANES * j, axis=1), ab)
            hi, mid, lo = _split3(a_neg * _softplus(ab + dt_b))
            gc = _dot(tri, hi) + _dot(tri, mid) + _dot(tri, lo)
            rows = pl.ds(grp * c, c)
            gc_s[rows, :] = gc
            gct_s[rows, :] = gc.T
            egc_s[rows, :] = jnp.exp(gc)
            ekd_s[rows, :] = jnp.exp(gc[c - 1:c, :] - gc)
            beta_s[rows, :] = jax.nn.sigmoid(ab)

    def l2n(x):
        return x * lax.rsqrt(jnp.sum(x * x, axis=-1, keepdims=True) + EPS)

    def operands(w, u):
        ci, hl, h = u % n_c, lanes_of(u), head_of(u)
        r0 = ci * c
        rows = pl.ds(r0, c)

        def conv_silu(x_ref, cw_ref, stage):
            cur = x_ref[0, rows, hl].astype(F32)
            if ci > 0:
                tail = x_ref[0, pl.ds(r0 - BF16_ROWS, BF16_ROWS), hl].astype(F32)[BF16_ROWS - SUBLANES:]
            else:
                tail = jnp.zeros((SUBLANES, cur.shape[1]), F32)
            stage[0:SUBLANES, :] = tail
            stage[SUBLANES:SUBLANES + c, :] = cur
            cw = cw_ref[:, hl]
            y = cur * cw[CONV_W - 1:CONV_W]
            for s in range(1, CONV_W):
                y = y + stage[SUBLANES - s:SUBLANES - s + c, :] * cw[CONV_W - 1 - s:CONV_W - s]
            return _silu(y)

        g_rows = pl.ds((ci // GATE_PACK) * c, c)
        g_lane = (ci % GATE_PACK) * GATE_LANES + h

        def column(scr, idx):
            return jnp.sum(jnp.where(col == idx, scr[g_rows, :], 0.0), axis=-1, keepdims=True)

        q = l2n(conv_silu(q_ref, cwq_ref, w["cq"])) * (GDN_DK ** -0.5)
        k = l2n(conv_silu(k_ref, cwk_ref, w["ck"]))
        v = conv_silu(v_ref, cwv_ref, w["cv"])
        gcc = column(gc_s, g_lane)
        egc = column(egc_s, g_lane)
        ekd = column(ekd_s, g_lane)
        beta = column(beta_s, g_lane + GDN_HEADS)
        grow = gct_s[pl.ds((ci // GATE_PACK) * c + g_lane, 1), :]
        tri_i = row >= col
        w["dinc"][...] = jnp.where(tri_i, jnp.exp(jnp.where(tri_i, gcc - grow, 0.0)), 0.0)
        kb = k * beta
        w["k"][...] = k.astype(BF16)
        w["kb"][...] = kb.astype(BF16)
        w["q"][...] = q.astype(BF16)
        w["kd"][...] = (k * ekd).astype(BF16)
        w["vk"][:, :GDN_DV] = (v * beta).astype(BF16)
        w["vk"][:, GDN_DV:] = (kb * egc).astype(BF16)
        qa_s[u, :, 0:GDN_DK] = (q * egc).astype(BF16)

    def gram(w, u):
        r = _dot_nt(jnp.concatenate([w["kb"][...], w["q"][...]], axis=0), w["k"][...])
        d_incl = w["dinc"][...]
        low = r[:c] * jnp.where(row > col, d_incl, 0.0)
        qa_s[u, :, GDN_DK:] = (r[c:] * d_incl).astype(BF16)
        w["low"][...] = low
        ld = jnp.where((row >> INV_BASE_LOG2) == (col >> INV_BASE_LOG2), low, 0.0)
        w["x"][...] = jnp.where(row == col, 1.0, 0.0) - ld
        w["p"][...] = ld.astype(BF16)

    def neumann(w, first, last):
        pb = w["p"][...]
        if first:
            w["p"][...] = _dot(pb, pb).astype(BF16)
        elif last:
            x = w["x"][...]
            w["x"][...] = x + _dot(x.astype(BF16), pb)
        else:
            x = w["x"][...]
            r = _dot(jnp.concatenate([x.astype(BF16), pb], axis=0), pb)
            w["x"][...] = x + r[:c]
            w["p"][...] = r[c:].astype(BF16)

    def merge_a(w, lg):
        pair = (row >> (lg + 1)) == (col >> (lg + 1))
        m = jnp.where(pair, (row >> lg) - (col >> lg), 0) > 0
        lm = jnp.where(m, w["low"][...], 0.0).astype(BF16)
        w["p"][...] = _dot(lm, w["x"][...].astype(BF16)).astype(BF16)

    def merge_b(w):
        x = w["x"][...]
        w["x"][...] = x - _dot(x.astype(BF16), w["p"][...])

    def solve(w, ci):
        uw = _dot(w["x"][...].astype(BF16), w["vk"][...])
        u_s[ci] = uw[:, :GDN_DV]
        uwb = uw.astype(BF16)
        w["wu"][...] = uwb
        pw_s[ci, c:2 * c, :] = uwb[:, GDN_DV:]

    def outer(w, ci):
        np_ = _dot_tn(w["kd"][...], w["wu"][...])
        n_s[ci] = np_[:, :GDN_DV]
        pw_s[ci, 0:c, :] = np_[:, GDN_DV:].astype(BF16)

    n_sq = INV_BASE_LOG2 - 1
    lgc = c.bit_length() - 1

    def prepare_stages(cis):
        over = lambda fn, *a: (lambda: [fn(work[ci], *[ci if x is cis else x for x in a])
                                        for ci in cis])
        stages = [over(operands, cis), over(gram, cis)]
        stages += [over(neumann, step == 0, step == n_sq) for step in range(n_sq + 1)]
        for lg in range(INV_BASE_LOG2, lgc):
            stages += [over(merge_a, lg), over(merge_b)]
        return stages + [over(solve, cis), over(outer, cis)]

    gn = gn_ref[...]
    lane1 = lax.broadcasted_iota(jnp.int32, (1, LANES), 1)

    def advance(u, s_mat):
        ci = u % n_c
        if ci == 0:
            s_mat = jnp.zeros((GDN_DK, GDN_DV), F32)
        sb = s_mat.astype(BF16)
        r = _dot(pw_s[u], sb)
        g_last = jnp.sum(jnp.where(lane1 == (ci % GATE_PACK) * GATE_LANES + head_of(u),
                                   gc_s[pl.ds((ci // GATE_PACK) * c + c - 1, 1), :], 0.0),
                         axis=-1, keepdims=True)
        sv_s[u, 0:c, :] = sb
        sv_s[u, c:2 * c, :] = (u_s[u] - r[c:2 * c]).astype(BF16)
        s_next = s_mat * jnp.exp(g_last) - r[0:c] + n_s[u]
        if ci == n_c - 1:
            s_ref[0, u // n_c] = s_next
        return s_next

    def emit(u):
        rows, hl = pl.ds((u % n_c) * c, c), lanes_of(u)
        o = _dot(qa_s[u], sv_s[u])
        zz = z_ref[0, rows, hl].astype(F32)
        o_ref[0, rows, hl] = (_rms(o, gn) * _silu(zz)).astype(o_ref.dtype)

    s_mat = None
    chain_todo, emit_todo = [], []
    for g in range(hps * n_c // group):
        us = list(range(g * group, (g + 1) * group))
        for stage in prepare_stages(us):
            stage()
            if chain_todo:
                emit_todo.append(chain_todo[0])
                s_mat = advance(chain_todo.pop(0), s_mat)
            elif emit_todo:
                emit(emit_todo.pop(0))
        chain_todo += us
    while chain_todo:
        if emit_todo:
            emit(emit_todo.pop(0))
        emit_todo.append(chain_todo[0])
        s_mat = advance(chain_todo.pop(0), s_mat)
    for u in emit_todo:
        emit(u)


def _gdn_prompt(proj3, ab3, conv_w, gate_par, gdn_norm):
    bsz, t_len, _ = proj3.shape
    nh = GDN_HEADS
    hps = 2
    hw = hps * GDN_DK
    blk = lambda off: pl.BlockSpec((1, t_len, hw), lambda b, p: (b, 0, off // hps + p))
    cw = lambda off: pl.BlockSpec((CONV_W, hw), lambda b, p: (0, off // hps + p))
    c = GDN_CHUNK
    n_c = t_len // c
    n_u = hps * n_c
    assert n_c % GATE_PACK == 0 and nh % hps == 0
    gate_scr = pltpu.VMEM((n_c // GATE_PACK * c, LANES), F32)
    group = 16
    work = [pltpu.VMEM((c + extra, wide * LANES), dt)
            for _ in range(n_u) for _, wide, dt, extra in _GDN_WORK]
    return pl.pallas_call(
        functools.partial(_gdn_prompt_kernel, group=group),
        grid=(bsz, nh // hps),
        in_specs=[blk(0), blk(nh), blk(2 * nh), blk(3 * nh),
                  pl.BlockSpec((1, t_len, LANES), lambda b, p: (b, 0, 0)),
                  cw(0), cw(nh), cw(2 * nh),
                  pl.BlockSpec((SUBLANES, LANES), lambda b, p: (0, 0)),
                  pl.BlockSpec((1, LANES), lambda b, p: (0, 0))],
        out_specs=[pl.BlockSpec((1, t_len, hw), lambda b, p: (b, 0, p)),
                   pl.BlockSpec((1, hps, GDN_DK, GDN_DV), lambda b, p: (b, p, 0, 0))],
        out_shape=[jax.ShapeDtypeStruct((bsz, t_len, MIX_A), BF16),
                   jax.ShapeDtypeStruct((bsz, nh, GDN_DK, GDN_DV), F32)],
        scratch_shapes=[gate_scr, gate_scr, gate_scr, gate_scr, gate_scr,
                        pltpu.VMEM((n_u, c, GDN_DV), F32), pltpu.VMEM((n_u, GDN_DK, GDN_DV), F32),
                        pltpu.VMEM((n_u, 2 * c, GDN_DK), BF16), pltpu.VMEM((n_u, c, GDN_DK + c), BF16),
                        pltpu.VMEM((n_u, GDN_DK + c, GDN_DV), BF16)] + work,
        compiler_params=_cparams(("parallel", "arbitrary")),
        name="gdn_prompt",
    )(proj3, proj3, proj3, proj3, ab3, conv_w, conv_w, conv_w, gate_par, gdn_norm.reshape(1, LANES))


def _gdn_sample_kernel(x_ref, ab_ref, hist_ref, cw_ref, gp_ref, gn_ref, s0_ref, o_ref, s_ref,
                       *, t_len):
    nh = GDN_HEADS
    g8 = SUBLANES
    n = nh * g8
    assert t_len + CONV_W - 1 <= g8
    cw = cw_ref[...]
    gp = gp_ref[...]
    gn = gn_ref[...]
    row8 = lax.broadcasted_iota(jnp.int32, (g8, QKV_W), 0)
    tpos = lax.broadcasted_iota(jnp.int32, (n, 1), 0) & (g8 - 1)
    rhead = lax.broadcasted_iota(jnp.int32, (n, GDN_DV), 0) >> 3
    ri = lax.broadcasted_iota(jnp.int32, (n, n), 0)
    ci = lax.broadcasted_iota(jnp.int32, (n, n), 1)
    tri = jnp.where((ri >> 3) == (ci >> 3), ri - ci, -1)
    eye = jnp.where(ri == ci, 1.0, 0.0)
    sel0 = jnp.where(lax.broadcasted_iota(jnp.int32, (n, LANES), 1) == 0, 1.0, 0.0).astype(BF16)
    valid = tpos < t_len
    a_neg = jnp.concatenate([jnp.broadcast_to(-jnp.exp(gp[0:1, h:h + 1]), (g8, 1)) for h in range(nh)], 0)
    dt_b = jnp.concatenate([jnp.broadcast_to(gp[1:2, h:h + 1], (g8, 1)) for h in range(nh)], 0)

    def l2n(x):
        return x * lax.rsqrt(jnp.sum(x * x, axis=-1, keepdims=True) + EPS)

    def operands(b):
        xb = x_ref[b]
        x = xb[:, :QKV_W].astype(F32)
        hst = hist_ref[b]
        y = x * cw[CONV_W - 1:CONV_W]
        for s in range(1, CONV_W):
            y = y + pltpu.roll(x, s, axis=0) * cw[CONV_W - 1 - s:CONV_W - s]
        for j in range(CONV_W - 1):
            hj = hst if j == 0 else pltpu.roll(hst, g8 - j, axis=0)
            y = y + jnp.where(row8 + j < CONV_W - 1, hj, 0.0) * cw[j:j + 1]
        qkv = jnp.where(row8 < t_len, _silu(y), 0.0)

        def stack(off):
            return jnp.concatenate([qkv[:, off + h * GDN_DK:off + (h + 1) * GDN_DK] for h in range(nh)], 0)

        q = l2n(stack(0)) * (GDN_DK ** -0.5)
        k = l2n(stack(QK_W))
        v = stack(2 * QK_W)
        ab = ab_ref[b]
        alpha = jnp.concatenate([ab[:, h:h + 1] for h in range(nh)], 0)
        braw = jnp.concatenate([ab[:, nh + h:nh + h + 1] for h in range(nh)], 0)
        g = jnp.where(valid, a_neg * _softplus(alpha + dt_b), 0.0)
        beta = jnp.where(valid, jax.nn.sigmoid(braw), 0.0)
        gc = g
        s = 1
        while s < g8:
            gc = gc + jnp.where(tpos >= s, pltpu.roll(gc, s, axis=0), 0.0)
            s *= 2
        g_last = jnp.concatenate([jnp.broadcast_to(gc[h * g8 + g8 - 1:(h + 1) * g8, :], (g8, 1))
                                  for h in range(nh)], 0)
        egc = jnp.exp(gc)
        kb = k * beta
        return dict(gc=gc, g_last=g_last, kbf=k.astype(BF16), kbb=kb.astype(BF16), qb=q.astype(BF16),
                    vb=(v * beta).astype(BF16), kg=(kb * egc).astype(BF16), qg=q * egc,
                    k_dec=(k * jnp.exp(g_last - gc)).astype(BF16))

    def gram(st):
        hi, mid, lo = _split3(jnp.broadcast_to(st["gc"], (n, LANES)))
        grow = _dot_nt(sel0, hi) + _dot_nt(sel0, mid) + _dot_nt(sel0, lo)
        d_incl = jnp.where(tri >= 0, jnp.exp(jnp.where(tri >= 0, st["gc"] - grow, 0.0)), 0.0)
        st["low"] = _dot_nt(st["kbb"], st["kbf"]) * jnp.where(tri > 0, d_incl, 0.0)
        st["a_in"] = (_dot_nt(st["qb"], st["kbf"]) * d_incl).astype(BF16)
        st["t_inv"] = eye - st["low"]
        st["p"] = st["low"].astype(BF16)

    def neumann(st):
        p = _dot(st["p"], st["p"])
        st["p"] = p.astype(BF16)
        st["t_inv"] = st["t_inv"] + _dot(st["t_inv"].astype(BF16), st["p"])

    def solve(st):
        tb = st["t_inv"].astype(BF16)
        st["u"] = _dot(tb, st["vb"])
        st["w"] = _dot(tb, st["kg"])

    def apply_state(st, b):
        ws, qs = [], []
        for h in range(nh):
            hs = slice(h * g8, (h + 1) * g8)
            lhs = jnp.concatenate([st["w"][hs], st["qg"][hs]], 0).astype(BF16)
            r = _dot(lhs, s0_ref[b, h].astype(BF16))
            ws.append(r[:g8])
            qs.append(r[g8:])
        st["v_new"] = st["u"] - jnp.concatenate(ws, 0)
        st["oq"] = jnp.concatenate(qs, 0)

    def finish(st, b):
        v_new = st["v_new"]
        o = st["oq"] + _dot(st["a_in"], v_new.astype(BF16))
        for h in range(nh):
            vm = jnp.where(rhead == h, v_new, 0.0).astype(BF16)
            dec = jnp.exp(st["g_last"][h * g8:h * g8 + 1, :])
            s_ref[b, h] = s0_ref[b, h] * dec + _dot_tn(st["k_dec"], vm)
        z = jnp.concatenate([x_ref[b][:, 3 * QK_W + h * GDN_DV:3 * QK_W + (h + 1) * GDN_DV].astype(F32)
                             for h in range(nh)], 0)
        o_ref[b] = (_rms(o, gn) * _silu(z)).astype(o_ref.dtype)

    bs_ = range(x_ref.shape[0])
    sts = [operands(b) for b in bs_]
    for st in sts:
        gram(st)
    covered = 2
    while covered < t_len:
        for st in sts:
            neumann(st)
        covered *= 2
    for st in sts:
        solve(st)
    for b in bs_:
        apply_state(sts[b], b)
    for b in bs_:
        finish(sts[b], b)


def _gdn_sample(proj3, ab3, hist, conv_w, gate_par, gdn_norm, s0, *, bb):
    bsz, t_len, _ = proj3.shape
    nh, g8 = GDN_HEADS, SUBLANES
    pad_t = lambda a, rows: jnp.pad(a, ((0, 0), (0, g8 - rows), (0, 0)))
    x8 = pad_t(proj3[:, :, :4 * QK_W], t_len)
    ab8 = pad_t(ab3, t_len)
    hist8 = pad_t(hist, CONV_W - 1)
    o, s_new = pl.pallas_call(
        functools.partial(_gdn_sample_kernel, t_len=t_len),
        grid=(bsz // bb,),
        in_specs=[pl.BlockSpec((bb, g8, 4 * QK_W), lambda i: (i, 0, 0)),
                  pl.BlockSpec((bb, g8, LANES), lambda i: (i, 0, 0)),
                  pl.BlockSpec((bb, g8, QKV_W), lambda i: (i, 0, 0)),
                  pl.BlockSpec((CONV_W, QKV_W), lambda i: (0, 0)),
                  pl.BlockSpec((SUBLANES, LANES), lambda i: (0, 0)),
                  pl.BlockSpec((1, LANES), lambda i: (0, 0)),
                  pl.BlockSpec((bb, nh, GDN_DK, GDN_DV), lambda i: (i, 0, 0, 0))],
        out_specs=[pl.BlockSpec((bb, nh * g8, GDN_DV), lambda i: (i, 0, 0)),
                   pl.BlockSpec((bb, nh, GDN_DK, GDN_DV), lambda i: (i, 0, 0, 0))],
        out_shape=[jax.ShapeDtypeStruct((bsz, nh * g8, GDN_DV), BF16),
                   jax.ShapeDtypeStruct((bsz, nh, GDN_DK, GDN_DV), F32)],
        compiler_params=_cparams(("parallel",)),
        name="gdn_sample",
    )(x8, ab8, hist8, conv_w, gate_par, gdn_norm.reshape(1, LANES), s0)
    o = o.reshape(bsz, nh, g8, GDN_DV)[:, :, :t_len].transpose(0, 2, 1, 3)
    return o.reshape(bsz, t_len, MIX_A), s_new


def _pool_finish(d_groups, pw_ref, ps_ref, zb):
    outs = [_dot(d.astype(BF16), pw_ref[gi]) for gi, d in enumerate(d_groups)]
    return jnp.concatenate(outs, axis=-1) * ps_ref[...] * _silu(zb)


def _pool_tile(u, halo, zb, pos, pw_ref, ps_ref):
    ext = jnp.concatenate([halo, u], axis=0)
    d_groups = []
    for gi, win in enumerate(POOL_WINDOWS):
        sl = slice(gi * POOL_CH, (gi + 1) * POOL_CH)
        lvl = ext[:, sl]
        s = 1
        while s < win:
            lvl = lvl + pltpu.roll(lvl, s, axis=0)
            s *= 2
        cnt = jnp.minimum(win, pos + 1).astype(F32)
        d_groups.append(lvl[POOL_HALO:, :] / cnt - u[:, sl])
    return _pool_finish(d_groups, pw_ref, ps_ref, zb)


def _mix_pool_kernel(x_ref, oa_ref, u_ref, halo_ref, zb_ref, pw_ref, ps_ref, wo_ref, nc_ref, wq_ref,
                     x1_ref, qx_ref, *, pos0):
    i = pl.program_id(1)
    tt = u_ref.shape[1]
    halo = jnp.where(i > 0, halo_ref[0].astype(F32), 0.0)
    pos = pos0 + i * tt + lax.broadcasted_iota(jnp.int32, (tt, 1), 0)
    ob = _pool_tile(u_ref[0].astype(F32), halo, zb_ref[0].astype(F32), pos, pw_ref, ps_ref)
    acc = _dot(oa_ref[0], wo_ref[0:MIX_A, :]) + _dot(ob.astype(BF16), wo_ref[MIX_A:D_MODEL, :])
    x1 = x_ref[0] + acc
    x1_ref[0] = x1
    qx_ref[0] = _dot(_rms(x1, nc_ref[...]).astype(BF16), wq_ref[...]).astype(qx_ref.dtype)


def _mix_pool(x3, oa3, proj3, pool_w, pool_scale, w_out, norm_cross, w_cq, *, tt, pos0):
    bsz, t_len, d = x3.shape
    ub, zb = COL_ZA // MIX_B + 1, COL_ZA // MIX_B + 2
    hb = tt // POOL_HALO
    const2 = lambda b, i: (0, 0)
    resident = dict(pipeline_mode=pl.Buffered(1))
    row = lambda w: pl.BlockSpec((1, tt, w), lambda b, i: (b, i, 0))
    return pl.pallas_call(
        functools.partial(_mix_pool_kernel, pos0=pos0),
        grid=(bsz, t_len // tt),
        in_specs=[row(d), row(MIX_A),
                  pl.BlockSpec((1, tt, MIX_B), lambda b, i: (b, i, ub)),
                  pl.BlockSpec((1, POOL_HALO, MIX_B), lambda b, i: (b, jnp.maximum(i * hb - 1, 0), ub)),
                  pl.BlockSpec((1, tt, MIX_B), lambda b, i: (b, i, zb)),
                  pl.BlockSpec((POOL_GROUPS, POOL_CH, POOL_CH), lambda b, i: (0, 0, 0), **resident),
                  pl.BlockSpec((1, MIX_B), const2, **resident),
                  pl.BlockSpec((d, d), const2, **resident),
                  pl.BlockSpec((1, d), const2, **resident),
                  pl.BlockSpec((d, d), const2, **resident)],
        out_specs=[row(d), row(d)],
        out_shape=[jax.ShapeDtypeStruct((bsz, t_len, d), F32), jax.ShapeDtypeStruct((bsz, t_len, d), BF16)],
        compiler_params=_cparams(("parallel", "arbitrary")),
        name="mix_pool",
    )(x3, oa3, proj3, proj3, proj3, pool_w, pool_scale.reshape(1, MIX_B), w_out,
      norm_cross.reshape(1, d), w_cq)


def _pool_sample_kernel(ext_ref, zb_ref, pw_ref, ps_ref, o_ref, *, pos0):
    t_len = zb_ref.shape[0]
    for t in range(t_len):
        e = POOL_BUF + t
        d_groups = []
        for gi, win in enumerate(POOL_WINDOWS):
            sl = slice(gi * POOL_CH, (gi + 1) * POOL_CH)
            tot = ext_ref[e, :, sl]
            for j in range(1, win):
                tot = tot + ext_ref[e - j, :, sl]
            cnt = float(min(win, pos0 + t + 1))
            d_groups.append(tot / cnt - ext_ref[e, :, sl])
        o_ref[t] = _pool_finish(d_groups, pw_ref, ps_ref, zb_ref[t].astype(F32)).astype(o_ref.dtype)


def _pool_sample(ext_tm, zb_tm, pool_w, pool_scale, *, bb, pos0):
    t_len, bsz, _ = zb_tm.shape
    return pl.pallas_call(
        functools.partial(_pool_sample_kernel, pos0=pos0),
        grid=(bsz // bb,),
        in_specs=[pl.BlockSpec((POOL_BUF + t_len, bb, MIX_B), lambda i: (0, i, 0)),
                  pl.BlockSpec((t_len, bb, MIX_B), lambda i: (0, i, 0)),
                  pl.BlockSpec((POOL_GROUPS, POOL_CH, POOL_CH), lambda i: (0, 0, 0)),
                  pl.BlockSpec((1, MIX_B), lambda i: (0, 0))],
        out_specs=pl.BlockSpec((t_len, bb, MIX_B), lambda i: (0, i, 0)),
        out_shape=jax.ShapeDtypeStruct((t_len, bsz, MIX_B), BF16),
        compiler_params=_cparams(("parallel",)),
        name="pool_sample",
    )(ext_tm, zb_tm, pool_w, pool_scale.reshape(1, MIX_B))


def _mix_out_kernel(x_ref, oa_ref, ob_ref, wo_ref, nc_ref, wq_ref, x1_ref, qx_ref):
    acc = _dot(oa_ref[...], wo_ref[0:MIX_A, :]) + _dot(ob_ref[...], wo_ref[MIX_A:D_MODEL, :])
    x1 = x_ref[...] + acc
    x1_ref[...] = x1
    qx_ref[...] = _dot(_rms(x1, nc_ref[...]).astype(BF16), wq_ref[...]).astype(qx_ref.dtype)


def _mix_out(x, oa, ob, w_out, norm_cross, w_cq, *, tm):
    n, d = x.shape
    const = lambda i: (0, 0)
    return pl.pallas_call(
        _mix_out_kernel,
        grid=(n // tm,),
        in_specs=[pl.BlockSpec((tm, d), lambda i: (i, 0)),
                  pl.BlockSpec((tm, MIX_A), lambda i: (i, 0)),
                  pl.BlockSpec((tm, MIX_B), lambda i: (i, 0)),
                  pl.BlockSpec((d, d), const), pl.BlockSpec((1, d), const), pl.BlockSpec((d, d), const)],
        out_specs=[pl.BlockSpec((tm, d), lambda i: (i, 0)), pl.BlockSpec((tm, d), lambda i: (i, 0))],
        out_shape=[jax.ShapeDtypeStruct((n, d), F32), jax.ShapeDtypeStruct((n, d), BF16)],
        compiler_params=_cparams(("parallel",)),
        name="mix_out",
    )(x, oa, ob, w_out, norm_cross.reshape(1, d), w_cq)


def _xattn_out_kernel(q_ref, k_ref, v_ref, x1_ref, wco_ref, nf_ref, y_ref):
    q = q_ref[0]
    scale = X_HEAD_DIM ** -0.5
    sls = [slice(h * X_HEAD_DIM, (h + 1) * X_HEAD_DIM) for h in range(X_HEADS)]
    ss = [_dot_nt(q[:, sl], k_ref[0, :, sl].astype(BF16)) * scale for sl in sls]
    ps = []
    for s in ss:
        p = jnp.exp(s - jnp.max(s, axis=-1, keepdims=True))
        ps.append((p / jnp.sum(p, axis=-1, keepdims=True)).astype(BF16))
    ctx = jnp.concatenate([_dot(p, v_ref[0, :, sl].astype(BF16)).astype(BF16)
                           for p, sl in zip(ps, sls)], axis=-1)
    y_ref[0] = _rms(x1_ref[0] + _dot(ctx, wco_ref[...]), nf_ref[...])


def _xattn_out(qx3, mk3, mv3, x13, w_co, norm_final, *, tq):
    bsz, t_len, d = qx3.shape
    n_mem = mk3.shape[1]
    row = pl.BlockSpec((1, tq, d), lambda b, i: (b, i, 0))
    mem = pl.BlockSpec((1, n_mem, d), lambda b, i: (b, 0, 0))
    return pl.pallas_call(
        _xattn_out_kernel,
        grid=(bsz, t_len // tq),
        in_specs=[row, mem, mem, row,
                  pl.BlockSpec((d, d), lambda b, i: (0, 0), pipeline_mode=pl.Buffered(1)),
                  pl.BlockSpec((1, d), lambda b, i: (0, 0), pipeline_mode=pl.Buffered(1))],
        out_specs=row,
        out_shape=jax.ShapeDtypeStruct((bsz, t_len, d), F32),
        compiler_params=_cparams(("parallel", "arbitrary")),
        name="xattn_out",
    )(qx3, mk3, mv3, x13, w_co, norm_final.reshape(1, d))


def _xattn_native_stages(q_ref, k_ref, v_ref, o_ref, t_len):
    nj = X_HEAD_DIM // LANES
    grp = nj * X_HEADS
    th = t_len * X_HEADS
    scale = X_HEAD_DIM ** -0.5
    ncol = k_ref.shape[1]
    r = lax.broadcasted_iota(jnp.int32, (th, ncol), 0) & (X_HEADS - 1)
    c = lax.broadcasted_iota(jnp.int32, (th, ncol), 1) & (grp - 1)

    def probs(z):
        s = None
        for j in range(nj):
            zj = jnp.where(c == r + j * X_HEADS, z[j * th:(j + 1) * th, :], 0.0)
            if j:
                zj = pltpu.roll(zj, ncol - j * X_HEADS, axis=1)
            s = zj if s is None else s + zj
        sm = jnp.where(c == r, s * scale, -jnp.inf)
        p = jnp.exp(sm - jnp.max(sm, axis=1, keepdims=True))
        p = p / jnp.sum(p, axis=1, keepdims=True)
        return jnp.concatenate([p if j == 0 else pltpu.roll(p, j * X_HEADS, axis=1)
                                for j in range(nj)], axis=0).astype(BF16)

    bs_ = range(q_ref.shape[0])
    vals = {}

    def scores():
        vals["z"] = [_dot_nt(q_ref[b], k_ref[b].astype(BF16)) for b in bs_]

    def probabilities():
        vals["p"] = [probs(z) for z in vals["z"]]

    def context():
        for b in bs_:
            o_ref[b] = _dot(vals["p"][b], v_ref[b].astype(BF16)).astype(o_ref.dtype)

    return [scores, probabilities, context]


def _kv_rows(cache):
    bsz, n_mem, nh, dh = cache.shape
    nj = dh // LANES
    return cache.reshape(bsz, n_mem, nh, nj, LANES).transpose(0, 1, 3, 2, 4).reshape(
        bsz, n_mem * nj * nh, LANES)


def _attn_out_kernel(x1_ref, ctx_ref, wco_ref, nf_ref, y_ref):
    x2 = x1_ref[...] + _dot(ctx_ref[...], wco_ref[...])
    y_ref[...] = _rms(x2, nf_ref[...])


def _attn_out(x1, ctx, w_co, norm_final, *, tm):
    n, d = x1.shape
    const = lambda i: (0, 0)
    return pl.pallas_call(
        _attn_out_kernel,
        grid=(n // tm,),
        in_specs=[pl.BlockSpec((tm, d), lambda i: (i, 0)), pl.BlockSpec((tm, d), lambda i: (i, 0)),
                  pl.BlockSpec((d, d), const, pipeline_mode=pl.Buffered(1)),
                  pl.BlockSpec((1, d), const, pipeline_mode=pl.Buffered(1))],
        out_specs=pl.BlockSpec((tm, d), lambda i: (i, 0)),
        out_shape=jax.ShapeDtypeStruct((n, d), F32),
        compiler_params=_cparams(("parallel",)),
        name="attn_out",
    )(x1, ctx, w_co, norm_final.reshape(1, d))


def kernel(x_prompt, x_sample, mem_prompt, cache_mem_k, cache_mem_v, state_delta, state_conv,
           state_pool, norm_mix, w_in, conv_w, a_log, dt_bias, gdn_norm, pool_w, pool_scale,
           w_out, norm_mem, norm_cross, w_cq, w_ck, w_cv, w_co, norm_final):
    bp, tp, d = x_prompt.shape
    bs, ts, _ = x_sample.shape
    n_mem = mem_prompt.shape[1]

    w_main, w_gate = _repack_w_in(w_in[0].T, tk=2048, tn=1024)
    wo, wcq, wco = (w[0].astype(BF16) for w in (w_out, w_cq, w_co))
    pw = pool_w[0].astype(BF16)
    gate_par = jnp.zeros((SUBLANES, LANES), F32)
    gate_par = gate_par.at[0, :GDN_HEADS].set(a_log[0]).at[1, :GDN_HEADS].set(dt_bias[0])
    cw = conv_w[0]

    mem2d = mem_prompt.reshape(bp * n_mem, d)
    mk = _norm_proj(mem2d, norm_mem[0], w_ck[0], tm=bp * n_mem, tn=1024, out_dtype=F32)
    mv = _norm_proj(mem2d, norm_mem[0], w_cv[0], tm=bp * n_mem, tn=1024, out_dtype=F32)

    proj_s, ab_s = _norm_proj(x_sample.reshape(bs * ts, d), norm_mix[0], w_main, w_gate,
                              tm=bs * ts, tn=1024, out_dtype=BF16)
    proj_s3 = proj_s.reshape(bs, ts, W_MAIN)
    oa_s, delta_s = _gdn_sample(proj_s3, ab_s.reshape(bs, ts, LANES), state_conv[0], cw, gate_par,
                                gdn_norm[0], state_delta[0], bb=8)
    u_s = proj_s3[:, :, COL_ZA + MIX_A:COL_ZA + MIX_A + MIX_B].astype(F32)
    ext_s = jnp.concatenate([state_pool[0], u_s], axis=1)
    zb_tm = proj_s3[:, :, COL_ZA + MIX_A + MIX_B:].transpose(1, 0, 2)
    ob_s = _pool_sample(ext_s.transpose(1, 0, 2), zb_tm, pw, pool_scale[0], bb=32,
                        pos0=PAST_LEN).transpose(1, 0, 2)
    x1_s, qx_s = _mix_out(x_sample.reshape(bs * ts, d), oa_s.reshape(-1, MIX_A),
                          ob_s.reshape(-1, MIX_B), wo, norm_cross[0], wcq, tm=256)
    nj = X_HEAD_DIM // LANES
    q_rows = qx_s.reshape(bs, ts, X_HEADS, nj, LANES).transpose(0, 3, 1, 2, 4).reshape(
        bs, nj * ts * X_HEADS, LANES)

    proj_p, ab_p, ctx_rows = _norm_proj(
        x_prompt.reshape(bp * tp, d), norm_mix[0], w_main, w_gate, tm=1024, tn=768, out_dtype=BF16,
        attn=(q_rows, _kv_rows(cache_mem_k[0]), _kv_rows(cache_mem_v[0]), ts))
    proj_p3 = proj_p.reshape(bp, tp, W_MAIN)
    oa_p, delta_p = _gdn_prompt(proj_p3, ab_p.reshape(bp, tp, LANES), cw, gate_par, gdn_norm[0])
    x1_p, qx_p = _mix_pool(x_prompt, oa_p, proj_p3, pw, pool_scale[0], wo, norm_cross[0], wcq,
                           tt=512, pos0=0)
    y_p = _xattn_out(qx_p, mk.reshape(bp, n_mem, d), mv.reshape(bp, n_mem, d), x1_p, wco,
                     norm_final, tq=512)
    conv_p = proj_p3[:, tp - (CONV_W - 1):, :QKV_W].astype(F32)
    pool_p = proj_p3[:, tp - POOL_BUF:, COL_ZA + MIX_A:COL_ZA + MIX_A + MIX_B].astype(F32)

    ctx_s = ctx_rows.reshape(bs, nj, ts, X_HEADS, LANES).transpose(0, 2, 3, 1, 4).reshape(bs * ts, d)
    y_s = _attn_out(x1_s, ctx_s, wco, norm_final, tm=256).reshape(bs, ts, d)
    conv_s = jnp.concatenate([state_conv[0], proj_s3[:, :, :QKV_W].astype(F32)], axis=1)[:, ts:]
    pool_s = ext_s[:, ts:]

    hd = (X_HEADS, X_HEAD_DIM)
    return (y_p, y_s, mk.reshape(1, bp, n_mem, *hd), mv.reshape(1, bp, n_mem, *hd),
            delta_p[None], conv_p[None], pool_p[None], delta_s[None], conv_s[None], pool_s[None])
```

```python
import functools

import jax
import jax.numpy as jnp
from jax import lax
from jax.experimental import pallas as pl
from jax.experimental.pallas import tpu as pltpu

F32 = jnp.float32
BF16 = jnp.bfloat16

D_MODEL = 2048
MIX_A = D_MODEL // 2
MIX_B = D_MODEL - MIX_A
GDN_HEADS = 8
GDN_DK = MIX_A // GDN_HEADS
GDN_DV = MIX_A // GDN_HEADS
QK_W = GDN_HEADS * GDN_DK
QKV_W = 2 * QK_W + GDN_HEADS * GDN_DV
CONV_W = 4
POOL_WINDOWS = (2, 4, 8, 16)
POOL_GROUPS = len(POOL_WINDOWS)
POOL_CH = MIX_B // POOL_GROUPS
POOL_BUF = max(POOL_WINDOWS) - 1
X_HEADS = 4
X_HEAD_DIM = D_MODEL // X_HEADS
PAST_LEN = 16384
EPS = 1e-6
COL_ZA = QKV_W
COL_A = COL_ZA + GDN_HEADS * GDN_DV
COL_B = COL_A + GDN_HEADS
COL_U = COL_B + GDN_HEADS
COL_ZB = COL_U + MIX_B
IN_COLS = COL_ZB + MIX_B

W_MAIN = IN_COLS - 2 * GDN_HEADS
LANES = 128
SUBLANES = 8
BF16_ROWS = 16
GDN_CHUNK = 128
POOL_HALO = 16
VMEM_LIMIT = 56 * 1024 * 1024


def _cparams(sem):
    return pltpu.CompilerParams(dimension_semantics=sem, vmem_limit_bytes=VMEM_LIMIT)


def _dot(a, b):
    return jnp.dot(a, b, preferred_element_type=F32)


def _dot_nt(a, b):
    return lax.dot_general(a, b, (((1,), (1,)), ((), ())), preferred_element_type=F32)


def _dot_tn(a, b):
    return lax.dot_general(a, b, (((0,), (0,)), ((), ())), preferred_element_type=F32)


def _rms(x, g):
    return x * lax.rsqrt(jnp.mean(x * x, axis=-1, keepdims=True) + EPS) * g


def _silu(x):
    return x * jax.nn.sigmoid(x)


def _softplus(x):
    return jnp.maximum(x, 0.0) + jnp.log1p(jnp.exp(-jnp.abs(x)))


def _repack_kernel(wt_ref, gt_ref, main_ref, gate_ref):
    main_ref[...] = wt_ref[...].T.astype(main_ref.dtype)

    @pl.when(pl.program_id(1) == 0)
    def _():
        head = gt_ref[...].T
        lane = lax.broadcasted_iota(jnp.int32, head.shape, 1)
        gate_ref[...] = jnp.where(lane < COL_U - COL_A, head, 0.0).astype(gate_ref.dtype)


def _repack_w_in(wt, *, tk, tn):
    k_dim = wt.shape[1]
    n_gate = COL_U - COL_A
    src_row = lambda j: pl.multiple_of(j * tn + (j // (COL_A // tn)) * n_gate, n_gate)
    return pl.pallas_call(
        _repack_kernel,
        grid=(k_dim // tk, W_MAIN // tn),
        in_specs=[pl.BlockSpec((pl.Element(tn), pl.Element(tk)), lambda i, j: (src_row(j), i * tk)),
                  pl.BlockSpec((pl.Element(LANES), pl.Element(tk)), lambda i, j: (COL_A, i * tk))],
        out_specs=[pl.BlockSpec((tk, tn), lambda i, j: (i, j)),
                   pl.BlockSpec((tk, LANES), lambda i, j: (i, 0))],
        out_shape=[jax.ShapeDtypeStruct((k_dim, W_MAIN), BF16),
                   jax.ShapeDtypeStruct((k_dim, LANES), BF16)],
        compiler_params=_cparams(("parallel", "arbitrary")),
        name="repack_w_in",
    )(wt, wt)


def _norm_proj_kernel(x_ref, g_ref, w_ref, *rest, with_side, attn_t):
    rest = list(rest)
    ws_ref = rest.pop(0) if with_side else None
    xq_ref, xk_ref, xv_ref = (rest.pop(0), rest.pop(0), rest.pop(0)) if attn_t else (None,) * 3
    out_ref = rest.pop(0)
    side_ref = rest.pop(0) if with_side else None
    xo_ref = rest.pop(0) if attn_t else None
    h_scr = rest.pop(0)

    @pl.when(pl.program_id(1) == 0)
    def _():
        h = _rms(x_ref[...], g_ref[...]).astype(BF16)
        h_scr[...] = h
        if with_side:
            side_ref[...] = _dot(h, ws_ref[...])

    stages = _xattn_native_stages(xq_ref, xk_ref, xv_ref, xo_ref, attn_t) if attn_t else []
    for stage in stages[:1]:
        stage()
    out_ref[...] = _dot(h_scr[...], w_ref[...].astype(BF16)).astype(out_ref.dtype)
    for stage in stages[1:]:
        stage()


def _norm_proj(x, g, w, w_side=None, *, tm, tn, out_dtype, attn=None):
    n, d = x.shape
    ncol = w.shape[1]
    with_side = w_side is not None
    in_specs = [pl.BlockSpec((tm, d), lambda i, j: (i, 0)),
                pl.BlockSpec((1, d), lambda i, j: (0, 0)),
                pl.BlockSpec((d, tn), lambda i, j: (0, j))]
    out_specs = [pl.BlockSpec((tm, tn), lambda i, j: (i, j))]
    out_shape = [jax.ShapeDtypeStruct((n, ncol), out_dtype)]
    args = [x, g.reshape(1, d), w]
    if with_side:
        in_specs.append(pl.BlockSpec((d, LANES), lambda i, j: (0, 0)))
        out_specs.append(pl.BlockSpec((tm, LANES), lambda i, j: (i, 0)))
        out_shape.append(jax.ShapeDtypeStruct((n, LANES), F32))
        args.append(w_side)
    if attn is not None:
        xq, xk, xv, attn_t = attn
        n_j = ncol // tn
        steps = (n // tm) * n_j
        assert xq.shape[0] % steps == 0
        xbb = xq.shape[0] // steps
        share = lambda rows: pl.BlockSpec((xbb, rows, LANES), lambda i, j: (i * n_j + j, 0, 0))
        in_specs += [share(xq.shape[1]), share(xk.shape[1]), share(xv.shape[1])]
        out_specs.append(share(xq.shape[1]))
        out_shape.append(jax.ShapeDtypeStruct(xq.shape, BF16))
        args += [xq, xk, xv]
    res = pl.pallas_call(
        functools.partial(_norm_proj_kernel, with_side=with_side,
                          attn_t=attn[3] if attn is not None else 0),
        grid=(n // tm, ncol // tn),
        in_specs=in_specs, out_specs=out_specs, out_shape=out_shape,
        scratch_shapes=[pltpu.VMEM((tm, d), BF16)],
        compiler_params=_cparams(("parallel", "arbitrary")),
        name=("norm_proj_attn" if attn is not None else "norm_proj_side") if with_side else "norm_proj",
    )(*args)
    return res if len(res) > 1 else res[0]


INV_BASE_LOG2 = 3
GATE_LANES = 2 * GDN_HEADS
GATE_PACK = LANES // GATE_LANES
_GDN_WORK = (("k", 1, BF16, 0), ("kb", 1, BF16, 0), ("q", 1, BF16, 0), ("kd", 1, BF16, 0),
             ("vk", 2, BF16, 0), ("dinc", 1, F32, 0), ("low", 1, F32, 0), ("x", 1, F32, 0),
             ("p", 1, BF16, 0), ("wu", 2, BF16, 0),
             ("cq", 1, F32, SUBLANES), ("ck", 1, F32, SUBLANES), ("cv", 1, F32, SUBLANES))


def _split3(x):
    hi = x.astype(BF16)
    r1 = x - hi.astype(F32)
    mid = r1.astype(BF16)
    lo = (r1 - mid.astype(F32)).astype(BF16)
    return hi, mid, lo


def _gdn_prompt_kernel(q_ref, k_ref, v_ref, z_ref, ab_ref, cwq_ref, cwk_ref, cwv_ref, gp_ref,
                       gn_ref, o_ref, s_ref,
                       gc_s, gct_s, egc_s, ekd_s, beta_s, u_s, n_s, pw_s, qa_s, sv_s, *work_refs, group):
    nw = len(_GDN_WORK)
    work = [{spec[0]: ref for spec, ref in zip(_GDN_WORK, work_refs[j * nw:(j + 1) * nw])}
            for j in range(len(work_refs) // nw)]
    t_len = q_ref.shape[1]
    c = GDN_CHUNK
    n_c = t_len // c
    hps = q_ref.shape[2] // GDN_DK
    head_of = lambda u: pl.program_id(1) * hps + u // n_c
    lanes_of = lambda u: slice((u // n_c) * GDN_DK, (u // n_c + 1) * GDN_DK)
    row = lax.broadcasted_iota(jnp.int32, (c, c), 0)
    col = lax.broadcasted_iota(jnp.int32, (c, c), 1)

    @pl.when(pl.program_id(1) == 0)
    def _():
        gp = gp_ref[...]
        slot = col[0:1] >> (GATE_LANES.bit_length() - 1)
        par = gp
        for j in range(1, GATE_PACK):
            par = par + jnp.where(slot == j, pltpu.roll(gp, GATE_LANES * j, axis=1), 0.0)
        a_neg = -jnp.exp(par[0:1])
        dt_b = par[1:2]
        tri = jnp.where(row >= col, 1.0, 0.0).astype(BF16)
        for grp in range(n_c // GATE_PACK):
            ab = ab_ref[0, pl.ds(grp * GATE_PACK * c, c), :]
            for j in range(1, GATE_PACK):
                nxt = ab_ref[0, pl.ds((grp * GATE_PACK + j) * c, c), :]
                ab = jnp.where(slot == j, pltpu.roll(nxt, GATE_LANES * j, axis=1), ab)
            hi, mid, lo = _split3(a_neg * _softplus(ab + dt_b))
            gc = _dot(tri, hi) + _dot(tri, mid) + _dot(tri, lo)
            rows = pl.ds(grp * c, c)
            gc_s[rows, :] = gc
            gct_s[rows, :] = gc.T
            egc_s[rows, :] = jnp.exp(gc)
            ekd_s[rows, :] = jnp.exp(gc[c - 1:c, :] - gc)
            beta_s[rows, :] = jax.nn.sigmoid(ab)

    def l2n(x):
        return x * lax.rsqrt(jnp.sum(x * x, axis=-1, keepdims=True) + EPS)

    def operands(w, u):
        ci, hl, h = u % n_c, lanes_of(u), head_of(u)
        r0 = ci * c
        rows = pl.ds(r0, c)

        def conv_silu(x_ref, cw_ref, stage):
            cur = x_ref[0, rows, hl].astype(F32)
            if ci > 0:
                tail = x_ref[0, pl.ds(r0 - BF16_ROWS, BF16_ROWS), hl].astype(F32)[BF16_ROWS - SUBLANES:]
            else:
                tail = jnp.zeros((SUBLANES, cur.shape[1]), F32)
            stage[0:SUBLANES, :] = tail
            stage[SUBLANES:SUBLANES + c, :] = cur
            cw = cw_ref[:, hl]
            y = cur * cw[CONV_W - 1:CONV_W]
            for s in range(1, CONV_W):
                y = y + stage[SUBLANES - s:SUBLANES - s + c, :] * cw[CONV_W - 1 - s:CONV_W - s]
            return _silu(y)

        g_rows = pl.ds((ci // GATE_PACK) * c, c)
        g_lane = (ci % GATE_PACK) * GATE_LANES + h

        def column(scr, idx):
            return jnp.sum(jnp.where(col == idx, scr[g_rows, :], 0.0), axis=-1, keepdims=True)

        q = l2n(conv_silu(q_ref, cwq_ref, w["cq"])) * (GDN_DK ** -0.5)
        k = l2n(conv_silu(k_ref, cwk_ref, w["ck"]))
        v = conv_silu(v_ref, cwv_ref, w["cv"])
        gcc = column(gc_s, g_lane)
        egc = column(egc_s, g_lane)
        ekd = column(ekd_s, g_lane)
        beta = column(beta_s, g_lane + GDN_HEADS)
        grow = gct_s[pl.ds((ci // GATE_PACK) * c + g_lane, 1), :]
        tri_i = row >= col
        w["dinc"][...] = jnp.where(tri_i, jnp.exp(jnp.where(tri_i, gcc - grow, 0.0)), 0.0)
        kb = k * beta
        w["k"][...] = k.astype(BF16)
        w["kb"][...] = kb.astype(BF16)
        w["q"][...] = q.astype(BF16)
        w["kd"][...] = (k * ekd).astype(BF16)
        w["vk"][:, :GDN_DV] = (v * beta).astype(BF16)
        w["vk"][:, GDN_DV:] = (kb * egc).astype(BF16)
        qa_s[u, :, 0:GDN_DK] = (q * egc).astype(BF16)

    def gram(w, u):
        r = _dot_nt(jnp.concatenate([w["kb"][...], w["q"][...]], axis=0), w["k"][...])
        d_incl = w["dinc"][...]
        low = r[:c] * jnp.where(row > col, d_incl, 0.0)
        qa_s[u, :, GDN_DK:] = (r[c:] * d_incl).astype(BF16)
        w["low"][...] = low
        ld = jnp.where((row >> INV_BASE_LOG2) == (col >> INV_BASE_LOG2), low, 0.0)
        w["x"][...] = jnp.where(row == col, 1.0, 0.0) - ld
        w["p"][...] = ld.astype(BF16)

    def neumann(w, first, last):
        pb = w["p"][...]
        if first:
            w["p"][...] = _dot(pb, pb).astype(BF16)
        elif last:
            x = w["x"][...]
            w["x"][...] = x + _dot(x.astype(BF16), pb)
        else:
            x = w["x"][...]
            r = _dot(jnp.concatenate([x.astype(BF16), pb], axis=0), pb)
            w["x"][...] = x + r[:c]
            w["p"][...] = r[c:].astype(BF16)

    def merge_a(w, lg):
        pair = (row >> (lg + 1)) == (col >> (lg + 1))
        m = jnp.where(pair, (row >> lg) - (col >> lg), 0) > 0
        lm = jnp.where(m, w["low"][...], 0.0).astype(BF16)
        w["p"][...] = _dot(lm, w["x"][...].astype(BF16)).astype(BF16)

    def merge_b(w):
        x = w["x"][...]
        w["x"][...] = x - _dot(x.astype(BF16), w["p"][...])

    def solve(w, ci):
        uw = _dot(w["x"][...].astype(BF16), w["vk"][...])
        u_s[ci] = uw[:, :GDN_DV]
        uwb = uw.astype(BF16)
        w["wu"][...] = uwb
        pw_s[ci, c:2 * c, :] = uwb[:, GDN_DV:]

    def outer(w, ci):
        np_ = _dot_tn(w["kd"][...], w["wu"][...])
        n_s[ci] = np_[:, :GDN_DV]
        pw_s[ci, 0:c, :] = np_[:, GDN_DV:].astype(BF16)

    n_sq = INV_BASE_LOG2 - 1
    lgc = c.bit_length() - 1

    def prepare_stages(cis):
        over = lambda fn, *a: (lambda: [fn(work[ci], *[ci if x is cis else x for x in a])
                                        for ci in cis])
        stages = [over(operands, cis), over(gram, cis)]
        stages += [over(neumann, step == 0, step == n_sq) for step in range(n_sq + 1)]
        for lg in range(INV_BASE_LOG2, lgc):
            stages += [over(merge_a, lg), over(merge_b)]
        return stages + [over(solve, cis), over(outer, cis)]

    gn = gn_ref[...]
    lane1 = lax.broadcasted_iota(jnp.int32, (1, LANES), 1)

    def advance(u, s_mat):
        ci = u % n_c
        if ci == 0:
            s_mat = jnp.zeros((GDN_DK, GDN_DV), F32)
        sb = s_mat.astype(BF16)
        r = _dot(pw_s[u], sb)
        g_last = jnp.sum(jnp.where(lane1 == (ci % GATE_PACK) * GATE_LANES + head_of(u),
                                   gc_s[pl.ds((ci // GATE_PACK) * c + c - 1, 1), :], 0.0),
                         axis=-1, keepdims=True)
        sv_s[u, 0:c, :] = sb
        sv_s[u, c:2 * c, :] = (u_s[u] - r[c:2 * c]).astype(BF16)
        s_next = s_mat * jnp.exp(g_last) - r[0:c] + n_s[u]
        if ci == n_c - 1:
            s_ref[0, u // n_c] = s_next
        return s_next

    def emit(u):
        rows, hl = pl.ds((u % n_c) * c, c), lanes_of(u)
        o = _dot(qa_s[u], sv_s[u])
        zz = z_ref[0, rows, hl].astype(F32)
        o_ref[0, rows, hl] = (_rms(o, gn) * _silu(zz)).astype(o_ref.dtype)

    s_mat = None
    chain_todo, emit_todo = [], []
    for g in range(hps * n_c // group):
        us = list(range(g * group, (g + 1) * group))
        for stage in prepare_stages(us):
            stage()
            if chain_todo:
                emit_todo.append(chain_todo[0])
                s_mat = advance(chain_todo.pop(0), s_mat)
            elif emit_todo:
                emit(emit_todo.pop(0))
        chain_todo += us
    while chain_todo:
        if emit_todo:
            emit(emit_todo.pop(0))
        emit_todo.append(chain_todo[0])
        s_mat = advance(chain_todo.pop(0), s_mat)
    for u in emit_todo:
        emit(u)


def _gdn_prompt(proj3, ab3, conv_w, gate_par, gdn_norm):
    bsz, t_len, _ = proj3.shape
    nh = GDN_HEADS
    hps = 2
    hw = hps * GDN_DK
    blk = lambda off: pl.BlockSpec((1, t_len, hw), lambda b, p: (b, 0, off // hps + p))
    cw = lambda off: pl.BlockSpec((CONV_W, hw), lambda b, p: (0, off // hps + p))
    c = GDN_CHUNK
    n_c = t_len // c
    n_u = hps * n_c
    assert n_c % GATE_PACK == 0 and nh % hps == 0
    gate_scr = pltpu.VMEM((n_c // GATE_PACK * c, LANES), F32)
    group = 16
    work = [pltpu.VMEM((c + extra, wide * LANES), dt)
            for _ in range(n_u) for _, wide, dt, extra in _GDN_WORK]
    return pl.pallas_call(
        functools.partial(_gdn_prompt_kernel, group=group),
        grid=(bsz, nh // hps),
        in_specs=[blk(0), blk(nh), blk(2 * nh), blk(3 * nh),
                  pl.BlockSpec((1, t_len, LANES), lambda b, p: (b, 0, 0)),
                  cw(0), cw(nh), cw(2 * nh),
                  pl.BlockSpec((SUBLANES, LANES), lambda b, p: (0, 0)),
                  pl.BlockSpec((1, LANES), lambda b, p: (0, 0))],
        out_specs=[pl.BlockSpec((1, t_len, hw), lambda b, p: (b, 0, p)),
                   pl.BlockSpec((1, hps, GDN_DK, GDN_DV), lambda b, p: (b, p, 0, 0))],
        out_shape=[jax.ShapeDtypeStruct((bsz, t_len, MIX_A), BF16),
                   jax.ShapeDtypeStruct((bsz, nh, GDN_DK, GDN_DV), F32)],
        scratch_shapes=[gate_scr, gate_scr, gate_scr, gate_scr, gate_scr,
                        pltpu.VMEM((n_u, c, GDN_DV), F32), pltpu.VMEM((n_u, GDN_DK, GDN_DV), F32),
                        pltpu.VMEM((n_u, 2 * c, GDN_DK), BF16), pltpu.VMEM((n_u, c, GDN_DK + c), BF16),
                        pltpu.VMEM((n_u, GDN_DK + c, GDN_DV), BF16)] + work,
        compiler_params=_cparams(("parallel", "arbitrary")),
        name="gdn_prompt",
    )(proj3, proj3, proj3, proj3, ab3, conv_w, conv_w, conv_w, gate_par, gdn_norm.reshape(1, LANES))


def _gdn_sample_kernel(x_ref, ab_ref, hist_ref, cw_ref, gp_ref, gn_ref, s0_ref, o_ref, s_ref,
                       *, t_len):
    nh = GDN_HEADS
    g8 = SUBLANES
    n = nh * g8
    assert t_len + CONV_W - 1 <= g8
    cw = cw_ref[...]
    gp = gp_ref[...]
    gn = gn_ref[...]
    row8 = lax.broadcasted_iota(jnp.int32, (g8, QKV_W), 0)
    tpos = lax.broadcasted_iota(jnp.int32, (n, 1), 0) & (g8 - 1)
    rhead = lax.broadcasted_iota(jnp.int32, (n, GDN_DV), 0) >> 3
    ri = lax.broadcasted_iota(jnp.int32, (n, n), 0)
    ci = lax.broadcasted_iota(jnp.int32, (n, n), 1)
    tri = jnp.where((ri >> 3) == (ci >> 3), ri - ci, -1)
    eye = jnp.where(ri == ci, 1.0, 0.0)
    sel0 = jnp.where(lax.broadcasted_iota(jnp.int32, (n, LANES), 1) == 0, 1.0, 0.0).astype(BF16)
    valid = tpos < t_len
    a_neg = jnp.concatenate([jnp.broadcast_to(-jnp.exp(gp[0:1, h:h + 1]), (g8, 1)) for h in range(nh)], 0)
    dt_b = jnp.concatenate([jnp.broadcast_to(gp[1:2, h:h + 1], (g8, 1)) for h in range(nh)], 0)

    def l2n(x):
        return x * lax.rsqrt(jnp.sum(x * x, axis=-1, keepdims=True) + EPS)

    hist_w = [sum(jnp.where(row8 == t, cw[e - t:e - t + 1], 0.0) for t in range(e + 1))
              for e in range(CONV_W - 1)]

    def operands(b):
        xb = x_ref[b]
        x = xb[:, :QKV_W].astype(F32)
        y = x * cw[CONV_W - 1:CONV_W]
        for s in range(1, CONV_W):
            y = y + pltpu.roll(x, s, axis=0) * cw[CONV_W - 1 - s:CONV_W - s]
        for e in range(CONV_W - 1):
            y = y + hist_ref[e, b:b + 1, :] * hist_w[e]
        qkv = jnp.where(row8 < t_len, _silu(y), 0.0)

        def stack(off):
            return jnp.concatenate([qkv[:, off + h * GDN_DK:off + (h + 1) * GDN_DK] for h in range(nh)], 0)

        q = l2n(stack(0)) * (GDN_DK ** -0.5)
        k = l2n(stack(QK_W))
        v = stack(2 * QK_W)
        ab = ab_ref[b]
        alpha = jnp.concatenate([ab[:, h:h + 1] for h in range(nh)], 0)
        braw = jnp.concatenate([ab[:, nh + h:nh + h + 1] for h in range(nh)], 0)
        g = jnp.where(valid, a_neg * _softplus(alpha + dt_b), 0.0)
        beta = jnp.where(valid, jax.nn.sigmoid(braw), 0.0)
        gc = g
        s = 1
        while s < g8:
            gc = gc + jnp.where(tpos >= s, pltpu.roll(gc, s, axis=0), 0.0)
            s *= 2
        g_last = jnp.concatenate([jnp.broadcast_to(gc[h * g8 + g8 - 1:(h + 1) * g8, :], (g8, 1))
                                  for h in range(nh)], 0)
        egc = jnp.exp(gc)
        kb = k * beta
        return dict(gc=gc, g_last=g_last, kbf=k.astype(BF16), kbb=kb.astype(BF16), qb=q.astype(BF16),
                    vb=(v * beta).astype(BF16), kg=(kb * egc).astype(BF16), qg=q * egc,
                    k_dec=(k * jnp.exp(g_last - gc)).astype(BF16))

    def gram(st):
        hi, mid, lo = _split3(jnp.broadcast_to(st["gc"], (n, LANES)))
        grow = _dot_nt(sel0, hi) + _dot_nt(sel0, mid) + _dot_nt(sel0, lo)
        d_incl = jnp.where(tri >= 0, jnp.exp(jnp.where(tri >= 0, st["gc"] - grow, 0.0)), 0.0)
        st["low"] = _dot_nt(st["kbb"], st["kbf"]) * jnp.where(tri > 0, d_incl, 0.0)
        st["a_in"] = (_dot_nt(st["qb"], st["kbf"]) * d_incl).astype(BF16)
        st["t_inv"] = eye - st["low"]
        st["p"] = st["low"].astype(BF16)

    def neumann(st):
        p = _dot(st["p"], st["p"])
        st["p"] = p.astype(BF16)
        st["t_inv"] = st["t_inv"] + _dot(st["t_inv"].astype(BF16), st["p"])

    def solve(st):
        tb = st["t_inv"].astype(BF16)
        st["u"] = _dot(tb, st["vb"])
        st["w"] = _dot(tb, st["kg"])

    def apply_state(st, b):
        ws, qs = [], []
        for h in range(nh):
            hs = slice(h * g8, (h + 1) * g8)
            lhs = jnp.concatenate([st["w"][hs], st["qg"][hs]], 0).astype(BF16)
            r = _dot(lhs, s0_ref[b, h].astype(BF16))
            ws.append(r[:g8])
            qs.append(r[g8:])
        st["v_new"] = st["u"] - jnp.concatenate(ws, 0)
        st["oq"] = jnp.concatenate(qs, 0)

    def finish(st, b):
        v_new = st["v_new"]
        o = st["oq"] + _dot(st["a_in"], v_new.astype(BF16))
        for h in range(nh):
            vm = jnp.where(rhead == h, v_new, 0.0).astype(BF16)
            dec = jnp.exp(st["g_last"][h * g8:h * g8 + 1, :])
            s_ref[b, h] = s0_ref[b, h] * dec + _dot_tn(st["k_dec"], vm)
        z = jnp.concatenate([x_ref[b][:, 3 * QK_W + h * GDN_DV:3 * QK_W + (h + 1) * GDN_DV].astype(F32)
                             for h in range(nh)], 0)
        o_ref[b] = (_rms(o, gn) * _silu(z)).astype(o_ref.dtype)

    bs_ = range(x_ref.shape[0])
    sts = [operands(b) for b in bs_]
    for st in sts:
        gram(st)
    covered = 2
    while covered < t_len:
        for st in sts:
            neumann(st)
        covered *= 2
    for st in sts:
        solve(st)
    for b in bs_:
        apply_state(sts[b], b)
    for b in bs_:
        finish(sts[b], b)


def _gdn_sample(proj3, ab3, hist, conv_w, gate_par, gdn_norm, s0, *, bb):
    bsz, t_len, _ = proj3.shape
    nh, g8 = GDN_HEADS, SUBLANES
    pad_t = lambda a, rows: jnp.pad(a, ((0, 0), (0, g8 - rows), (0, 0)))
    x8 = pad_t(proj3[:, :, :4 * QK_W], t_len)
    ab8 = pad_t(ab3, t_len)
    hist_tm = hist.transpose(1, 0, 2)
    o, s_new = pl.pallas_call(
        functools.partial(_gdn_sample_kernel, t_len=t_len),
        grid=(bsz // bb,),
        in_specs=[pl.BlockSpec((bb, g8, 4 * QK_W), lambda i: (i, 0, 0)),
                  pl.BlockSpec((bb, g8, LANES), lambda i: (i, 0, 0)),
                  pl.BlockSpec((CONV_W - 1, bb, QKV_W), lambda i: (0, i, 0)),
                  pl.BlockSpec((CONV_W, QKV_W), lambda i: (0, 0)),
                  pl.BlockSpec((SUBLANES, LANES), lambda i: (0, 0)),
                  pl.BlockSpec((1, LANES), lambda i: (0, 0)),
                  pl.BlockSpec((bb, nh, GDN_DK, GDN_DV), lambda i: (i, 0, 0, 0))],
        out_specs=[pl.BlockSpec((bb, nh * g8, GDN_DV), lambda i: (i, 0, 0)),
                   pl.BlockSpec((bb, nh, GDN_DK, GDN_DV), lambda i: (i, 0, 0, 0))],
        out_shape=[jax.ShapeDtypeStruct((bsz, nh * g8, GDN_DV), BF16),
                   jax.ShapeDtypeStruct((bsz, nh, GDN_DK, GDN_DV), F32)],
        compiler_params=_cparams(("parallel",)),
        name="gdn_sample",
    )(x8, ab8, hist_tm, conv_w, gate_par, gdn_norm.reshape(1, LANES), s0)
    o = o.reshape(bsz, nh, g8, GDN_DV)[:, :, :t_len].transpose(0, 2, 1, 3)
    return o.reshape(bsz, t_len, MIX_A), s_new


def _pool_finish(d_groups, pw_ref, ps_ref, zb):
    outs = [_dot(d.astype(BF16), pw_ref[gi]) for gi, d in enumerate(d_groups)]
    return jnp.concatenate(outs, axis=-1) * ps_ref[...] * _silu(zb)


def _pool_tile(u, halo, zb, pos, pw_ref, ps_ref):
    ext = jnp.concatenate([halo, u], axis=0)
    d_groups = []
    for gi, win in enumerate(POOL_WINDOWS):
        sl = slice(gi * POOL_CH, (gi + 1) * POOL_CH)
        lvl = ext[:, sl]
        s = 1
        while s < win:
            lvl = lvl + pltpu.roll(lvl, s, axis=0)
            s *= 2
        cnt = jnp.minimum(win, pos + 1).astype(F32)
        d_groups.append(lvl[POOL_HALO:, :] / cnt - u[:, sl])
    return _pool_finish(d_groups, pw_ref, ps_ref, zb)


def _mix_pool_kernel(x_ref, oa_ref, u_ref, halo_ref, zb_ref, pw_ref, ps_ref, wo_ref, nc_ref, wq_ref,
                     x1_ref, qx_ref, *, pos0):
    i = pl.program_id(1)
    tt = u_ref.shape[1]
    halo = jnp.where(i > 0, halo_ref[0].astype(F32), 0.0)
    pos = pos0 + i * tt + lax.broadcasted_iota(jnp.int32, (tt, 1), 0)
    ob = _pool_tile(u_ref[0].astype(F32), halo, zb_ref[0].astype(F32), pos, pw_ref, ps_ref)
    acc = _dot(oa_ref[0], wo_ref[0:MIX_A, :]) + _dot(ob.astype(BF16), wo_ref[MIX_A:D_MODEL, :])
    x1 = x_ref[0] + acc
    x1_ref[0] = x1
    qx_ref[0] = _dot(_rms(x1, nc_ref[...]).astype(BF16), wq_ref[...]).astype(qx_ref.dtype)


def _mix_pool(x3, oa3, proj3, pool_w, pool_scale, w_out, norm_cross, w_cq, *, tt, pos0):
    bsz, t_len, d = x3.shape
    ub, zb = COL_ZA // MIX_B + 1, COL_ZA // MIX_B + 2
    hb = tt // POOL_HALO
    const2 = lambda b, i: (0, 0)
    resident = dict(pipeline_mode=pl.Buffered(1))
    row = lambda w: pl.BlockSpec((1, tt, w), lambda b, i: (b, i, 0))
    return pl.pallas_call(
        functools.partial(_mix_pool_kernel, pos0=pos0),
        grid=(bsz, t_len // tt),
        in_specs=[row(d), row(MIX_A),
                  pl.BlockSpec((1, tt, MIX_B), lambda b, i: (b, i, ub)),
                  pl.BlockSpec((1, POOL_HALO, MIX_B), lambda b, i: (b, jnp.maximum(i * hb - 1, 0), ub)),
                  pl.BlockSpec((1, tt, MIX_B), lambda b, i: (b, i, zb)),
                  pl.BlockSpec((POOL_GROUPS, POOL_CH, POOL_CH), lambda b, i: (0, 0, 0), **resident),
                  pl.BlockSpec((1, MIX_B), const2, **resident),
                  pl.BlockSpec((d, d), const2, **resident),
                  pl.BlockSpec((1, d), const2, **resident),
                  pl.BlockSpec((d, d), const2, **resident)],
        out_specs=[row(d), row(d)],
        out_shape=[jax.ShapeDtypeStruct((bsz, t_len, d), F32), jax.ShapeDtypeStruct((bsz, t_len, d), BF16)],
        compiler_params=_cparams(("parallel", "arbitrary")),
        name="mix_pool",
    )(x3, oa3, proj3, proj3, proj3, pool_w, pool_scale.reshape(1, MIX_B), w_out,
      norm_cross.reshape(1, d), w_cq)


def _pool_sample_kernel(hist_ref, u_ref, zb_ref, pw_ref, ps_ref, o_ref, *, pos0):
    t_len = zb_ref.shape[0]

    def ext(e, sl):
        return hist_ref[e, :, sl] if e < POOL_BUF else u_ref[e - POOL_BUF, :, sl].astype(F32)

    for t in range(t_len):
        e = POOL_BUF + t
        d_groups = []
        for gi, win in enumerate(POOL_WINDOWS):
            sl = slice(gi * POOL_CH, (gi + 1) * POOL_CH)
            tot = ext(e, sl)
            for j in range(1, win):
                tot = tot + ext(e - j, sl)
            cnt = float(min(win, pos0 + t + 1))
            d_groups.append(tot / cnt - ext(e, sl))
        o_ref[t] = _pool_finish(d_groups, pw_ref, ps_ref, zb_ref[t].astype(F32)).astype(o_ref.dtype)


def _pool_sample(hist_tm, u_tm, zb_tm, pool_w, pool_scale, *, bb, pos0):
    t_len, bsz, _ = zb_tm.shape
    return pl.pallas_call(
        functools.partial(_pool_sample_kernel, pos0=pos0),
        grid=(bsz // bb,),
        in_specs=[pl.BlockSpec((POOL_BUF, bb, MIX_B), lambda i: (0, i, 0)),
                  pl.BlockSpec((t_len, bb, MIX_B), lambda i: (0, i, 0)),
                  pl.BlockSpec((t_len, bb, MIX_B), lambda i: (0, i, 0)),
                  pl.BlockSpec((POOL_GROUPS, POOL_CH, POOL_CH), lambda i: (0, 0, 0)),
                  pl.BlockSpec((1, MIX_B), lambda i: (0, 0))],
        out_specs=pl.BlockSpec((t_len, bb, MIX_B), lambda i: (0, i, 0)),
        out_shape=jax.ShapeDtypeStruct((t_len, bsz, MIX_B), BF16),
        compiler_params=_cparams(("parallel",)),
        name="pool_sample",
    )(hist_tm, u_tm, zb_tm, pool_w, pool_scale.reshape(1, MIX_B))


def _mix_out_kernel(x_ref, oa_ref, ob_ref, wo_ref, nc_ref, wq_ref, x1_ref, qx_ref):
    acc = _dot(oa_ref[...], wo_ref[0:MIX_A, :]) + _dot(ob_ref[...], wo_ref[MIX_A:D_MODEL, :])
    x1 = x_ref[...] + acc
    x1_ref[...] = x1
    qx_ref[...] = _dot(_rms(x1, nc_ref[...]).astype(BF16), wq_ref[...]).astype(qx_ref.dtype)


def _mix_out(x, oa, ob, w_out, norm_cross, w_cq, *, tm):
    n, d = x.shape
    const = lambda i: (0, 0)
    return pl.pallas_call(
        _mix_out_kernel,
        grid=(n // tm,),
        in_specs=[pl.BlockSpec((tm, d), lambda i: (i, 0)),
                  pl.BlockSpec((tm, MIX_A), lambda i: (i, 0)),
                  pl.BlockSpec((tm, MIX_B), lambda i: (i, 0)),
                  pl.BlockSpec((d, d), const), pl.BlockSpec((1, d), const), pl.BlockSpec((d, d), const)],
        out_specs=[pl.BlockSpec((tm, d), lambda i: (i, 0)), pl.BlockSpec((tm, d), lambda i: (i, 0))],
        out_shape=[jax.ShapeDtypeStruct((n, d), F32), jax.ShapeDtypeStruct((n, d), BF16)],
        compiler_params=_cparams(("parallel",)),
        name="mix_out",
    )(x, oa, ob, w_out, norm_cross.reshape(1, d), w_cq)


def _xattn_out_kernel(q_ref, k_ref, v_ref, x1_ref, wco_ref, nf_ref, y_ref):
    q = q_ref[0]
    scale = X_HEAD_DIM ** -0.5
    sls = [slice(h * X_HEAD_DIM, (h + 1) * X_HEAD_DIM) for h in range(X_HEADS)]
    ss = [_dot_nt(q[:, sl], k_ref[0, :, sl].astype(BF16)) * scale for sl in sls]
    ps = []
    for s in ss:
        p = jnp.exp(s - jnp.max(s, axis=-1, keepdims=True))
        ps.append((p / jnp.sum(p, axis=-1, keepdims=True)).astype(BF16))
    ctx = jnp.concatenate([_dot(p, v_ref[0, :, sl].astype(BF16)).astype(BF16)
                           for p, sl in zip(ps, sls)], axis=-1)
    y_ref[0] = _rms(x1_ref[0] + _dot(ctx, wco_ref[...]), nf_ref[...])


def _xattn_out(qx3, mk3, mv3, x13, w_co, norm_final, *, tq):
    bsz, t_len, d = qx3.shape
    n_mem = mk3.shape[1]
    row = pl.BlockSpec((1, tq, d), lambda b, i: (b, i, 0))
    mem = pl.BlockSpec((1, n_mem, d), lambda b, i: (b, 0, 0))
    return pl.pallas_call(
        _xattn_out_kernel,
        grid=(bsz, t_len // tq),
        in_specs=[row, mem, mem, row,
                  pl.BlockSpec((d, d), lambda b, i: (0, 0), pipeline_mode=pl.Buffered(1)),
                  pl.BlockSpec((1, d), lambda b, i: (0, 0), pipeline_mode=pl.Buffered(1))],
        out_specs=row,
        out_shape=jax.ShapeDtypeStruct((bsz, t_len, d), F32),
        compiler_params=_cparams(("parallel", "arbitrary")),
        name="xattn_out",
    )(qx3, mk3, mv3, x13, w_co, norm_final.reshape(1, d))


def _xattn_native_stages(q_ref, k_ref, v_ref, o_ref, t_len):
    nj = X_HEAD_DIM // LANES
    grp = nj * X_HEADS
    th = t_len * X_HEADS
    scale = X_HEAD_DIM ** -0.5
    ncol = k_ref.shape[1]
    r = lax.broadcasted_iota(jnp.int32, (th, ncol), 0) & (X_HEADS - 1)
    c = lax.broadcasted_iota(jnp.int32, (th, ncol), 1) & (grp - 1)

    def probs(z):
        s = None
        for j in range(nj):
            zj = jnp.where(c == r + j * X_HEADS, z[j * th:(j + 1) * th, :], 0.0)
            if j:
                zj = pltpu.roll(zj, ncol - j * X_HEADS, axis=1)
            s = zj if s is None else s + zj
        sm = jnp.where(c == r, s * scale, -jnp.inf)
        p = jnp.exp(sm - jnp.max(sm, axis=1, keepdims=True))
        p = p / jnp.sum(p, axis=1, keepdims=True)
        return jnp.concatenate([p if j == 0 else pltpu.roll(p, j * X_HEADS, axis=1)
                                for j in range(nj)], axis=0).astype(BF16)

    bs_ = range(q_ref.shape[0])
    vals = {}

    def scores():
        vals["z"] = [_dot_nt(q_ref[b], k_ref[b].astype(BF16)) for b in bs_]

    def probabilities():
        vals["p"] = [probs(z) for z in vals["z"]]

    def context():
        for b in bs_:
            o_ref[b] = _dot(vals["p"][b], v_ref[b].astype(BF16)).astype(o_ref.dtype)

    return [scores, probabilities, context]


def _kv_rows(cache):
    bsz, n_mem, nh, dh = cache.shape
    nj = dh // LANES
    return cache.reshape(bsz, n_mem, nh, nj, LANES).transpose(0, 1, 3, 2, 4).reshape(
        bsz, n_mem * nj * nh, LANES)


def _attn_out_kernel(x1_ref, ctx_ref, wco_ref, nf_ref, y_ref):
    x2 = x1_ref[...] + _dot(ctx_ref[...], wco_ref[...])
    y_ref[...] = _rms(x2, nf_ref[...])


def _attn_out(x1, ctx, w_co, norm_final, *, tm):
    n, d = x1.shape
    const = lambda i: (0, 0)
    return pl.pallas_call(
        _attn_out_kernel,
        grid=(n // tm,),
        in_specs=[pl.BlockSpec((tm, d), lambda i: (i, 0)), pl.BlockSpec((tm, d), lambda i: (i, 0)),
                  pl.BlockSpec((d, d), const, pipeline_mode=pl.Buffered(1)),
                  pl.BlockSpec((1, d), const, pipeline_mode=pl.Buffered(1))],
        out_specs=pl.BlockSpec((tm, d), lambda i: (i, 0)),
        out_shape=jax.ShapeDtypeStruct((n, d), F32),
        compiler_params=_cparams(("parallel",)),
        name="attn_out",
    )(x1, ctx, w_co, norm_final.reshape(1, d))


def kernel(x_prompt, x_sample, mem_prompt, cache_mem_k, cache_mem_v, state_delta, state_conv,
           state_pool, norm_mix, w_in, conv_w, a_log, dt_bias, gdn_norm, pool_w, pool_scale,
           w_out, norm_mem, norm_cross, w_cq, w_ck, w_cv, w_co, norm_final):
    bp, tp, d = x_prompt.shape
    bs, ts, _ = x_sample.shape
    n_mem = mem_prompt.shape[1]

    w_main, w_gate = _repack_w_in(w_in[0].T, tk=2048, tn=1024)
    wo, wcq, wco = (w[0].astype(BF16) for w in (w_out, w_cq, w_co))
    pw = pool_w[0].astype(BF16)
    gate_par = jnp.zeros((SUBLANES, LANES), F32)
    gate_par = gate_par.at[0, :GDN_HEADS].set(a_log[0]).at[1, :GDN_HEADS].set(dt_bias[0])
    cw = conv_w[0]

    mem2d = mem_prompt.reshape(bp * n_mem, d)
    mk = _norm_proj(mem2d, norm_mem[0], w_ck[0], tm=bp * n_mem, tn=1024, out_dtype=F32)
    mv = _norm_proj(mem2d, norm_mem[0], w_cv[0], tm=bp * n_mem, tn=1024, out_dtype=F32)

    proj_s, ab_s = _norm_proj(x_sample.reshape(bs * ts, d), norm_mix[0], w_main, w_gate,
                              tm=bs * ts, tn=1024, out_dtype=BF16)
    proj_s3 = proj_s.reshape(bs, ts, W_MAIN)
    oa_s, delta_s = _gdn_sample(proj_s3, ab_s.reshape(bs, ts, LANES), state_conv[0], cw, gate_par,
                                gdn_norm[0], state_delta[0], bb=8)
    u_s = proj_s3[:, :, COL_ZA + MIX_A:COL_ZA + MIX_A + MIX_B]
    ext_s = jnp.concatenate([state_pool[0], u_s.astype(F32)], axis=1)
    zb_tm = proj_s3[:, :, COL_ZA + MIX_A + MIX_B:].transpose(1, 0, 2)
    ob_s = _pool_sample(state_pool[0].transpose(1, 0, 2), u_s.transpose(1, 0, 2), zb_tm, pw,
                        pool_scale[0], bb=32, pos0=PAST_LEN).transpose(1, 0, 2)
    x1_s, qx_s = _mix_out(x_sample.reshape(bs * ts, d), oa_s.reshape(-1, MIX_A),
                          ob_s.reshape(-1, MIX_B), wo, norm_cross[0], wcq, tm=256)
    nj = X_HEAD_DIM // LANES
    q_rows = qx_s.reshape(bs, ts, X_HEADS, nj, LANES).transpose(0, 3, 1, 2, 4).reshape(
        bs, nj * ts * X_HEADS, LANES)

    proj_p, ab_p, ctx_rows = _norm_proj(
        x_prompt.reshape(bp * tp, d), norm_mix[0], w_main, w_gate, tm=1024, tn=768, out_dtype=BF16,
        attn=(q_rows, _kv_rows(cache_mem_k[0]), _kv_rows(cache_mem_v[0]), ts))
    proj_p3 = proj_p.reshape(bp, tp, W_MAIN)
    oa_p, delta_p = _gdn_prompt(proj_p3, ab_p.reshape(bp, tp, LANES), cw, gate_par, gdn_norm[0])
    x1_p, qx_p = _mix_pool(x_prompt, oa_p, proj_p3, pw, pool_scale[0], wo, norm_cross[0], wcq,
                           tt=512, pos0=0)
    y_p = _xattn_out(qx_p, mk.reshape(bp, n_mem, d), mv.reshape(bp, n_mem, d), x1_p, wco,
                     norm_final, tq=512)
    conv_p = proj_p3[:, tp - (CONV_W - 1):, :QKV_W].astype(F32)
    pool_p = proj_p3[:, tp - POOL_BUF:, COL_ZA + MIX_A:COL_ZA + MIX_A + MIX_B].astype(F32)

    ctx_s = ctx_rows.reshape(bs, nj, ts, X_HEADS, LANES).transpose(0, 2, 3, 1, 4).reshape(bs * ts, d)
    y_s = _attn_out(x1_s, ctx_s, wco, norm_final, tm=256).reshape(bs, ts, d)
    conv_s = jnp.concatenate([state_conv[0], proj_s3[:, :, :QKV_W].astype(F32)], axis=1)[:, ts:]
    pool_s = ext_s[:, ts:]

    hd = (X_HEADS, X_HEAD_DIM)
    return (y_p, y_s, mk.reshape(1, bp, n_mem, *hd), mv.reshape(1, bp, n_mem, *hd),
            delta_p[None], conv_p[None], pool_p[None], delta_s[None], conv_s[None], pool_s[None])
```

```python
import functools

import jax
import jax.numpy as jnp
from jax import lax
from jax.experimental import pallas as pl
from jax.experimental.pallas import tpu as pltpu

F32 = jnp.float32
BF16 = jnp.bfloat16

D_MODEL = 2048
MIX_A = D_MODEL // 2
MIX_B = D_MODEL - MIX_A
GDN_HEADS = 8
GDN_DK = MIX_A // GDN_HEADS
GDN_DV = MIX_A // GDN_HEADS
QK_W = GDN_HEADS * GDN_DK
QKV_W = 2 * QK_W + GDN_HEADS * GDN_DV
CONV_W = 4
POOL_WINDOWS = (2, 4, 8, 16)
POOL_GROUPS = len(POOL_WINDOWS)
POOL_CH = MIX_B // POOL_GROUPS
POOL_BUF = max(POOL_WINDOWS) - 1
X_HEADS = 4
X_HEAD_DIM = D_MODEL // X_HEADS
PAST_LEN = 16384
EPS = 1e-6
COL_ZA = QKV_W
COL_A = COL_ZA + GDN_HEADS * GDN_DV
COL_B = COL_A + GDN_HEADS
COL_U = COL_B + GDN_HEADS
COL_ZB = COL_U + MIX_B
IN_COLS = COL_ZB + MIX_B

W_MAIN = IN_COLS - 2 * GDN_HEADS
LANES = 128
SUBLANES = 8
BF16_ROWS = 16
GDN_CHUNK = 128
POOL_HALO = 16
VMEM_LIMIT = 56 * 1024 * 1024


def _cparams(sem):
    return pltpu.CompilerParams(dimension_semantics=sem, vmem_limit_bytes=VMEM_LIMIT)


def _dot(a, b):
    return jnp.dot(a, b, preferred_element_type=F32)


def _dot_nt(a, b):
    return lax.dot_general(a, b, (((1,), (1,)), ((), ())), preferred_element_type=F32)


def _dot_tn(a, b):
    return lax.dot_general(a, b, (((0,), (0,)), ((), ())), preferred_element_type=F32)


def _rms(x, g):
    return x * lax.rsqrt(jnp.mean(x * x, axis=-1, keepdims=True) + EPS) * g


def _silu(x):
    return x * jax.nn.sigmoid(x)


def _softplus(x):
    return jnp.maximum(x, 0.0) + jnp.log1p(jnp.exp(-jnp.abs(x)))


def _repack_kernel(wt_ref, gt_ref, main_ref, gate_ref):
    main_ref[...] = wt_ref[...].T.astype(main_ref.dtype)

    @pl.when(pl.program_id(1) == 0)
    def _():
        head = gt_ref[...].T
        lane = lax.broadcasted_iota(jnp.int32, head.shape, 1)
        gate_ref[...] = jnp.where(lane < COL_U - COL_A, head, 0.0).astype(gate_ref.dtype)


def _repack_w_in(wt, *, tk, tn):
    k_dim = wt.shape[1]
    n_gate = COL_U - COL_A
    src_row = lambda j: pl.multiple_of(j * tn + (j // (COL_A // tn)) * n_gate, n_gate)
    return pl.pallas_call(
        _repack_kernel,
        grid=(k_dim // tk, W_MAIN // tn),
        in_specs=[pl.BlockSpec((pl.Element(tn), pl.Element(tk)), lambda i, j: (src_row(j), i * tk)),
                  pl.BlockSpec((pl.Element(LANES), pl.Element(tk)), lambda i, j: (COL_A, i * tk))],
        out_specs=[pl.BlockSpec((tk, tn), lambda i, j: (i, j)),
                   pl.BlockSpec((tk, LANES), lambda i, j: (i, 0))],
        out_shape=[jax.ShapeDtypeStruct((k_dim, W_MAIN), BF16),
                   jax.ShapeDtypeStruct((k_dim, LANES), BF16)],
        compiler_params=_cparams(("parallel", "arbitrary")),
        name="repack_w_in",
    )(wt, wt)


def _norm_proj_kernel(x_ref, g_ref, w_ref, *rest, with_side, attn_t):
    rest = list(rest)
    ws_ref = rest.pop(0) if with_side else None
    xq_ref, xk_ref, xv_ref = (rest.pop(0), rest.pop(0), rest.pop(0)) if attn_t else (None,) * 3
    out_ref = rest.pop(0)
    side_ref = rest.pop(0) if with_side else None
    xo_ref = rest.pop(0) if attn_t else None
    h_scr = rest.pop(0)

    @pl.when(pl.program_id(1) == 0)
    def _():
        h = _rms(x_ref[...], g_ref[...]).astype(BF16)
        h_scr[...] = h
        if with_side:
            side_ref[...] = _dot(h, ws_ref[...])

    stages = _xattn_native_stages(xq_ref, xk_ref, xv_ref, xo_ref, attn_t) if attn_t else []
    for stage in stages[:1]:
        stage()
    out_ref[...] = _dot(h_scr[...], w_ref[...].astype(BF16)).astype(out_ref.dtype)
    for stage in stages[1:]:
        stage()


def _norm_proj(x, g, w, w_side=None, *, tm, tn, out_dtype, attn=None):
    n, d = x.shape
    ncol = w.shape[1]
    with_side = w_side is not None
    in_specs = [pl.BlockSpec((tm, d), lambda i, j: (i, 0)),
                pl.BlockSpec((1, d), lambda i, j: (0, 0)),
                pl.BlockSpec((d, tn), lambda i, j: (0, j))]
    out_specs = [pl.BlockSpec((tm, tn), lambda i, j: (i, j))]
    out_shape = [jax.ShapeDtypeStruct((n, ncol), out_dtype)]
    args = [x, g.reshape(1, d), w]
    if with_side:
        in_specs.append(pl.BlockSpec((d, LANES), lambda i, j: (0, 0)))
        out_specs.append(pl.BlockSpec((tm, LANES), lambda i, j: (i, 0)))
        out_shape.append(jax.ShapeDtypeStruct((n, LANES), F32))
        args.append(w_side)
    if attn is not None:
        xq, xk, xv, attn_t = attn
        n_j = ncol // tn
        steps = (n // tm) * n_j
        assert xq.shape[0] % steps == 0
        xbb = xq.shape[0] // steps
        share = lambda rows: pl.BlockSpec((xbb, rows, LANES), lambda i, j: (i * n_j + j, 0, 0))
        in_specs += [share(xq.shape[1]), share(xk.shape[1]), share(xv.shape[1])]
        out_specs.append(share(xq.shape[1]))
        out_shape.append(jax.ShapeDtypeStruct(xq.shape, BF16))
        args += [xq, xk, xv]
    res = pl.pallas_call(
        functools.partial(_norm_proj_kernel, with_side=with_side,
                          attn_t=attn[3] if attn is not None else 0),
        grid=(n // tm, ncol // tn),
        in_specs=in_specs, out_specs=out_specs, out_shape=out_shape,
        scratch_shapes=[pltpu.VMEM((tm, d), BF16)],
        compiler_params=_cparams(("parallel", "arbitrary")),
        name=("norm_proj_attn" if attn is not None else "norm_proj_side") if with_side else "norm_proj",
    )(*args)
    return res if len(res) > 1 else res[0]


INV_BASE_LOG2 = 3
GATE_LANES = 2 * GDN_HEADS
GATE_PACK = LANES // GATE_LANES
_GDN_WORK = (("k", 1, BF16, 0), ("kb", 1, BF16, 0), ("q", 1, BF16, 0), ("kd", 1, BF16, 0),
             ("vk", 2, BF16, 0), ("dinc", 1, F32, 0), ("low", 1, F32, 0), ("x", 1, F32, 0),
             ("p", 1, BF16, 0), ("wu", 2, BF16, 0),
             ("cq", 1, F32, SUBLANES), ("ck", 1, F32, SUBLANES), ("cv", 1, F32, SUBLANES))


def _split3(x):
    hi = x.astype(BF16)
    r1 = x - hi.astype(F32)
    mid = r1.astype(BF16)
    lo = (r1 - mid.astype(F32)).astype(BF16)
    return hi, mid, lo


def _gdn_prompt_kernel(q_ref, k_ref, v_ref, z_ref, ab_ref, cwq_ref, cwk_ref, cwv_ref, gp_ref,
                       gn_ref, o_ref, s_ref,
                       gc_s, gct_s, egc_s, ekd_s, beta_s, u_s, n_s, pw_s, qa_s, sv_s, *work_refs, group):
    nw = len(_GDN_WORK)
    work = [{spec[0]: ref for spec, ref in zip(_GDN_WORK, work_refs[j * nw:(j + 1) * nw])}
            for j in range(len(work_refs) // nw)]
    t_len = q_ref.shape[1]
    c = GDN_CHUNK
    n_c = t_len // c
    hps = q_ref.shape[2] // GDN_DK
    head_of = lambda u: pl.program_id(1) * hps + u // n_c
    lanes_of = lambda u: slice((u // n_c) * GDN_DK, (u // n_c + 1) * GDN_DK)
    row = lax.broadcasted_iota(jnp.int32, (c, c), 0)
    col = lax.broadcasted_iota(jnp.int32, (c, c), 1)

    @pl.when(pl.program_id(1) == 0)
    def _():
        gp = gp_ref[...]
        slot = col[0:1] >> (GATE_LANES.bit_length() - 1)
        par = gp
        for j in range(1, GATE_PACK):
            par = par + jnp.where(slot == j, pltpu.roll(gp, GATE_LANES * j, axis=1), 0.0)
        a_neg = -jnp.exp(par[0:1])
        dt_b = par[1:2]
        tri = jnp.where(row >= col, 1.0, 0.0).astype(BF16)
        for grp in range(n_c // GATE_PACK):
            ab = ab_ref[0, pl.ds(grp * GATE_PACK * c, c), :]
            for j in range(1, GATE_PACK):
                nxt = ab_ref[0, pl.ds((grp * GATE_PACK + j) * c, c), :]
                ab = jnp.where(slot == j, pltpu.roll(nxt, GATE_LANES * j, axis=1), ab)
            hi, mid, lo = _split3(a_neg * _softplus(ab + dt_b))
            gc = _dot(tri, hi) + _dot(tri, mid) + _dot(tri, lo)
            rows = pl.ds(grp * c, c)
            gc_s[rows, :] = gc
            gct_s[rows, :] = gc.T
            egc_s[rows, :] = jnp.exp(gc)
            ekd_s[rows, :] = jnp.exp(gc[c - 1:c, :] - gc)
            beta_s[rows, :] = jax.nn.sigmoid(ab)

    def l2n(x):
        return x * lax.rsqrt(jnp.sum(x * x, axis=-1, keepdims=True) + EPS)

    def operands(w, u):
        ci, hl, h = u % n_c, lanes_of(u), head_of(u)
        r0 = ci * c
        rows = pl.ds(r0, c)

        def conv_silu(x_ref, cw_ref, stage):
            cur = x_ref[0, rows, hl].astype(F32)
            if ci > 0:
                tail = x_ref[0, pl.ds(r0 - BF16_ROWS, BF16_ROWS), hl].astype(F32)[BF16_ROWS - SUBLANES:]
            else:
                tail = jnp.zeros((SUBLANES, cur.shape[1]), F32)
            stage[0:SUBLANES, :] = tail
            stage[SUBLANES:SUBLANES + c, :] = cur
            cw = cw_ref[:, hl]
            y = cur * cw[CONV_W - 1:CONV_W]
            for s in range(1, CONV_W):
                y = y + stage[SUBLANES - s:SUBLANES - s + c, :] * cw[CONV_W - 1 - s:CONV_W - s]
            return _silu(y)

        g_rows = pl.ds((ci // GATE_PACK) * c, c)
        g_lane = (ci % GATE_PACK) * GATE_LANES + h

        def column(scr, idx):
            return jnp.sum(jnp.where(col == idx, scr[g_rows, :], 0.0), axis=-1, keepdims=True)

        q = l2n(conv_silu(q_ref, cwq_ref, w["cq"])) * (GDN_DK ** -0.5)
        k = l2n(conv_silu(k_ref, cwk_ref, w["ck"]))
        v = conv_silu(v_ref, cwv_ref, w["cv"])
        gcc = column(gc_s, g_lane)
        egc = column(egc_s, g_lane)
        ekd = column(ekd_s, g_lane)
        beta = column(beta_s, g_lane + GDN_HEADS)
        grow = gct_s[pl.ds((ci // GATE_PACK) * c + g_lane, 1), :]
        tri_i = row >= col
        w["dinc"][...] = jnp.where(tri_i, jnp.exp(jnp.where(tri_i, gcc - grow, 0.0)), 0.0)
        kb = k * beta
        w["k"][...] = k.astype(BF16)
        w["kb"][...] = kb.astype(BF16)
        w["q"][...] = q.astype(BF16)
        w["kd"][...] = (k * ekd).astype(BF16)
        w["vk"][:, :GDN_DV] = (v * beta).astype(BF16)
        w["vk"][:, GDN_DV:] = (kb * egc).astype(BF16)
        qa_s[u, :, 0:GDN_DK] = (q * egc).astype(BF16)

    def gram(w, u):
        r = _dot_nt(jnp.concatenate([w["kb"][...], w["q"][...]], axis=0), w["k"][...])
        d_incl = w["dinc"][...]
        low = r[:c] * jnp.where(row > col, d_incl, 0.0)
        qa_s[u, :, GDN_DK:] = (r[c:] * d_incl).astype(BF16)
        w["low"][...] = low
        ld = jnp.where((row >> INV_BASE_LOG2) == (col >> INV_BASE_LOG2), low, 0.0)
        w["x"][...] = jnp.where(row == col, 1.0, 0.0) - ld
        w["p"][...] = ld.astype(BF16)

    def neumann(w, first, last):
        pb = w["p"][...]
        if first:
            w["p"][...] = _dot(pb, pb).astype(BF16)
        elif last:
            x = w["x"][...]
            w["x"][...] = x + _dot(x.astype(BF16), pb)
        else:
            x = w["x"][...]
            r = _dot(jnp.concatenate([x.astype(BF16), pb], axis=0), pb)
            w["x"][...] = x + r[:c]
            w["p"][...] = r[c:].astype(BF16)

    def merge_a(w, lg):
        pair = (row >> (lg + 1)) == (col >> (lg + 1))
        m = jnp.where(pair, (row >> lg) - (col >> lg), 0) > 0
        lm = jnp.where(m, w["low"][...], 0.0).astype(BF16)
        w["p"][...] = _dot(lm, w["x"][...].astype(BF16)).astype(BF16)

    def merge_b(w):
        x = w["x"][...]
        w["x"][...] = x - _dot(x.astype(BF16), w["p"][...])

    def solve(w, ci):
        uw = _dot(w["x"][...].astype(BF16), w["vk"][...])
        u_s[ci] = uw[:, :GDN_DV]
        uwb = uw.astype(BF16)
        w["wu"][...] = uwb
        pw_s[ci, c:2 * c, :] = uwb[:, GDN_DV:]

    def outer(w, ci):
        np_ = _dot_tn(w["kd"][...], w["wu"][...])
        n_s[ci] = np_[:, :GDN_DV]
        pw_s[ci, 0:c, :] = np_[:, GDN_DV:].astype(BF16)

    n_sq = INV_BASE_LOG2 - 1
    lgc = c.bit_length() - 1

    def prepare_stages(cis):
        over = lambda fn, *a: (lambda: [fn(work[ci], *[ci if x is cis else x for x in a])
                                        for ci in cis])
        stages = [over(operands, cis), over(gram, cis)]
        stages += [over(neumann, step == 0, step == n_sq) for step in range(n_sq + 1)]
        for lg in range(INV_BASE_LOG2, lgc):
            stages += [over(merge_a, lg), over(merge_b)]
        return stages + [over(solve, cis), over(outer, cis)]

    gn = gn_ref[...]
    lane1 = lax.broadcasted_iota(jnp.int32, (1, LANES), 1)

    def advance(u, s_mat):
        ci = u % n_c
        if ci == 0:
            s_mat = jnp.zeros((GDN_DK, GDN_DV), F32)
        sb = s_mat.astype(BF16)
        r = _dot(pw_s[u], sb)
        g_last = jnp.sum(jnp.where(lane1 == (ci % GATE_PACK) * GATE_LANES + head_of(u),
                                   gc_s[pl.ds((ci // GATE_PACK) * c + c - 1, 1), :], 0.0),
                         axis=-1, keepdims=True)
        sv_s[u, 0:c, :] = sb
        sv_s[u, c:2 * c, :] = (u_s[u] - r[c:2 * c]).astype(BF16)
        s_next = s_mat * jnp.exp(g_last) - r[0:c] + n_s[u]
        if ci == n_c - 1:
            s_ref[0, u // n_c] = s_next
        return s_next

    def emit(u):
        rows, hl = pl.ds((u % n_c) * c, c), lanes_of(u)
        o = _dot(qa_s[u], sv_s[u])
        zz = z_ref[0, rows, hl].astype(F32)
        o_ref[0, rows, hl] = (_rms(o, gn) * _silu(zz)).astype(o_ref.dtype)

    s_mat = None
    chain_todo, emit_todo = [], []
    for g in range(hps * n_c // group):
        us = list(range(g * group, (g + 1) * group))
        for stage in prepare_stages(us):
            stage()
            if chain_todo:
                emit_todo.append(chain_todo[0])
                s_mat = advance(chain_todo.pop(0), s_mat)
            elif emit_todo:
                emit(emit_todo.pop(0))
        chain_todo += us
    while chain_todo:
        if emit_todo:
            emit(emit_todo.pop(0))
        emit_todo.append(chain_todo[0])
        s_mat = advance(chain_todo.pop(0), s_mat)
    for u in emit_todo:
        emit(u)


def _gdn_prompt(proj3, ab3, conv_w, gate_par, gdn_norm):
    bsz, t_len, _ = proj3.shape
    nh = GDN_HEADS
    hps = 2
    hw = hps * GDN_DK
    blk = lambda off: pl.BlockSpec((1, t_len, hw), lambda b, p: (b, 0, off // hps + p))
    cw = lambda off: pl.BlockSpec((CONV_W, hw), lambda b, p: (0, off // hps + p))
    c = GDN_CHUNK
    n_c = t_len // c
    n_u = hps * n_c
    assert n_c % GATE_PACK == 0 and nh % hps == 0
    gate_scr = pltpu.VMEM((n_c // GATE_PACK * c, LANES), F32)
    group = 16
    work = [pltpu.VMEM((c + extra, wide * LANES), dt)
            for _ in range(n_u) for _, wide, dt, extra in _GDN_WORK]
    return pl.pallas_call(
        functools.partial(_gdn_prompt_kernel, group=group),
        grid=(bsz, nh // hps),
        in_specs=[blk(0), blk(nh), blk(2 * nh), blk(3 * nh),
                  pl.BlockSpec((1, t_len, LANES), lambda b, p: (b, 0, 0)),
                  cw(0), cw(nh), cw(2 * nh),
                  pl.BlockSpec((SUBLANES, LANES), lambda b, p: (0, 0)),
                  pl.BlockSpec((1, LANES), lambda b, p: (0, 0))],
        out_specs=[pl.BlockSpec((1, t_len, hw), lambda b, p: (b, 0, p)),
                   pl.BlockSpec((1, hps, GDN_DK, GDN_DV), lambda b, p: (b, p, 0, 0))],
        out_shape=[jax.ShapeDtypeStruct((bsz, t_len, MIX_A), BF16),
                   jax.ShapeDtypeStruct((bsz, nh, GDN_DK, GDN_DV), F32)],
        scratch_shapes=[gate_scr, gate_scr, gate_scr, gate_scr, gate_scr,
                        pltpu.VMEM((n_u, c, GDN_DV), F32), pltpu.VMEM((n_u, GDN_DK, GDN_DV), F32),
                        pltpu.VMEM((n_u, 2 * c, GDN_DK), BF16), pltpu.VMEM((n_u, c, GDN_DK + c), BF16),
                        pltpu.VMEM((n_u, GDN_DK + c, GDN_DV), BF16)] + work,
        compiler_params=_cparams(("parallel", "arbitrary")),
        name="gdn_prompt",
    )(proj3, proj3, proj3, proj3, ab3, conv_w, conv_w, conv_w, gate_par, gdn_norm.reshape(1, LANES))


def _gdn_sample_kernel(x_ref, ab_ref, hist_ref, cw_ref, gp_ref, gn_ref, s0_ref, o_ref, s_ref,
                       *, t_len):
    nh = GDN_HEADS
    g8 = SUBLANES
    n = nh * g8
    assert t_len + CONV_W - 1 <= g8
    cw = cw_ref[...]
    gp = gp_ref[...]
    gn = gn_ref[...]
    row8 = lax.broadcasted_iota(jnp.int32, (g8, QKV_W), 0)
    tpos = lax.broadcasted_iota(jnp.int32, (n, 1), 0) & (g8 - 1)
    rhead = lax.broadcasted_iota(jnp.int32, (n, GDN_DV), 0) >> 3
    ri = lax.broadcasted_iota(jnp.int32, (n, n), 0)
    ci = lax.broadcasted_iota(jnp.int32, (n, n), 1)
    tri = jnp.where((ri >> 3) == (ci >> 3), ri - ci, -1)
    eye = jnp.where(ri == ci, 1.0, 0.0)
    sel0 = jnp.where(lax.broadcasted_iota(jnp.int32, (n, LANES), 1) == 0, 1.0, 0.0).astype(BF16)
    valid = tpos < t_len
    a_neg = jnp.concatenate([jnp.broadcast_to(-jnp.exp(gp[0:1, h:h + 1]), (g8, 1)) for h in range(nh)], 0)
    dt_b = jnp.concatenate([jnp.broadcast_to(gp[1:2, h:h + 1], (g8, 1)) for h in range(nh)], 0)

    def l2n(x):
        return x * lax.rsqrt(jnp.sum(x * x, axis=-1, keepdims=True) + EPS)

    hist_w = [sum(jnp.where(row8 == t, cw[e - t:e - t + 1], 0.0) for t in range(e + 1))
              for e in range(CONV_W - 1)]

    def operands(b):
        xb = x_ref[b]
        x = xb[:, :QKV_W].astype(F32)
        y = x * cw[CONV_W - 1:CONV_W]
        for s in range(1, CONV_W):
            y = y + pltpu.roll(x, s, axis=0) * cw[CONV_W - 1 - s:CONV_W - s]
        for e in range(CONV_W - 1):
            y = y + hist_ref[e, b:b + 1, :] * hist_w[e]
        qkv = jnp.where(row8 < t_len, _silu(y), 0.0)

        def stack(off):
            return jnp.concatenate([qkv[:, off + h * GDN_DK:off + (h + 1) * GDN_DK] for h in range(nh)], 0)

        q = l2n(stack(0)) * (GDN_DK ** -0.5)
        k = l2n(stack(QK_W))
        v = stack(2 * QK_W)
        ab = ab_ref[b]
        alpha = jnp.concatenate([ab[:, h:h + 1] for h in range(nh)], 0)
        braw = jnp.concatenate([ab[:, nh + h:nh + h + 1] for h in range(nh)], 0)
        g = jnp.where(valid, a_neg * _softplus(alpha + dt_b), 0.0)
        beta = jnp.where(valid, jax.nn.sigmoid(braw), 0.0)
        gc = g
        s = 1
        while s < g8:
            gc = gc + jnp.where(tpos >= s, pltpu.roll(gc, s, axis=0), 0.0)
            s *= 2
        g_last = jnp.concatenate([jnp.broadcast_to(gc[h * g8 + g8 - 1:(h + 1) * g8, :], (g8, 1))
                                  for h in range(nh)], 0)
        egc = jnp.exp(gc)
        kb = k * beta
        return dict(gc=gc, g_last=g_last, kbf=k.astype(BF16), kbb=kb.astype(BF16), qb=q.astype(BF16),
                    vb=(v * beta).astype(BF16), kg=(kb * egc).astype(BF16), qg=q * egc,
                    k_dec=(k * jnp.exp(g_last - gc)).astype(BF16))

    def gram(st):
        hi, mid, lo = _split3(jnp.broadcast_to(st["gc"], (n, LANES)))
        grow = _dot_nt(sel0, hi) + _dot_nt(sel0, mid) + _dot_nt(sel0, lo)
        d_incl = jnp.where(tri >= 0, jnp.exp(jnp.where(tri >= 0, st["gc"] - grow, 0.0)), 0.0)
        st["low"] = _dot_nt(st["kbb"], st["kbf"]) * jnp.where(tri > 0, d_incl, 0.0)
        st["a_in"] = (_dot_nt(st["qb"], st["kbf"]) * d_incl).astype(BF16)
        st["t_inv"] = eye - st["low"]
        st["p"] = st["low"].astype(BF16)

    def neumann(st):
        p = _dot(st["p"], st["p"])
        st["p"] = p.astype(BF16)
        st["t_inv"] = st["t_inv"] + _dot(st["t_inv"].astype(BF16), st["p"])

    def solve(st):
        tb = st["t_inv"].astype(BF16)
        st["u"] = _dot(tb, st["vb"])
        st["w"] = _dot(tb, st["kg"])

    def apply_state(st, b):
        ws, qs = [], []
        for h in range(nh):
            hs = slice(h * g8, (h + 1) * g8)
            lhs = jnp.concatenate([st["w"][hs], st["qg"][hs]], 0).astype(BF16)
            r = _dot(lhs, s0_ref[b, h].astype(BF16))
            ws.append(r[:g8])
            qs.append(r[g8:])
        st["v_new"] = st["u"] - jnp.concatenate(ws, 0)
        st["oq"] = jnp.concatenate(qs, 0)

    def finish(st, b):
        v_new = st["v_new"]
        o = st["oq"] + _dot(st["a_in"], v_new.astype(BF16))
        for h in range(nh):
            vm = jnp.where(rhead == h, v_new, 0.0).astype(BF16)
            dec = jnp.exp(st["g_last"][h * g8:h * g8 + 1, :])
            s_ref[b, h] = s0_ref[b, h] * dec + _dot_tn(st["k_dec"], vm)
        z = jnp.concatenate([x_ref[b][:, 3 * QK_W + h * GDN_DV:3 * QK_W + (h + 1) * GDN_DV].astype(F32)
                             for h in range(nh)], 0)
        o_ref[b] = (_rms(o, gn) * _silu(z)).astype(o_ref.dtype)

    bs_ = range(x_ref.shape[0])
    sts = [operands(b) for b in bs_]
    for st in sts:
        gram(st)
    covered = 2
    while covered < t_len:
        for st in sts:
            neumann(st)
        covered *= 2
    for st in sts:
        solve(st)
    for b in bs_:
        apply_state(sts[b], b)
    for b in bs_:
        finish(sts[b], b)


def _gdn_sample(proj3, ab3, hist, conv_w, gate_par, gdn_norm, s0, *, bb):
    bsz, t_len, _ = proj3.shape
    nh, g8 = GDN_HEADS, SUBLANES
    pad_t = lambda a, rows: jnp.pad(a, ((0, 0), (0, g8 - rows), (0, 0)))
    x8 = pad_t(proj3[:, :, :4 * QK_W], t_len)
    ab8 = pad_t(ab3, t_len)
    hist_tm = hist.transpose(1, 0, 2)
    o, s_new = pl.pallas_call(
        functools.partial(_gdn_sample_kernel, t_len=t_len),
        grid=(bsz // bb,),
        in_specs=[pl.BlockSpec((bb, g8, 4 * QK_W), lambda i: (i, 0, 0)),
                  pl.BlockSpec((bb, g8, LANES), lambda i: (i, 0, 0)),
                  pl.BlockSpec((CONV_W - 1, bb, QKV_W), lambda i: (0, i, 0)),
                  pl.BlockSpec((CONV_W, QKV_W), lambda i: (0, 0)),
                  pl.BlockSpec((SUBLANES, LANES), lambda i: (0, 0)),
                  pl.BlockSpec((1, LANES), lambda i: (0, 0)),
                  pl.BlockSpec((bb, nh, GDN_DK, GDN_DV), lambda i: (i, 0, 0, 0))],
        out_specs=[pl.BlockSpec((bb, nh * g8, GDN_DV), lambda i: (i, 0, 0)),
                   pl.BlockSpec((bb, nh, GDN_DK, GDN_DV), lambda i: (i, 0, 0, 0))],
        out_shape=[jax.ShapeDtypeStruct((bsz, nh * g8, GDN_DV), BF16),
                   jax.ShapeDtypeStruct((bsz, nh, GDN_DK, GDN_DV), F32)],
        compiler_params=_cparams(("parallel",)),
        name="gdn_sample",
    )(x8, ab8, hist_tm, conv_w, gate_par, gdn_norm.reshape(1, LANES), s0)
    o = o.reshape(bsz, nh, g8, GDN_DV)[:, :, :t_len].transpose(0, 2, 1, 3)
    return o.reshape(bsz, t_len, MIX_A), s_new


def _pool_finish(d_groups, pw_ref, ps_ref, zb):
    outs = [_dot(d.astype(BF16), pw_ref[gi]) for gi, d in enumerate(d_groups)]
    return jnp.concatenate(outs, axis=-1) * ps_ref[...] * _silu(zb)


def _pool_tile(u, halo, zb, pos, pw_ref, ps_ref):
    ext = jnp.concatenate([halo, u], axis=0)
    d_groups = []
    for gi, win in enumerate(POOL_WINDOWS):
        sl = slice(gi * POOL_CH, (gi + 1) * POOL_CH)
        lvl = ext[:, sl]
        s = 1
        while s < win:
            lvl = lvl + pltpu.roll(lvl, s, axis=0)
            s *= 2
        cnt = jnp.minimum(win, pos + 1).astype(F32)
        d_groups.append(lvl[POOL_HALO:, :] / cnt - u[:, sl])
    return _pool_finish(d_groups, pw_ref, ps_ref, zb)


def _mix_pool_kernel(x_ref, oa_ref, u_ref, halo_ref, zb_ref, pw_ref, ps_ref, wo_ref, nc_ref, wq_ref,
                     x1_ref, qx_ref, *, pos0):
    i = pl.program_id(1)
    tt = u_ref.shape[1]
    halo = jnp.where(i > 0, halo_ref[0].astype(F32), 0.0)
    pos = pos0 + i * tt + lax.broadcasted_iota(jnp.int32, (tt, 1), 0)
    ob = _pool_tile(u_ref[0].astype(F32), halo, zb_ref[0].astype(F32), pos, pw_ref, ps_ref)
    acc = _dot(oa_ref[0], wo_ref[0:MIX_A, :]) + _dot(ob.astype(BF16), wo_ref[MIX_A:D_MODEL, :])
    x1 = x_ref[0] + acc
    x1_ref[0] = x1
    qx_ref[0] = _dot(_rms(x1, nc_ref[...]).astype(BF16), wq_ref[...]).astype(qx_ref.dtype)


def _mix_pool(x3, oa3, proj3, pool_w, pool_scale, w_out, norm_cross, w_cq, *, tt, pos0):
    bsz, t_len, d = x3.shape
    ub, zb = COL_ZA // MIX_B + 1, COL_ZA // MIX_B + 2
    hb = tt // POOL_HALO
    const2 = lambda b, i: (0, 0)
    resident = dict(pipeline_mode=pl.Buffered(1))
    row = lambda w: pl.BlockSpec((1, tt, w), lambda b, i: (b, i, 0))
    return pl.pallas_call(
        functools.partial(_mix_pool_kernel, pos0=pos0),
        grid=(bsz, t_len // tt),
        in_specs=[row(d), row(MIX_A),
                  pl.BlockSpec((1, tt, MIX_B), lambda b, i: (b, i, ub)),
                  pl.BlockSpec((1, POOL_HALO, MIX_B), lambda b, i: (b, jnp.maximum(i * hb - 1, 0), ub)),
                  pl.BlockSpec((1, tt, MIX_B), lambda b, i: (b, i, zb)),
                  pl.BlockSpec((POOL_GROUPS, POOL_CH, POOL_CH), lambda b, i: (0, 0, 0), **resident),
                  pl.BlockSpec((1, MIX_B), const2, **resident),
                  pl.BlockSpec((d, d), const2, **resident),
                  pl.BlockSpec((1, d), const2, **resident),
                  pl.BlockSpec((d, d), const2, **resident)],
        out_specs=[row(d), row(d)],
        out_shape=[jax.ShapeDtypeStruct((bsz, t_len, d), F32), jax.ShapeDtypeStruct((bsz, t_len, d), BF16)],
        compiler_params=_cparams(("parallel", "arbitrary")),
        name="mix_pool",
    )(x3, oa3, proj3, proj3, proj3, pool_w, pool_scale.reshape(1, MIX_B), w_out,
      norm_cross.reshape(1, d), w_cq)


def _pool_sample_kernel(hist_ref, u_ref, zb_ref, pw_ref, ps_ref, o_ref, *, pos0):
    t_len = zb_ref.shape[0]

    def ext(e, sl):
        return hist_ref[e, :, sl] if e < POOL_BUF else u_ref[e - POOL_BUF, :, sl].astype(F32)

    for t in range(t_len):
        e = POOL_BUF + t
        d_groups = []
        for gi, win in enumerate(POOL_WINDOWS):
            sl = slice(gi * POOL_CH, (gi + 1) * POOL_CH)
            tot = ext(e, sl)
            for j in range(1, win):
                tot = tot + ext(e - j, sl)
            cnt = float(min(win, pos0 + t + 1))
            d_groups.append(tot / cnt - ext(e, sl))
        o_ref[t] = _pool_finish(d_groups, pw_ref, ps_ref, zb_ref[t].astype(F32)).astype(o_ref.dtype)


def _pool_sample(hist_tm, u_tm, zb_tm, pool_w, pool_scale, *, bb, pos0):
    t_len, bsz, _ = zb_tm.shape
    return pl.pallas_call(
        functools.partial(_pool_sample_kernel, pos0=pos0),
        grid=(bsz // bb,),
        in_specs=[pl.BlockSpec((POOL_BUF, bb, MIX_B), lambda i: (0, i, 0)),
                  pl.BlockSpec((t_len, bb, MIX_B), lambda i: (0, i, 0)),
                  pl.BlockSpec((t_len, bb, MIX_B), lambda i: (0, i, 0)),
                  pl.BlockSpec((POOL_GROUPS, POOL_CH, POOL_CH), lambda i: (0, 0, 0)),
                  pl.BlockSpec((1, MIX_B), lambda i: (0, 0))],
        out_specs=pl.BlockSpec((t_len, bb, MIX_B), lambda i: (0, i, 0)),
        out_shape=jax.ShapeDtypeStruct((t_len, bsz, MIX_B), BF16),
        compiler_params=_cparams(("parallel",)),
        name="pool_sample",
    )(hist_tm, u_tm, zb_tm, pool_w, pool_scale.reshape(1, MIX_B))


def _mix_out_kernel(x_ref, oa_ref, ob_ref, wo_ref, nc_ref, wq_ref, x1_ref, qx_ref):
    acc = _dot(oa_ref[...], wo_ref[0:MIX_A, :]) + _dot(ob_ref[...], wo_ref[MIX_A:D_MODEL, :])
    x1 = x_ref[...] + acc
    x1_ref[...] = x1
    qx_ref[...] = _dot(_rms(x1, nc_ref[...]).astype(BF16), wq_ref[...]).astype(qx_ref.dtype)


def _mix_out(x, oa, ob, w_out, norm_cross, w_cq, *, tm):
    n, d = x.shape
    const = lambda i: (0, 0)
    return pl.pallas_call(
        _mix_out_kernel,
        grid=(n // tm,),
        in_specs=[pl.BlockSpec((tm, d), lambda i: (i, 0)),
                  pl.BlockSpec((tm, MIX_A), lambda i: (i, 0)),
                  pl.BlockSpec((tm, MIX_B), lambda i: (i, 0)),
                  pl.BlockSpec((d, d), const), pl.BlockSpec((1, d), const), pl.BlockSpec((d, d), const)],
        out_specs=[pl.BlockSpec((tm, d), lambda i: (i, 0)), pl.BlockSpec((tm, d), lambda i: (i, 0))],
        out_shape=[jax.ShapeDtypeStruct((n, d), F32), jax.ShapeDtypeStruct((n, d), BF16)],
        compiler_params=_cparams(("parallel",)),
        name="mix_out",
    )(x, oa, ob, w_out, norm_cross.reshape(1, d), w_cq)


def _xattn_out_kernel(q_ref, k_ref, v_ref, x1_ref, wco_ref, nf_ref, y_ref):
    q = q_ref[0]
    scale = X_HEAD_DIM ** -0.5
    sls = [slice(h * X_HEAD_DIM, (h + 1) * X_HEAD_DIM) for h in range(X_HEADS)]
    ss = [_dot_nt(q[:, sl], k_ref[0, :, sl].astype(BF16)) * scale for sl in sls]
    ps = []
    for s in ss:
        p = jnp.exp(s - jnp.max(s, axis=-1, keepdims=True))
        ps.append((p / jnp.sum(p, axis=-1, keepdims=True)).astype(BF16))
    ctx = jnp.concatenate([_dot(p, v_ref[0, :, sl].astype(BF16)).astype(BF16)
                           for p, sl in zip(ps, sls)], axis=-1)
    y_ref[0] = _rms(x1_ref[0] + _dot(ctx, wco_ref[...]), nf_ref[...])


def _xattn_out(qx3, mk3, mv3, x13, w_co, norm_final, *, tq):
    bsz, t_len, d = qx3.shape
    n_mem = mk3.shape[1]
    row = pl.BlockSpec((1, tq, d), lambda b, i: (b, i, 0))
    mem = pl.BlockSpec((1, n_mem, d), lambda b, i: (b, 0, 0))
    return pl.pallas_call(
        _xattn_out_kernel,
        grid=(bsz, t_len // tq),
        in_specs=[row, mem, mem, row,
                  pl.BlockSpec((d, d), lambda b, i: (0, 0), pipeline_mode=pl.Buffered(1)),
                  pl.BlockSpec((1, d), lambda b, i: (0, 0), pipeline_mode=pl.Buffered(1))],
        out_specs=row,
        out_shape=jax.ShapeDtypeStruct((bsz, t_len, d), F32),
        compiler_params=_cparams(("parallel", "arbitrary")),
        name="xattn_out",
    )(qx3, mk3, mv3, x13, w_co, norm_final.reshape(1, d))


def _xattn_native_stages(q_ref, k_ref, v_ref, o_ref, t_len):
    nj = X_HEAD_DIM // LANES
    grp = nj * X_HEADS
    th = t_len * X_HEADS
    scale = X_HEAD_DIM ** -0.5
    ncol = k_ref.shape[1]
    r = lax.broadcasted_iota(jnp.int32, (th, ncol), 0) & (X_HEADS - 1)
    c = lax.broadcasted_iota(jnp.int32, (th, ncol), 1) & (grp - 1)

    def probs(z):
        s = None
        for j in range(nj):
            zj = jnp.where(c == r + j * X_HEADS, z[j * th:(j + 1) * th, :], 0.0)
            if j:
                zj = pltpu.roll(zj, ncol - j * X_HEADS, axis=1)
            s = zj if s is None else s + zj
        sm = jnp.where(c == r, s * scale, -jnp.inf)
        p = jnp.exp(sm - jnp.max(sm, axis=1, keepdims=True))
        p = p / jnp.sum(p, axis=1, keepdims=True)
        return jnp.concatenate([p if j == 0 else pltpu.roll(p, j * X_HEADS, axis=1)
                                for j in range(nj)], axis=0).astype(BF16)

    bs_ = range(q_ref.shape[0])
    vals = {}

    def scores():
        vals["z"] = [_dot_nt(q_ref[b], k_ref[b].astype(BF16)) for b in bs_]

    def probabilities():
        vals["p"] = [probs(z) for z in vals["z"]]

    def context():
        for b in bs_:
            o_ref[b] = _dot(vals["p"][b], v_ref[b].astype(BF16)).astype(o_ref.dtype)

    return [scores, probabilities, context]


def _kv_rows(cache):
    bsz, n_mem, nh, dh = cache.shape
    nj = dh // LANES
    return cache.reshape(bsz, n_mem, nh, nj, LANES).transpose(0, 1, 3, 2, 4).reshape(
        bsz, n_mem * nj * nh, LANES)


def _attn_out_kernel(x1_ref, ctx_ref, wco_ref, nf_ref, y_ref):
    x2 = x1_ref[...] + _dot(ctx_ref[...], wco_ref[...])
    y_ref[...] = _rms(x2, nf_ref[...])


def _attn_out(x1, ctx, w_co, norm_final, *, tm):
    n, d = x1.shape
    const = lambda i: (0, 0)
    return pl.pallas_call(
        _attn_out_kernel,
        grid=(n // tm,),
        in_specs=[pl.BlockSpec((tm, d), lambda i: (i, 0)), pl.BlockSpec((tm, d), lambda i: (i, 0)),
                  pl.BlockSpec((d, d), const, pipeline_mode=pl.Buffered(1)),
                  pl.BlockSpec((1, d), const, pipeline_mode=pl.Buffered(1))],
        out_specs=pl.BlockSpec((tm, d), lambda i: (i, 0)),
        out_shape=jax.ShapeDtypeStruct((n, d), F32),
        compiler_params=_cparams(("parallel",)),
        name="attn_out",
    )(x1, ctx, w_co, norm_final.reshape(1, d))


def kernel(x_prompt, x_sample, mem_prompt, cache_mem_k, cache_mem_v, state_delta, state_conv,
           state_pool, norm_mix, w_in, conv_w, a_log, dt_bias, gdn_norm, pool_w, pool_scale,
           w_out, norm_mem, norm_cross, w_cq, w_ck, w_cv, w_co, norm_final):
    bp, tp, d = x_prompt.shape
    bs, ts, _ = x_sample.shape
    n_mem = mem_prompt.shape[1]

    w_main, w_gate = _repack_w_in(w_in[0].T, tk=2048, tn=1024)
    wo, wcq, wco = (w[0].astype(BF16) for w in (w_out, w_cq, w_co))
    pw = pool_w[0].astype(BF16)
    gate_par = jnp.zeros((SUBLANES, LANES), F32)
    gate_par = gate_par.at[0, :GDN_HEADS].set(a_log[0]).at[1, :GDN_HEADS].set(dt_bias[0])
    cw = conv_w[0]

    mem2d = mem_prompt.reshape(bp * n_mem, d)
    mk = _norm_proj(mem2d, norm_mem[0], w_ck[0], tm=bp * n_mem, tn=1024, out_dtype=F32)
    mv = _norm_proj(mem2d, norm_mem[0], w_cv[0], tm=bp * n_mem, tn=1024, out_dtype=F32)

    proj_s, ab_s = _norm_proj(x_sample.reshape(bs * ts, d), norm_mix[0], w_main, w_gate,
                              tm=bs * ts, tn=1024, out_dtype=BF16)
    proj_s3 = proj_s.reshape(bs, ts, W_MAIN)
    oa_s, delta_s = _gdn_sample(proj_s3, ab_s.reshape(bs, ts, LANES), state_conv[0], cw, gate_par,
                                gdn_norm[0], state_delta[0], bb=16)
    u_s = proj_s3[:, :, COL_ZA + MIX_A:COL_ZA + MIX_A + MIX_B]
    ext_s = jnp.concatenate([state_pool[0], u_s.astype(F32)], axis=1)
    zb_tm = proj_s3[:, :, COL_ZA + MIX_A + MIX_B:].transpose(1, 0, 2)
    ob_s = _pool_sample(state_pool[0].transpose(1, 0, 2), u_s.transpose(1, 0, 2), zb_tm, pw,
                        pool_scale[0], bb=32, pos0=PAST_LEN).transpose(1, 0, 2)
    x1_s, qx_s = _mix_out(x_sample.reshape(bs * ts, d), oa_s.reshape(-1, MIX_A),
                          ob_s.reshape(-1, MIX_B), wo, norm_cross[0], wcq, tm=256)
    nj = X_HEAD_DIM // LANES
    q_rows = qx_s.reshape(bs, ts, X_HEADS, nj, LANES).transpose(0, 3, 1, 2, 4).reshape(
        bs, nj * ts * X_HEADS, LANES)

    proj_p, ab_p, ctx_rows = _norm_proj(
        x_prompt.reshape(bp * tp, d), norm_mix[0], w_main, w_gate, tm=1024, tn=768, out_dtype=BF16,
        attn=(q_rows, _kv_rows(cache_mem_k[0]), _kv_rows(cache_mem_v[0]), ts))
    proj_p3 = proj_p.reshape(bp, tp, W_MAIN)
    oa_p, delta_p = _gdn_prompt(proj_p3, ab_p.reshape(bp, tp, LANES), cw, gate_par, gdn_norm[0])
    x1_p, qx_p = _mix_pool(x_prompt, oa_p, proj_p3, pw, pool_scale[0], wo, norm_cross[0], wcq,
                           tt=512, pos0=0)
    y_p = _xattn_out(qx_p, mk.reshape(bp, n_mem, d), mv.reshape(bp, n_mem, d), x1_p, wco,
                     norm_final, tq=512)
    conv_p = proj_p3[:, tp - (CONV_W - 1):, :QKV_W].astype(F32)
    pool_p = proj_p3[:, tp - POOL_BUF:, COL_ZA + MIX_A:COL_ZA + MIX_A + MIX_B].astype(F32)

    ctx_s = ctx_rows.reshape(bs, nj, ts, X_HEADS, LANES).transpose(0, 2, 3, 1, 4).reshape(bs * ts, d)
    y_s = _attn_out(x1_s, ctx_s, wco, norm_final, tm=256).reshape(bs, ts, d)
    conv_s = jnp.concatenate([state_conv[0], proj_s3[:, :, :QKV_W].astype(F32)], axis=1)[:, ts:]
    pool_s = ext_s[:, ts:]

    hd = (X_HEADS, X_HEAD_DIM)
    return (y_p, y_s, mk.reshape(1, bp, n_mem, *hd), mv.reshape(1, bp, n_mem, *hd),
            delta_p[None], conv_p[None], pool_p[None], delta_s[None], conv_s[None], pool_s[None])
```
